```python
import jax, jax.numpy as jnp
from jax import lax
import numpy as np

D_MODEL = 1024
BATCH = 8
SEQ = 8192
DEPTH = 2

CHUNK = 64
N_META = 16
D_MIX = D_MODEL
D_POOL = D_MIX // 4
D_CONV = D_MIX // 4
D_RET = D_MIX - D_POOL - D_CONV
POOL_WINDOWS = (2, 4, 8, 16)
N_POOL_GROUPS = len(POOL_WINDOWS)
POOL_GROUP = D_POOL // N_POOL_GROUPS
CONV_WIDTH = 31
RET_HEADS = 4
RET_HEAD_DIM = D_RET // RET_HEADS
ROPE_BASE = 10000.0
D_FF = ((8 * D_MODEL // 3 + 63) // 64) * 64
D_IN = D_POOL + 2 * D_CONV + 4 * D_RET
DEEPNORM_ALPHA = (2.0 * DEPTH) ** 0.25
DEEPNORM_BETA = (8.0 * DEPTH) ** -0.25
LN_EPS = 1e-5

kernel_name = "hybrid_pool_conv_retention_deepnorm_trunk"


def layer_norm(x, g, b):
    xf = x.astype(jnp.float32)
    mu = jnp.mean(xf, axis=-1, keepdims=True)
    var = jnp.mean(jnp.square(xf - mu), axis=-1, keepdims=True)
    return ((xf - mu) * lax.rsqrt(var + LN_EPS) * g + b).astype(x.dtype)


def swiglu_ffn(x, w13, w2):
    a, u = jnp.split(x @ w13, 2, axis=-1)
    return (jax.nn.silu(a) * u) @ w2


def pool_mixer(xp, w_pool, scale):
    B, L, _ = xp.shape
    xf = xp.astype(jnp.float32)
    cs = jnp.concatenate([jnp.zeros((B, 1, D_POOL), jnp.float32), jnp.cumsum(xf, axis=1)], axis=1)
    t = jnp.arange(L)
    outs = []
    for gi, w in enumerate(POOL_WINDOWS):
        lo, hi = gi * POOL_GROUP, (gi + 1) * POOL_GROUP
        csg = cs[..., lo:hi]
        start = jnp.maximum(t + 1 - w, 0)
        win_sum = csg[:, 1:] - csg[:, start]
        count = (t + 1 - start).astype(jnp.float32)
        outs.append(win_sum / count[None, :, None] - xf[..., lo:hi])
    y = jnp.stack(outs, axis=2)
    y = jnp.einsum('blgc,gcd->blgd', y, w_pool.astype(jnp.float32)).reshape(B, L, D_POOL)
    return (y * scale).astype(xp.dtype)


def conv_module(a, gate, w_dw, b_dw, ln_g, ln_b, w_pw):
    u = a * jax.nn.sigmoid(gate)
    y = lax.conv_general_dilated(u, w_dw[:, None, :], window_strides=(1,),
                                 padding=[(CONV_WIDTH - 1, 0)],
                                 dimension_numbers=('NWC', 'WIO', 'NWC'),
                                 feature_group_count=D_CONV) + b_dw
    y = jax.nn.silu(layer_norm(y, ln_g, ln_b))
    return y @ w_pw


def rope(x, cos, sin):
    x1, x2 = jnp.split(x, 2, axis=-1)
    return jnp.concatenate([x1 * cos - x2 * sin, x2 * cos + x1 * sin], axis=-1)


def retention(q, k, v, g, gn_g):
    B, L, _ = q.shape
    f32 = jnp.float32
    pos = jnp.arange(L, dtype=f32)
    inv_freq = ROPE_BASE ** (-jnp.arange(0, RET_HEAD_DIM, 2, dtype=f32) / RET_HEAD_DIM)
    ang = pos[:, None] * inv_freq[None, :]
    cos, sin = jnp.cos(ang), jnp.sin(ang)
    heads = lambda t: t.astype(f32).reshape(B, L, RET_HEADS, RET_HEAD_DIM).transpose(0, 2, 1, 3)
    qh = rope(heads(q), cos, sin)
    kh = rope(heads(k), cos, sin) * (RET_HEAD_DIM ** -0.5)
    vh = heads(v)
    P = (-N_META) % CHUNK
    NC = (L + P) // CHUNK
    chunk = lambda t: jnp.pad(t, ((0, 0), (0, 0), (P, 0), (0, 0))).reshape(B, RET_HEADS, NC, CHUNK, RET_HEAD_DIM)
    qc, kc, vc = chunk(qh), chunk(kh), chunk(vh)
    log_gamma = jnp.log(1.0 - 2.0 ** (-5.0 - jnp.arange(RET_HEADS, dtype=f32)))
    i = jnp.arange(CHUNK, dtype=f32)
    intra_decay = jnp.exp(log_gamma[:, None, None] * jnp.abs(i[:, None] - i[None, :]))
    s = jnp.einsum('bhnid,bhnjd->bhnij', qc, kc) * intra_decay[None, :, None]
    o_intra = jnp.einsum('bhnij,bhnjd->bhnid', s, vc)
    q_decay = jnp.exp(log_gamma[:, None] * (i + 1.0))[None, :, :, None]
    k_decay = jnp.exp(log_gamma[:, None] * (CHUNK - 1.0 - i))[None, :, :, None]
    chunk_decay = jnp.exp(log_gamma * CHUNK)[None, :, None, None]

    def step(state, inp):
        q_n, k_n, v_n = inp
        o = jnp.einsum('bhid,bhde->bhie', q_n * q_decay, state)
        state = state * chunk_decay + jnp.einsum('bhjd,bhje->bhde', k_n * k_decay, v_n)
        return state, o

    to_scan = lambda t: t.transpose(2, 0, 1, 3, 4)
    state0 = jnp.zeros((B, RET_HEADS, RET_HEAD_DIM, RET_HEAD_DIM), f32)
    _, o_cross = lax.scan(step, state0, (to_scan(qc), to_scan(kc), to_scan(vc)))
    o = o_intra + o_cross.transpose(1, 2, 0, 3, 4)
    o = o.reshape(B, RET_HEADS, NC * CHUNK, RET_HEAD_DIM)[:, :, P:]
    mu = jnp.mean(o, axis=-1, keepdims=True)
    var = jnp.mean(jnp.square(o - mu), axis=-1, keepdims=True)
    o = ((o - mu) * lax.rsqrt(var + LN_EPS)).transpose(0, 2, 1, 3).reshape(B, L, D_RET) * gn_g
    return (jax.nn.silu(g.astype(f32)) * o).astype(g.dtype)


def token_mix(h, w_in, pool_w, pool_scale, conv_dw, conv_db, conv_ln_g, conv_ln_b, conv_pw, ret_gn_g, w_out):
    z = h @ w_in
    splits = [D_POOL, D_POOL + D_CONV, D_POOL + 2 * D_CONV,
              D_POOL + 2 * D_CONV + D_RET, D_POOL + 2 * D_CONV + 2 * D_RET,
              D_POOL + 2 * D_CONV + 3 * D_RET]
    xp, ca, cg, q, k, v, g = jnp.split(z, splits, axis=-1)
    y_pool = pool_mixer(xp, pool_w, pool_scale)
    y_conv = conv_module(ca, cg, conv_dw, conv_db, conv_ln_g, conv_ln_b, conv_pw)
    y_ret = retention(q, k, v, g, ret_gn_g)
    return jnp.concatenate([y_pool, y_conv, y_ret], axis=-1) @ w_out


def _fwd_setup_inputs(seed: int = 0) -> dict:
    key = jax.random.key(seed)
    ks = jax.random.split(key, 24)
    nrm = lambda k, shape, s: jax.random.normal(k, shape, jnp.float32) * s
    ones_n = lambda k, shape: 1.0 + 0.05 * jax.random.normal(k, shape, jnp.float32)
    return {
        "x": nrm(ks[0], (BATCH, SEQ, D_MODEL), 1.0),
        "meta": nrm(ks[1], (N_META, D_MODEL), 1.0),
        "ln_in_g": ones_n(ks[2], (D_MODEL,)),
        "ln_in_b": nrm(ks[3], (D_MODEL,), 0.02),
        "ffn1_w13": nrm(ks[4], (DEPTH, D_MODEL, 2 * D_FF), D_MODEL ** -0.5),
        "ffn1_w2": nrm(ks[5], (DEPTH, D_FF, D_MODEL), DEEPNORM_BETA * D_FF ** -0.5),
        "w_in": nrm(ks[6], (DEPTH, D_MODEL, D_IN), D_MODEL ** -0.5),
        "pool_w": nrm(ks[7], (DEPTH, N_POOL_GROUPS, POOL_GROUP, POOL_GROUP), POOL_GROUP ** -0.5),
        "pool_scale": ones_n(ks[8], (DEPTH, D_POOL)),
        "conv_dw": nrm(ks[9], (DEPTH, CONV_WIDTH, D_CONV), CONV_WIDTH ** -0.5),
        "conv_db": nrm(ks[10], (DEPTH, D_CONV), 0.02),
        "conv_ln_g": ones_n(ks[11], (DEPTH, D_CONV)),
        "conv_ln_b": nrm(ks[12], (DEPTH, D_CONV), 0.02),
        "conv_pw": nrm(ks[13], (DEPTH, D_CONV, D_CONV), D_CONV ** -0.5),
        "ret_gn_g": ones_n(ks[14], (DEPTH, D_RET)),
        "w_out": nrm(ks[15], (DEPTH, D_MIX, D_MODEL), DEEPNORM_BETA * D_MIX ** -0.5),
        "ffn2_w13": nrm(ks[16], (DEPTH, D_MODEL, 2 * D_FF), D_MODEL ** -0.5),
        "ffn2_w2": nrm(ks[17], (DEPTH, D_FF, D_MODEL), DEEPNORM_BETA * D_FF ** -0.5),
        "ln_g": ones_n(ks[18], (DEPTH, 3, D_MODEL)),
        "ln_b": nrm(ks[19], (DEPTH, 3, D_MODEL), 0.02),
    }


def _fwd_reference(x, meta, ln_in_g, ln_in_b, ffn1_w13, ffn1_w2, w_in, pool_w, pool_scale,
              conv_dw, conv_db, conv_ln_g, conv_ln_b, conv_pw, ret_gn_g, w_out,
              ffn2_w13, ffn2_w2, ln_g, ln_b):
    B = x.shape[0]
    h = jnp.concatenate([jnp.broadcast_to(meta[None].astype(x.dtype), (B, N_META, D_MODEL)), x], axis=1)
    h = layer_norm(h, ln_in_g, ln_in_b)
    for l in range(DEPTH):
        h = layer_norm(DEEPNORM_ALPHA * h + 0.5 * swiglu_ffn(h, ffn1_w13[l], ffn1_w2[l]), ln_g[l, 0], ln_b[l, 0])
        mix = token_mix(h, w_in[l], pool_w[l], pool_scale[l], conv_dw[l], conv_db[l],
                        conv_ln_g[l], conv_ln_b[l], conv_pw[l], ret_gn_g[l], w_out[l])
        h = layer_norm(DEEPNORM_ALPHA * h + mix, ln_g[l, 1], ln_b[l, 1])
        h = layer_norm(DEEPNORM_ALPHA * h + 0.5 * swiglu_ffn(h, ffn2_w13[l], ffn2_w2[l]), ln_g[l, 2], ln_b[l, 2])
    return h[:, N_META:]


import jax as _jax
import jax.numpy as _jnp

TWIN_FORMAT = 'train_step'
FWD_PARAMS = ['x', 'meta', 'ln_in_g', 'ln_in_b', 'ffn1_w13', 'ffn1_w2', 'w_in', 'pool_w', 'pool_scale', 'conv_dw', 'conv_db', 'conv_ln_g', 'conv_ln_b', 'conv_pw', 'ret_gn_g', 'w_out', 'ffn2_w13', 'ffn2_w2', 'ln_g', 'ln_b']
TWIN_WEIGHTS = ['meta', 'ln_in_g', 'ln_in_b', 'ffn1_w13', 'ffn1_w2', 'w_in', 'pool_w', 'pool_scale', 'conv_dw', 'conv_db', 'conv_ln_g', 'conv_ln_b', 'conv_pw', 'ret_gn_g', 'w_out', 'ffn2_w13', 'ffn2_w2', 'ln_g', 'ln_b']
TWIN_DIFF_INPUT = 'x'
TWIN_INPUTS = ['x', 'meta', 'ln_in_g', 'ln_in_b', 'ffn1_w13', 'ffn1_w2', 'w_in', 'pool_w', 'pool_scale', 'conv_dw', 'conv_db', 'conv_ln_g', 'conv_ln_b', 'conv_pw', 'ret_gn_g', 'w_out', 'ffn2_w13', 'ffn2_w2', 'ln_g', 'ln_b', 'loss_target', 'm_meta', 'm_ln_in_g', 'm_ln_in_b', 'm_ffn1_w13', 'm_ffn1_w2', 'm_w_in', 'm_pool_w', 'm_pool_scale', 'm_conv_dw', 'm_conv_db', 'm_conv_ln_g', 'm_conv_ln_b', 'm_conv_pw', 'm_ret_gn_g', 'm_w_out', 'm_ffn2_w13', 'm_ffn2_w2', 'm_ln_g', 'm_ln_b', 'v_meta', 'v_ln_in_g', 'v_ln_in_b', 'v_ffn1_w13', 'v_ffn1_w2', 'v_w_in', 'v_pool_w', 'v_pool_scale', 'v_conv_dw', 'v_conv_db', 'v_conv_ln_g', 'v_conv_ln_b', 'v_conv_pw', 'v_ret_gn_g', 'v_w_out', 'v_ffn2_w13', 'v_ffn2_w2', 'v_ln_g', 'v_ln_b']
TWIN_OUTPUTS = ['loss', 'grad_x', 'grad_meta', 'grad_ln_in_g', 'grad_ln_in_b', 'grad_ffn1_w13', 'grad_ffn1_w2', 'grad_w_in', 'grad_pool_w', 'grad_pool_scale', 'grad_conv_dw', 'grad_conv_db', 'grad_conv_ln_g', 'grad_conv_ln_b', 'grad_conv_pw', 'grad_ret_gn_g', 'grad_w_out', 'grad_ffn2_w13', 'grad_ffn2_w2', 'grad_ln_g', 'grad_ln_b', 'delta_meta', 'delta_ln_in_g', 'delta_ln_in_b', 'delta_ffn1_w13', 'delta_ffn1_w2', 'delta_w_in', 'delta_pool_w', 'delta_pool_scale', 'delta_conv_dw', 'delta_conv_db', 'delta_conv_ln_g', 'delta_conv_ln_b', 'delta_conv_pw', 'delta_ret_gn_g', 'delta_w_out', 'delta_ffn2_w13', 'delta_ffn2_w2', 'delta_ln_g', 'delta_ln_b', 'new_m_meta', 'new_m_ln_in_g', 'new_m_ln_in_b', 'new_m_ffn1_w13', 'new_m_ffn1_w2', 'new_m_w_in', 'new_m_pool_w', 'new_m_pool_scale', 'new_m_conv_dw', 'new_m_conv_db', 'new_m_conv_ln_g', 'new_m_conv_ln_b', 'new_m_conv_pw', 'new_m_ret_gn_g', 'new_m_w_out', 'new_m_ffn2_w13', 'new_m_ffn2_w2', 'new_m_ln_g', 'new_m_ln_b', 'new_v_meta', 'new_v_ln_in_g', 'new_v_ln_in_b', 'new_v_ffn1_w13', 'new_v_ffn1_w2', 'new_v_w_in', 'new_v_pool_w', 'new_v_pool_scale', 'new_v_conv_dw', 'new_v_conv_db', 'new_v_conv_ln_g', 'new_v_conv_ln_b', 'new_v_conv_pw', 'new_v_ret_gn_g', 'new_v_w_out', 'new_v_ffn2_w13', 'new_v_ffn2_w2', 'new_v_ln_g', 'new_v_ln_b']
TWIN_LEAF_KINDS = {'loss': 'loss', 'grad_x': 'grad_x', 'grad_meta': 'grad_w', 'grad_ln_in_g': 'grad_w', 'grad_ln_in_b': 'grad_w', 'grad_ffn1_w13': 'grad_w', 'grad_ffn1_w2': 'grad_w', 'grad_w_in': 'grad_w', 'grad_pool_w': 'grad_w', 'grad_pool_scale': 'grad_w', 'grad_conv_dw': 'grad_w', 'grad_conv_db': 'grad_w', 'grad_conv_ln_g': 'grad_w', 'grad_conv_ln_b': 'grad_w', 'grad_conv_pw': 'grad_w', 'grad_ret_gn_g': 'grad_w', 'grad_w_out': 'grad_w', 'grad_ffn2_w13': 'grad_w', 'grad_ffn2_w2': 'grad_w', 'grad_ln_g': 'grad_w', 'grad_ln_b': 'grad_w', 'delta_meta': 'delta_w', 'delta_ln_in_g': 'delta_w', 'delta_ln_in_b': 'delta_w', 'delta_ffn1_w13': 'delta_w', 'delta_ffn1_w2': 'delta_w', 'delta_w_in': 'delta_w', 'delta_pool_w': 'delta_w', 'delta_pool_scale': 'delta_w', 'delta_conv_dw': 'delta_w', 'delta_conv_db': 'delta_w', 'delta_conv_ln_g': 'delta_w', 'delta_conv_ln_b': 'delta_w', 'delta_conv_pw': 'delta_w', 'delta_ret_gn_g': 'delta_w', 'delta_w_out': 'delta_w', 'delta_ffn2_w13': 'delta_w', 'delta_ffn2_w2': 'delta_w', 'delta_ln_g': 'delta_w', 'delta_ln_b': 'delta_w', 'new_m_meta': 'new_m', 'new_m_ln_in_g': 'new_m', 'new_m_ln_in_b': 'new_m', 'new_m_ffn1_w13': 'new_m', 'new_m_ffn1_w2': 'new_m', 'new_m_w_in': 'new_m', 'new_m_pool_w': 'new_m', 'new_m_pool_scale': 'new_m', 'new_m_conv_dw': 'new_m', 'new_m_conv_db': 'new_m', 'new_m_conv_ln_g': 'new_m', 'new_m_conv_ln_b': 'new_m', 'new_m_conv_pw': 'new_m', 'new_m_ret_gn_g': 'new_m', 'new_m_w_out': 'new_m', 'new_m_ffn2_w13': 'new_m', 'new_m_ffn2_w2': 'new_m', 'new_m_ln_g': 'new_m', 'new_m_ln_b': 'new_m', 'new_v_meta': 'new_v', 'new_v_ln_in_g': 'new_v', 'new_v_ln_in_b': 'new_v', 'new_v_ffn1_w13': 'new_v', 'new_v_ffn1_w2': 'new_v', 'new_v_w_in': 'new_v', 'new_v_pool_w': 'new_v', 'new_v_pool_scale': 'new_v', 'new_v_conv_dw': 'new_v', 'new_v_conv_db': 'new_v', 'new_v_conv_ln_g': 'new_v', 'new_v_conv_ln_b': 'new_v', 'new_v_conv_pw': 'new_v', 'new_v_ret_gn_g': 'new_v', 'new_v_w_out': 'new_v', 'new_v_ffn2_w13': 'new_v', 'new_v_ffn2_w2': 'new_v', 'new_v_ln_g': 'new_v', 'new_v_ln_b': 'new_v'}


def _forward(args):
    return _fwd_reference(*[args[k] for k in FWD_PARAMS])


def _output_shape():
    def fwd():
        inp = _fwd_setup_inputs(0)
        return _fwd_reference(*[inp[k] for k in FWD_PARAMS])
    out = _jax.eval_shape(fwd)
    return out.shape, out.dtype

N_MICROBATCH = 1
ADAM_LR = 0.001
ADAM_B1 = 0.9
ADAM_B2 = 0.999
ADAM_EPS = 1e-08
ADAM_WD = 0.01
ADAM_STEP = 10
PER_EXAMPLE_BATCH_AXIS = {'x': 0, 'loss_target': 0}
SHARED_INPUTS = []
_WEIGHT_DTYPES = {'meta': _jnp.float32, 'ln_in_g': _jnp.float32, 'ln_in_b': _jnp.float32, 'ffn1_w13': _jnp.float32, 'ffn1_w2': _jnp.float32, 'w_in': _jnp.float32, 'pool_w': _jnp.float32, 'pool_scale': _jnp.float32, 'conv_dw': _jnp.float32, 'conv_db': _jnp.float32, 'conv_ln_g': _jnp.float32, 'conv_ln_b': _jnp.float32, 'conv_pw': _jnp.float32, 'ret_gn_g': _jnp.float32, 'w_out': _jnp.float32, 'ffn2_w13': _jnp.float32, 'ffn2_w2': _jnp.float32, 'ln_g': _jnp.float32, 'ln_b': _jnp.float32}
MOMENT_SCALE = {'meta': 4.920496e-03, 'ln_in_g': 5.202072e+00, 'ln_in_b': 9.721493e-01, 'ffn1_w13': 1.746946e-02, 'ffn1_w2': 5.660636e-02, 'w_in': 5.537476e-02, 'pool_w': 8.313558e-02, 'pool_scale': 7.952892e-02, 'conv_dw': 5.760968e-02, 'conv_db': 3.250304e-01, 'conv_ln_g': 1.118537e-01, 'conv_ln_b': 1.754227e-01, 'conv_pw': 7.823217e-02, 'ret_gn_g': 5.546360e-02, 'w_out': 1.339340e-01, 'ffn2_w13': 1.684356e-02, 'ffn2_w2': 5.459309e-02, 'ln_g': 2.731847e+01, 'ln_b': 1.881064e+00}


def _to_microbatches(a, axis):
    t = _jnp.moveaxis(a, axis, 0)
    t = t.reshape((N_MICROBATCH, t.shape[0] // N_MICROBATCH) + t.shape[1:])
    return _jnp.moveaxis(t, 1, axis + 1)


def setup_inputs(seed: int = 0) -> dict:
    inp = _fwd_setup_inputs(seed)
    key = _jax.random.fold_in(_jax.random.key(seed), 7919)
    shape, _ = _output_shape()
    out = dict(inp)
    out["loss_target"] = _jax.random.normal(_jax.random.fold_in(key, 0), shape, _jnp.float32)
    for i, name in enumerate(TWIN_WEIGHTS):
        w = inp[name].astype(_jnp.float32)
        if MOMENT_SCALE is None:
            s = _jnp.sqrt(_jnp.mean(_jnp.square(w)) + 1e-30)
        else:
            s = MOMENT_SCALE[name]
        km, kv = _jax.random.split(_jax.random.fold_in(key, i + 1))
        out[name] = w
        out["m_" + name] = s * _jax.random.normal(km, w.shape, _jnp.float32)
        out["v_" + name] = (s * s) * _jax.random.uniform(kv, w.shape, _jnp.float32, 0.5, 1.5)
    if N_MICROBATCH > 1:
        for name, axis in PER_EXAMPLE_BATCH_AXIS.items():
            out[name] = _to_microbatches(out[name], axis)
    return {'x': out['x'], 'meta': out['meta'], 'ln_in_g': out['ln_in_g'], 'ln_in_b': out['ln_in_b'], 'ffn1_w13': out['ffn1_w13'], 'ffn1_w2': out['ffn1_w2'], 'w_in': out['w_in'], 'pool_w': out['pool_w'], 'pool_scale': out['pool_scale'], 'conv_dw': out['conv_dw'], 'conv_db': out['conv_db'], 'conv_ln_g': out['conv_ln_g'], 'conv_ln_b': out['conv_ln_b'], 'conv_pw': out['conv_pw'], 'ret_gn_g': out['ret_gn_g'], 'w_out': out['w_out'], 'ffn2_w13': out['ffn2_w13'], 'ffn2_w2': out['ffn2_w2'], 'ln_g': out['ln_g'], 'ln_b': out['ln_b'], 'loss_target': out['loss_target'], 'm_meta': out['m_meta'], 'm_ln_in_g': out['m_ln_in_g'], 'm_ln_in_b': out['m_ln_in_b'], 'm_ffn1_w13': out['m_ffn1_w13'], 'm_ffn1_w2': out['m_ffn1_w2'], 'm_w_in': out['m_w_in'], 'm_pool_w': out['m_pool_w'], 'm_pool_scale': out['m_pool_scale'], 'm_conv_dw': out['m_conv_dw'], 'm_conv_db': out['m_conv_db'], 'm_conv_ln_g': out['m_conv_ln_g'], 'm_conv_ln_b': out['m_conv_ln_b'], 'm_conv_pw': out['m_conv_pw'], 'm_ret_gn_g': out['m_ret_gn_g'], 'm_w_out': out['m_w_out'], 'm_ffn2_w13': out['m_ffn2_w13'], 'm_ffn2_w2': out['m_ffn2_w2'], 'm_ln_g': out['m_ln_g'], 'm_ln_b': out['m_ln_b'], 'v_meta': out['v_meta'], 'v_ln_in_g': out['v_ln_in_g'], 'v_ln_in_b': out['v_ln_in_b'], 'v_ffn1_w13': out['v_ffn1_w13'], 'v_ffn1_w2': out['v_ffn1_w2'], 'v_w_in': out['v_w_in'], 'v_pool_w': out['v_pool_w'], 'v_pool_scale': out['v_pool_scale'], 'v_conv_dw': out['v_conv_dw'], 'v_conv_db': out['v_conv_db'], 'v_conv_ln_g': out['v_conv_ln_g'], 'v_conv_ln_b': out['v_conv_ln_b'], 'v_conv_pw': out['v_conv_pw'], 'v_ret_gn_g': out['v_ret_gn_g'], 'v_w_out': out['v_w_out'], 'v_ffn2_w13': out['v_ffn2_w13'], 'v_ffn2_w2': out['v_ffn2_w2'], 'v_ln_g': out['v_ln_g'], 'v_ln_b': out['v_ln_b']}


def _loss(weights, diff, rest, loss_target):
    with _jax.named_scope("forward"):
        args = {**rest, TWIN_DIFF_INPUT: diff, **{k: w.astype(_WEIGHT_DTYPES[k]) for k, w in weights.items()}}
        y = _forward(args)
    with _jax.named_scope("loss_head"):
        err = _jnp.square(y.astype(_jnp.float32) - loss_target)
        return 0.5 * _jnp.sum(_jnp.mean(err, axis=-1)) if err.ndim else 0.5 * err


def _adamw(w, g, m, v):
    m = ADAM_B1 * m + (1.0 - ADAM_B1) * g
    v = ADAM_B2 * v + (1.0 - ADAM_B2) * _jnp.square(g)
    m_hat = m / (1.0 - ADAM_B1 ** ADAM_STEP)
    v_hat = v / (1.0 - ADAM_B2 ** ADAM_STEP)
    delta = -ADAM_LR * (m_hat / (_jnp.sqrt(v_hat) + ADAM_EPS) + ADAM_WD * w)
    return delta, m, v


def reference(x, meta, ln_in_g, ln_in_b, ffn1_w13, ffn1_w2, w_in, pool_w, pool_scale, conv_dw, conv_db, conv_ln_g, conv_ln_b, conv_pw, ret_gn_g, w_out, ffn2_w13, ffn2_w2, ln_g, ln_b, loss_target, m_meta, m_ln_in_g, m_ln_in_b, m_ffn1_w13, m_ffn1_w2, m_w_in, m_pool_w, m_pool_scale, m_conv_dw, m_conv_db, m_conv_ln_g, m_conv_ln_b, m_conv_pw, m_ret_gn_g, m_w_out, m_ffn2_w13, m_ffn2_w2, m_ln_g, m_ln_b, v_meta, v_ln_in_g, v_ln_in_b, v_ffn1_w13, v_ffn1_w2, v_w_in, v_pool_w, v_pool_scale, v_conv_dw, v_conv_db, v_conv_ln_g, v_conv_ln_b, v_conv_pw, v_ret_gn_g, v_w_out, v_ffn2_w13, v_ffn2_w2, v_ln_g, v_ln_b):
    given = dict(x=x, meta=meta, ln_in_g=ln_in_g, ln_in_b=ln_in_b, ffn1_w13=ffn1_w13, ffn1_w2=ffn1_w2, w_in=w_in, pool_w=pool_w, pool_scale=pool_scale, conv_dw=conv_dw, conv_db=conv_db, conv_ln_g=conv_ln_g, conv_ln_b=conv_ln_b, conv_pw=conv_pw, ret_gn_g=ret_gn_g, w_out=w_out, ffn2_w13=ffn2_w13, ffn2_w2=ffn2_w2, ln_g=ln_g, ln_b=ln_b, loss_target=loss_target, m_meta=m_meta, m_ln_in_g=m_ln_in_g, m_ln_in_b=m_ln_in_b, m_ffn1_w13=m_ffn1_w13, m_ffn1_w2=m_ffn1_w2, m_w_in=m_w_in, m_pool_w=m_pool_w, m_pool_scale=m_pool_scale, m_conv_dw=m_conv_dw, m_conv_db=m_conv_db, m_conv_ln_g=m_conv_ln_g, m_conv_ln_b=m_conv_ln_b, m_conv_pw=m_conv_pw, m_ret_gn_g=m_ret_gn_g, m_w_out=m_w_out, m_ffn2_w13=m_ffn2_w13, m_ffn2_w2=m_ffn2_w2, m_ln_g=m_ln_g, m_ln_b=m_ln_b, v_meta=v_meta, v_ln_in_g=v_ln_in_g, v_ln_in_b=v_ln_in_b, v_ffn1_w13=v_ffn1_w13, v_ffn1_w2=v_ffn1_w2, v_w_in=v_w_in, v_pool_w=v_pool_w, v_pool_scale=v_pool_scale, v_conv_dw=v_conv_dw, v_conv_db=v_conv_db, v_conv_ln_g=v_conv_ln_g, v_conv_ln_b=v_conv_ln_b, v_conv_pw=v_conv_pw, v_ret_gn_g=v_ret_gn_g, v_w_out=v_w_out, v_ffn2_w13=v_ffn2_w13, v_ffn2_w2=v_ffn2_w2, v_ln_g=v_ln_g, v_ln_b=v_ln_b)
    weights = {n: given[n] for n in TWIN_WEIGHTS}
    shared = {n: given[n] for n in SHARED_INPUTS}
    per_example = {n: given[n] for n in ['x']}
    grad_fn = _jax.value_and_grad(_loss, argnums=(0, 1))

    def one_microbatch(ex, loss_target):
        ex = dict(ex)
        diff = ex.pop(TWIN_DIFF_INPUT)
        return grad_fn(weights, diff, {**shared, **ex}, loss_target)

    if N_MICROBATCH == 1:
        loss, (grad_w, grad_x) = one_microbatch(per_example, given["loss_target"])
    else:
        def body(carry, xs):
            loss_sum, grad_sum = carry
            l_k, (gw_k, gx_k) = one_microbatch(xs[0], xs[1])
            with _jax.named_scope("update"):
                return (loss_sum + l_k, _jax.tree.map(_jnp.add, grad_sum, gw_k)), gx_k

        init = (_jnp.zeros((), _jnp.float32), _jax.tree.map(_jnp.zeros_like, weights))
        (loss, grad_w), grad_x = _jax.lax.scan(body, init, (per_example, given["loss_target"]))
    with _jax.named_scope("update"):
        delta_w, new_m, new_v = {}, {}, {}
        for n in TWIN_WEIGHTS:
            delta_w[n], new_m[n], new_v[n] = _adamw(weights[n], grad_w[n], given["m_" + n], given["v_" + n])
    return (loss, grad_x, *[grad_w[n] for n in TWIN_WEIGHTS], *[delta_w[n] for n in TWIN_WEIGHTS],
            *[new_m[n] for n in TWIN_WEIGHTS], *[new_v[n] for n in TWIN_WEIGHTS])
```

```python
import functools
import math

import jax
import jax.numpy as jnp
from jax import lax
from jax.experimental import pallas as pl
from jax.experimental.pallas import tpu as pltpu

F32 = jnp.float32
BF16 = jnp.bfloat16

N_DEV = 8
MESH_AXES = ("x", "y", "c")
CHUNK = 64
N_META = 16
PAD_ROWS = 240
FRONT = PAD_ROWS + N_META
LN_EPS = 1e-5
LANE = 128
POOL_WINDOWS = (2, 4, 8, 16)
CONV_WIDTH = 31
CONV_HALO = 32
POOL_HALO = 16
RET_HEADS = 4
ROPE_BASE = 10000.0

ADAM_LR = 0.001
ADAM_B1 = 0.9
ADAM_B2 = 0.999
ADAM_EPS = 1e-08
ADAM_WD = 0.01
ADAM_STEP = 10

VMEM_LIMIT = 56 * 1024 * 1024
DENSE_TM_FWD = 528
DENSE_TM_BWD = 352
MIX_TM = 768
LANE_SUB = 768


def _cparams(sem):
    return pltpu.CompilerParams(dimension_semantics=sem, vmem_limit_bytes=VMEM_LIMIT)


def _pick_tm(lp, target, mult=16):
    best = None
    for t in range(mult, min(lp, target) + 1, mult):
        if lp % t == 0:
            best = t
    assert best is not None, (lp, target, mult)
    return best


def _subchunks(n, width):
    out, o = [], 0
    while o < n:
        w = min(width, n - o)
        out.append((o, w))
        o += w
    return out


def _dot(a, b):
    return jnp.dot(a, b, preferred_element_type=F32)


def _dot_nt(a, b):
    return lax.dot_general(a, b, (((1,), (1,)), ((), ())), preferred_element_type=F32)


def _dot_tn(a, b):
    return lax.dot_general(a, b, (((0,), (0,)), ((), ())), preferred_element_type=F32)


def _sigmoid(x):
    return 1.0 / (1.0 + jnp.exp(-x))


def _ln_stats(r):
    mu = jnp.mean(r, axis=-1, keepdims=True)
    xc = r - mu
    var = jnp.mean(xc * xc, axis=-1, keepdims=True)
    rstd = lax.rsqrt(var + LN_EPS)
    return xc * rstd, rstd


def _ln_bwd(dh, xhat, rstd, g):
    dxh = dh * g
    m1 = jnp.mean(dxh, axis=-1, keepdims=True)
    m2 = jnp.mean(dxh * xhat, axis=-1, keepdims=True)
    return rstd * (dxh - m1 - xhat * m2)


def _colsum(x):
    return jnp.sum(x, axis=0, keepdims=True)


def _row_ids(tile, tm):
    return lax.broadcasted_iota(jnp.int32, (tm, 1), 0) + tile * tm


def _ffn_fwd(r_prev, g, b, w1, w3, w2, alpha):
    lp, d = r_prev.shape
    fp = w1.shape[1]
    nf = 2
    fc = fp // nf
    tm = _pick_tm(lp, DENSE_TM_FWD)
    subs = _subchunks(fc, LANE_SUB)

    def body(r_ref, g_ref, b_ref, w1_ref, w3_ref, w2_ref, out_ref, a_ref, u_ref, hb, acc):
        f = pl.program_id(1)

        @pl.when(f == 0)
        def _():
            xhat, _ = _ln_stats(r_ref[...])
            hb[...] = (xhat * g_ref[...] + b_ref[...]).astype(BF16)
            acc[...] = jnp.zeros_like(acc)

        h = hb[...]
        for o, n in subs:
            a = _dot(h, w1_ref[:, o:o + n])
            u = _dot(h, w3_ref[:, o:o + n])
            a_ref[:, o:o + n] = a.astype(BF16)
            u_ref[:, o:o + n] = u.astype(BF16)
            s = a * _sigmoid(a) * u
            acc[...] += _dot(s.astype(BF16), w2_ref[o:o + n, :])

        @pl.when(f == nf - 1)
        def _():
            xhat, _ = _ln_stats(r_ref[...])
            out_ref[...] = alpha * (xhat * g_ref[...] + b_ref[...]) + 0.5 * acc[...]

    return pl.pallas_call(
        body,
        grid=(lp // tm, nf),
        in_specs=[
            pl.BlockSpec((tm, d), lambda i, f: (i, 0)),
            pl.BlockSpec((1, d), lambda i, f: (0, 0)),
            pl.BlockSpec((1, d), lambda i, f: (0, 0)),
            pl.BlockSpec((d, fc), lambda i, f: (0, f)),
            pl.BlockSpec((d, fc), lambda i, f: (0, f)),
            pl.BlockSpec((fc, d), lambda i, f: (f, 0)),
        ],
        out_specs=[
            pl.BlockSpec((tm, d), lambda i, f: (i, 0)),
            pl.BlockSpec((tm, fc), lambda i, f: (i, f)),
            pl.BlockSpec((tm, fc), lambda i, f: (i, f)),
        ],
        out_shape=[
            jax.ShapeDtypeStruct((lp, d), F32),
            jax.ShapeDtypeStruct((lp, fp), BF16),
            jax.ShapeDtypeStruct((lp, fp), BF16),
        ],
        scratch_shapes=[pltpu.VMEM((tm, d), BF16), pltpu.VMEM((tm, d), F32)],
        compiler_params=_cparams(("parallel", "arbitrary")),
        name="ffn_fwd",
    )(r_prev, g, b, w1, w3, w2)


def _ffn_bwd(dr_next, r_prev, a, u, g, b, w1, w3, w2, alpha):
    lp, d = r_prev.shape
    fp = w1.shape[1]
    nf = 2
    fc = fp // nf
    tm = _pick_tm(lp, DENSE_TM_BWD)
    subs = _subchunks(fc, LANE_SUB)

    def body(dr_ref, r_ref, a_ref, u_ref, g_ref, b_ref, w1_ref, w3_ref, w2_ref,
             drp_ref, hb_ref, s_ref, da_ref, du_ref, dg_ref, db_ref, dyb, dhacc):
        i = pl.program_id(0)
        f = pl.program_id(1)

        @pl.when(jnp.logical_and(i == 0, f == 0))
        def _():
            dg_ref[...] = jnp.zeros_like(dg_ref)
            db_ref[...] = jnp.zeros_like(db_ref)

        @pl.when(f == 0)
        def _():
            dyb[...] = (0.5 * dr_ref[...]).astype(BF16)
            dhacc[...] = jnp.zeros_like(dhacc)
            xhat, _ = _ln_stats(r_ref[...])
            hb_ref[...] = (xhat * g_ref[...] + b_ref[...]).astype(BF16)

        dy = dyb[...]
        for o, n in subs:
            ds = _dot_nt(dy, w2_ref[o:o + n, :])
            av = a_ref[:, o:o + n].astype(F32)
            uv = u_ref[:, o:o + n].astype(F32)
            sig = _sigmoid(av)
            sl = av * sig
            da = (ds * uv * (sig * (1.0 + av * (1.0 - sig)))).astype(BF16)
            du = (ds * sl).astype(BF16)
            s_ref[:, o:o + n] = (sl * uv).astype(BF16)
            da_ref[:, o:o + n] = da
            du_ref[:, o:o + n] = du
            dhacc[...] += _dot_nt(da, w1_ref[:, o:o + n]) + _dot_nt(du, w3_ref[:, o:o + n])

        @pl.when(f == nf - 1)
        def _():
            dh = alpha * dr_ref[...] + dhacc[...]
            xhat, rstd = _ln_stats(r_ref[...])
            drp_ref[...] = _ln_bwd(dh, xhat, rstd, g_ref[...])
            dg_ref[...] += _colsum(dh * xhat)
            db_ref[...] += _colsum(dh)

    row = lambda i, f: (i, 0)
    const = lambda i, f: (0, 0)
    return pl.pallas_call(
        body,
        grid=(lp // tm, nf),
        in_specs=[
            pl.BlockSpec((tm, d), row),
            pl.BlockSpec((tm, d), row),
            pl.BlockSpec((tm, fc), lambda i, f: (i, f)),
            pl.BlockSpec((tm, fc), lambda i, f: (i, f)),
            pl.BlockSpec((1, d), const),
            pl.BlockSpec((1, d), const),
            pl.BlockSpec((d, fc), lambda i, f: (0, f)),
            pl.BlockSpec((d, fc), lambda i, f: (0, f)),
            pl.BlockSpec((fc, d), lambda i, f: (f, 0)),
        ],
        out_specs=[
            pl.BlockSpec((tm, d), row),
            pl.BlockSpec((tm, d), row),
            pl.BlockSpec((tm, fc), lambda i, f: (i, f)),
            pl.BlockSpec((tm, fc), lambda i, f: (i, f)),
            pl.BlockSpec((tm, fc), lambda i, f: (i, f)),
            pl.BlockSpec((1, d), const),
            pl.BlockSpec((1, d), const),
        ],
        out_shape=[
            jax.ShapeDtypeStruct((lp, d), F32),
            jax.ShapeDtypeStruct((lp, d), BF16),
            jax.ShapeDtypeStruct((lp, fp), BF16),
            jax.ShapeDtypeStruct((lp, fp), BF16),
            jax.ShapeDtypeStruct((lp, fp), BF16),
            jax.ShapeDtypeStruct((1, d), F32),
            jax.ShapeDtypeStruct((1, d), F32),
        ],
        scratch_shapes=[pltpu.VMEM((tm, d), BF16), pltpu.VMEM((tm, d), F32)],
        compiler_params=_cparams(("arbitrary", "arbitrary")),
        name="ffn_bwd",
    )(dr_next, r_prev, a, u, g, b, w1, w3, w2)


def _mm_tn(a, b, scale=1.0, bn_target=1408, tk_target=768):
    t, m = a.shape
    n = b.shape[1]
    bn = n if n <= bn_target else _pick_tm(n, bn_target, LANE)
    bm = m if m <= 1408 else _pick_tm(m, 1408, LANE)
    tk = _pick_tm(t, tk_target)
    nt = t // tk

    def body(a_ref, b_ref, o_ref):
        k = pl.program_id(2)

        @pl.when(k == 0)
        def _():
            o_ref[...] = jnp.zeros_like(o_ref)

        o_ref[...] += _dot_tn(a_ref[...].astype(BF16), b_ref[...].astype(BF16))

        if scale != 1.0:
            @pl.when(k == nt - 1)
            def _():
                o_ref[...] = o_ref[...] * scale

    return pl.pallas_call(
        body,
        grid=(m // bm, n // bn, nt),
        in_specs=[
            pl.BlockSpec((tk, bm), lambda i, j, k: (k, i)),
            pl.BlockSpec((tk, bn), lambda i, j, k: (k, j)),
        ],
        out_specs=pl.BlockSpec((bm, bn), lambda i, j, k: (i, j)),
        out_shape=jax.ShapeDtypeStruct((m, n), F32),
        compiler_params=_cparams(("parallel", "parallel", "arbitrary")),
        name="mm_tn",
    )(a, b)


def _proj_fwd(r_prev, g, b, w_in):
    lp, d = r_prev.shape
    n = w_in.shape[1]
    tm = _pick_tm(lp, DENSE_TM_FWD)

    def body(r_ref, g_ref, b_ref, w_ref, z_ref):
        xhat, _ = _ln_stats(r_ref[...])
        h = (xhat * g_ref[...] + b_ref[...]).astype(BF16)
        z = _dot(h, w_ref[...])
        rows = _row_ids(pl.program_id(0), tm)
        z_ref[...] = jnp.where(rows >= PAD_ROWS, z, 0.0)

    return pl.pallas_call(
        body,
        grid=(lp // tm,),
        in_specs=[
            pl.BlockSpec((tm, d), lambda i: (i, 0)),
            pl.BlockSpec((1, d), lambda i: (0, 0)),
            pl.BlockSpec((1, d), lambda i: (0, 0)),
            pl.BlockSpec((d, n), lambda i: (0, 0)),
        ],
        out_specs=pl.BlockSpec((tm, n), lambda i: (i, 0)),
        out_shape=jax.ShapeDtypeStruct((lp, n), F32),
        compiler_params=_cparams(("parallel",)),
        name="proj_fwd",
    )(r_prev, g, b, w_in)


def _out_fwd(r_prev, g, b, cat, w_out, alpha):
    lp, d = r_prev.shape
    k = cat.shape[1]
    tm = _pick_tm(lp, DENSE_TM_FWD)

    def body(r_ref, g_ref, b_ref, c_ref, w_ref, o_ref):
        xhat, _ = _ln_stats(r_ref[...])
        o_ref[...] = alpha * (xhat * g_ref[...] + b_ref[...]) + _dot(c_ref[...], w_ref[...])

    return pl.pallas_call(
        body,
        grid=(lp // tm,),
        in_specs=[
            pl.BlockSpec((tm, d), lambda i: (i, 0)),
            pl.BlockSpec((1, d), lambda i: (0, 0)),
            pl.BlockSpec((1, d), lambda i: (0, 0)),
            pl.BlockSpec((tm, k), lambda i: (i, 0)),
            pl.BlockSpec((k, d), lambda i: (0, 0)),
        ],
        out_specs=pl.BlockSpec((tm, d), lambda i: (i, 0)),
        out_shape=jax.ShapeDtypeStruct((lp, d), F32),
        compiler_params=_cparams(("parallel",)),
        name="out_fwd",
    )(r_prev, g, b, cat, w_out)


def _mm_nt(x, w):
    lp, n = x.shape
    k = w.shape[0]
    tm = _pick_tm(lp, DENSE_TM_FWD)

    def body(x_ref, w_ref, o_ref):
        o_ref[...] = _dot_nt(x_ref[...].astype(BF16), w_ref[...])

    return pl.pallas_call(
        body,
        grid=(lp // tm,),
        in_specs=[pl.BlockSpec((tm, n), lambda i: (i, 0)), pl.BlockSpec((k, n), lambda i: (0, 0))],
        out_specs=pl.BlockSpec((tm, k), lambda i: (i, 0)),
        out_shape=jax.ShapeDtypeStruct((lp, k), F32),
        compiler_params=_cparams(("parallel",)),
        name="mm_nt",
    )(x, w)


def _in_bwd(dr_next, r_prev, dz, g, b, w_in, alpha):
    lp, d = r_prev.shape
    n = w_in.shape[1]
    tm = _pick_tm(lp, DENSE_TM_FWD)

    def body(dr_ref, r_ref, dz_ref, g_ref, b_ref, w_ref, drp_ref, hb_ref, dg_ref, db_ref):
        @pl.when(pl.program_id(0) == 0)
        def _():
            dg_ref[...] = jnp.zeros_like(dg_ref)
            db_ref[...] = jnp.zeros_like(db_ref)

        dh = alpha * dr_ref[...] + _dot_nt(dz_ref[...], w_ref[...])
        xhat, rstd = _ln_stats(r_ref[...])
        hb_ref[...] = (xhat * g_ref[...] + b_ref[...]).astype(BF16)
        drp_ref[...] = _ln_bwd(dh, xhat, rstd, g_ref[...])
        dg_ref[...] += _colsum(dh * xhat)
        db_ref[...] += _colsum(dh)

    row = lambda i: (i, 0)
    const = lambda i: (0, 0)
    return pl.pallas_call(
        body,
        grid=(lp // tm,),
        in_specs=[
            pl.BlockSpec((tm, d), row),
            pl.BlockSpec((tm, d), row),
            pl.BlockSpec((tm, n), row),
            pl.BlockSpec((1, d), const),
            pl.BlockSpec((1, d), const),
            pl.BlockSpec((d, n), const),
        ],
        out_specs=[
            pl.BlockSpec((tm, d), row),
            pl.BlockSpec((tm, d), row),
            pl.BlockSpec((1, d), const),
            pl.BlockSpec((1, d), const),
        ],
        out_shape=[
            jax.ShapeDtypeStruct((lp, d), F32),
            jax.ShapeDtypeStruct((lp, d), BF16),
            jax.ShapeDtypeStruct((1, d), F32),
            jax.ShapeDtypeStruct((1, d), F32),
        ],
        compiler_params=_cparams(("arbitrary",)),
        name="in_bwd",
    )(dr_next, r_prev, dz, g, b, w_in)


def _loss_bwd(r_last, target, g, b):
    lp, d = r_last.shape
    tm = _pick_tm(lp, DENSE_TM_FWD)

    def body(r_ref, t_ref, g_ref, b_ref, dr_ref, dg_ref, db_ref, ls_ref):
        i = pl.program_id(0)

        @pl.when(i == 0)
        def _():
            dg_ref[...] = jnp.zeros_like(dg_ref)
            db_ref[...] = jnp.zeros_like(db_ref)
            ls_ref[...] = jnp.zeros_like(ls_ref)

        xhat, rstd = _ln_stats(r_ref[...])
        y = xhat * g_ref[...] + b_ref[...]
        rows = _row_ids(i, tm)
        err = jnp.where(rows >= FRONT, y - t_ref[...], 0.0)
        ls_ref[...] += _colsum(err * err)
        dy = err * (1.0 / d)
        dr_ref[...] = _ln_bwd(dy, xhat, rstd, g_ref[...])
        dg_ref[...] += _colsum(dy * xhat)
        db_ref[...] += _colsum(dy)

    row = lambda i: (i, 0)
    const = lambda i: (0, 0)
    return pl.pallas_call(
        body,
        grid=(lp // tm,),
        in_specs=[
            pl.BlockSpec((tm, d), row),
            pl.BlockSpec((tm, d), row),
            pl.BlockSpec((1, d), const),
            pl.BlockSpec((1, d), const),
        ],
        out_specs=[
            pl.BlockSpec((tm, d), row),
            pl.BlockSpec((1, d), const),
            pl.BlockSpec((1, d), const),
            pl.BlockSpec((1, d), const),
        ],
        out_shape=[
            jax.ShapeDtypeStruct((lp, d), F32),
            jax.ShapeDtypeStruct((1, d), F32),
            jax.ShapeDtypeStruct((1, d), F32),
            jax.ShapeDtypeStruct((1, d), F32),
        ],
        compiler_params=_cparams(("arbitrary",)),
        name="loss_bwd",
    )(r_last, target, g, b)


def _pool_counts(tile, tm, width):
    pos = _row_ids(tile, tm) - PAD_ROWS
    lane = lax.broadcasted_iota(jnp.int32, (1, width), 1)
    group = width // len(POOL_WINDOWS)
    win = jnp.full((1, width), POOL_WINDOWS[-1], jnp.int32)
    for gi in range(len(POOL_WINDOWS) - 2, -1, -1):
        win = jnp.where(lane < (gi + 1) * group, POOL_WINDOWS[gi], win)
    cnt = jnp.clip(pos + 1, 1, win)
    return cnt.astype(F32), lane, group


def _pool_select(sums, lane, group):
    out = sums[-1]
    for gi in range(len(POOL_WINDOWS) - 2, -1, -1):
        out = jnp.where(lane < (gi + 1) * group, sums[gi], out)
    return out


def _pool_window_sums(ext, tm, sign):
    base = POOL_HALO if sign < 0 else 0
    acc = ext[pl.ds(base, tm), :]
    sums, k = [], 1
    for w in POOL_WINDOWS:
        while k < w:
            acc = acc + ext[pl.ds(base + sign * k, tm), :]
            k += 1
        sums.append(acc)
    return sums


def _pool_fwd(z, wbd, scale):
    lp = z.shape[0]
    c = wbd.shape[0]
    tm = _pick_tm(lp, MIX_TM, CHUNK)

    def body(x_ref, w_ref, s_ref, y_ref, ext):
        i = pl.program_id(0)

        @pl.when(i == 0)
        def _():
            ext[0:POOL_HALO, :] = jnp.zeros((POOL_HALO, c), F32)

        x = x_ref[...]
        ext[POOL_HALO:POOL_HALO + tm, :] = x
        cnt, lane, group = _pool_counts(i, tm, c)
        sums = _pool_window_sums(ext, tm, -1)
        y = _pool_select(sums, lane, group) / cnt - x
        ext[0:POOL_HALO, :] = ext[tm:tm + POOL_HALO, :]
        y_ref[...] = (_dot(y.astype(BF16), w_ref[...]) * s_ref[...]).astype(BF16)

    return pl.pallas_call(
        body,
        grid=(lp // tm,),
        in_specs=[
            pl.BlockSpec((tm, c), lambda i: (i, 0)),
            pl.BlockSpec((c, c), lambda i: (0, 0)),
            pl.BlockSpec((1, c), lambda i: (0, 0)),
        ],
        out_specs=pl.BlockSpec((tm, c), lambda i: (i, 0)),
        out_shape=jax.ShapeDtypeStruct((lp, c), BF16),
        scratch_shapes=[pltpu.VMEM((tm + POOL_HALO, c), F32)],
        compiler_params=_cparams(("arbitrary",)),
        name="pool_fwd",
    )(z, wbd, scale)


def _pool_bwd(z, dcat, wbd, scale):
    lp = z.shape[0]
    c = wbd.shape[0]
    tm = _pick_tm(lp, MIX_TM, CHUNK)
    ni = lp // tm
    hpt = tm // POOL_HALO

    def body(x_ref, xh_ref, dy_ref, w_ref, s_ref, dx_ref, dw_ref, ds_ref, ext, ext2):
        i = pl.program_id(0)
        t = ni - 1 - i

        @pl.when(i == 0)
        def _():
            dw_ref[...] = jnp.zeros_like(dw_ref)
            ds_ref[...] = jnp.zeros_like(ds_ref)
            ext2[tm:tm + POOL_HALO, :] = jnp.zeros((POOL_HALO, c), F32)

        x = x_ref[...]
        ext[0:POOL_HALO, :] = jnp.where(t > 0, xh_ref[...], 0.0)
        ext[POOL_HALO:POOL_HALO + tm, :] = x
        cnt, lane, group = _pool_counts(t, tm, c)
        y = (_pool_select(_pool_window_sums(ext, tm, -1), lane, group) / cnt - x).astype(BF16)
        w = w_ref[...]
        dyv = dy_ref[...]
        ds_ref[...] += _colsum(_dot(y, w) * dyv)
        do = (dyv * s_ref[...]).astype(BF16)
        dw_ref[...] += _dot_tn(y, do)
        dyp = _dot_nt(do, w)
        ext2[0:tm, :] = dyp / cnt
        dx = _pool_select(_pool_window_sums(ext2, tm, 1), lane, group) - dyp
        ext2[tm:tm + POOL_HALO, :] = ext2[0:POOL_HALO, :]
        rows = _row_ids(t, tm)
        dx_ref[...] = jnp.where(rows >= PAD_ROWS, dx, 0.0).astype(BF16)

    rev = lambda i: (ni - 1 - i, 0)
    return pl.pallas_call(
        body,
        grid=(ni,),
        in_specs=[
            pl.BlockSpec((tm, c), rev),
            pl.BlockSpec((POOL_HALO, c), lambda i: (jnp.maximum((ni - 1 - i) * hpt - 1, 0), 0)),
            pl.BlockSpec((tm, c), rev),
            pl.BlockSpec((c, c), lambda i: (0, 0)),
            pl.BlockSpec((1, c), lambda i: (0, 0)),
        ],
        out_specs=[
            pl.BlockSpec((tm, c), rev),
            pl.BlockSpec((c, c), lambda i: (0, 0)),
            pl.BlockSpec((1, c), lambda i: (0, 0)),
        ],
        out_shape=[
            jax.ShapeDtypeStruct((lp, c), BF16),
            jax.ShapeDtypeStruct((c, c), F32),
            jax.ShapeDtypeStruct((1, c), F32),
        ],
        scratch_shapes=[pltpu.VMEM((tm + POOL_HALO, c), F32), pltpu.VMEM((tm + POOL_HALO, c), F32)],
        compiler_params=_cparams(("arbitrary",)),
        name="pool_bwd",
    )(z, z, dcat, wbd, scale)


def _conv_taps(ext, w_ref, tm, first):
    acc = w_ref[0:1, :] * ext[pl.ds(first, tm), :]
    for k in range(1, CONV_WIDTH):
        acc = acc + w_ref[k:k + 1, :] * ext[pl.ds(first + k, tm), :]
    return acc


def _conv_fwd(z, w_dw, b_dw, ln_g, ln_b, w_pw):
    lp = z.shape[0]
    c = w_pw.shape[0]
    tm = _pick_tm(lp, MIX_TM, CHUNK)

    def body(a_ref, gt_ref, w_ref, bd_ref, g_ref, b_ref, pw_ref, y_ref, ext):
        i = pl.program_id(0)

        @pl.when(i == 0)
        def _():
            ext[0:CONV_HALO, :] = jnp.zeros((CONV_HALO, c), F32)

        ext[CONV_HALO:CONV_HALO + tm, :] = a_ref[...] * _sigmoid(gt_ref[...])
        acc = _conv_taps(ext, w_ref, tm, CONV_HALO - CONV_WIDTH + 1) + bd_ref[...]
        ext[0:CONV_HALO, :] = ext[tm:tm + CONV_HALO, :]
        xhat, _ = _ln_stats(acc)
        n = xhat * g_ref[...] + b_ref[...]
        act = n * _sigmoid(n)
        y_ref[...] = _dot(act.astype(BF16), pw_ref[...]).astype(BF16)

    const = lambda i: (0, 0)
    return pl.pallas_call(
        body,
        grid=(lp // tm,),
        in_specs=[
            pl.BlockSpec((tm, c), lambda i: (i, 1)),
            pl.BlockSpec((tm, c), lambda i: (i, 2)),
            pl.BlockSpec((CONV_HALO, c), const),
            pl.BlockSpec((1, c), const),
            pl.BlockSpec((1, c), const),
            pl.BlockSpec((1, c), const),
            pl.BlockSpec((c, c), const),
        ],
        out_specs=pl.BlockSpec((tm, c), lambda i: (i, 0)),
        out_shape=jax.ShapeDtypeStruct((lp, c), BF16),
        scratch_shapes=[pltpu.VMEM((tm + CONV_HALO, c), F32)],
        compiler_params=_cparams(("arbitrary",)),
        name="conv_fwd",
    )(z, z, w_dw, b_dw, ln_g, ln_b, w_pw)


def _conv_bwd(z, dcat, w_dw, b_dw, ln_g, ln_b, w_pw):
    lp = z.shape[0]
    c = w_pw.shape[0]
    tm = _pick_tm(lp, MIX_TM, CHUNK)
    ni = lp // tm
    hpt = tm // CONV_HALO
    first = CONV_HALO - CONV_WIDTH + 1

    def body(a_ref, gt_ref, ah_ref, gh_ref, dy_ref, w_ref, bd_ref, g_ref, b_ref, pw_ref,
             dca_ref, dcg_ref, dw_ref, dbd_ref, dg_ref, db_ref, dpw_ref, extu, extd):
        i = pl.program_id(0)
        t = ni - 1 - i

        @pl.when(i == 0)
        def _():
            dw_ref[...] = jnp.zeros_like(dw_ref)
            dbd_ref[...] = jnp.zeros_like(dbd_ref)
            dg_ref[...] = jnp.zeros_like(dg_ref)
            db_ref[...] = jnp.zeros_like(db_ref)
            dpw_ref[...] = jnp.zeros_like(dpw_ref)
            extd[tm:tm + CONV_HALO, :] = jnp.zeros((CONV_HALO, c), F32)

        ca = a_ref[...]
        sg = _sigmoid(gt_ref[...])
        extu[0:CONV_HALO, :] = jnp.where(t > 0, ah_ref[...] * _sigmoid(gh_ref[...]), 0.0)
        extu[CONV_HALO:CONV_HALO + tm, :] = ca * sg
        acc = _conv_taps(extu, w_ref, tm, first) + bd_ref[...]
        xhat, rstd = _ln_stats(acc)
        gam = g_ref[...]
        n = xhat * gam + b_ref[...]
        sn = _sigmoid(n)
        act = (n * sn).astype(BF16)
        do = dy_ref[...].astype(BF16)
        dpw_ref[...] += _dot_tn(act, do)
        dn = _dot_nt(do, pw_ref[...]) * (sn * (1.0 + n * (1.0 - sn)))
        dg_ref[...] += _colsum(dn * xhat)
        db_ref[...] += _colsum(dn)
        dyc = _ln_bwd(dn, xhat, rstd, gam)
        dbd_ref[...] += _colsum(dyc)
        extd[0:tm, :] = dyc
        du = None
        for k in range(CONV_WIDTH):
            dw_ref[k:k + 1, :] += _colsum(dyc * extu[pl.ds(first + k, tm), :])
            term = w_ref[k:k + 1, :] * extd[pl.ds(CONV_WIDTH - 1 - k, tm), :]
            du = term if du is None else du + term
        extd[tm:tm + CONV_HALO, :] = extd[0:CONV_HALO, :]
        du = jnp.where(_row_ids(t, tm) >= PAD_ROWS, du, 0.0)
        dca_ref[...] = (du * sg).astype(BF16)
        dcg_ref[...] = (du * ca * sg * (1.0 - sg)).astype(BF16)

    const = lambda i: (0, 0)
    rev = lambda col: (lambda i: (ni - 1 - i, col))
    halo = lambda col: (lambda i: (jnp.maximum((ni - 1 - i) * hpt - 1, 0), col))
    return pl.pallas_call(
        body,
        grid=(ni,),
        in_specs=[
            pl.BlockSpec((tm, c), rev(1)),
            pl.BlockSpec((tm, c), rev(2)),
            pl.BlockSpec((CONV_HALO, c), halo(1)),
            pl.BlockSpec((CONV_HALO, c), halo(2)),
            pl.BlockSpec((tm, c), rev(1)),
            pl.BlockSpec((CONV_HALO, c), const),
            pl.BlockSpec((1, c), const),
            pl.BlockSpec((1, c), const),
            pl.BlockSpec((1, c), const),
            pl.BlockSpec((c, c), const),
        ],
        out_specs=[
            pl.BlockSpec((tm, c), rev(0)),
            pl.BlockSpec((tm, c), rev(0)),
            pl.BlockSpec((CONV_HALO, c), const),
            pl.BlockSpec((1, c), const),
            pl.BlockSpec((1, c), const),
            pl.BlockSpec((1, c), const),
            pl.BlockSpec((c, c), const),
        ],
        out_shape=[
            jax.ShapeDtypeStruct((lp, c), BF16),
            jax.ShapeDtypeStruct((lp, c), BF16),
            jax.ShapeDtypeStruct((CONV_HALO, c), F32),
            jax.ShapeDtypeStruct((1, c), F32),
            jax.ShapeDtypeStruct((1, c), F32),
            jax.ShapeDtypeStruct((1, c), F32),
            jax.ShapeDtypeStruct((c, c), F32),
        ],
        scratch_shapes=[pltpu.VMEM((tm + CONV_HALO, c), F32), pltpu.VMEM((tm + CONV_HALO, c), F32)],
        compiler_params=_cparams(("arbitrary",)),
        name="conv_bwd",
    )(z, z, z, z, dcat, w_dw, b_dw, ln_g, ln_b, w_pw)


def _rope(x, cos, sgn_sin):
    return x * cos + pltpu.roll(x, LANE // 2, 1) * sgn_sin


def _rope_t(dy, cos, sgn_sin):
    return dy * cos + pltpu.roll(dy * sgn_sin, LANE // 2, 1)


def _ret_chunk_fwd(q, k, v, cos, sn, state, dmv, qdv, kdv):
    qr = _rope(q, cos, sn)
    kr = _rope(k, cos, sn) * (LANE ** -0.5)
    am = _dot_nt(qr.astype(BF16), kr.astype(BF16)) * dmv
    o = _dot(am.astype(BF16), v.astype(BF16)) + _dot((qr * qdv).astype(BF16), state.astype(BF16))
    return qr, kr, am, o


def _ret_specs(tm, q_blk):
    def mk(off, tile_of):
        return pl.BlockSpec((tm, LANE), lambda h, i: (tile_of(i), off + h))
    return lambda tile_of: [mk(q_blk + j * RET_HEADS, tile_of) for j in range(4)]


def _ret_fwd(z, rope_c, rope_s, dm, qd, kd, cd, gn, q_blk):
    lp = z.shape[0]
    tm = _pick_tm(lp, MIX_TM, CHUNK)
    nc = tm // CHUNK
    ni = lp // tm

    def body(q_ref, k_ref, v_ref, g_ref, c_ref, s_ref, dm_ref, qd_ref, kd_ref, cd_ref, gn_ref,
             y_ref, st_ref, state):
        @pl.when(pl.program_id(1) == 0)
        def _():
            state[...] = jnp.zeros_like(state)

        dmv, qdv, kdv, cdv, gnv = dm_ref[0], qd_ref[0], kd_ref[0], cd_ref[0, 0:1, :], gn_ref[...]

        def chunk(c, carry):
            rows = pl.ds(pl.multiple_of(c * CHUNK, CHUNK), CHUNK)
            v = v_ref[rows, :]
            s0 = state[...]
            st_ref[c, 0, :, :] = s0
            _, kr, _, o = _ret_chunk_fwd(q_ref[rows, :], k_ref[rows, :], v, c_ref[rows, :], s_ref[rows, :],
                                         s0, dmv, qdv, kdv)
            state[...] = s0 * cdv + _dot_tn((kr * kdv).astype(BF16), v.astype(BF16))
            on, _ = _ln_stats(o)
            gg = g_ref[rows, :]
            y_ref[rows, :] = (gg * _sigmoid(gg) * (on * gnv)).astype(BF16)
            return carry

        lax.fori_loop(0, nc, chunk, 0)

    tile = lambda i: i
    tab3 = lambda n: pl.BlockSpec((1, n, LANE), lambda h, i: (h, 0, 0))
    return pl.pallas_call(
        body,
        grid=(RET_HEADS, ni),
        in_specs=_ret_specs(tm, q_blk)(tile) + [
            pl.BlockSpec((tm, LANE), lambda h, i: (i, 0)),
            pl.BlockSpec((tm, LANE), lambda h, i: (i, 0)),
            pl.BlockSpec((1, CHUNK, CHUNK), lambda h, i: (h, 0, 0)),
            tab3(CHUNK), tab3(CHUNK), tab3(8),
            pl.BlockSpec((1, LANE), lambda h, i: (0, h)),
        ],
        out_specs=[
            pl.BlockSpec((tm, LANE), lambda h, i: (i, h)),
            pl.BlockSpec((nc, 1, LANE, LANE), lambda h, i: (i, h, 0, 0)),
        ],
        out_shape=[
            jax.ShapeDtypeStruct((lp, RET_HEADS * LANE), BF16),
            jax.ShapeDtypeStruct((lp // CHUNK, RET_HEADS, LANE, LANE), F32),
        ],
        scratch_shapes=[pltpu.VMEM((LANE, LANE), F32)],
        compiler_params=_cparams(("parallel", "arbitrary")),
        name="ret_fwd",
    )(z, z, z, z, rope_c, rope_s, dm, qd, kd, cd, gn)


def _ret_bwd(z, dcat, states, rope_c, rope_s, dm, qd, kd, cd, gn, q_blk, dy_blk):
    lp = z.shape[0]
    tm = _pick_tm(lp, MIX_TM, CHUNK)
    nc = tm // CHUNK
    ni = lp // tm

    def body(q_ref, k_ref, v_ref, g_ref, dy_ref, st_ref, c_ref, s_ref, dm_ref, qd_ref, kd_ref, cd_ref, gn_ref,
             dq_ref, dk_ref, dv_ref, dgt_ref, dgn_ref, dstate):
        i = pl.program_id(1)
        t = ni - 1 - i

        @pl.when(i == 0)
        def _():
            dstate[...] = jnp.zeros_like(dstate)
            dgn_ref[...] = jnp.zeros_like(dgn_ref)

        dmv, qdv, kdv, cdv, gnv = dm_ref[0], qd_ref[0], kd_ref[0], cd_ref[0, 0:1, :], gn_ref[...]

        def chunk(cc, carry):
            c = nc - 1 - cc
            rows = pl.ds(pl.multiple_of(c * CHUNK, CHUNK), CHUNK)
            cos, sn = c_ref[rows, :], s_ref[rows, :]
            v = v_ref[rows, :]
            vb = v.astype(BF16)
            s0 = st_ref[c, 0, :, :]
            qr, kr, am, o = _ret_chunk_fwd(q_ref[rows, :], k_ref[rows, :], v, cos, sn, s0, dmv, qdv, kdv)
            on, rstd = _ln_stats(o)
            gg = g_ref[rows, :]
            sg = _sigmoid(gg)
            sl = gg * sg
            dyv = dy_ref[rows, :]
            dgn_ref[...] += _colsum(dyv * sl * on)
            keep = (lax.broadcasted_iota(jnp.int32, (CHUNK, 1), 0) + (t * tm + c * CHUNK)) >= PAD_ROWS
            dgt = dyv * (on * gnv) * (sg * (1.0 + gg * (1.0 - sg)))
            dgt_ref[rows, :] = jnp.where(keep, dgt, 0.0).astype(BF16)
            dob = _ln_bwd(dyv * sl, on, rstd, gnv).astype(BF16)
            ds1 = dstate[...]
            ds1b = ds1.astype(BF16)
            qdb = (qr * qdv).astype(BF16)
            kdb = (kr * kdv).astype(BF16)
            da = (_dot_nt(dob, vb) * dmv).astype(BF16)
            dv = _dot_tn(am.astype(BF16), dob) + _dot(kdb, ds1b)
            dqr = _dot(da, kr.astype(BF16)) + _dot_nt(dob, s0.astype(BF16)) * qdv
            dkr = _dot_tn(da, qr.astype(BF16)) + _dot_nt(vb, ds1b) * kdv
            dstate[...] = ds1 * cdv + _dot_tn(qdb, dob)
            dq = _rope_t(dqr, cos, sn)
            dk = _rope_t(dkr * (LANE ** -0.5), cos, sn)
            dq_ref[rows, :] = jnp.where(keep, dq, 0.0).astype(BF16)
            dk_ref[rows, :] = jnp.where(keep, dk, 0.0).astype(BF16)
            dv_ref[rows, :] = jnp.where(keep, dv, 0.0).astype(BF16)
            return carry

        lax.fori_loop(0, nc, chunk, 0)

    tile = lambda i: ni - 1 - i
    tab3 = lambda n: pl.BlockSpec((1, n, LANE), lambda h, i: (h, 0, 0))
    out_blk = pl.BlockSpec((tm, LANE), lambda h, i: (ni - 1 - i, h))
    out_sds = jax.ShapeDtypeStruct((lp, RET_HEADS * LANE), BF16)
    return pl.pallas_call(
        body,
        grid=(RET_HEADS, ni),
        in_specs=_ret_specs(tm, q_blk)(tile) + [
            pl.BlockSpec((tm, LANE), lambda h, i: (ni - 1 - i, dy_blk + h)),
            pl.BlockSpec((nc, 1, LANE, LANE), lambda h, i: (ni - 1 - i, h, 0, 0)),
            pl.BlockSpec((tm, LANE), lambda h, i: (ni - 1 - i, 0)),
            pl.BlockSpec((tm, LANE), lambda h, i: (ni - 1 - i, 0)),
            pl.BlockSpec((1, CHUNK, CHUNK), lambda h, i: (h, 0, 0)),
            tab3(CHUNK), tab3(CHUNK), tab3(8),
            pl.BlockSpec((1, LANE), lambda h, i: (0, h)),
        ],
        out_specs=[out_blk, out_blk, out_blk, out_blk, pl.BlockSpec((1, LANE), lambda h, i: (0, h))],
        out_shape=[out_sds, out_sds, out_sds, out_sds, jax.ShapeDtypeStruct((1, RET_HEADS * LANE), F32)],
        scratch_shapes=[pltpu.VMEM((LANE, LANE), F32)],
        compiler_params=_cparams(("parallel", "arbitrary")),
        name="ret_bwd",
    )(z, z, z, z, dcat, states, rope_c, rope_s, dm, qd, kd, cd, gn)


def _adamw(parts, w, m, v):
    r = w.shape[0]
    tr = _pick_tm(r, 1024, 8)

    def body(p_ref, w_ref, m_ref, v_ref, g_ref, d_ref, nm_ref, nv_ref):
        g = p_ref[0]
        for j in range(1, N_DEV):
            g = g + p_ref[j]
        m1 = ADAM_B1 * m_ref[...] + (1.0 - ADAM_B1) * g
        v1 = ADAM_B2 * v_ref[...] + (1.0 - ADAM_B2) * (g * g)
        m_hat = m1 / (1.0 - ADAM_B1 ** ADAM_STEP)
        v_hat = v1 / (1.0 - ADAM_B2 ** ADAM_STEP)
        g_ref[...] = g
        d_ref[...] = -ADAM_LR * (m_hat / (jnp.sqrt(v_hat) + ADAM_EPS) + ADAM_WD * w_ref[...])
        nm_ref[...] = m1
        nv_ref[...] = v1

    blk = pl.BlockSpec((tr, LANE), lambda i: (i, 0))
    sds = jax.ShapeDtypeStruct((r, LANE), F32)
    return pl.pallas_call(
        body,
        grid=(r // tr,),
        in_specs=[pl.BlockSpec((N_DEV, tr, LANE), lambda i: (0, i, 0)), blk, blk, blk],
        out_specs=[blk, blk, blk, blk],
        out_shape=[sds, sds, sds, sds],
        compiler_params=_cparams(("parallel",)),
        name="adamw",
    )(parts, w, m, v)


def _flip(v, bit):
    return 1 - v if bit else v


def _exchange(x, scatter):
    shape = x.shape[1:] if scatter else x.shape

    def body(x_ref, o_ref, send_sems, recv_sems, local_sem):
        mx, my, mc = lax.axis_index("x"), lax.axis_index("y"), lax.axis_index("c")
        me = 4 * mx + 2 * my + mc

        def peer_of(k):
            return (_flip(mx, (k >> 2) & 1), _flip(my, (k >> 1) & 1), _flip(mc, k & 1))

        def copy(k):
            peer = peer_of(k)
            src = x_ref.at[4 * peer[0] + 2 * peer[1] + peer[2]] if scatter else x_ref
            return pltpu.make_async_remote_copy(
                src_ref=src, dst_ref=o_ref.at[me], send_sem=send_sems.at[k - 1], recv_sem=recv_sems.at[k - 1],
                device_id=peer, device_id_type=pl.DeviceIdType.MESH)

        def arrival(k):
            peer = peer_of(k)
            slot = o_ref.at[4 * peer[0] + 2 * peer[1] + peer[2]]
            return pltpu.make_async_remote_copy(
                src_ref=slot, dst_ref=slot, send_sem=send_sems.at[k - 1], recv_sem=recv_sems.at[k - 1],
                device_id=peer, device_id_type=pl.DeviceIdType.MESH)

        local = pltpu.make_async_copy(x_ref.at[me] if scatter else x_ref, o_ref.at[me], local_sem)
        local.start()
        sends = [copy(k) for k in range(1, N_DEV)]
        for cp in sends:
            cp.start()
        for k in range(1, N_DEV):
            arrival(k).wait_recv()
        for cp in sends:
            cp.wait_send()
        local.wait()

    hbm = pl.BlockSpec(memory_space=pltpu.HBM)
    return pl.pallas_call(
        body,
        in_specs=[hbm],
        out_specs=hbm,
        out_shape=jax.ShapeDtypeStruct((N_DEV,) + tuple(shape), x.dtype),
        scratch_shapes=[
            pltpu.SemaphoreType.DMA((N_DEV - 1,)),
            pltpu.SemaphoreType.DMA((N_DEV - 1,)),
            pltpu.SemaphoreType.DMA(()),
        ],
        name="reduce_scatter_parts" if scatter else "all_gather",
    )(x)


PACK_ALIGN = 2048


def _padded(n):
    return -(-n // PACK_ALIGN) * PACK_ALIGN


def _pack(arrs, dtype):
    flat = []
    for a in arrs:
        v = a.astype(dtype).reshape(-1)
        flat.append(jnp.pad(v, (0, _padded(v.size) - v.size)))
    return jnp.concatenate(flat).reshape(-1, LANE)


def _pack8(arrs, dtype):
    flat = []
    for a in arrs:
        v = a.astype(dtype).reshape(N_DEV, -1)
        flat.append(jnp.pad(v, ((0, 0), (0, _padded(v.shape[1]) - v.shape[1]))))
    return jnp.concatenate(flat, axis=1).reshape(N_DEV, -1, LANE)


def _unpack(slab, shapes, lead=()):
    flat = slab.reshape(lead + (-1,))
    out, off = [], 0
    for s in shapes:
        n = math.prod(s)
        out.append(flat[..., off:off + n].reshape(lead + tuple(s)))
        off += _padded(n)
    return out


def _unshard(parts, ax):
    full = jnp.moveaxis(parts, 0, ax)
    s = full.shape
    return full.reshape(s[:ax] + (s[ax] * s[ax + 1],) + s[ax + 2:])


def _to_shards(full, ax):
    s = full.shape
    return jnp.moveaxis(full.reshape(s[:ax] + (N_DEV, s[ax] // N_DEV) + s[ax + 1:]), ax, 0)


def _pad_to(a, axis, size):
    pad = [(0, 0)] * a.ndim
    pad[axis] = (0, size - a.shape[axis])
    return jnp.pad(a, pad)


SHARDED = (("meta", 1), ("ffn1_w13", 2), ("ffn1_w2", 1), ("w_in", 2), ("conv_dw", 2), ("conv_pw", 1),
           ("w_out", 1), ("ffn2_w13", 2), ("ffn2_w2", 1), ("ln_g", 2), ("ln_b", 2))
MATMUL_WEIGHTS = ("ffn1_w13", "ffn1_w2", "w_in", "conv_pw", "w_out", "ffn2_w13", "ffn2_w2")
REPLICATED = ("ln_in_g", "ln_in_b", "pool_w", "pool_scale", "conv_db", "conv_ln_g", "conv_ln_b", "ret_gn_g")
WEIGHT_ORDER = ("meta", "ln_in_g", "ln_in_b", "ffn1_w13", "ffn1_w2", "w_in", "pool_w", "pool_scale", "conv_dw",
                "conv_db", "conv_ln_g", "conv_ln_b", "conv_pw", "ret_gn_g", "w_out", "ffn2_w13", "ffn2_w2",
                "ln_g", "ln_b")


def _retention_tables(lp, heads):
    pos = jnp.arange(lp, dtype=F32) - PAD_ROWS
    inv_freq = ROPE_BASE ** (-jnp.arange(0, LANE, 2, dtype=F32) / LANE)
    ang = pos[:, None] * inv_freq[None, :]
    cos, sin = jnp.cos(ang), jnp.sin(ang)
    rope_c = jnp.concatenate([cos, cos], axis=1)
    rope_s = jnp.concatenate([-sin, sin], axis=1)
    log_gamma = jnp.log(1.0 - 2.0 ** (-5.0 - jnp.arange(heads, dtype=F32)))
    i = jnp.arange(CHUNK, dtype=F32)
    dm = jnp.exp(log_gamma[:, None, None] * jnp.abs(i[:, None] - i[None, :]))
    lanes = lambda t: jnp.broadcast_to(t[:, :, None], t.shape + (LANE,))
    qd = lanes(jnp.exp(log_gamma[:, None] * (i + 1.0)))
    kd = lanes(jnp.exp(log_gamma[:, None] * (CHUNK - 1.0 - i)))
    cd = lanes(jnp.broadcast_to(jnp.exp(log_gamma * CHUNK)[:, None], (heads, 8)))
    return rope_c, rope_s, dm, qd, kd, cd


def _block_diag(w):
    g, n, _ = w.shape
    out = jnp.zeros((g * n, g * n), w.dtype)
    for i in range(g):
        out = out.at[i * n:(i + 1) * n, i * n:(i + 1) * n].set(w[i])
    return out


def kernel(x, meta, ln_in_g, ln_in_b, ffn1_w13, ffn1_w2, w_in, pool_w, pool_scale, conv_dw, conv_db, conv_ln_g, conv_ln_b, conv_pw, ret_gn_g, w_out, ffn2_w13, ffn2_w2, ln_g, ln_b, loss_target, m_meta, m_ln_in_g, m_ln_in_b, m_ffn1_w13, m_ffn1_w2, m_w_in, m_pool_w, m_pool_scale, m_conv_dw, m_conv_db, m_conv_ln_g, m_conv_ln_b, m_conv_pw, m_ret_gn_g, m_w_out, m_ffn2_w13, m_ffn2_w2, m_ln_g, m_ln_b, v_meta, v_ln_in_g, v_ln_in_b, v_ffn1_w13, v_ffn1_w2, v_w_in, v_pool_w, v_pool_scale, v_conv_dw, v_conv_db, v_conv_ln_g, v_conv_ln_b, v_conv_pw, v_ret_gn_g, v_w_out, v_ffn2_w13, v_ffn2_w2, v_ln_g, v_ln_b):
    local = dict(meta=meta, ln_in_g=ln_in_g, ln_in_b=ln_in_b, ffn1_w13=ffn1_w13, ffn1_w2=ffn1_w2, w_in=w_in,
                 pool_w=pool_w, pool_scale=pool_scale, conv_dw=conv_dw, conv_db=conv_db, conv_ln_g=conv_ln_g,
                 conv_ln_b=conv_ln_b, conv_pw=conv_pw, ret_gn_g=ret_gn_g, w_out=w_out, ffn2_w13=ffn2_w13,
                 ffn2_w2=ffn2_w2, ln_g=ln_g, ln_b=ln_b)
    mom1 = dict(meta=m_meta, ln_in_g=m_ln_in_g, ln_in_b=m_ln_in_b, ffn1_w13=m_ffn1_w13, ffn1_w2=m_ffn1_w2,
                w_in=m_w_in, pool_w=m_pool_w, pool_scale=m_pool_scale, conv_dw=m_conv_dw, conv_db=m_conv_db,
                conv_ln_g=m_conv_ln_g, conv_ln_b=m_conv_ln_b, conv_pw=m_conv_pw, ret_gn_g=m_ret_gn_g,
                w_out=m_w_out, ffn2_w13=m_ffn2_w13, ffn2_w2=m_ffn2_w2, ln_g=m_ln_g, ln_b=m_ln_b)
    mom2 = dict(meta=v_meta, ln_in_g=v_ln_in_g, ln_in_b=v_ln_in_b, ffn1_w13=v_ffn1_w13, ffn1_w2=v_ffn1_w2,
                w_in=v_w_in, pool_w=v_pool_w, pool_scale=v_pool_scale, conv_dw=v_conv_dw, conv_db=v_conv_db,
                conv_ln_g=v_conv_ln_g, conv_ln_b=v_conv_ln_b, conv_pw=v_conv_pw, ret_gn_g=v_ret_gn_g,
                w_out=v_w_out, ffn2_w13=v_ffn2_w13, ffn2_w2=v_ffn2_w2, ln_g=v_ln_g, ln_b=v_ln_b)

    depth = ffn1_w13.shape[0]
    alpha = (2.0 * depth) ** 0.25
    seq, d = x.shape[1], x.shape[2]
    lp = FRONT + seq
    ff = ffn1_w2.shape[1] * N_DEV
    ffp = -(-ff // (2 * LANE)) * (2 * LANE)
    c_pool = pool_scale.shape[1]
    c_conv = conv_db.shape[1]
    q_blk = (c_pool + 2 * c_conv) // LANE
    dy_blk = (c_pool + c_conv) // LANE
    heads = ret_gn_g.shape[1] // LANE
    assert meta.shape[0] == N_META and heads == RET_HEADS and conv_dw.shape[1] == CONV_WIDTH

    ax_of = dict(SHARDED)
    small = [n for n, _ in SHARDED if n not in MATMUL_WEIGHTS]
    big_parts = _unpack(_exchange(_pack([local[n] for n in MATMUL_WEIGHTS], BF16), False),
                        [local[n].shape for n in MATMUL_WEIGHTS], (N_DEV,))
    small_parts = _unpack(_exchange(_pack([local[n] for n in small], F32), False),
                          [local[n].shape for n in small], (N_DEV,))
    full = {n: _unshard(p, ax_of[n]) for n, p in zip(MATMUL_WEIGHTS, big_parts)}
    full.update({n: _unshard(p, ax_of[n]) for n, p in zip(small, small_parts)})

    def ffn_weights(w13, w2):
        return (_pad_to(w13[:, :ff], 1, ffp), _pad_to(w13[:, ff:], 1, ffp), _pad_to(w2, 0, ffp))

    row = lambda v: v.reshape(1, -1)
    ln_params = [(row(ln_in_g), row(ln_in_b))]
    for l in range(depth):
        ln_params += [(row(full["ln_g"][l, j]), row(full["ln_b"][l, j])) for j in range(3)]
    rope_c, rope_s, dm, qd, kd, cd = _retention_tables(lp, heads)

    layers = []
    for l in range(depth):
        layers.append(dict(
            ffn1=ffn_weights(full["ffn1_w13"][l], full["ffn1_w2"][l]),
            ffn2=ffn_weights(full["ffn2_w13"][l], full["ffn2_w2"][l]),
            w_in=full["w_in"][l], w_out=full["w_out"][l],
            pool=(_block_diag(pool_w[l]).astype(BF16), row(pool_scale[l])),
            conv=(_pad_to(full["conv_dw"][l], 0, CONV_HALO), row(conv_db[l]), row(conv_ln_g[l]), row(conv_ln_b[l]),
                  full["conv_pw"][l]),
            gn=row(ret_gn_g[l]),
        ))

    r = [jnp.concatenate([jnp.zeros((PAD_ROWS, d), F32), full["meta"], x[0]], axis=0)]
    saved = []
    for l, p in enumerate(layers):
        k = 3 * l
        r1, a1, u1 = _ffn_fwd(r[k], *ln_params[k], *p["ffn1"], alpha)
        z = _proj_fwd(r1, *ln_params[k + 1], p["w_in"])
        y_pool = _pool_fwd(z, *p["pool"])
        y_conv = _conv_fwd(z, *p["conv"])
        y_ret, states = _ret_fwd(z, rope_c, rope_s, dm, qd, kd, cd, p["gn"], q_blk)
        cat = jnp.concatenate([y_pool, y_conv, y_ret], axis=1)
        r2 = _out_fwd(r1, *ln_params[k + 1], cat, p["w_out"], alpha)
        r3, a2, u2 = _ffn_fwd(r2, *ln_params[k + 2], *p["ffn2"], alpha)
        r += [r1, r2, r3]
        saved.append((a1, u1, z, states, cat, a2, u2))

    target = jnp.concatenate([jnp.zeros((FRONT, d), F32), loss_target[0]], axis=0)
    dr, dg, db, loss_cols = _loss_bwd(r[-1], target, *ln_params[-1])
    loss = lax.psum(0.5 * jnp.sum(loss_cols) / d, MESH_AXES)
    ln_grads = {3 * depth: (dg, db)}
    g_layers = []
    for l in reversed(range(depth)):
        p = layers[l]
        k = 3 * l
        a1, u1, z, states, cat, a2, u2 = saved[l]
        gl = {}
        w1, w3, w2 = p["ffn2"]
        dr3 = dr
        dr, hb, s, da, du, dg, db = _ffn_bwd(dr3, r[k + 2], a2, u2, *ln_params[k + 2], w1, w3, w2, alpha)
        ln_grads[k + 2] = (dg, db)
        gl["ffn2_w13"] = jnp.concatenate([_mm_tn(hb, da)[:, :ff], _mm_tn(hb, du)[:, :ff]], axis=1)
        gl["ffn2_w2"] = _mm_tn(s, dr3, 0.5)[:ff]

        dr2 = dr
        dcat = _mm_nt(dr2, p["w_out"])
        gl["w_out"] = _mm_tn(cat, dr2)
        dxp, dwbd, dscale = _pool_bwd(z, dcat, *p["pool"])
        dca, dcg, ddw, ddb, dcg_ln, dcb_ln, dpw = _conv_bwd(z, dcat, *p["conv"])
        dq, dk, dv, dgt, dgn = _ret_bwd(z, dcat, states, rope_c, rope_s, dm, qd, kd, cd, p["gn"], q_blk, dy_blk)
        dz = jnp.concatenate([dxp, dca, dcg, dq, dk, dv, dgt], axis=1)
        dr, hb, dg, db = _in_bwd(dr2, r[k + 1], dz, *ln_params[k + 1], p["w_in"], alpha)
        ln_grads[k + 1] = (dg, db)
        gl["w_in"] = _mm_tn(hb, dz)
        n_grp, grp = pool_w.shape[1], pool_w.shape[2]
        gl["pool_w"] = jnp.stack([dwbd[i * grp:(i + 1) * grp, i * grp:(i + 1) * grp] for i in range(n_grp)])
        gl["pool_scale"] = dscale[0]
        gl["conv_dw"] = ddw[:CONV_WIDTH]
        gl["conv_db"], gl["conv_ln_g"], gl["conv_ln_b"], gl["conv_pw"] = ddb[0], dcg_ln[0], dcb_ln[0], dpw
        gl["ret_gn_g"] = dgn[0]

        w1, w3, w2 = p["ffn1"]
        dr1 = dr
        dr, hb, s, da, du, dg, db = _ffn_bwd(dr1, r[k], a1, u1, *ln_params[k], w1, w3, w2, alpha)
        ln_grads[k] = (dg, db)
        gl["ffn1_w13"] = jnp.concatenate([_mm_tn(hb, da)[:, :ff], _mm_tn(hb, du)[:, :ff]], axis=1)
        gl["ffn1_w2"] = _mm_tn(s, dr1, 0.5)[:ff]
        g_layers.append(gl)
    g_layers.reverse()

    grad_x = dr[FRONT:][None]
    gfull = {n: jnp.stack([gl[n] for gl in g_layers]) for n in g_layers[0]}
    gfull["meta"] = dr[PAD_ROWS:FRONT]
    gfull["ln_in_g"], gfull["ln_in_b"] = ln_grads[0][0][0], ln_grads[0][1][0]
    gfull["ln_g"] = jnp.stack([jnp.stack([ln_grads[3 * l + j + 1][0][0] for j in range(3)]) for l in range(depth)])
    gfull["ln_b"] = jnp.stack([jnp.stack([ln_grads[3 * l + j + 1][1][0] for j in range(3)]) for l in range(depth)])

    sh_names = [n for n, _ in SHARDED]
    sh_shapes = [local[n].shape for n in sh_names]
    parts = _exchange(_pack8([_to_shards(gfull[n], ax_of[n]) for n in sh_names], F32), True)
    sh_out = _adamw(parts, *[_pack([src[n] for n in sh_names], F32) for src in (local, mom1, mom2)])
    rep_shapes = [local[n].shape for n in REPLICATED]
    rep_parts = _exchange(_pack([gfull[n] for n in REPLICATED], F32), False)
    rep_out = _adamw(rep_parts, *[_pack([src[n] for n in REPLICATED], F32) for src in (local, mom1, mom2)])

    results = []
    for slab_sh, slab_rep in zip(sh_out, rep_out):
        vals = dict(zip(sh_names, _unpack(slab_sh, sh_shapes)))
        vals.update(zip(REPLICATED, _unpack(slab_rep, rep_shapes)))
        results.append(vals)
    outs = [loss, grad_x]
    for vals in results:
        outs += [vals[n] for n in WEIGHT_ORDER]
    return tuple(outs)
```

```python
import math

import jax
import jax.numpy as jnp
from jax import lax
from jax.experimental import pallas as pl
from jax.experimental.pallas import tpu as pltpu

F32 = jnp.float32
BF16 = jnp.bfloat16

N_DEV = 8
MESH_AXES = ("x", "y", "c")
CHUNK = 64
N_META = 16
PAD_ROWS = 240
FRONT = PAD_ROWS + N_META
LN_EPS = 1e-5
LANE = 128
POOL_WINDOWS = (2, 4, 8, 16)
CONV_WIDTH = 31
CONV_HALO = 32
POOL_HALO = 16
RET_HEADS = 4
ROPE_BASE = 10000.0

ADAM_LR = 0.001
ADAM_B1 = 0.9
ADAM_B2 = 0.999
ADAM_EPS = 1e-08
ADAM_WD = 0.01
ADAM_STEP = 10

VMEM_LIMIT = 56 * 1024 * 1024
DENSE_TM_FWD = 704
DENSE_TM_BWD = 528
MIX_TM = 768


def _cparams(sem):
    return pltpu.CompilerParams(dimension_semantics=sem, vmem_limit_bytes=VMEM_LIMIT)


def _pick_tm(lp, target, mult=16):
    best = None
    for t in range(mult, min(lp, target) + 1, mult):
        if lp % t == 0:
            best = t
    assert best is not None, (lp, target, mult)
    return best


def _dot(a, b):
    return jnp.dot(a, b, preferred_element_type=F32)


def _dot_nt(a, b):
    return lax.dot_general(a, b, (((1,), (1,)), ((), ())), preferred_element_type=F32)


def _dot_tn(a, b):
    return lax.dot_general(a, b, (((0,), (0,)), ((), ())), preferred_element_type=F32)


def _sigmoid(x):
    return 1.0 / (1.0 + jnp.exp(-x))


def _ln_stats(r):
    mu = jnp.mean(r, axis=-1, keepdims=True)
    xc = r - mu
    var = jnp.mean(xc * xc, axis=-1, keepdims=True)
    rstd = lax.rsqrt(var + LN_EPS)
    return xc * rstd, rstd


def _ln_bwd(dh, xhat, rstd, g):
    dxh = dh * g
    m1 = jnp.mean(dxh, axis=-1, keepdims=True)
    m2 = jnp.mean(dxh * xhat, axis=-1, keepdims=True)
    return rstd * (dxh - m1 - xhat * m2)


def _colsum(x):
    return jnp.sum(x, axis=0, keepdims=True)


def _row_ids(tile, tm):
    return lax.broadcasted_iota(jnp.int32, (tm, 1), 0) + tile * tm


def _ffn_fwd(r_prev, g, b, w13g, w2g, alpha):
    lp, d = r_prev.shape
    nf = N_DEV // 2
    fc = w13g.shape[2]
    w2c = w2g.reshape(nf, fc, d)
    tm = _pick_tm(lp, DENSE_TM_FWD)

    def body(r_ref, g_ref, b_ref, w1_ref, w3_ref, w2_ref, out_ref, a_ref, u_ref, hb, acc):
        f = pl.program_id(1)

        @pl.when(f == 0)
        def _():
            xhat, _ = _ln_stats(r_ref[...])
            hb[...] = (xhat * g_ref[...] + b_ref[...]).astype(BF16)
            acc[...] = jnp.zeros_like(acc)

        h = hb[...]
        a = _dot(h, w1_ref[...])
        u = _dot(h, w3_ref[...])
        a_ref[...] = a.astype(BF16)
        u_ref[...] = u.astype(BF16)
        s = a * _sigmoid(a) * u
        acc[...] += _dot(s.astype(BF16), w2_ref[...])

        @pl.when(f == nf - 1)
        def _():
            xhat, _ = _ln_stats(r_ref[...])
            out_ref[...] = alpha * (xhat * g_ref[...] + b_ref[...]) + 0.5 * acc[...]

    return pl.pallas_call(
        body,
        grid=(lp // tm, nf),
        in_specs=[
            pl.BlockSpec((tm, d), lambda i, f: (i, 0)),
            pl.BlockSpec((1, d), lambda i, f: (0, 0)),
            pl.BlockSpec((1, d), lambda i, f: (0, 0)),
            pl.BlockSpec((None, d, fc), lambda i, f: (f, 0, 0)),
            pl.BlockSpec((None, d, fc), lambda i, f: (nf + f, 0, 0)),
            pl.BlockSpec((None, fc, d), lambda i, f: (f, 0, 0)),
        ],
        out_specs=[
            pl.BlockSpec((tm, d), lambda i, f: (i, 0)),
            pl.BlockSpec((tm, fc), lambda i, f: (i, f)),
            pl.BlockSpec((tm, fc), lambda i, f: (i, f)),
        ],
        out_shape=[
            jax.ShapeDtypeStruct((lp, d), F32),
            jax.ShapeDtypeStruct((lp, nf * fc), BF16),
            jax.ShapeDtypeStruct((lp, nf * fc), BF16),
        ],
        scratch_shapes=[pltpu.VMEM((tm, d), BF16), pltpu.VMEM((tm, d), F32)],
        compiler_params=_cparams(("parallel", "arbitrary")),
        name="ffn_fwd",
    )(r_prev, g, b, w13g, w13g, w2c)


def _ffn_bwd(dr_next, r_prev, a, u, g, b, w13g, w2g, alpha):
    lp, d = r_prev.shape
    nf = N_DEV // 2
    fc = w13g.shape[2]
    w2c = w2g.reshape(nf, fc, d)
    tm = _pick_tm(lp, DENSE_TM_BWD)

    def body(dr_ref, r_ref, a_ref, u_ref, g_ref, b_ref, w1_ref, w3_ref, w2_ref,
             drp_ref, hb_ref, s_ref, dz_ref, dg_ref, db_ref, dyb, dhacc):
        i = pl.program_id(0)
        f = pl.program_id(1)

        @pl.when(jnp.logical_and(i == 0, f == 0))
        def _():
            dg_ref[...] = jnp.zeros_like(dg_ref)
            db_ref[...] = jnp.zeros_like(db_ref)

        @pl.when(f == 0)
        def _():
            dyb[...] = (0.5 * dr_ref[...]).astype(BF16)
            dhacc[...] = jnp.zeros_like(dhacc)
            xhat, _ = _ln_stats(r_ref[...])
            hb_ref[...] = (xhat * g_ref[...] + b_ref[...]).astype(BF16)

        ds = _dot_nt(dyb[...], w2_ref[...])
        av = a_ref[...].astype(F32)
        uv = u_ref[...].astype(F32)
        sig = _sigmoid(av)
        sl = av * sig
        da = (ds * uv * (sig * (1.0 + av * (1.0 - sig)))).astype(BF16)
        du = (ds * sl).astype(BF16)
        s_ref[...] = (sl * uv).astype(BF16)
        dz_ref[:, 0:fc] = da
        dz_ref[:, fc:2 * fc] = du
        dhacc[...] += _dot_nt(da, w1_ref[...]) + _dot_nt(du, w3_ref[...])

        @pl.when(f == nf - 1)
        def _():
            dh = alpha * dr_ref[...] + dhacc[...]
            xhat, rstd = _ln_stats(r_ref[...])
            drp_ref[...] = _ln_bwd(dh, xhat, rstd, g_ref[...])
            dg_ref[...] += _colsum(dh * xhat)
            db_ref[...] += _colsum(dh)

    row = lambda i, f: (i, 0)
    const = lambda i, f: (0, 0)
    chunk = lambda i, f: (i, f)
    return pl.pallas_call(
        body,
        grid=(lp // tm, nf),
        in_specs=[
            pl.BlockSpec((tm, d), row),
            pl.BlockSpec((tm, d), row),
            pl.BlockSpec((tm, fc), chunk),
            pl.BlockSpec((tm, fc), chunk),
            pl.BlockSpec((1, d), const),
            pl.BlockSpec((1, d), const),
            pl.BlockSpec((None, d, fc), lambda i, f: (f, 0, 0)),
            pl.BlockSpec((None, d, fc), lambda i, f: (nf + f, 0, 0)),
            pl.BlockSpec((None, fc, d), lambda i, f: (f, 0, 0)),
        ],
        out_specs=[
            pl.BlockSpec((tm, d), row),
            pl.BlockSpec((tm, d), row),
            pl.BlockSpec((tm, fc), chunk),
            pl.BlockSpec((tm, 2 * fc), chunk),
            pl.BlockSpec((1, d), const),
            pl.BlockSpec((1, d), const),
        ],
        out_shape=[
            jax.ShapeDtypeStruct((lp, d), F32),
            jax.ShapeDtypeStruct((lp, d), BF16),
            jax.ShapeDtypeStruct((lp, nf * fc), BF16),
            jax.ShapeDtypeStruct((lp, 2 * nf * fc), BF16),
            jax.ShapeDtypeStruct((1, d), F32),
            jax.ShapeDtypeStruct((1, d), F32),
        ],
        scratch_shapes=[pltpu.VMEM((tm, d), BF16), pltpu.VMEM((tm, d), F32)],
        compiler_params=_cparams(("arbitrary", "arbitrary")),
        name="ffn_bwd",
    )(dr_next, r_prev, a, u, g, b, w13g, w13g, w2c)


def _mm_tn(a, b, scale=1.0, bn=None, slot_of=None):
    t, m = a.shape
    n = b.shape[1]
    if bn is None:
        bn = n if n <= 1408 else _pick_tm(n, 1408, LANE)
    bm = m if m <= 1536 else _pick_tm(m, 1536, LANE)
    tk = _pick_tm(t, 768)
    nt = t // tk

    def body(a_ref, b_ref, o_ref):
        k = pl.program_id(2)

        @pl.when(k == 0)
        def _():
            o_ref[...] = jnp.zeros_like(o_ref)

        o_ref[...] += _dot_tn(a_ref[...].astype(BF16), b_ref[...].astype(BF16))

        if scale != 1.0:
            @pl.when(k == nt - 1)
            def _():
                o_ref[...] = o_ref[...] * scale

    if slot_of is None:
        out_spec = pl.BlockSpec((bm, bn), lambda i, j, k: (i, j))
        out_shape = jax.ShapeDtypeStruct((m, n), F32)
    else:
        out_spec = pl.BlockSpec((None, bm, bn), lambda i, j, k: (slot_of(j), i, 0))
        out_shape = jax.ShapeDtypeStruct((n // bn, m, bn), F32)
    return pl.pallas_call(
        body,
        grid=(m // bm, n // bn, nt),
        in_specs=[
            pl.BlockSpec((tk, bm), lambda i, j, k: (k, i)),
            pl.BlockSpec((tk, bn), lambda i, j, k: (k, j)),
        ],
        out_specs=out_spec,
        out_shape=out_shape,
        compiler_params=_cparams(("parallel", "parallel", "arbitrary")),
        name="mm_tn",
    )(a, b)


def _proj_fwd(r_prev, g, b, w_in):
    lp, d = r_prev.shape
    n = w_in.shape[1]
    tm = _pick_tm(lp, DENSE_TM_FWD)

    def body(r_ref, g_ref, b_ref, w_ref, z_ref):
        xhat, _ = _ln_stats(r_ref[...])
        h = (xhat * g_ref[...] + b_ref[...]).astype(BF16)
        z = _dot(h, w_ref[...])
        rows = _row_ids(pl.program_id(0), tm)
        z_ref[...] = jnp.where(rows >= PAD_ROWS, z, 0.0)

    return pl.pallas_call(
        body,
        grid=(lp // tm,),
        in_specs=[
            pl.BlockSpec((tm, d), lambda i: (i, 0)),
            pl.BlockSpec((1, d), lambda i: (0, 0)),
            pl.BlockSpec((1, d), lambda i: (0, 0)),
            pl.BlockSpec((d, n), lambda i: (0, 0)),
        ],
        out_specs=pl.BlockSpec((tm, n), lambda i: (i, 0)),
        out_shape=jax.ShapeDtypeStruct((lp, n), F32),
        compiler_params=_cparams(("parallel",)),
        name="proj_fwd",
    )(r_prev, g, b, w_in)


def _out_fwd(r_prev, g, b, cat, w_out, alpha):
    lp, d = r_prev.shape
    k = cat.shape[1]
    tm = _pick_tm(lp, DENSE_TM_FWD)

    def body(r_ref, g_ref, b_ref, c_ref, w_ref, o_ref):
        xhat, _ = _ln_stats(r_ref[...])
        o_ref[...] = alpha * (xhat * g_ref[...] + b_ref[...]) + _dot(c_ref[...], w_ref[...])

    return pl.pallas_call(
        body,
        grid=(lp // tm,),
        in_specs=[
            pl.BlockSpec((tm, d), lambda i: (i, 0)),
            pl.BlockSpec((1, d), lambda i: (0, 0)),
            pl.BlockSpec((1, d), lambda i: (0, 0)),
            pl.BlockSpec((tm, k), lambda i: (i, 0)),
            pl.BlockSpec((k, d), lambda i: (0, 0)),
        ],
        out_specs=pl.BlockSpec((tm, d), lambda i: (i, 0)),
        out_shape=jax.ShapeDtypeStruct((lp, d), F32),
        compiler_params=_cparams(("parallel",)),
        name="out_fwd",
    )(r_prev, g, b, cat, w_out)


def _mm_nt(x, w):
    lp, n = x.shape
    k = w.shape[0]
    tm = _pick_tm(lp, DENSE_TM_FWD)

    def body(x_ref, w_ref, o_ref):
        o_ref[...] = _dot_nt(x_ref[...].astype(BF16), w_ref[...])

    return pl.pallas_call(
        body,
        grid=(lp // tm,),
        in_specs=[pl.BlockSpec((tm, n), lambda i: (i, 0)), pl.BlockSpec((k, n), lambda i: (0, 0))],
        out_specs=pl.BlockSpec((tm, k), lambda i: (i, 0)),
        out_shape=jax.ShapeDtypeStruct((lp, k), F32),
        compiler_params=_cparams(("parallel",)),
        name="mm_nt",
    )(x, w)


def _in_bwd(dr_next, r_prev, dz, g, b, w_in, alpha):
    lp, d = r_prev.shape
    n = w_in.shape[1]
    tm = _pick_tm(lp, DENSE_TM_FWD)

    def body(dr_ref, r_ref, dz_ref, g_ref, b_ref, w_ref, drp_ref, hb_ref, dg_ref, db_ref):
        @pl.when(pl.program_id(0) == 0)
        def _():
            dg_ref[...] = jnp.zeros_like(dg_ref)
            db_ref[...] = jnp.zeros_like(db_ref)

        dh = alpha * dr_ref[...] + _dot_nt(dz_ref[...], w_ref[...])
        xhat, rstd = _ln_stats(r_ref[...])
        hb_ref[...] = (xhat * g_ref[...] + b_ref[...]).astype(BF16)
        drp_ref[...] = _ln_bwd(dh, xhat, rstd, g_ref[...])
        dg_ref[...] += _colsum(dh * xhat)
        db_ref[...] += _colsum(dh)

    row = lambda i: (i, 0)
    const = lambda i: (0, 0)
    return pl.pallas_call(
        body,
        grid=(lp // tm,),
        in_specs=[
            pl.BlockSpec((tm, d), row),
            pl.BlockSpec((tm, d), row),
            pl.BlockSpec((tm, n), row),
            pl.BlockSpec((1, d), const),
            pl.BlockSpec((1, d), const),
            pl.BlockSpec((d, n), const),
        ],
        out_specs=[
            pl.BlockSpec((tm, d), row),
            pl.BlockSpec((tm, d), row),
            pl.BlockSpec((1, d), const),
            pl.BlockSpec((1, d), const),
        ],
        out_shape=[
            jax.ShapeDtypeStruct((lp, d), F32),
            jax.ShapeDtypeStruct((lp, d), BF16),
            jax.ShapeDtypeStruct((1, d), F32),
            jax.ShapeDtypeStruct((1, d), F32),
        ],
        compiler_params=_cparams(("arbitrary",)),
        name="in_bwd",
    )(dr_next, r_prev, dz, g, b, w_in)


def _loss_bwd(r_last, target, g, b):
    lp, d = r_last.shape
    tm = _pick_tm(lp, DENSE_TM_FWD)

    def body(r_ref, t_ref, g_ref, b_ref, dr_ref, dg_ref, db_ref, ls_ref):
        i = pl.program_id(0)

        @pl.when(i == 0)
        def _():
            dg_ref[...] = jnp.zeros_like(dg_ref)
            db_ref[...] = jnp.zeros_like(db_ref)
            ls_ref[...] = jnp.zeros_like(ls_ref)

        xhat, rstd = _ln_stats(r_ref[...])
        y = xhat * g_ref[...] + b_ref[...]
        rows = _row_ids(i, tm)
        err = jnp.where(rows >= FRONT, y - t_ref[...], 0.0)
        ls_ref[...] += _colsum(err * err)
        dy = err * (1.0 / d)
        dr_ref[...] = _ln_bwd(dy, xhat, rstd, g_ref[...])
        dg_ref[...] += _colsum(dy * xhat)
        db_ref[...] += _colsum(dy)

    row = lambda i: (i, 0)
    const = lambda i: (0, 0)
    return pl.pallas_call(
        body,
        grid=(lp // tm,),
        in_specs=[
            pl.BlockSpec((tm, d), row),
            pl.BlockSpec((tm, d), row),
            pl.BlockSpec((1, d), const),
            pl.BlockSpec((1, d), const),
        ],
        out_specs=[
            pl.BlockSpec((tm, d), row),
            pl.BlockSpec((1, d), const),
            pl.BlockSpec((1, d), const),
            pl.BlockSpec((1, d), const),
        ],
        out_shape=[
            jax.ShapeDtypeStruct((lp, d), F32),
            jax.ShapeDtypeStruct((1, d), F32),
            jax.ShapeDtypeStruct((1, d), F32),
            jax.ShapeDtypeStruct((1, d), F32),
        ],
        compiler_params=_cparams(("arbitrary",)),
        name="loss_bwd",
    )(r_last, target, g, b)


def _pool_counts(tile, tm, width):
    pos = _row_ids(tile, tm) - PAD_ROWS
    lane = lax.broadcasted_iota(jnp.int32, (1, width), 1)
    group = width // len(POOL_WINDOWS)
    win = jnp.full((1, width), POOL_WINDOWS[-1], jnp.int32)
    for gi in range(len(POOL_WINDOWS) - 2, -1, -1):
        win = jnp.where(lane < (gi + 1) * group, POOL_WINDOWS[gi], win)
    cnt = jnp.clip(pos + 1, 1, win)
    return cnt.astype(F32), lane, group


def _pool_select(sums, lane, group):
    out = sums[-1]
    for gi in range(len(POOL_WINDOWS) - 2, -1, -1):
        out = jnp.where(lane < (gi + 1) * group, sums[gi], out)
    return out


def _pool_window_sums(ext, tm, sign):
    base = POOL_HALO if sign < 0 else 0
    acc = ext[pl.ds(base, tm), :]
    sums, k = [], 1
    for w in POOL_WINDOWS:
        while k < w:
            acc = acc + ext[pl.ds(base + sign * k, tm), :]
            k += 1
        sums.append(acc)
    return sums


def _pool_fwd(z, wbd, scale):
    lp = z.shape[0]
    c = wbd.shape[0]
    tm = _pick_tm(lp, MIX_TM, CHUNK)

    def body(x_ref, w_ref, s_ref, y_ref, ext):
        i = pl.program_id(0)

        @pl.when(i == 0)
        def _():
            ext[0:POOL_HALO, :] = jnp.zeros((POOL_HALO, c), F32)

        x = x_ref[...]
        ext[POOL_HALO:POOL_HALO + tm, :] = x
        cnt, lane, group = _pool_counts(i, tm, c)
        sums = _pool_window_sums(ext, tm, -1)
        y = _pool_select(sums, lane, group) / cnt - x
        ext[0:POOL_HALO, :] = ext[tm:tm + POOL_HALO, :]
        y_ref[...] = (_dot(y.astype(BF16), w_ref[...]) * s_ref[...]).astype(BF16)

    return pl.pallas_call(
        body,
        grid=(lp // tm,),
        in_specs=[
            pl.BlockSpec((tm, c), lambda i: (i, 0)),
            pl.BlockSpec((c, c), lambda i: (0, 0)),
            pl.BlockSpec((1, c), lambda i: (0, 0)),
        ],
        out_specs=pl.BlockSpec((tm, c), lambda i: (i, 0)),
        out_shape=jax.ShapeDtypeStruct((lp, c), BF16),
        scratch_shapes=[pltpu.VMEM((tm + POOL_HALO, c), F32)],
        compiler_params=_cparams(("arbitrary",)),
        name="pool_fwd",
    )(z, wbd, scale)


def _pool_bwd(z, dcat, wbd, scale):
    lp = z.shape[0]
    c = wbd.shape[0]
    tm = _pick_tm(lp, MIX_TM, CHUNK)
    ni = lp // tm
    hpt = tm // POOL_HALO

    def body(x_ref, xh_ref, dy_ref, w_ref, s_ref, dx_ref, dw_ref, ds_ref, ext, ext2):
        i = pl.program_id(0)
        t = ni - 1 - i

        @pl.when(i == 0)
        def _():
            dw_ref[...] = jnp.zeros_like(dw_ref)
            ds_ref[...] = jnp.zeros_like(ds_ref)
            ext2[tm:tm + POOL_HALO, :] = jnp.zeros((POOL_HALO, c), F32)

        x = x_ref[...]
        ext[0:POOL_HALO, :] = jnp.where(t > 0, xh_ref[...], 0.0)
        ext[POOL_HALO:POOL_HALO + tm, :] = x
        cnt, lane, group = _pool_counts(t, tm, c)
        y = (_pool_select(_pool_window_sums(ext, tm, -1), lane, group) / cnt - x).astype(BF16)
        w = w_ref[...]
        dyv = dy_ref[...]
        ds_ref[...] += _colsum(_dot(y, w) * dyv)
        do = (dyv * s_ref[...]).astype(BF16)
        dw_ref[...] += _dot_tn(y, do)
        dyp = _dot_nt(do, w)
        ext2[0:tm, :] = dyp / cnt
        dx = _pool_select(_pool_window_sums(ext2, tm, 1), lane, group) - dyp
        ext2[tm:tm + POOL_HALO, :] = ext2[0:POOL_HALO, :]
        rows = _row_ids(t, tm)
        dx_ref[...] = jnp.where(rows >= PAD_ROWS, dx, 0.0).astype(BF16)

    rev = lambda i: (ni - 1 - i, 0)
    return pl.pallas_call(
        body,
        grid=(ni,),
        in_specs=[
            pl.BlockSpec((tm, c), rev),
            pl.BlockSpec((POOL_HALO, c), lambda i: (jnp.maximum((ni - 1 - i) * hpt - 1, 0), 0)),
            pl.BlockSpec((tm, c), rev),
            pl.BlockSpec((c, c), lambda i: (0, 0)),
            pl.BlockSpec((1, c), lambda i: (0, 0)),
        ],
        out_specs=[
            pl.BlockSpec((tm, c), rev),
            pl.BlockSpec((c, c), lambda i: (0, 0)),
            pl.BlockSpec((1, c), lambda i: (0, 0)),
        ],
        out_shape=[
            jax.ShapeDtypeStruct((lp, c), BF16),
            jax.ShapeDtypeStruct((c, c), F32),
            jax.ShapeDtypeStruct((1, c), F32),
        ],
        scratch_shapes=[pltpu.VMEM((tm + POOL_HALO, c), F32), pltpu.VMEM((tm + POOL_HALO, c), F32)],
        compiler_params=_cparams(("arbitrary",)),
        name="pool_bwd",
    )(z, z, dcat, wbd, scale)


def _conv_taps(ext, w_ref, tm, first):
    acc = w_ref[0:1, :] * ext[pl.ds(first, tm), :]
    for k in range(1, CONV_WIDTH):
        acc = acc + w_ref[k:k + 1, :] * ext[pl.ds(first + k, tm), :]
    return acc


def _conv_fwd(z, w_dw, b_dw, ln_g, ln_b, w_pw):
    lp = z.shape[0]
    c = w_pw.shape[0]
    tm = _pick_tm(lp, MIX_TM, CHUNK)

    def body(a_ref, gt_ref, w_ref, bd_ref, g_ref, b_ref, pw_ref, y_ref, ext):
        i = pl.program_id(0)

        @pl.when(i == 0)
        def _():
            ext[0:CONV_HALO, :] = jnp.zeros((CONV_HALO, c), F32)

        ext[CONV_HALO:CONV_HALO + tm, :] = a_ref[...] * _sigmoid(gt_ref[...])
        acc = _conv_taps(ext, w_ref, tm, CONV_HALO - CONV_WIDTH + 1) + bd_ref[...]
        ext[0:CONV_HALO, :] = ext[tm:tm + CONV_HALO, :]
        xhat, _ = _ln_stats(acc)
        n = xhat * g_ref[...] + b_ref[...]
        act = n * _sigmoid(n)
        y_ref[...] = _dot(act.astype(BF16), pw_ref[...]).astype(BF16)

    const = lambda i: (0, 0)
    return pl.pallas_call(
        body,
        grid=(lp // tm,),
        in_specs=[
            pl.BlockSpec((tm, c), lambda i: (i, 1)),
            pl.BlockSpec((tm, c), lambda i: (i, 2)),
            pl.BlockSpec((CONV_HALO, c), const),
            pl.BlockSpec((1, c), const),
            pl.BlockSpec((1, c), const),
            pl.BlockSpec((1, c), const),
            pl.BlockSpec((c, c), const),
        ],
        out_specs=pl.BlockSpec((tm, c), lambda i: (i, 0)),
        out_shape=jax.ShapeDtypeStruct((lp, c), BF16),
        scratch_shapes=[pltpu.VMEM((tm + CONV_HALO, c), F32)],
        compiler_params=_cparams(("arbitrary",)),
        name="conv_fwd",
    )(z, z, w_dw, b_dw, ln_g, ln_b, w_pw)


def _conv_bwd(z, dcat, w_dw, b_dw, ln_g, ln_b, w_pw):
    lp = z.shape[0]
    c = w_pw.shape[0]
    tm = _pick_tm(lp, MIX_TM, CHUNK)
    ni = lp // tm
    hpt = tm // CONV_HALO
    first = CONV_HALO - CONV_WIDTH + 1

    def body(a_ref, gt_ref, ah_ref, gh_ref, dy_ref, w_ref, bd_ref, g_ref, b_ref, pw_ref,
             dca_ref, dcg_ref, dw_ref, dbd_ref, dg_ref, db_ref, dpw_ref, extu, extd):
        i = pl.program_id(0)
        t = ni - 1 - i

        @pl.when(i == 0)
        def _():
            dw_ref[...] = jnp.zeros_like(dw_ref)
            dbd_ref[...] = jnp.zeros_like(dbd_ref)
            dg_ref[...] = jnp.zeros_like(dg_ref)
            db_ref[...] = jnp.zeros_like(db_ref)
            dpw_ref[...] = jnp.zeros_like(dpw_ref)
            extd[tm:tm + CONV_HALO, :] = jnp.zeros((CONV_HALO, c), F32)

        ca = a_ref[...]
        sg = _sigmoid(gt_ref[...])
        extu[0:CONV_HALO, :] = jnp.where(t > 0, ah_ref[...] * _sigmoid(gh_ref[...]), 0.0)
        extu[CONV_HALO:CONV_HALO + tm, :] = ca * sg
        acc = _conv_taps(extu, w_ref, tm, first) + bd_ref[...]
        xhat, rstd = _ln_stats(acc)
        gam = g_ref[...]
        n = xhat * gam + b_ref[...]
        sn = _sigmoid(n)
        act = (n * sn).astype(BF16)
        do = dy_ref[...].astype(BF16)
        dpw_ref[...] += _dot_tn(act, do)
        dn = _dot_nt(do, pw_ref[...]) * (sn * (1.0 + n * (1.0 - sn)))
        dg_ref[...] += _colsum(dn * xhat)
        db_ref[...] += _colsum(dn)
        dyc = _ln_bwd(dn, xhat, rstd, gam)
        dbd_ref[...] += _colsum(dyc)
        extd[0:tm, :] = dyc
        du = None
        for k in range(CONV_WIDTH):
            dw_ref[k:k + 1, :] += _colsum(dyc * extu[pl.ds(first + k, tm), :])
            term = w_ref[k:k + 1, :] * extd[pl.ds(CONV_WIDTH - 1 - k, tm), :]
            du = term if du is None else du + term
        extd[tm:tm + CONV_HALO, :] = extd[0:CONV_HALO, :]
        du = jnp.where(_row_ids(t, tm) >= PAD_ROWS, du, 0.0)
        dca_ref[...] = (du * sg).astype(BF16)
        dcg_ref[...] = (du * ca * sg * (1.0 - sg)).astype(BF16)

    const = lambda i: (0, 0)
    rev = lambda col: (lambda i: (ni - 1 - i, col))
    halo = lambda col: (lambda i: (jnp.maximum((ni - 1 - i) * hpt - 1, 0), col))
    return pl.pallas_call(
        body,
        grid=(ni,),
        in_specs=[
            pl.BlockSpec((tm, c), rev(1)),
            pl.BlockSpec((tm, c), rev(2)),
            pl.BlockSpec((CONV_HALO, c), halo(1)),
            pl.BlockSpec((CONV_HALO, c), halo(2)),
            pl.BlockSpec((tm, c), rev(1)),
            pl.BlockSpec((CONV_HALO, c), const),
            pl.BlockSpec((1, c), const),
            pl.BlockSpec((1, c), const),
            pl.BlockSpec((1, c), const),
            pl.BlockSpec((c, c), const),
        ],
        out_specs=[
            pl.BlockSpec((tm, c), rev(0)),
            pl.BlockSpec((tm, c), rev(0)),
            pl.BlockSpec((CONV_HALO, c), const),
            pl.BlockSpec((1, c), const),
            pl.BlockSpec((1, c), const),
            pl.BlockSpec((1, c), const),
            pl.BlockSpec((c, c), const),
        ],
        out_shape=[
            jax.ShapeDtypeStruct((lp, c), BF16),
            jax.ShapeDtypeStruct((lp, c), BF16),
            jax.ShapeDtypeStruct((CONV_HALO, c), F32),
            jax.ShapeDtypeStruct((1, c), F32),
            jax.ShapeDtypeStruct((1, c), F32),
            jax.ShapeDtypeStruct((1, c), F32),
            jax.ShapeDtypeStruct((c, c), F32),
        ],
        scratch_shapes=[pltpu.VMEM((tm + CONV_HALO, c), F32), pltpu.VMEM((tm + CONV_HALO, c), F32)],
        compiler_params=_cparams(("arbitrary",)),
        name="conv_bwd",
    )(z, z, z, z, dcat, w_dw, b_dw, ln_g, ln_b, w_pw)


def _rope(x, cos, sgn_sin):
    return x * cos + pltpu.roll(x, LANE // 2, 1) * sgn_sin


def _rope_t(dy, cos, sgn_sin):
    return dy * cos + pltpu.roll(dy * sgn_sin, LANE // 2, 1)


def _ret_chunk_fwd(q, k, v, cos, sn, state, dmv, qdv, kdv):
    qr = _rope(q, cos, sn)
    kr = _rope(k, cos, sn) * (LANE ** -0.5)
    am = _dot_nt(qr.astype(BF16), kr.astype(BF16)) * dmv
    o = _dot(am.astype(BF16), v.astype(BF16)) + _dot((qr * qdv).astype(BF16), state.astype(BF16))
    return qr, kr, am, o


def _ret_specs(tm, q_blk):
    def mk(off, tile_of):
        return pl.BlockSpec((tm, LANE), lambda h, i: (tile_of(i), off + h))
    return lambda tile_of: [mk(q_blk + j * RET_HEADS, tile_of) for j in range(4)]


def _ret_fwd(z, rope_c, rope_s, dm, qd, kd, cd, gn, q_blk):
    lp = z.shape[0]
    tm = _pick_tm(lp, MIX_TM, CHUNK)
    nc = tm // CHUNK
    ni = lp // tm

    def body(q_ref, k_ref, v_ref, g_ref, c_ref, s_ref, dm_ref, qd_ref, kd_ref, cd_ref, gn_ref,
             y_ref, st_ref, state):
        @pl.when(pl.program_id(1) == 0)
        def _():
            state[...] = jnp.zeros_like(state)

        dmv, qdv, kdv, cdv, gnv = dm_ref[0], qd_ref[0], kd_ref[0], cd_ref[0, 0:1, :], gn_ref[...]

        def chunk(c, carry):
            rows = pl.ds(pl.multiple_of(c * CHUNK, CHUNK), CHUNK)
            v = v_ref[rows, :]
            s0 = state[...]
            st_ref[c, 0, :, :] = s0
            _, kr, _, o = _ret_chunk_fwd(q_ref[rows, :], k_ref[rows, :], v, c_ref[rows, :], s_ref[rows, :],
                                         s0, dmv, qdv, kdv)
            state[...] = s0 * cdv + _dot_tn((kr * kdv).astype(BF16), v.astype(BF16))
            on, _ = _ln_stats(o)
            gg = g_ref[rows, :]
            y_ref[rows, :] = (gg * _sigmoid(gg) * (on * gnv)).astype(BF16)
            return carry

        lax.fori_loop(0, nc, chunk, 0)

    tile = lambda i: i
    tab3 = lambda n: pl.BlockSpec((1, n, LANE), lambda h, i: (h, 0, 0))
    return pl.pallas_call(
        body,
        grid=(RET_HEADS, ni),
        in_specs=_ret_specs(tm, q_blk)(tile) + [
            pl.BlockSpec((tm, LANE), lambda h, i: (i, 0)),
            pl.BlockSpec((tm, LANE), lambda h, i: (i, 0)),
            pl.BlockSpec((1, CHUNK, CHUNK), lambda h, i: (h, 0, 0)),
            tab3(CHUNK), tab3(CHUNK), tab3(8),
            pl.BlockSpec((1, LANE), lambda h, i: (0, h)),
        ],
        out_specs=[
            pl.BlockSpec((tm, LANE), lambda h, i: (i, h)),
            pl.BlockSpec((nc, 1, LANE, LANE), lambda h, i: (i, h, 0, 0)),
        ],
        out_shape=[
            jax.ShapeDtypeStruct((lp, RET_HEADS * LANE), BF16),
            jax.ShapeDtypeStruct((lp // CHUNK, RET_HEADS, LANE, LANE), F32),
        ],
        scratch_shapes=[pltpu.VMEM((LANE, LANE), F32)],
        compiler_params=_cparams(("parallel", "arbitrary")),
        name="ret_fwd",
    )(z, z, z, z, rope_c, rope_s, dm, qd, kd, cd, gn)


def _ret_bwd(z, dcat, states, rope_c, rope_s, dm, qd, kd, cd, gn, q_blk, dy_blk):
    lp = z.shape[0]
    tm = _pick_tm(lp, MIX_TM, CHUNK)
    nc = tm // CHUNK
    ni = lp // tm

    def body(q_ref, k_ref, v_ref, g_ref, dy_ref, st_ref, c_ref, s_ref, dm_ref, qd_ref, kd_ref, cd_ref, gn_ref,
             dq_ref, dk_ref, dv_ref, dgt_ref, dgn_ref, dstate):
        i = pl.program_id(1)
        t = ni - 1 - i

        @pl.when(i == 0)
        def _():
            dstate[...] = jnp.zeros_like(dstate)
            dgn_ref[...] = jnp.zeros_like(dgn_ref)

        dmv, qdv, kdv, cdv, gnv = dm_ref[0], qd_ref[0], kd_ref[0], cd_ref[0, 0:1, :], gn_ref[...]

        def chunk(cc, carry):
            c = nc - 1 - cc
            rows = pl.ds(pl.multiple_of(c * CHUNK, CHUNK), CHUNK)
            cos, sn = c_ref[rows, :], s_ref[rows, :]
            v = v_ref[rows, :]
            vb = v.astype(BF16)
            s0 = st_ref[c, 0, :, :]
            qr, kr, am, o = _ret_chunk_fwd(q_ref[rows, :], k_ref[rows, :], v, cos, sn, s0, dmv, qdv, kdv)
            on, rstd = _ln_stats(o)
            gg = g_ref[rows, :]
            sg = _sigmoid(gg)
            sl = gg * sg
            dyv = dy_ref[rows, :]
            dgn_ref[...] += _colsum(dyv * sl * on)
            keep = (lax.broadcasted_iota(jnp.int32, (CHUNK, 1), 0) + (t * tm + c * CHUNK)) >= PAD_ROWS
            dgt = dyv * (on * gnv) * (sg * (1.0 + gg * (1.0 - sg)))
            dgt_ref[rows, :] = jnp.where(keep, dgt, 0.0).astype(BF16)
            dob = _ln_bwd(dyv * sl, on, rstd, gnv).astype(BF16)
            ds1 = dstate[...]
            ds1b = ds1.astype(BF16)
            qdb = (qr * qdv).astype(BF16)
            kdb = (kr * kdv).astype(BF16)
            da = (_dot_nt(dob, vb) * dmv).astype(BF16)
            dv = _dot_tn(am.astype(BF16), dob) + _dot(kdb, ds1b)
            dqr = _dot(da, kr.astype(BF16)) + _dot_nt(dob, s0.astype(BF16)) * qdv
            dkr = _dot_tn(da, qr.astype(BF16)) + _dot_nt(vb, ds1b) * kdv
            dstate[...] = ds1 * cdv + _dot_tn(qdb, dob)
            dq = _rope_t(dqr, cos, sn)
            dk = _rope_t(dkr * (LANE ** -0.5), cos, sn)
            dq_ref[rows, :] = jnp.where(keep, dq, 0.0).astype(BF16)
            dk_ref[rows, :] = jnp.where(keep, dk, 0.0).astype(BF16)
            dv_ref[rows, :] = jnp.where(keep, dv, 0.0).astype(BF16)
            return carry

        lax.fori_loop(0, nc, chunk, 0)

    tile = lambda i: ni - 1 - i
    tab3 = lambda n: pl.BlockSpec((1, n, LANE), lambda h, i: (h, 0, 0))
    out_blk = pl.BlockSpec((tm, LANE), lambda h, i: (ni - 1 - i, h))
    out_sds = jax.ShapeDtypeStruct((lp, RET_HEADS * LANE), BF16)
    return pl.pallas_call(
        body,
        grid=(RET_HEADS, ni),
        in_specs=_ret_specs(tm, q_blk)(tile) + [
            pl.BlockSpec((tm, LANE), lambda h, i: (ni - 1 - i, dy_blk + h)),
            pl.BlockSpec((nc, 1, LANE, LANE), lambda h, i: (ni - 1 - i, h, 0, 0)),
            pl.BlockSpec((tm, LANE), lambda h, i: (ni - 1 - i, 0)),
            pl.BlockSpec((tm, LANE), lambda h, i: (ni - 1 - i, 0)),
            pl.BlockSpec((1, CHUNK, CHUNK), lambda h, i: (h, 0, 0)),
            tab3(CHUNK), tab3(CHUNK), tab3(8),
            pl.BlockSpec((1, LANE), lambda h, i: (0, h)),
        ],
        out_specs=[out_blk, out_blk, out_blk, out_blk, pl.BlockSpec((1, LANE), lambda h, i: (0, h))],
        out_shape=[out_sds, out_sds, out_sds, out_sds, jax.ShapeDtypeStruct((1, RET_HEADS * LANE), F32)],
        scratch_shapes=[pltpu.VMEM((LANE, LANE), F32)],
        compiler_params=_cparams(("parallel", "arbitrary")),
        name="ret_bwd",
    )(z, z, z, z, dcat, states, rope_c, rope_s, dm, qd, kd, cd, gn)


def _adamw(parts_list, w, m, v):
    nl, r, c = w.shape
    assert len(parts_list) == nl
    tr = _pick_tm(r, max(8, (1 << 17) // c), 8)
    nr = r // tr

    def body(*refs):
        p_refs = refs[:nl]
        w_ref, m_ref, v_ref, g_ref, d_ref, nm_ref, nv_ref = refs[nl:]
        layer = pl.program_id(0)
        for k in range(nl):
            @pl.when(layer == k)
            def _(k=k):
                g = p_refs[k][0]
                for j in range(1, N_DEV):
                    g = g + p_refs[k][j]
                m1 = ADAM_B1 * m_ref[...] + (1.0 - ADAM_B1) * g
                v1 = ADAM_B2 * v_ref[...] + (1.0 - ADAM_B2) * (g * g)
                m_hat = m1 / (1.0 - ADAM_B1 ** ADAM_STEP)
                v_hat = v1 / (1.0 - ADAM_B2 ** ADAM_STEP)
                g_ref[...] = g
                d_ref[...] = -ADAM_LR * (m_hat / (jnp.sqrt(v_hat) + ADAM_EPS) + ADAM_WD * w_ref[...])
                nm_ref[...] = m1
                nv_ref[...] = v1

    def parts_spec(k):
        return pl.BlockSpec((N_DEV, tr, c), lambda l, i: (0, jnp.where(l == k, i, jnp.where(l < k, 0, nr - 1)), 0))

    blk = pl.BlockSpec((None, tr, c), lambda l, i: (l, i, 0))
    sds = jax.ShapeDtypeStruct((nl, r, c), F32)
    return pl.pallas_call(
        body,
        grid=(nl, nr),
        in_specs=[parts_spec(k) for k in range(nl)] + [blk, blk, blk],
        out_specs=[blk, blk, blk, blk],
        out_shape=[sds, sds, sds, sds],
        compiler_params=_cparams(("arbitrary", "arbitrary")),
        name="adamw",
    )(*parts_list, w, m, v)


def _flip(v, bit):
    return 1 - v if bit else v


def _exchange(arrs, scatter):
    n = len(arrs)
    shapes = [tuple(a.shape[1:] if scatter else a.shape) for a in arrs]

    def body(*refs):
        x_refs, o_refs = refs[:n], refs[n:2 * n]
        send_sems, recv_sems, local_sems = refs[2 * n:]
        mx, my, mc = lax.axis_index("x"), lax.axis_index("y"), lax.axis_index("c")
        me = 4 * mx + 2 * my + mc

        def peer_of(k):
            return (_flip(mx, (k >> 2) & 1), _flip(my, (k >> 1) & 1), _flip(mc, k & 1))

        def copy(a, k):
            peer = peer_of(k)
            src = x_refs[a].at[4 * peer[0] + 2 * peer[1] + peer[2]] if scatter else x_refs[a]
            return pltpu.make_async_remote_copy(
                src_ref=src, dst_ref=o_refs[a].at[me], send_sem=send_sems.at[a, k - 1],
                recv_sem=recv_sems.at[a, k - 1], device_id=peer, device_id_type=pl.DeviceIdType.MESH)

        def arrival(a, k):
            peer = peer_of(k)
            slot = o_refs[a].at[4 * peer[0] + 2 * peer[1] + peer[2]]
            return pltpu.make_async_remote_copy(
                src_ref=slot, dst_ref=slot, send_sem=send_sems.at[a, k - 1], recv_sem=recv_sems.at[a, k - 1],
                device_id=peer, device_id_type=pl.DeviceIdType.MESH)

        locals_ = [pltpu.make_async_copy(x_refs[a].at[me] if scatter else x_refs[a], o_refs[a].at[me],
                                         local_sems.at[a]) for a in range(n)]
        sends = [copy(a, k) for a in range(n) for k in range(1, N_DEV)]
        for cp in locals_ + sends:
            cp.start()
        for a in range(n):
            for k in range(1, N_DEV):
                arrival(a, k).wait_recv()
        for cp in sends:
            cp.wait_send()
        for cp in locals_:
            cp.wait()

    hbm = pl.BlockSpec(memory_space=pltpu.HBM)
    return pl.pallas_call(
        body,
        in_specs=[hbm] * n,
        out_specs=[hbm] * n,
        out_shape=[jax.ShapeDtypeStruct((N_DEV,) + s, a.dtype) for s, a in zip(shapes, arrs)],
        scratch_shapes=[
            pltpu.SemaphoreType.DMA((n, N_DEV - 1)),
            pltpu.SemaphoreType.DMA((n, N_DEV - 1)),
            pltpu.SemaphoreType.DMA((n,)),
        ],
        name="reduce_scatter_parts" if scatter else "all_gather",
    )(*arrs)


PACK_ALIGN = 2048


def _padded(n):
    return -(-n // PACK_ALIGN) * PACK_ALIGN


def _pad_to(a, axis, size):
    pad = [(0, 0)] * a.ndim
    pad[axis] = (0, size - a.shape[axis])
    return jnp.pad(a, pad)


def _pack(arrs, lead=0):
    flat = []
    for a in arrs:
        v = a.reshape(a.shape[:lead] + (-1,))
        flat.append(_pad_to(v, lead, _padded(v.shape[lead])))
    out = jnp.concatenate(flat, axis=lead)
    return out.reshape(out.shape[:lead] + (-1, LANE))


def _unpack(slab, shapes, lead=0):
    flat = slab.reshape(slab.shape[:lead] + (-1,))
    out, off = [], 0
    for s in shapes:
        n = math.prod(s)
        out.append(flat[..., off:off + n].reshape(slab.shape[:lead] + tuple(s)))
        off += _padded(n)
    return out


def _unshard(parts, ax):
    return jnp.concatenate([parts[j] for j in range(N_DEV)], axis=ax)


def _to_shards(full, ax):
    n = full.shape[ax] // N_DEV
    return jnp.stack([lax.slice_in_dim(full, j * n, (j + 1) * n, axis=ax) for j in range(N_DEV)])


def _pad_halves(w, hp):
    h = w.shape[-1] // 2
    zeros = jnp.zeros(w.shape[:-1] + (hp - h,), w.dtype)
    return jnp.concatenate([w[..., :h], zeros, w[..., h:], zeros], axis=-1)


def _unpad_halves(w, h):
    hp = w.shape[-1] // 2
    return jnp.concatenate([w[..., :h], w[..., hp:hp + h]], axis=-1)


SMALL_SHARDED = (("meta", 1), ("conv_dw", 2), ("ln_g", 2), ("ln_b", 2))
MATMUL_WEIGHTS = ("ffn1_w13", "ffn1_w2", "w_in", "conv_pw", "w_out", "ffn2_w13", "ffn2_w2")
REPLICATED = ("ln_in_g", "ln_in_b", "pool_w", "pool_scale", "conv_db", "conv_ln_g", "conv_ln_b", "ret_gn_g")
WEIGHT_ORDER = ("meta", "ln_in_g", "ln_in_b", "ffn1_w13", "ffn1_w2", "w_in", "pool_w", "pool_scale", "conv_dw",
                "conv_db", "conv_ln_g", "conv_ln_b", "conv_pw", "ret_gn_g", "w_out", "ffn2_w13", "ffn2_w2",
                "ln_g", "ln_b")


def _retention_tables(lp, heads):
    pos = jnp.arange(lp, dtype=F32) - PAD_ROWS
    inv_freq = ROPE_BASE ** (-jnp.arange(0, LANE, 2, dtype=F32) / LANE)
    ang = pos[:, None] * inv_freq[None, :]
    cos, sin = jnp.cos(ang), jnp.sin(ang)
    rope_c = jnp.concatenate([cos, cos], axis=1)
    rope_s = jnp.concatenate([-sin, sin], axis=1)
    log_gamma = jnp.log(1.0 - 2.0 ** (-5.0 - jnp.arange(heads, dtype=F32)))
    i = jnp.arange(CHUNK, dtype=F32)
    dm = jnp.exp(log_gamma[:, None, None] * jnp.abs(i[:, None] - i[None, :]))
    lanes = lambda t: jnp.broadcast_to(t[:, :, None], t.shape + (LANE,))
    qd = lanes(jnp.exp(log_gamma[:, None] * (i + 1.0)))
    kd = lanes(jnp.exp(log_gamma[:, None] * (CHUNK - 1.0 - i)))
    cd = lanes(jnp.broadcast_to(jnp.exp(log_gamma * CHUNK)[:, None], (heads, 8)))
    return rope_c, rope_s, dm, qd, kd, cd


def _block_diag(w):
    g, n, _ = w.shape
    rows = []
    for i in range(g):
        rows.append(jnp.concatenate([w[i] if j == i else jnp.zeros((n, n), w.dtype) for j in range(g)], axis=1))
    return jnp.concatenate(rows, axis=0)


def kernel(x, meta, ln_in_g, ln_in_b, ffn1_w13, ffn1_w2, w_in, pool_w, pool_scale, conv_dw, conv_db, conv_ln_g, conv_ln_b, conv_pw, ret_gn_g, w_out, ffn2_w13, ffn2_w2, ln_g, ln_b, loss_target, m_meta, m_ln_in_g, m_ln_in_b, m_ffn1_w13, m_ffn1_w2, m_w_in, m_pool_w, m_pool_scale, m_conv_dw, m_conv_db, m_conv_ln_g, m_conv_ln_b, m_conv_pw, m_ret_gn_g, m_w_out, m_ffn2_w13, m_ffn2_w2, m_ln_g, m_ln_b, v_meta, v_ln_in_g, v_ln_in_b, v_ffn1_w13, v_ffn1_w2, v_w_in, v_pool_w, v_pool_scale, v_conv_dw, v_conv_db, v_conv_ln_g, v_conv_ln_b, v_conv_pw, v_ret_gn_g, v_w_out, v_ffn2_w13, v_ffn2_w2, v_ln_g, v_ln_b):
    local = dict(meta=meta, ln_in_g=ln_in_g, ln_in_b=ln_in_b, ffn1_w13=ffn1_w13, ffn1_w2=ffn1_w2, w_in=w_in,
                 pool_w=pool_w, pool_scale=pool_scale, conv_dw=conv_dw, conv_db=conv_db, conv_ln_g=conv_ln_g,
                 conv_ln_b=conv_ln_b, conv_pw=conv_pw, ret_gn_g=ret_gn_g, w_out=w_out, ffn2_w13=ffn2_w13,
                 ffn2_w2=ffn2_w2, ln_g=ln_g, ln_b=ln_b)
    mom1 = dict(meta=m_meta, ln_in_g=m_ln_in_g, ln_in_b=m_ln_in_b, ffn1_w13=m_ffn1_w13, ffn1_w2=m_ffn1_w2,
                w_in=m_w_in, pool_w=m_pool_w, pool_scale=m_pool_scale, conv_dw=m_conv_dw, conv_db=m_conv_db,
                conv_ln_g=m_conv_ln_g, conv_ln_b=m_conv_ln_b, conv_pw=m_conv_pw, ret_gn_g=m_ret_gn_g,
                w_out=m_w_out, ffn2_w13=m_ffn2_w13, ffn2_w2=m_ffn2_w2, ln_g=m_ln_g, ln_b=m_ln_b)
    mom2 = dict(meta=v_meta, ln_in_g=v_ln_in_g, ln_in_b=v_ln_in_b, ffn1_w13=v_ffn1_w13, ffn1_w2=v_ffn1_w2,
                w_in=v_w_in, pool_w=v_pool_w, pool_scale=v_pool_scale, conv_dw=v_conv_dw, conv_db=v_conv_db,
                conv_ln_g=v_conv_ln_g, conv_ln_b=v_conv_ln_b, conv_pw=v_conv_pw, ret_gn_g=v_ret_gn_g,
                w_out=v_w_out, ffn2_w13=v_ffn2_w13, ffn2_w2=v_ffn2_w2, ln_g=v_ln_g, ln_b=v_ln_b)

    depth = ffn1_w13.shape[0]
    alpha = (2.0 * depth) ** 0.25
    seq, d = x.shape[1], x.shape[2]
    lp = FRONT + seq
    h_loc = ffn1_w2.shape[1]
    hp = -(-h_loc // LANE) * LANE
    c_pool = pool_scale.shape[1]
    c_conv = conv_db.shape[1]
    q_blk = (c_pool + 2 * c_conv) // LANE
    dy_blk = (c_pool + c_conv) // LANE
    heads = ret_gn_g.shape[1] // LANE
    assert meta.shape[0] == N_META and heads == RET_HEADS and conv_dw.shape[1] == CONV_WIDTH
    assert ffn1_w13.shape[2] == 2 * h_loc

    def to_wire(name, w):
        if name in ("ffn1_w13", "ffn2_w13"):
            return _pad_halves(w, hp)
        if name in ("ffn1_w2", "ffn2_w2"):
            return _pad_to(w, w.ndim - 2, hp)
        return w

    def from_wire(name, w):
        if name in ("ffn1_w13", "ffn2_w13"):
            return _unpad_halves(w, h_loc)
        if name in ("ffn1_w2", "ffn2_w2"):
            return w[..., :h_loc, :]
        return w

    small_names = [n for n, _ in SMALL_SHARDED]
    small_ax = dict(SMALL_SHARDED)
    n_mm = len(MATMUL_WEIGHTS)
    wire = [to_wire(n, local[n][l]).astype(BF16) for l in range(depth) for n in MATMUL_WEIGHTS]
    gathered = _exchange(wire + [_pack([local[n] for n in small_names])], False)
    small_parts = _unpack(gathered[-1], [local[n].shape for n in small_names], 1)
    small_full = {n: _unshard(p, small_ax[n]) for n, p in zip(small_names, small_parts)}

    row = lambda v: v.reshape(1, -1)
    ln_params = [(row(ln_in_g), row(ln_in_b))]
    for l in range(depth):
        ln_params += [(row(small_full["ln_g"][l, j]), row(small_full["ln_b"][l, j])) for j in range(3)]
    rope_c, rope_s, dm, qd, kd, cd = _retention_tables(lp, heads)

    layers = []
    for l in range(depth):
        gw = dict(zip(MATMUL_WEIGHTS, gathered[l * n_mm:(l + 1) * n_mm]))
        layers.append(dict(
            ffn1=(gw["ffn1_w13"], gw["ffn1_w2"]),
            ffn2=(gw["ffn2_w13"], gw["ffn2_w2"]),
            w_in=_unshard(gw["w_in"], 1),
            w_out=gw["w_out"].reshape(-1, d),
            pool=(_block_diag(pool_w[l]).astype(BF16), row(pool_scale[l])),
            conv=(_pad_to(small_full["conv_dw"][l], 0, CONV_HALO), row(conv_db[l]), row(conv_ln_g[l]),
                  row(conv_ln_b[l]), gw["conv_pw"].reshape(c_conv, c_conv)),
            gn=row(ret_gn_g[l]),
        ))

    r = [jnp.concatenate([jnp.zeros((PAD_ROWS, d), F32), small_full["meta"], x[0]], axis=0)]
    saved = []
    for l, p in enumerate(layers):
        k = 3 * l
        r1, a1, u1 = _ffn_fwd(r[k], *ln_params[k], *p["ffn1"], alpha)
        z = _proj_fwd(r1, *ln_params[k + 1], p["w_in"])
        y_pool = _pool_fwd(z, *p["pool"])
        y_conv = _conv_fwd(z, *p["conv"])
        y_ret, states = _ret_fwd(z, rope_c, rope_s, dm, qd, kd, cd, p["gn"], q_blk)
        cat = jnp.concatenate([y_pool, y_conv, y_ret], axis=1)
        r2 = _out_fwd(r1, *ln_params[k + 1], cat, p["w_out"], alpha)
        r3, a2, u2 = _ffn_fwd(r2, *ln_params[k + 2], *p["ffn2"], alpha)
        r += [r1, r2, r3]
        saved.append((a1, u1, z, states, cat, a2, u2))

    target = jnp.concatenate([jnp.zeros((FRONT, d), F32), loss_target[0]], axis=0)
    dr, dg, db, loss_cols = _loss_bwd(r[-1], target, *ln_params[-1])
    loss = lax.psum(0.5 * jnp.sum(loss_cols) / d, MESH_AXES)
    ln_grads = {3 * depth: (dg, db)}
    w13_slot = lambda j: (j % 2) * (N_DEV // 2) + j // 2
    g_parts = [None] * depth
    g_rep = [None] * depth

    def ffn_grads(dr_out, r_in, a, u, ln_p, wts):
        dr_in, hb, s, dz, dg, db = _ffn_bwd(dr_out, r_in, a, u, *ln_p, *wts, alpha)
        g13 = _mm_tn(hb, dz, bn=2 * hp, slot_of=w13_slot)
        g2 = _mm_tn(s, dr_out, 0.5).reshape(N_DEV, hp, d)
        return dr_in, (dg, db), g13, g2

    for l in reversed(range(depth)):
        p = layers[l]
        k = 3 * l
        a1, u1, z, states, cat, a2, u2 = saved[l]
        gp, gr = {}, {}
        dr, ln_grads[k + 2], gp["ffn2_w13"], gp["ffn2_w2"] = ffn_grads(dr, r[k + 2], a2, u2, ln_params[k + 2], p["ffn2"])

        dr2 = dr
        dcat = _mm_nt(dr2, p["w_out"])
        gp["w_out"] = _mm_tn(cat, dr2).reshape(N_DEV, -1, d)
        dxp, dwbd, gr["pool_scale"] = _pool_bwd(z, dcat, *p["pool"])
        dca, dcg, ddw, gr["conv_db"], gr["conv_ln_g"], gr["conv_ln_b"], dpw = _conv_bwd(z, dcat, *p["conv"])
        dq, dk, dv, dgt, gr["ret_gn_g"] = _ret_bwd(z, dcat, states, rope_c, rope_s, dm, qd, kd, cd, p["gn"],
                                                  q_blk, dy_blk)
        dz = jnp.concatenate([dxp, dca, dcg, dq, dk, dv, dgt], axis=1)
        dr, hb, dg, db = _in_bwd(dr2, r[k + 1], dz, *ln_params[k + 1], p["w_in"], alpha)
        ln_grads[k + 1] = (dg, db)
        gp["w_in"] = _to_shards(_mm_tn(hb, dz), 1)
        gp["conv_pw"] = dpw.reshape(N_DEV, -1, c_conv)
        n_grp, grp = pool_w.shape[1], pool_w.shape[2]
        gr["pool_w"] = jnp.stack([dwbd[i * grp:(i + 1) * grp, i * grp:(i + 1) * grp] for i in range(n_grp)])
        gr["conv_dw"] = ddw[:CONV_WIDTH]

        dr, ln_grads[k], gp["ffn1_w13"], gp["ffn1_w2"] = ffn_grads(dr, r[k], a1, u1, ln_params[k], p["ffn1"])
        g_parts[l], g_rep[l] = gp, gr

    grad_x = dr[FRONT:][None]
    local_shape = lambda name: ((CONV_WIDTH, c_conv) if name == "conv_dw" else local[name].shape[1:])
    stack_layers = lambda name: jnp.stack([g_rep[l][name].reshape(local_shape(name)) for l in range(depth)])
    small_grad = dict(
        meta=dr[PAD_ROWS:FRONT],
        conv_dw=stack_layers("conv_dw"),
        ln_g=jnp.stack([jnp.stack([ln_grads[3 * l + j + 1][0][0] for j in range(3)]) for l in range(depth)]),
        ln_b=jnp.stack([jnp.stack([ln_grads[3 * l + j + 1][1][0] for j in range(3)]) for l in range(depth)]),
    )
    rep_grad = {n: stack_layers(n) for n in REPLICATED if n not in ("ln_in_g", "ln_in_b")}
    rep_grad["ln_in_g"], rep_grad["ln_in_b"] = ln_grads[0][0][0], ln_grads[0][1][0]

    small_pack8 = _pack([_to_shards(small_grad[n], small_ax[n]) for n in small_names], 1)
    scattered = _exchange([g_parts[l][n] for l in range(depth) for n in MATMUL_WEIGHTS] + [small_pack8], True)
    rep_parts = _exchange([_pack([rep_grad[n] for n in REPLICATED])], False)[0]

    outs_by_name = {}
    for i, n in enumerate(MATMUL_WEIGHTS):
        parts_list = [scattered[l * n_mm + i] for l in range(depth)]
        res = _adamw(parts_list, *[to_wire(n, src[n]) for src in (local, mom1, mom2)])
        outs_by_name[n] = [from_wire(n, t) for t in res]
    for names, parts in ((small_names, scattered[-1]), (REPLICATED, rep_parts)):
        shapes = [local[n].shape for n in names]
        res = _adamw([parts], *[_pack([src[n] for n in names])[None] for src in (local, mom1, mom2)])
        for n, vals in zip(names, zip(*[_unpack(t[0], shapes) for t in res])):
            outs_by_name[n] = list(vals)

    outs = [loss, grad_x]
    for kind in range(4):
        outs += [outs_by_name[n][kind] for n in WEIGHT_ORDER]
    return tuple(outs)
```

```python
import math

import jax
import jax.numpy as jnp
from jax import lax
from jax.experimental import pallas as pl
from jax.experimental.pallas import tpu as pltpu

F32 = jnp.float32
BF16 = jnp.bfloat16
WIRE = jnp.bfloat16

N_DEV = 8
MESH_AXES = ("x", "y", "c")
CHUNK = 64
N_META = 16
PAD_ROWS = 240
FRONT = PAD_ROWS + N_META
LN_EPS = 1e-5
LANE = 128
POOL_WINDOWS = (2, 4, 8, 16)
CONV_WIDTH = 31
CONV_HALO = 32
POOL_HALO = 16
RET_HEADS = 4
ROPE_BASE = 10000.0

ADAM_LR = 0.001
ADAM_B1 = 0.9
ADAM_B2 = 0.999
ADAM_EPS = 1e-08
ADAM_WD = 0.01
ADAM_STEP = 10

VMEM_LIMIT = 56 * 1024 * 1024
DENSE_TM_FWD = 704
DENSE_TM_BWD = 528
MIX_TM = 768


def _cparams(sem):
    return pltpu.CompilerParams(dimension_semantics=sem, vmem_limit_bytes=VMEM_LIMIT)


def _pick_tm(lp, target, mult=16):
    best = None
    for t in range(mult, min(lp, target) + 1, mult):
        if lp % t == 0:
            best = t
    assert best is not None, (lp, target, mult)
    return best


def _dot(a, b):
    return jnp.dot(a, b, preferred_element_type=F32)


def _dot_nt(a, b):
    return lax.dot_general(a, b, (((1,), (1,)), ((), ())), preferred_element_type=F32)


def _dot_tn(a, b):
    return lax.dot_general(a, b, (((0,), (0,)), ((), ())), preferred_element_type=F32)


def _sigmoid(x):
    return 1.0 / (1.0 + jnp.exp(-x))


def _ln_stats(r):
    mu = jnp.mean(r, axis=-1, keepdims=True)
    xc = r - mu
    var = jnp.mean(xc * xc, axis=-1, keepdims=True)
    rstd = lax.rsqrt(var + LN_EPS)
    return xc * rstd, rstd


def _ln_bwd(dh, xhat, rstd, g):
    dxh = dh * g
    m1 = jnp.mean(dxh, axis=-1, keepdims=True)
    m2 = jnp.mean(dxh * xhat, axis=-1, keepdims=True)
    return rstd * (dxh - m1 - xhat * m2)


def _colsum(x):
    return jnp.sum(x, axis=0, keepdims=True)


def _row_ids(tile, tm):
    return lax.broadcasted_iota(jnp.int32, (tm, 1), 0) + tile * tm


def _ffn_fwd(r_prev, g, b, w13g, w2g, alpha):
    lp, d = r_prev.shape
    nf = N_DEV // 2
    fc = w13g.shape[2]
    w2c = w2g.reshape(nf, fc, d)
    tm = _pick_tm(lp, DENSE_TM_FWD)

    def body(r_ref, g_ref, b_ref, w1_ref, w3_ref, w2_ref, out_ref, a_ref, u_ref, hb, acc):
        f = pl.program_id(1)

        @pl.when(f == 0)
        def _():
            xhat, _ = _ln_stats(r_ref[...])
            hb[...] = (xhat * g_ref[...] + b_ref[...]).astype(BF16)
            acc[...] = jnp.zeros_like(acc)

        h = hb[...]
        a = _dot(h, w1_ref[...])
        u = _dot(h, w3_ref[...])
        a_ref[...] = a.astype(BF16)
        u_ref[...] = u.astype(BF16)
        s = a * _sigmoid(a) * u
        acc[...] += _dot(s.astype(BF16), w2_ref[...])

        @pl.when(f == nf - 1)
        def _():
            xhat, _ = _ln_stats(r_ref[...])
            out_ref[...] = alpha * (xhat * g_ref[...] + b_ref[...]) + 0.5 * acc[...]

    return pl.pallas_call(
        body,
        grid=(lp // tm, nf),
        in_specs=[
            pl.BlockSpec((tm, d), lambda i, f: (i, 0)),
            pl.BlockSpec((1, d), lambda i, f: (0, 0)),
            pl.BlockSpec((1, d), lambda i, f: (0, 0)),
            pl.BlockSpec((None, d, fc), lambda i, f: (f, 0, 0)),
            pl.BlockSpec((None, d, fc), lambda i, f: (nf + f, 0, 0)),
            pl.BlockSpec((None, fc, d), lambda i, f: (f, 0, 0)),
        ],
        out_specs=[
            pl.BlockSpec((tm, d), lambda i, f: (i, 0)),
            pl.BlockSpec((tm, fc), lambda i, f: (i, f)),
            pl.BlockSpec((tm, fc), lambda i, f: (i, f)),
        ],
        out_shape=[
            jax.ShapeDtypeStruct((lp, d), F32),
            jax.ShapeDtypeStruct((lp, nf * fc), BF16),
            jax.ShapeDtypeStruct((lp, nf * fc), BF16),
        ],
        scratch_shapes=[pltpu.VMEM((tm, d), BF16), pltpu.VMEM((tm, d), F32)],
        compiler_params=_cparams(("parallel", "arbitrary")),
        name="ffn_fwd",
    )(r_prev, g, b, w13g, w13g, w2c)


def _ffn_bwd(dr_next, r_prev, a, u, g, b, w13g, w2g, alpha):
    lp, d = r_prev.shape
    nf = N_DEV // 2
    fc = w13g.shape[2]
    w2c = w2g.reshape(nf, fc, d)
    tm = _pick_tm(lp, DENSE_TM_BWD)

    def body(dr_ref, r_ref, a_ref, u_ref, g_ref, b_ref, w1_ref, w3_ref, w2_ref,
             drp_ref, hb_ref, s_ref, dz_ref, dg_ref, db_ref, dyb, dhacc):
        i = pl.program_id(0)
        f = pl.program_id(1)

        @pl.when(jnp.logical_and(i == 0, f == 0))
        def _():
            dg_ref[...] = jnp.zeros_like(dg_ref)
            db_ref[...] = jnp.zeros_like(db_ref)

        @pl.when(f == 0)
        def _():
            dyb[...] = (0.5 * dr_ref[...]).astype(BF16)
            dhacc[...] = jnp.zeros_like(dhacc)
            xhat, _ = _ln_stats(r_ref[...])
            hb_ref[...] = (xhat * g_ref[...] + b_ref[...]).astype(BF16)

        ds = _dot_nt(dyb[...], w2_ref[...])
        av = a_ref[...].astype(F32)
        uv = u_ref[...].astype(F32)
        sig = _sigmoid(av)
        sl = av * sig
        da = (ds * uv * (sig * (1.0 + av * (1.0 - sig)))).astype(BF16)
        du = (ds * sl).astype(BF16)
        s_ref[...] = (sl * uv).astype(BF16)
        dz_ref[:, 0:fc] = da
        dz_ref[:, fc:2 * fc] = du
        dhacc[...] += _dot_nt(da, w1_ref[...]) + _dot_nt(du, w3_ref[...])

        @pl.when(f == nf - 1)
        def _():
            dh = alpha * dr_ref[...] + dhacc[...]
            xhat, rstd = _ln_stats(r_ref[...])
            drp_ref[...] = _ln_bwd(dh, xhat, rstd, g_ref[...])
            dg_ref[...] += _colsum(dh * xhat)
            db_ref[...] += _colsum(dh)

    row = lambda i, f: (i, 0)
    const = lambda i, f: (0, 0)
    chunk = lambda i, f: (i, f)
    return pl.pallas_call(
        body,
        grid=(lp // tm, nf),
        in_specs=[
            pl.BlockSpec((tm, d), row),
            pl.BlockSpec((tm, d), row),
            pl.BlockSpec((tm, fc), chunk),
            pl.BlockSpec((tm, fc), chunk),
            pl.BlockSpec((1, d), const),
            pl.BlockSpec((1, d), const),
            pl.BlockSpec((None, d, fc), lambda i, f: (f, 0, 0)),
            pl.BlockSpec((None, d, fc), lambda i, f: (nf + f, 0, 0)),
            pl.BlockSpec((None, fc, d), lambda i, f: (f, 0, 0)),
        ],
        out_specs=[
            pl.BlockSpec((tm, d), row),
            pl.BlockSpec((tm, d), row),
            pl.BlockSpec((tm, fc), chunk),
            pl.BlockSpec((tm, 2 * fc), chunk),
            pl.BlockSpec((1, d), const),
            pl.BlockSpec((1, d), const),
        ],
        out_shape=[
            jax.ShapeDtypeStruct((lp, d), F32),
            jax.ShapeDtypeStruct((lp, d), BF16),
            jax.ShapeDtypeStruct((lp, nf * fc), BF16),
            jax.ShapeDtypeStruct((lp, 2 * nf * fc), BF16),
            jax.ShapeDtypeStruct((1, d), F32),
            jax.ShapeDtypeStruct((1, d), F32),
        ],
        scratch_shapes=[pltpu.VMEM((tm, d), BF16), pltpu.VMEM((tm, d), F32)],
        compiler_params=_cparams(("arbitrary", "arbitrary")),
        name="ffn_bwd",
    )(dr_next, r_prev, a, u, g, b, w13g, w13g, w2c)


def _mm_tn(a, b, scale=1.0, bn=None, slot_of=None):
    t, m = a.shape
    n = b.shape[1]
    if bn is None:
        bn = n if n <= 1408 else _pick_tm(n, 1408, LANE)
    bm = m if m <= 1536 else _pick_tm(m, 1536, LANE)
    tk = _pick_tm(t, 768)
    nt = t // tk

    def body(a_ref, b_ref, o_ref, acc):
        k = pl.program_id(2)

        @pl.when(k == 0)
        def _():
            acc[...] = jnp.zeros_like(acc)

        acc[...] += _dot_tn(a_ref[...].astype(BF16), b_ref[...].astype(BF16))

        @pl.when(k == nt - 1)
        def _():
            o_ref[...] = (acc[...] * scale).astype(o_ref.dtype)

    if slot_of is None:
        out_spec = pl.BlockSpec((bm, bn), lambda i, j, k: (i, j))
        out_shape = jax.ShapeDtypeStruct((m, n), WIRE)
    else:
        out_spec = pl.BlockSpec((None, bm, bn), lambda i, j, k: (slot_of(j), i, 0))
        out_shape = jax.ShapeDtypeStruct((n // bn, m, bn), WIRE)
    return pl.pallas_call(
        body,
        grid=(m // bm, n // bn, nt),
        in_specs=[
            pl.BlockSpec((tk, bm), lambda i, j, k: (k, i)),
            pl.BlockSpec((tk, bn), lambda i, j, k: (k, j)),
        ],
        out_specs=out_spec,
        out_shape=out_shape,
        scratch_shapes=[pltpu.VMEM((bm, bn), F32)],
        compiler_params=_cparams(("parallel", "parallel", "arbitrary")),
        name="mm_tn",
    )(a, b)


def _proj_fwd(r_prev, g, b, w_in):
    lp, d = r_prev.shape
    n = w_in.shape[1]
    tm = _pick_tm(lp, DENSE_TM_FWD)

    def body(r_ref, g_ref, b_ref, w_ref, z_ref):
        xhat, _ = _ln_stats(r_ref[...])
        h = (xhat * g_ref[...] + b_ref[...]).astype(BF16)
        z = _dot(h, w_ref[...])
        rows = _row_ids(pl.program_id(0), tm)
        z_ref[...] = jnp.where(rows >= PAD_ROWS, z, 0.0)

    return pl.pallas_call(
        body,
        grid=(lp // tm,),
        in_specs=[
            pl.BlockSpec((tm, d), lambda i: (i, 0)),
            pl.BlockSpec((1, d), lambda i: (0, 0)),
            pl.BlockSpec((1, d), lambda i: (0, 0)),
            pl.BlockSpec((d, n), lambda i: (0, 0)),
        ],
        out_specs=pl.BlockSpec((tm, n), lambda i: (i, 0)),
        out_shape=jax.ShapeDtypeStruct((lp, n), F32),
        compiler_params=_cparams(("parallel",)),
        name="proj_fwd",
    )(r_prev, g, b, w_in)


def _out_fwd(r_prev, g, b, cat, w_out, alpha):
    lp, d = r_prev.shape
    k = cat.shape[1]
    tm = _pick_tm(lp, DENSE_TM_FWD)

    def body(r_ref, g_ref, b_ref, c_ref, w_ref, o_ref):
        xhat, _ = _ln_stats(r_ref[...])
        o_ref[...] = alpha * (xhat * g_ref[...] + b_ref[...]) + _dot(c_ref[...], w_ref[...])

    return pl.pallas_call(
        body,
        grid=(lp // tm,),
        in_specs=[
            pl.BlockSpec((tm, d), lambda i: (i, 0)),
            pl.BlockSpec((1, d), lambda i: (0, 0)),
            pl.BlockSpec((1, d), lambda i: (0, 0)),
            pl.BlockSpec((tm, k), lambda i: (i, 0)),
            pl.BlockSpec((k, d), lambda i: (0, 0)),
        ],
        out_specs=pl.BlockSpec((tm, d), lambda i: (i, 0)),
        out_shape=jax.ShapeDtypeStruct((lp, d), F32),
        compiler_params=_cparams(("parallel",)),
        name="out_fwd",
    )(r_prev, g, b, cat, w_out)


def _mm_nt(x, w):
    lp, n = x.shape
    k = w.shape[0]
    tm = _pick_tm(lp, DENSE_TM_FWD)

    def body(x_ref, w_ref, o_ref):
        o_ref[...] = _dot_nt(x_ref[...].astype(BF16), w_ref[...])

    return pl.pallas_call(
        body,
        grid=(lp // tm,),
        in_specs=[pl.BlockSpec((tm, n), lambda i: (i, 0)), pl.BlockSpec((k, n), lambda i: (0, 0))],
        out_specs=pl.BlockSpec((tm, k), lambda i: (i, 0)),
        out_shape=jax.ShapeDtypeStruct((lp, k), F32),
        compiler_params=_cparams(("parallel",)),
        name="mm_nt",
    )(x, w)


def _in_bwd(dr_next, r_prev, dz, g, b, w_in, alpha):
    lp, d = r_prev.shape
    n = w_in.shape[1]
    tm = _pick_tm(lp, DENSE_TM_FWD)

    def body(dr_ref, r_ref, dz_ref, g_ref, b_ref, w_ref, drp_ref, hb_ref, dg_ref, db_ref):
        @pl.when(pl.program_id(0) == 0)
        def _():
            dg_ref[...] = jnp.zeros_like(dg_ref)
            db_ref[...] = jnp.zeros_like(db_ref)

        dh = alpha * dr_ref[...] + _dot_nt(dz_ref[...], w_ref[...])
        xhat, rstd = _ln_stats(r_ref[...])
        hb_ref[...] = (xhat * g_ref[...] + b_ref[...]).astype(BF16)
        drp_ref[...] = _ln_bwd(dh, xhat, rstd, g_ref[...])
        dg_ref[...] += _colsum(dh * xhat)
        db_ref[...] += _colsum(dh)

    row = lambda i: (i, 0)
    const = lambda i: (0, 0)
    return pl.pallas_call(
        body,
        grid=(lp // tm,),
        in_specs=[
            pl.BlockSpec((tm, d), row),
            pl.BlockSpec((tm, d), row),
            pl.BlockSpec((tm, n), row),
            pl.BlockSpec((1, d), const),
            pl.BlockSpec((1, d), const),
            pl.BlockSpec((d, n), const),
        ],
        out_specs=[
            pl.BlockSpec((tm, d), row),
            pl.BlockSpec((tm, d), row),
            pl.BlockSpec((1, d), const),
            pl.BlockSpec((1, d), const),
        ],
        out_shape=[
            jax.ShapeDtypeStruct((lp, d), F32),
            jax.ShapeDtypeStruct((lp, d), BF16),
            jax.ShapeDtypeStruct((1, d), F32),
            jax.ShapeDtypeStruct((1, d), F32),
        ],
        compiler_params=_cparams(("arbitrary",)),
        name="in_bwd",
    )(dr_next, r_prev, dz, g, b, w_in)


def _loss_bwd(r_last, target, g, b):
    lp, d = r_last.shape
    tm = _pick_tm(lp, DENSE_TM_FWD)

    def body(r_ref, t_ref, g_ref, b_ref, dr_ref, dg_ref, db_ref, ls_ref):
        i = pl.program_id(0)

        @pl.when(i == 0)
        def _():
            dg_ref[...] = jnp.zeros_like(dg_ref)
            db_ref[...] = jnp.zeros_like(db_ref)
            ls_ref[...] = jnp.zeros_like(ls_ref)

        xhat, rstd = _ln_stats(r_ref[...])
        y = xhat * g_ref[...] + b_ref[...]
        rows = _row_ids(i, tm)
        err = jnp.where(rows >= FRONT, y - t_ref[...], 0.0)
        ls_ref[...] += _colsum(err * err)
        dy = err * (1.0 / d)
        dr_ref[...] = _ln_bwd(dy, xhat, rstd, g_ref[...])
        dg_ref[...] += _colsum(dy * xhat)
        db_ref[...] += _colsum(dy)

    row = lambda i: (i, 0)
    const = lambda i: (0, 0)
    return pl.pallas_call(
        body,
        grid=(lp // tm,),
        in_specs=[
            pl.BlockSpec((tm, d), row),
            pl.BlockSpec((tm, d), row),
            pl.BlockSpec((1, d), const),
            pl.BlockSpec((1, d), const),
        ],
        out_specs=[
            pl.BlockSpec((tm, d), row),
            pl.BlockSpec((1, d), const),
            pl.BlockSpec((1, d), const),
            pl.BlockSpec((1, d), const),
        ],
        out_shape=[
            jax.ShapeDtypeStruct((lp, d), F32),
            jax.ShapeDtypeStruct((1, d), F32),
            jax.ShapeDtypeStruct((1, d), F32),
            jax.ShapeDtypeStruct((1, d), F32),
        ],
        compiler_params=_cparams(("arbitrary",)),
        name="loss_bwd",
    )(r_last, target, g, b)


def _pool_counts(tile, tm, width):
    pos = _row_ids(tile, tm) - PAD_ROWS
    lane = lax.broadcasted_iota(jnp.int32, (1, width), 1)
    group = width // len(POOL_WINDOWS)
    win = jnp.full((1, width), POOL_WINDOWS[-1], jnp.int32)
    for gi in range(len(POOL_WINDOWS) - 2, -1, -1):
        win = jnp.where(lane < (gi + 1) * group, POOL_WINDOWS[gi], win)
    cnt = jnp.clip(pos + 1, 1, win)
    return cnt.astype(F32), lane, group


def _pool_select(sums, lane, group):
    out = sums[-1]
    for gi in range(len(POOL_WINDOWS) - 2, -1, -1):
        out = jnp.where(lane < (gi + 1) * group, sums[gi], out)
    return out


def _pool_window_sums(ext, tm, sign):
    base = POOL_HALO if sign < 0 else 0
    acc = ext[pl.ds(base, tm), :]
    sums, k = [], 1
    for w in POOL_WINDOWS:
        while k < w:
            acc = acc + ext[pl.ds(base + sign * k, tm), :]
            k += 1
        sums.append(acc)
    return sums


def _pool_fwd(z, wbd, scale):
    lp = z.shape[0]
    c = wbd.shape[0]
    tm = _pick_tm(lp, MIX_TM, CHUNK)

    def body(x_ref, w_ref, s_ref, y_ref, ext):
        i = pl.program_id(0)

        @pl.when(i == 0)
        def _():
            ext[0:POOL_HALO, :] = jnp.zeros((POOL_HALO, c), F32)

        x = x_ref[...]
        ext[POOL_HALO:POOL_HALO + tm, :] = x
        cnt, lane, group = _pool_counts(i, tm, c)
        sums = _pool_window_sums(ext, tm, -1)
        y = _pool_select(sums, lane, group) / cnt - x
        ext[0:POOL_HALO, :] = ext[tm:tm + POOL_HALO, :]
        y_ref[...] = (_dot(y.astype(BF16), w_ref[...]) * s_ref[...]).astype(BF16)

    return pl.pallas_call(
        body,
        grid=(lp // tm,),
        in_specs=[
            pl.BlockSpec((tm, c), lambda i: (i, 0)),
            pl.BlockSpec((c, c), lambda i: (0, 0)),
            pl.BlockSpec((1, c), lambda i: (0, 0)),
        ],
        out_specs=pl.BlockSpec((tm, c), lambda i: (i, 0)),
        out_shape=jax.ShapeDtypeStruct((lp, c), BF16),
        scratch_shapes=[pltpu.VMEM((tm + POOL_HALO, c), F32)],
        compiler_params=_cparams(("arbitrary",)),
        name="pool_fwd",
    )(z, wbd, scale)


def _pool_bwd(z, dcat, wbd, scale):
    lp = z.shape[0]
    c = wbd.shape[0]
    tm = _pick_tm(lp, MIX_TM, CHUNK)
    ni = lp // tm
    hpt = tm // POOL_HALO

    def body(x_ref, xh_ref, dy_ref, w_ref, s_ref, dx_ref, dw_ref, ds_ref, ext, ext2):
        i = pl.program_id(0)
        t = ni - 1 - i

        @pl.when(i == 0)
        def _():
            dw_ref[...] = jnp.zeros_like(dw_ref)
            ds_ref[...] = jnp.zeros_like(ds_ref)
            ext2[tm:tm + POOL_HALO, :] = jnp.zeros((POOL_HALO, c), F32)

        x = x_ref[...]
        ext[0:POOL_HALO, :] = jnp.where(t > 0, xh_ref[...], 0.0)
        ext[POOL_HALO:POOL_HALO + tm, :] = x
        cnt, lane, group = _pool_counts(t, tm, c)
        y = (_pool_select(_pool_window_sums(ext, tm, -1), lane, group) / cnt - x).astype(BF16)
        w = w_ref[...]
        dyv = dy_ref[...]
        ds_ref[...] += _colsum(_dot(y, w) * dyv)
        do = (dyv * s_ref[...]).astype(BF16)
        dw_ref[...] += _dot_tn(y, do)
        dyp = _dot_nt(do, w)
        ext2[0:tm, :] = dyp / cnt
        dx = _pool_select(_pool_window_sums(ext2, tm, 1), lane, group) - dyp
        ext2[tm:tm + POOL_HALO, :] = ext2[0:POOL_HALO, :]
        rows = _row_ids(t, tm)
        dx_ref[...] = jnp.where(rows >= PAD_ROWS, dx, 0.0).astype(BF16)

    rev = lambda i: (ni - 1 - i, 0)
    return pl.pallas_call(
        body,
        grid=(ni,),
        in_specs=[
            pl.BlockSpec((tm, c), rev),
            pl.BlockSpec((POOL_HALO, c), lambda i: (jnp.maximum((ni - 1 - i) * hpt - 1, 0), 0)),
            pl.BlockSpec((tm, c), rev),
            pl.BlockSpec((c, c), lambda i: (0, 0)),
            pl.BlockSpec((1, c), lambda i: (0, 0)),
        ],
        out_specs=[
            pl.BlockSpec((tm, c), rev),
            pl.BlockSpec((c, c), lambda i: (0, 0)),
            pl.BlockSpec((1, c), lambda i: (0, 0)),
        ],
        out_shape=[
            jax.ShapeDtypeStruct((lp, c), BF16),
            jax.ShapeDtypeStruct((c, c), F32),
            jax.ShapeDtypeStruct((1, c), F32),
        ],
        scratch_shapes=[pltpu.VMEM((tm + POOL_HALO, c), F32), pltpu.VMEM((tm + POOL_HALO, c), F32)],
        compiler_params=_cparams(("arbitrary",)),
        name="pool_bwd",
    )(z, z, dcat, wbd, scale)


def _conv_taps(ext, w_ref, tm, first):
    acc = w_ref[0:1, :] * ext[pl.ds(first, tm), :]
    for k in range(1, CONV_WIDTH):
        acc = acc + w_ref[k:k + 1, :] * ext[pl.ds(first + k, tm), :]
    return acc


def _conv_fwd(z, w_dw, b_dw, ln_g, ln_b, w_pw):
    lp = z.shape[0]
    c = w_pw.shape[0]
    tm = _pick_tm(lp, MIX_TM, CHUNK)

    def body(a_ref, gt_ref, w_ref, bd_ref, g_ref, b_ref, pw_ref, y_ref, ext):
        i = pl.program_id(0)

        @pl.when(i == 0)
        def _():
            ext[0:CONV_HALO, :] = jnp.zeros((CONV_HALO, c), F32)

        ext[CONV_HALO:CONV_HALO + tm, :] = a_ref[...] * _sigmoid(gt_ref[...])
        acc = _conv_taps(ext, w_ref, tm, CONV_HALO - CONV_WIDTH + 1) + bd_ref[...]
        ext[0:CONV_HALO, :] = ext[tm:tm + CONV_HALO, :]
        xhat, _ = _ln_stats(acc)
        n = xhat * g_ref[...] + b_ref[...]
        act = n * _sigmoid(n)
        y_ref[...] = _dot(act.astype(BF16), pw_ref[...]).astype(BF16)

    const = lambda i: (0, 0)
    return pl.pallas_call(
        body,
        grid=(lp // tm,),
        in_specs=[
            pl.BlockSpec((tm, c), lambda i: (i, 1)),
            pl.BlockSpec((tm, c), lambda i: (i, 2)),
            pl.BlockSpec((CONV_HALO, c), const),
            pl.BlockSpec((1, c), const),
            pl.BlockSpec((1, c), const),
            pl.BlockSpec((1, c), const),
            pl.BlockSpec((c, c), const),
        ],
        out_specs=pl.BlockSpec((tm, c), lambda i: (i, 0)),
        out_shape=jax.ShapeDtypeStruct((lp, c), BF16),
        scratch_shapes=[pltpu.VMEM((tm + CONV_HALO, c), F32)],
        compiler_params=_cparams(("arbitrary",)),
        name="conv_fwd",
    )(z, z, w_dw, b_dw, ln_g, ln_b, w_pw)


def _conv_bwd(z, dcat, w_dw, b_dw, ln_g, ln_b, w_pw):
    lp = z.shape[0]
    c = w_pw.shape[0]
    tm = _pick_tm(lp, MIX_TM, CHUNK)
    ni = lp // tm
    hpt = tm // CONV_HALO
    first = CONV_HALO - CONV_WIDTH + 1

    def body(a_ref, gt_ref, ah_ref, gh_ref, dy_ref, w_ref, bd_ref, g_ref, b_ref, pw_ref,
             dca_ref, dcg_ref, dw_ref, dbd_ref, dg_ref, db_ref, dpw_ref, extu, extd):
        i = pl.program_id(0)
        t = ni - 1 - i

        @pl.when(i == 0)
        def _():
            dw_ref[...] = jnp.zeros_like(dw_ref)
            dbd_ref[...] = jnp.zeros_like(dbd_ref)
            dg_ref[...] = jnp.zeros_like(dg_ref)
            db_ref[...] = jnp.zeros_like(db_ref)
            dpw_ref[...] = jnp.zeros_like(dpw_ref)
            extd[tm:tm + CONV_HALO, :] = jnp.zeros((CONV_HALO, c), F32)

        ca = a_ref[...]
        sg = _sigmoid(gt_ref[...])
        extu[0:CONV_HALO, :] = jnp.where(t > 0, ah_ref[...] * _sigmoid(gh_ref[...]), 0.0)
        extu[CONV_HALO:CONV_HALO + tm, :] = ca * sg
        acc = _conv_taps(extu, w_ref, tm, first) + bd_ref[...]
        xhat, rstd = _ln_stats(acc)
        gam = g_ref[...]
        n = xhat * gam + b_ref[...]
        sn = _sigmoid(n)
        act = (n * sn).astype(BF16)
        do = dy_ref[...].astype(BF16)
        dpw_ref[...] += _dot_tn(act, do)
        dn = _dot_nt(do, pw_ref[...]) * (sn * (1.0 + n * (1.0 - sn)))
        dg_ref[...] += _colsum(dn * xhat)
        db_ref[...] += _colsum(dn)
        dyc = _ln_bwd(dn, xhat, rstd, gam)
        dbd_ref[...] += _colsum(dyc)
        extd[0:tm, :] = dyc
        du = None
        for k in range(CONV_WIDTH):
            dw_ref[k:k + 1, :] += _colsum(dyc * extu[pl.ds(first + k, tm), :])
            term = w_ref[k:k + 1, :] * extd[pl.ds(CONV_WIDTH - 1 - k, tm), :]
            du = term if du is None else du + term
        extd[tm:tm + CONV_HALO, :] = extd[0:CONV_HALO, :]
        du = jnp.where(_row_ids(t, tm) >= PAD_ROWS, du, 0.0)
        dca_ref[...] = (du * sg).astype(BF16)
        dcg_ref[...] = (du * ca * sg * (1.0 - sg)).astype(BF16)

    const = lambda i: (0, 0)
    rev = lambda col: (lambda i: (ni - 1 - i, col))
    halo = lambda col: (lambda i: (jnp.maximum((ni - 1 - i) * hpt - 1, 0), col))
    return pl.pallas_call(
        body,
        grid=(ni,),
        in_specs=[
            pl.BlockSpec((tm, c), rev(1)),
            pl.BlockSpec((tm, c), rev(2)),
            pl.BlockSpec((CONV_HALO, c), halo(1)),
            pl.BlockSpec((CONV_HALO, c), halo(2)),
            pl.BlockSpec((tm, c), rev(1)),
            pl.BlockSpec((CONV_HALO, c), const),
            pl.BlockSpec((1, c), const),
            pl.BlockSpec((1, c), const),
            pl.BlockSpec((1, c), const),
            pl.BlockSpec((c, c), const),
        ],
        out_specs=[
            pl.BlockSpec((tm, c), rev(0)),
            pl.BlockSpec((tm, c), rev(0)),
            pl.BlockSpec((CONV_HALO, c), const),
            pl.BlockSpec((1, c), const),
            pl.BlockSpec((1, c), const),
            pl.BlockSpec((1, c), const),
            pl.BlockSpec((c, c), const),
        ],
        out_shape=[
            jax.ShapeDtypeStruct((lp, c), BF16),
            jax.ShapeDtypeStruct((lp, c), BF16),
            jax.ShapeDtypeStruct((CONV_HALO, c), F32),
            jax.ShapeDtypeStruct((1, c), F32),
            jax.ShapeDtypeStruct((1, c), F32),
            jax.ShapeDtypeStruct((1, c), F32),
            jax.ShapeDtypeStruct((c, c), F32),
        ],
        scratch_shapes=[pltpu.VMEM((tm + CONV_HALO, c), F32), pltpu.VMEM((tm + CONV_HALO, c), F32)],
        compiler_params=_cparams(("arbitrary",)),
        name="conv_bwd",
    )(z, z, z, z, dcat, w_dw, b_dw, ln_g, ln_b, w_pw)


def _rope(x, cos, sgn_sin):
    return x * cos + pltpu.roll(x, LANE // 2, 1) * sgn_sin


def _rope_t(dy, cos, sgn_sin):
    return dy * cos + pltpu.roll(dy * sgn_sin, LANE // 2, 1)


def _ret_chunk_fwd(q, k, v, cos, sn, state, dmv, qdv, kdv):
    qr = _rope(q, cos, sn)
    kr = _rope(k, cos, sn) * (LANE ** -0.5)
    am = _dot_nt(qr.astype(BF16), kr.astype(BF16)) * dmv
    o = _dot(am.astype(BF16), v.astype(BF16)) + _dot((qr * qdv).astype(BF16), state.astype(BF16))
    return qr, kr, am, o


def _ret_specs(tm, q_blk):
    def mk(off, tile_of):
        return pl.BlockSpec((tm, LANE), lambda h, i: (tile_of(i), off + h))
    return lambda tile_of: [mk(q_blk + j * RET_HEADS, tile_of) for j in range(4)]


def _ret_fwd(z, rope_c, rope_s, dm, qd, kd, cd, gn, q_blk):
    lp = z.shape[0]
    tm = _pick_tm(lp, MIX_TM, CHUNK)
    nc = tm // CHUNK
    ni = lp // tm

    def body(q_ref, k_ref, v_ref, g_ref, c_ref, s_ref, dm_ref, qd_ref, kd_ref, cd_ref, gn_ref,
             y_ref, st_ref, state):
        @pl.when(pl.program_id(1) == 0)
        def _():
            state[...] = jnp.zeros_like(state)

        dmv, qdv, kdv, cdv, gnv = dm_ref[0], qd_ref[0], kd_ref[0], cd_ref[0, 0:1, :], gn_ref[...]

        def chunk(c, carry):
            rows = pl.ds(pl.multiple_of(c * CHUNK, CHUNK), CHUNK)
            v = v_ref[rows, :]
            s0 = state[...]
            st_ref[c, 0, :, :] = s0
            _, kr, _, o = _ret_chunk_fwd(q_ref[rows, :], k_ref[rows, :], v, c_ref[rows, :], s_ref[rows, :],
                                         s0, dmv, qdv, kdv)
            state[...] = s0 * cdv + _dot_tn((kr * kdv).astype(BF16), v.astype(BF16))
            on, _ = _ln_stats(o)
            gg = g_ref[rows, :]
            y_ref[rows, :] = (gg * _sigmoid(gg) * (on * gnv)).astype(BF16)
            return carry

        lax.fori_loop(0, nc, chunk, 0)

    tile = lambda i: i
    tab3 = lambda n: pl.BlockSpec((1, n, LANE), lambda h, i: (h, 0, 0))
    return pl.pallas_call(
        body,
        grid=(RET_HEADS, ni),
        in_specs=_ret_specs(tm, q_blk)(tile) + [
            pl.BlockSpec((tm, LANE), lambda h, i: (i, 0)),
            pl.BlockSpec((tm, LANE), lambda h, i: (i, 0)),
            pl.BlockSpec((1, CHUNK, CHUNK), lambda h, i: (h, 0, 0)),
            tab3(CHUNK), tab3(CHUNK), tab3(8),
            pl.BlockSpec((1, LANE), lambda h, i: (0, h)),
        ],
        out_specs=[
            pl.BlockSpec((tm, LANE), lambda h, i: (i, h)),
            pl.BlockSpec((nc, 1, LANE, LANE), lambda h, i: (i, h, 0, 0)),
        ],
        out_shape=[
            jax.ShapeDtypeStruct((lp, RET_HEADS * LANE), BF16),
            jax.ShapeDtypeStruct((lp // CHUNK, RET_HEADS, LANE, LANE), F32),
        ],
        scratch_shapes=[pltpu.VMEM((LANE, LANE), F32)],
        compiler_params=_cparams(("parallel", "arbitrary")),
        name="ret_fwd",
    )(z, z, z, z, rope_c, rope_s, dm, qd, kd, cd, gn)


def _ret_bwd(z, dcat, states, rope_c, rope_s, dm, qd, kd, cd, gn, q_blk, dy_blk):
    lp = z.shape[0]
    tm = _pick_tm(lp, MIX_TM, CHUNK)
    nc = tm // CHUNK
    ni = lp // tm

    def body(q_ref, k_ref, v_ref, g_ref, dy_ref, st_ref, c_ref, s_ref, dm_ref, qd_ref, kd_ref, cd_ref, gn_ref,
             dq_ref, dk_ref, dv_ref, dgt_ref, dgn_ref, dstate):
        i = pl.program_id(1)
        t = ni - 1 - i

        @pl.when(i == 0)
        def _():
            dstate[...] = jnp.zeros_like(dstate)
            dgn_ref[...] = jnp.zeros_like(dgn_ref)

        dmv, qdv, kdv, cdv, gnv = dm_ref[0], qd_ref[0], kd_ref[0], cd_ref[0, 0:1, :], gn_ref[...]

        def chunk(cc, carry):
            c = nc - 1 - cc
            rows = pl.ds(pl.multiple_of(c * CHUNK, CHUNK), CHUNK)
            cos, sn = c_ref[rows, :], s_ref[rows, :]
            v = v_ref[rows, :]
            vb = v.astype(BF16)
            s0 = st_ref[c, 0, :, :]
            qr, kr, am, o = _ret_chunk_fwd(q_ref[rows, :], k_ref[rows, :], v, cos, sn, s0, dmv, qdv, kdv)
            on, rstd = _ln_stats(o)
            gg = g_ref[rows, :]
            sg = _sigmoid(gg)
            sl = gg * sg
            dyv = dy_ref[rows, :]
            dgn_ref[...] += _colsum(dyv * sl * on)
            keep = (lax.broadcasted_iota(jnp.int32, (CHUNK, 1), 0) + (t * tm + c * CHUNK)) >= PAD_ROWS
            dgt = dyv * (on * gnv) * (sg * (1.0 + gg * (1.0 - sg)))
            dgt_ref[rows, :] = jnp.where(keep, dgt, 0.0).astype(BF16)
            dob = _ln_bwd(dyv * sl, on, rstd, gnv).astype(BF16)
            ds1 = dstate[...]
            ds1b = ds1.astype(BF16)
            qdb = (qr * qdv).astype(BF16)
            kdb = (kr * kdv).astype(BF16)
            da = (_dot_nt(dob, vb) * dmv).astype(BF16)
            dv = _dot_tn(am.astype(BF16), dob) + _dot(kdb, ds1b)
            dqr = _dot(da, kr.astype(BF16)) + _dot_nt(dob, s0.astype(BF16)) * qdv
            dkr = _dot_tn(da, qr.astype(BF16)) + _dot_nt(vb, ds1b) * kdv
            dstate[...] = ds1 * cdv + _dot_tn(qdb, dob)
            dq = _rope_t(dqr, cos, sn)
            dk = _rope_t(dkr * (LANE ** -0.5), cos, sn)
            dq_ref[rows, :] = jnp.where(keep, dq, 0.0).astype(BF16)
            dk_ref[rows, :] = jnp.where(keep, dk, 0.0).astype(BF16)
            dv_ref[rows, :] = jnp.where(keep, dv, 0.0).astype(BF16)
            return carry

        lax.fori_loop(0, nc, chunk, 0)

    tile = lambda i: ni - 1 - i
    tab3 = lambda n: pl.BlockSpec((1, n, LANE), lambda h, i: (h, 0, 0))
    out_blk = pl.BlockSpec((tm, LANE), lambda h, i: (ni - 1 - i, h))
    out_sds = jax.ShapeDtypeStruct((lp, RET_HEADS * LANE), BF16)
    return pl.pallas_call(
        body,
        grid=(RET_HEADS, ni),
        in_specs=_ret_specs(tm, q_blk)(tile) + [
            pl.BlockSpec((tm, LANE), lambda h, i: (ni - 1 - i, dy_blk + h)),
            pl.BlockSpec((nc, 1, LANE, LANE), lambda h, i: (ni - 1 - i, h, 0, 0)),
            pl.BlockSpec((tm, LANE), lambda h, i: (ni - 1 - i, 0)),
            pl.BlockSpec((tm, LANE), lambda h, i: (ni - 1 - i, 0)),
            pl.BlockSpec((1, CHUNK, CHUNK), lambda h, i: (h, 0, 0)),
            tab3(CHUNK), tab3(CHUNK), tab3(8),
            pl.BlockSpec((1, LANE), lambda h, i: (0, h)),
        ],
        out_specs=[out_blk, out_blk, out_blk, out_blk, pl.BlockSpec((1, LANE), lambda h, i: (0, h))],
        out_shape=[out_sds, out_sds, out_sds, out_sds, jax.ShapeDtypeStruct((1, RET_HEADS * LANE), F32)],
        scratch_shapes=[pltpu.VMEM((LANE, LANE), F32)],
        compiler_params=_cparams(("parallel", "arbitrary")),
        name="ret_bwd",
    )(z, z, z, z, dcat, states, rope_c, rope_s, dm, qd, kd, cd, gn)


def _adamw(parts_list, w, m, v):
    nl, r, c = w.shape
    assert len(parts_list) == nl
    tr = _pick_tm(r, max(8, (1 << 17) // c), 8)
    nr = r // tr

    def body(*refs):
        p_refs = refs[:nl]
        w_ref, m_ref, v_ref, g_ref, d_ref, nm_ref, nv_ref = refs[nl:]
        layer = pl.program_id(0)
        for k in range(nl):
            @pl.when(layer == k)
            def _(k=k):
                g = p_refs[k][0].astype(F32)
                for j in range(1, N_DEV):
                    g = g + p_refs[k][j].astype(F32)
                m1 = ADAM_B1 * m_ref[...] + (1.0 - ADAM_B1) * g
                v1 = ADAM_B2 * v_ref[...] + (1.0 - ADAM_B2) * (g * g)
                m_hat = m1 / (1.0 - ADAM_B1 ** ADAM_STEP)
                v_hat = v1 / (1.0 - ADAM_B2 ** ADAM_STEP)
                g_ref[...] = g
                d_ref[...] = -ADAM_LR * (m_hat / (jnp.sqrt(v_hat) + ADAM_EPS) + ADAM_WD * w_ref[...])
                nm_ref[...] = m1
                nv_ref[...] = v1

    def parts_spec(k):
        return pl.BlockSpec((N_DEV, tr, c), lambda l, i: (0, jnp.where(l == k, i, jnp.where(l < k, 0, nr - 1)), 0))

    blk = pl.BlockSpec((None, tr, c), lambda l, i: (l, i, 0))
    sds = jax.ShapeDtypeStruct((nl, r, c), F32)
    return pl.pallas_call(
        body,
        grid=(nl, nr),
        in_specs=[parts_spec(k) for k in range(nl)] + [blk, blk, blk],
        out_specs=[blk, blk, blk, blk],
        out_shape=[sds, sds, sds, sds],
        compiler_params=_cparams(("arbitrary", "arbitrary")),
        name="adamw",
    )(*parts_list, w, m, v)


def _flip(v, bit):
    return 1 - v if bit else v


def _exchange(arrs, scatter):
    n = len(arrs)
    shapes = [tuple(a.shape[1:] if scatter else a.shape) for a in arrs]

    def body(*refs):
        x_refs, o_refs = refs[:n], refs[n:2 * n]
        send_sems, recv_sems, local_sems = refs[2 * n:]
        mx, my, mc = lax.axis_index("x"), lax.axis_index("y"), lax.axis_index("c")
        me = 4 * mx + 2 * my + mc

        def peer_of(k):
            return (_flip(mx, (k >> 2) & 1), _flip(my, (k >> 1) & 1), _flip(mc, k & 1))

        def copy(a, k):
            peer = peer_of(k)
            src = x_refs[a].at[4 * peer[0] + 2 * peer[1] + peer[2]] if scatter else x_refs[a]
            return pltpu.make_async_remote_copy(
                src_ref=src, dst_ref=o_refs[a].at[me], send_sem=send_sems.at[a, k - 1],
                recv_sem=recv_sems.at[a, k - 1], device_id=peer, device_id_type=pl.DeviceIdType.MESH)

        def arrival(a, k):
            peer = peer_of(k)
            slot = o_refs[a].at[4 * peer[0] + 2 * peer[1] + peer[2]]
            return pltpu.make_async_remote_copy(
                src_ref=slot, dst_ref=slot, send_sem=send_sems.at[a, k - 1], recv_sem=recv_sems.at[a, k - 1],
                device_id=peer, device_id_type=pl.DeviceIdType.MESH)

        locals_ = [pltpu.make_async_copy(x_refs[a].at[me] if scatter else x_refs[a], o_refs[a].at[me],
                                         local_sems.at[a]) for a in range(n)]
        sends = [copy(a, k) for a in range(n) for k in range(1, N_DEV)]
        for cp in locals_ + sends:
            cp.start()
        for a in range(n):
            for k in range(1, N_DEV):
                arrival(a, k).wait_recv()
        for cp in sends:
            cp.wait_send()
        for cp in locals_:
            cp.wait()

    hbm = pl.BlockSpec(memory_space=pltpu.HBM)
    return pl.pallas_call(
        body,
        in_specs=[hbm] * n,
        out_specs=[hbm] * n,
        out_shape=[jax.ShapeDtypeStruct((N_DEV,) + s, a.dtype) for s, a in zip(shapes, arrs)],
        scratch_shapes=[
            pltpu.SemaphoreType.DMA((n, N_DEV - 1)),
            pltpu.SemaphoreType.DMA((n, N_DEV - 1)),
            pltpu.SemaphoreType.DMA((n,)),
        ],
        name="reduce_scatter_parts" if scatter else "all_gather",
    )(*arrs)


def _exchange_descriptors(x_refs, land_refs, send_sems, recv_sems, scatter):
    mx, my, mc = lax.axis_index("x"), lax.axis_index("y"), lax.axis_index("c")
    me = 4 * mx + 2 * my + mc
    sends, arrivals = [], []
    for a in range(len(x_refs)):
        for k in range(1, N_DEV):
            peer = (_flip(mx, (k >> 2) & 1), _flip(my, (k >> 1) & 1), _flip(mc, k & 1))
            slot = 4 * peer[0] + 2 * peer[1] + peer[2]
            si = a * (N_DEV - 1) + k - 1
            sems = dict(send_sem=send_sems.at[si], recv_sem=recv_sems.at[si],
                        device_id=peer, device_id_type=pl.DeviceIdType.MESH)
            sends.append(pltpu.make_async_remote_copy(
                src_ref=x_refs[a].at[slot] if scatter else x_refs[a], dst_ref=land_refs[a].at[me], **sems))
            arrivals.append(pltpu.make_async_remote_copy(
                src_ref=land_refs[a].at[slot], dst_ref=land_refs[a].at[slot], **sems))
    return sends, arrivals


def _exchange_start(arrs, scatter, name, after=None):
    n = len(arrs)
    shapes = [tuple(a.shape[1:] if scatter else a.shape) for a in arrs]
    lands = [lax.empty((N_DEV,) + s, a.dtype) for s, a in zip(shapes, arrs)]
    extra = [] if after is None else [after]

    def body(*refs):
        x_refs, land_refs = refs[:n], refs[n:2 * n]
        send_sems, recv_sems = refs[2 * n + len(extra)], refs[2 * n + len(extra) + 1]
        token = refs[-1]
        sends, _ = _exchange_descriptors(x_refs, land_refs, send_sems, recv_sems, scatter)
        for cp in sends:
            cp.start()
        token[...] = jnp.zeros_like(token)

    hbm = pl.BlockSpec(memory_space=pltpu.HBM)
    sem = pl.BlockSpec(memory_space=pltpu.SEMAPHORE)
    sem_type = pltpu.SemaphoreType.DMA((n * (N_DEV - 1),))
    operands = [pltpu.with_memory_space_constraint(a, pltpu.HBM) for a in list(arrs) + lands]
    out = pl.pallas_call(
        body,
        in_specs=[hbm] * (2 * n) + [pl.BlockSpec(memory_space=pl.ANY)] * len(extra),
        out_specs=[sem, sem] + [hbm] * (2 * n) + [pl.BlockSpec(memory_space=pltpu.VMEM)],
        out_shape=[sem_type, sem_type] + [pltpu.HBM(a.shape, a.dtype) for a in operands]
        + [jax.ShapeDtypeStruct((8, LANE), F32)],
        input_output_aliases={i: 2 + i for i in range(2 * n)},
        compiler_params=pltpu.CompilerParams(has_side_effects=pltpu.SideEffectType.DATAFLOW_SIDE_EFFECTING),
        name=name,
    )(*operands, *extra)
    return (out[0], out[1], list(out[2:2 + n]), list(out[2 + n:2 + 2 * n]), scatter), out[-1]


def _exchange_wait(handle, after, name):
    send_sems, recv_sems, x_thru, land_thru, scatter = handle
    n = len(x_thru)

    def body(*refs):
        x_refs, land_refs = refs[:n], refs[n:2 * n]
        sends, arrivals = _exchange_descriptors(x_refs, land_refs, refs[2 * n], refs[2 * n + 1], scatter)
        for cp in sends:
            cp.wait_send()
        for cp in arrivals:
            cp.wait_recv()

    hbm = pl.BlockSpec(memory_space=pltpu.HBM)
    sem = pl.BlockSpec(memory_space=pltpu.SEMAPHORE)
    out = pl.pallas_call(
        body,
        in_specs=[hbm] * (2 * n) + [sem, sem, pl.BlockSpec(memory_space=pl.ANY)],
        out_specs=[hbm] * (2 * n),
        out_shape=[pltpu.HBM(a.shape, a.dtype) for a in x_thru + land_thru],
        input_output_aliases={i: i for i in range(2 * n)},
        compiler_params=pltpu.CompilerParams(has_side_effects=pltpu.SideEffectType.DATAFLOW_SIDE_EFFECTING),
        name=name,
    )(*x_thru, *land_thru, send_sems, recv_sems, after)
    me = 4 * lax.axis_index("x") + 2 * lax.axis_index("y") + lax.axis_index("c")
    got = []
    for x, land in zip(out[:n], out[n:]):
        own = lax.dynamic_index_in_dim(x, me, 0, keepdims=False) if scatter else x
        got.append(lax.dynamic_update_index_in_dim(land, own, me, 0))
    return got


PACK_ALIGN = 2048


def _padded(n):
    return -(-n // PACK_ALIGN) * PACK_ALIGN


def _pad_to(a, axis, size):
    pad = [(0, 0)] * a.ndim
    pad[axis] = (0, size - a.shape[axis])
    return jnp.pad(a, pad)


def _pack(arrs, lead=0):
    flat = []
    for a in arrs:
        v = a.reshape(a.shape[:lead] + (-1,))
        flat.append(_pad_to(v, lead, _padded(v.shape[lead])))
    out = jnp.concatenate(flat, axis=lead)
    return out.reshape(out.shape[:lead] + (-1, LANE))


def _unpack(slab, shapes, lead=0):
    flat = slab.reshape(slab.shape[:lead] + (-1,))
    out, off = [], 0
    for s in shapes:
        n = math.prod(s)
        out.append(flat[..., off:off + n].reshape(slab.shape[:lead] + tuple(s)))
        off += _padded(n)
    return out


def _unshard(parts, ax):
    return jnp.concatenate([parts[j] for j in range(N_DEV)], axis=ax)


def _to_shards(full, ax):
    n = full.shape[ax] // N_DEV
    return jnp.stack([lax.slice_in_dim(full, j * n, (j + 1) * n, axis=ax) for j in range(N_DEV)])


def _pad_halves(w, hp):
    h = w.shape[-1] // 2
    zeros = jnp.zeros(w.shape[:-1] + (hp - h,), w.dtype)
    return jnp.concatenate([w[..., :h], zeros, w[..., h:], zeros], axis=-1)


def _unpad_halves(w, h):
    hp = w.shape[-1] // 2
    return jnp.concatenate([w[..., :h], w[..., hp:hp + h]], axis=-1)


SMALL_SHARDED = (("meta", 1), ("conv_dw", 2), ("ln_g", 2), ("ln_b", 2))
MATMUL_WEIGHTS = ("ffn1_w13", "ffn1_w2", "w_in", "conv_pw", "w_out", "ffn2_w13", "ffn2_w2")
REPLICATED = ("ln_in_g", "ln_in_b", "pool_w", "pool_scale", "conv_db", "conv_ln_g", "conv_ln_b", "ret_gn_g")
WEIGHT_ORDER = ("meta", "ln_in_g", "ln_in_b", "ffn1_w13", "ffn1_w2", "w_in", "pool_w", "pool_scale", "conv_dw",
                "conv_db", "conv_ln_g", "conv_ln_b", "conv_pw", "ret_gn_g", "w_out", "ffn2_w13", "ffn2_w2",
                "ln_g", "ln_b")


def _retention_tables(lp, heads):
    pos = jnp.arange(lp, dtype=F32) - PAD_ROWS
    inv_freq = ROPE_BASE ** (-jnp.arange(0, LANE, 2, dtype=F32) / LANE)
    ang = pos[:, None] * inv_freq[None, :]
    cos, sin = jnp.cos(ang), jnp.sin(ang)
    rope_c = jnp.concatenate([cos, cos], axis=1)
    rope_s = jnp.concatenate([-sin, sin], axis=1)
    log_gamma = jnp.log(1.0 - 2.0 ** (-5.0 - jnp.arange(heads, dtype=F32)))
    i = jnp.arange(CHUNK, dtype=F32)
    dm = jnp.exp(log_gamma[:, None, None] * jnp.abs(i[:, None] - i[None, :]))
    lanes = lambda t: jnp.broadcast_to(t[:, :, None], t.shape + (LANE,))
    qd = lanes(jnp.exp(log_gamma[:, None] * (i + 1.0)))
    kd = lanes(jnp.exp(log_gamma[:, None] * (CHUNK - 1.0 - i)))
    cd = lanes(jnp.broadcast_to(jnp.exp(log_gamma * CHUNK)[:, None], (heads, 8)))
    return rope_c, rope_s, dm, qd, kd, cd


def _block_diag(w):
    g, n, _ = w.shape
    rows = []
    for i in range(g):
        rows.append(jnp.concatenate([w[i] if j == i else jnp.zeros((n, n), w.dtype) for j in range(g)], axis=1))
    return jnp.concatenate(rows, axis=0)


def kernel(x, meta, ln_in_g, ln_in_b, ffn1_w13, ffn1_w2, w_in, pool_w, pool_scale, conv_dw, conv_db, conv_ln_g, conv_ln_b, conv_pw, ret_gn_g, w_out, ffn2_w13, ffn2_w2, ln_g, ln_b, loss_target, m_meta, m_ln_in_g, m_ln_in_b, m_ffn1_w13, m_ffn1_w2, m_w_in, m_pool_w, m_pool_scale, m_conv_dw, m_conv_db, m_conv_ln_g, m_conv_ln_b, m_conv_pw, m_ret_gn_g, m_w_out, m_ffn2_w13, m_ffn2_w2, m_ln_g, m_ln_b, v_meta, v_ln_in_g, v_ln_in_b, v_ffn1_w13, v_ffn1_w2, v_w_in, v_pool_w, v_pool_scale, v_conv_dw, v_conv_db, v_conv_ln_g, v_conv_ln_b, v_conv_pw, v_ret_gn_g, v_w_out, v_ffn2_w13, v_ffn2_w2, v_ln_g, v_ln_b):
    local = dict(meta=meta, ln_in_g=ln_in_g, ln_in_b=ln_in_b, ffn1_w13=ffn1_w13, ffn1_w2=ffn1_w2, w_in=w_in,
                 pool_w=pool_w, pool_scale=pool_scale, conv_dw=conv_dw, conv_db=conv_db, conv_ln_g=conv_ln_g,
                 conv_ln_b=conv_ln_b, conv_pw=conv_pw, ret_gn_g=ret_gn_g, w_out=w_out, ffn2_w13=ffn2_w13,
                 ffn2_w2=ffn2_w2, ln_g=ln_g, ln_b=ln_b)
    mom1 = dict(meta=m_meta, ln_in_g=m_ln_in_g, ln_in_b=m_ln_in_b, ffn1_w13=m_ffn1_w13, ffn1_w2=m_ffn1_w2,
                w_in=m_w_in, pool_w=m_pool_w, pool_scale=m_pool_scale, conv_dw=m_conv_dw, conv_db=m_conv_db,
                conv_ln_g=m_conv_ln_g, conv_ln_b=m_conv_ln_b, conv_pw=m_conv_pw, ret_gn_g=m_ret_gn_g,
                w_out=m_w_out, ffn2_w13=m_ffn2_w13, ffn2_w2=m_ffn2_w2, ln_g=m_ln_g, ln_b=m_ln_b)
    mom2 = dict(meta=v_meta, ln_in_g=v_ln_in_g, ln_in_b=v_ln_in_b, ffn1_w13=v_ffn1_w13, ffn1_w2=v_ffn1_w2,
                w_in=v_w_in, pool_w=v_pool_w, pool_scale=v_pool_scale, conv_dw=v_conv_dw, conv_db=v_conv_db,
                conv_ln_g=v_conv_ln_g, conv_ln_b=v_conv_ln_b, conv_pw=v_conv_pw, ret_gn_g=v_ret_gn_g,
                w_out=v_w_out, ffn2_w13=v_ffn2_w13, ffn2_w2=v_ffn2_w2, ln_g=v_ln_g, ln_b=v_ln_b)

    depth = ffn1_w13.shape[0]
    alpha = (2.0 * depth) ** 0.25
    seq, d = x.shape[1], x.shape[2]
    lp = FRONT + seq
    h_loc = ffn1_w2.shape[1]
    hp = -(-h_loc // LANE) * LANE
    c_pool = pool_scale.shape[1]
    c_conv = conv_db.shape[1]
    q_blk = (c_pool + 2 * c_conv) // LANE
    dy_blk = (c_pool + c_conv) // LANE
    heads = ret_gn_g.shape[1] // LANE
    assert meta.shape[0] == N_META and heads == RET_HEADS and conv_dw.shape[1] == CONV_WIDTH
    assert ffn1_w13.shape[2] == 2 * h_loc

    def to_wire(name, w):
        if name in ("ffn1_w13", "ffn2_w13"):
            return _pad_halves(w, hp)
        if name in ("ffn1_w2", "ffn2_w2"):
            return _pad_to(w, w.ndim - 2, hp)
        return w

    def from_wire(name, w):
        if name in ("ffn1_w13", "ffn2_w13"):
            return _unpad_halves(w, h_loc)
        if name in ("ffn1_w2", "ffn2_w2"):
            return w[..., :h_loc, :]
        return w

    small_names = [n for n, _ in SMALL_SHARDED]
    small_ax = dict(SMALL_SHARDED)
    n_mm = len(MATMUL_WEIGHTS)
    first = ("ffn1_w13", "ffn1_w2")
    wire = lambda l, names: [to_wire(n, local[n][l]).astype(BF16) for n in names]
    rest = [n for n in MATMUL_WEIGHTS if n not in first]
    groups = [(wire(0, first) + [_pack([local[n] for n in small_names])]), wire(0, rest)]
    groups += [wire(l, MATMUL_WEIGHTS) for l in range(1, depth)]
    handles, token = [], None
    for gi, arrs in enumerate(groups):
        handle, token = _exchange_start(arrs, False, f"gather_start_{gi}", after=token)
        handles.append(handle)
    got = _exchange_wait(handles[0], token, "gather_wait_0")
    gw_first = dict(zip(first, got))
    small_parts = _unpack(got[-1], [local[n].shape for n in small_names], 1)
    small_full = {n: _unshard(p, small_ax[n]) for n, p in zip(small_names, small_parts)}

    row = lambda v: v.reshape(1, -1)
    ln_params = [(row(ln_in_g), row(ln_in_b))]
    for l in range(depth):
        ln_params += [(row(small_full["ln_g"][l, j]), row(small_full["ln_b"][l, j])) for j in range(3)]
    rope_c, rope_s, dm, qd, kd, cd = _retention_tables(lp, heads)

    def layer_params(l, gw):
        return dict(
            ffn1=(gw["ffn1_w13"], gw["ffn1_w2"]),
            ffn2=(gw["ffn2_w13"], gw["ffn2_w2"]),
            w_in=_unshard(gw["w_in"], 1),
            w_out=gw["w_out"].reshape(-1, d),
            pool=(_block_diag(pool_w[l]).astype(BF16), row(pool_scale[l])),
            conv=(_pad_to(small_full["conv_dw"][l], 0, CONV_HALO), row(conv_db[l]), row(conv_ln_g[l]),
                  row(conv_ln_b[l]), gw["conv_pw"].reshape(c_conv, c_conv)),
            gn=row(ret_gn_g[l]),
        )

    r = [jnp.concatenate([jnp.zeros((PAD_ROWS, d), F32), small_full["meta"], x[0]], axis=0)]
    saved, layers = [], []
    for l in range(depth):
        k = 3 * l
        if l == 0:
            r1, a1, u1 = _ffn_fwd(r[k], *ln_params[k], gw_first["ffn1_w13"], gw_first["ffn1_w2"], alpha)
            gw = dict(gw_first, **dict(zip(rest, _exchange_wait(handles[1], r1, "gather_wait_1"))))
            p = layer_params(l, gw)
        else:
            gw = dict(zip(MATMUL_WEIGHTS, _exchange_wait(handles[l + 1], r[k], f"gather_wait_{l + 1}")))
            p = layer_params(l, gw)
            r1, a1, u1 = _ffn_fwd(r[k], *ln_params[k], *p["ffn1"], alpha)
        layers.append(p)
        z = _proj_fwd(r1, *ln_params[k + 1], p["w_in"])
        y_pool = _pool_fwd(z, *p["pool"])
        y_conv = _conv_fwd(z, *p["conv"])
        y_ret, states = _ret_fwd(z, rope_c, rope_s, dm, qd, kd, cd, p["gn"], q_blk)
        cat = jnp.concatenate([y_pool, y_conv, y_ret], axis=1)
        r2 = _out_fwd(r1, *ln_params[k + 1], cat, p["w_out"], alpha)
        r3, a2, u2 = _ffn_fwd(r2, *ln_params[k + 2], *p["ffn2"], alpha)
        r += [r1, r2, r3]
        saved.append((a1, u1, z, states, cat, a2, u2))

    target = jnp.concatenate([jnp.zeros((FRONT, d), F32), loss_target[0]], axis=0)
    dr, dg, db, loss_cols = _loss_bwd(r[-1], target, *ln_params[-1])
    loss = lax.psum(0.5 * jnp.sum(loss_cols) / d, MESH_AXES)
    ln_grads = {3 * depth: (dg, db)}
    w13_slot = lambda j: (j % 2) * (N_DEV // 2) + j // 2
    g_parts = [None] * depth
    g_rep = [None] * depth
    scatter_handles = [None] * depth

    def ffn_grads(dr_out, r_in, a, u, ln_p, wts):
        dr_in, hb, s, dz, dg, db = _ffn_bwd(dr_out, r_in, a, u, *ln_p, *wts, alpha)
        g13 = _mm_tn(hb, dz, bn=2 * hp, slot_of=w13_slot)
        g2 = _mm_tn(s, dr_out, 0.5).reshape(N_DEV, hp, d)
        return dr_in, (dg, db), g13, g2

    for l in reversed(range(depth)):
        p = layers[l]
        k = 3 * l
        a1, u1, z, states, cat, a2, u2 = saved[l]
        gp, gr = {}, {}
        dr, ln_grads[k + 2], gp["ffn2_w13"], gp["ffn2_w2"] = ffn_grads(dr, r[k + 2], a2, u2, ln_params[k + 2], p["ffn2"])

        dr2 = dr
        dcat = _mm_nt(dr2, p["w_out"])
        gp["w_out"] = _mm_tn(cat, dr2).reshape(N_DEV, -1, d)
        dxp, dwbd, gr["pool_scale"] = _pool_bwd(z, dcat, *p["pool"])
        dca, dcg, ddw, gr["conv_db"], gr["conv_ln_g"], gr["conv_ln_b"], dpw = _conv_bwd(z, dcat, *p["conv"])
        dq, dk, dv, dgt, gr["ret_gn_g"] = _ret_bwd(z, dcat, states, rope_c, rope_s, dm, qd, kd, cd, p["gn"],
                                                  q_blk, dy_blk)
        dz = jnp.concatenate([dxp, dca, dcg, dq, dk, dv, dgt], axis=1)
        dr, hb, dg, db = _in_bwd(dr2, r[k + 1], dz, *ln_params[k + 1], p["w_in"], alpha)
        ln_grads[k + 1] = (dg, db)
        gp["w_in"] = _to_shards(_mm_tn(hb, dz), 1)
        gp["conv_pw"] = dpw.reshape(N_DEV, -1, c_conv)
        n_grp, grp = pool_w.shape[1], pool_w.shape[2]
        gr["pool_w"] = jnp.stack([dwbd[i * grp:(i + 1) * grp, i * grp:(i + 1) * grp] for i in range(n_grp)])
        gr["conv_dw"] = ddw[:CONV_WIDTH]

        dr, ln_grads[k], gp["ffn1_w13"], gp["ffn1_w2"] = ffn_grads(dr, r[k], a1, u1, ln_params[k], p["ffn1"])
        g_parts[l], g_rep[l] = gp, gr
        if l > 0:
            parts = [gp[n].astype(WIRE) for n in MATMUL_WEIGHTS]
            scatter_handles[l], token = _exchange_start(parts, True, f"scatter_start_{l}")
            g_prev, b_prev = ln_params[k - 1]
            ln_params[k - 1] = (g_prev + token[0:1, 0:1], b_prev)

    grad_x = dr[FRONT:][None]
    local_shape = lambda name: ((CONV_WIDTH, c_conv) if name == "conv_dw" else local[name].shape[1:])
    stack_layers = lambda name: jnp.stack([g_rep[l][name].reshape(local_shape(name)) for l in range(depth)])
    small_grad = dict(
        meta=dr[PAD_ROWS:FRONT],
        conv_dw=stack_layers("conv_dw"),
        ln_g=jnp.stack([jnp.stack([ln_grads[3 * l + j + 1][0][0] for j in range(3)]) for l in range(depth)]),
        ln_b=jnp.stack([jnp.stack([ln_grads[3 * l + j + 1][1][0] for j in range(3)]) for l in range(depth)]),
    )
    rep_grad = {n: stack_layers(n) for n in REPLICATED if n not in ("ln_in_g", "ln_in_b")}
    rep_grad["ln_in_g"], rep_grad["ln_in_b"] = ln_grads[0][0][0], ln_grads[0][1][0]

    small_pack8 = _pack([_to_shards(small_grad[n], small_ax[n]) for n in small_names], 1)
    parts = [g_parts[0][n].astype(WIRE) for n in MATMUL_WEIGHTS] + [small_pack8]
    scatter_handles[0], token = _exchange_start(parts, True, "scatter_start_0")
    rep_parts = _exchange([_pack([rep_grad[n] for n in REPLICATED])], False)[0]
    scattered = [None] * depth
    for l in reversed(range(depth)):
        scattered[l] = _exchange_wait(scatter_handles[l], rep_parts if token is None else token, f"scatter_wait_{l}")
        token = None
    small_scattered = scattered[0][-1]

    outs_by_name = {}
    for i, n in enumerate(MATMUL_WEIGHTS):
        res = _adamw([scattered[l][i] for l in range(depth)], *[to_wire(n, src[n]) for src in (local, mom1, mom2)])
        outs_by_name[n] = [from_wire(n, t) for t in res]
    for names, parts in ((small_names, small_scattered), (REPLICATED, rep_parts)):
        shapes = [local[n].shape for n in names]
        res = _adamw([parts], *[_pack([src[n] for n in names])[None] for src in (local, mom1, mom2)])
        for n, vals in zip(names, zip(*[_unpack(t[0], shapes) for t in res])):
            outs_by_name[n] = list(vals)

    outs = [loss, grad_x]
    for kind in range(4):
        outs += [outs_by_name[n][kind] for n in WEIGHT_ORDER]
    return tuple(outs)
```

```python
import math

import jax
import jax.numpy as jnp
from jax import lax
from jax.experimental import pallas as pl
from jax.experimental.pallas import tpu as pltpu

F32 = jnp.float32
BF16 = jnp.bfloat16
WIRE = jnp.bfloat16

N_DEV = 8
MESH_AXES = ("x", "y", "c")
CHUNK = 64
N_META = 16
PAD_ROWS = 240
FRONT = PAD_ROWS + N_META
LN_EPS = 1e-5
LANE = 128
POOL_WINDOWS = (2, 4, 8, 16)
CONV_WIDTH = 31
CONV_HALO = 32
POOL_HALO = 16
RET_HEADS = 4
ROPE_BASE = 10000.0

ADAM_LR = 0.001
ADAM_B1 = 0.9
ADAM_B2 = 0.999
ADAM_EPS = 1e-08
ADAM_WD = 0.01
ADAM_STEP = 10

VMEM_LIMIT = 56 * 1024 * 1024
DENSE_TM_FWD = 704
DENSE_TM_BWD = 528
MIX_TM = 768


def _cparams(sem):
    return pltpu.CompilerParams(dimension_semantics=sem, vmem_limit_bytes=VMEM_LIMIT)


def _pick_tm(lp, target, mult=16):
    best = None
    for t in range(mult, min(lp, target) + 1, mult):
        if lp % t == 0:
            best = t
    assert best is not None, (lp, target, mult)
    return best


def _dot(a, b):
    return jnp.dot(a, b, preferred_element_type=F32)


def _dot_nt(a, b):
    return lax.dot_general(a, b, (((1,), (1,)), ((), ())), preferred_element_type=F32)


def _dot_tn(a, b):
    return lax.dot_general(a, b, (((0,), (0,)), ((), ())), preferred_element_type=F32)


def _sigmoid(x):
    return 0.5 * jnp.tanh(0.5 * x) + 0.5


def _ln_stats(r):
    mu = jnp.mean(r, axis=-1, keepdims=True)
    xc = r - mu
    var = jnp.mean(xc * xc, axis=-1, keepdims=True)
    rstd = lax.rsqrt(var + LN_EPS)
    return xc * rstd, rstd


def _ln_bwd(dh, xhat, rstd, g):
    dxh = dh * g
    m1 = jnp.mean(dxh, axis=-1, keepdims=True)
    m2 = jnp.mean(dxh * xhat, axis=-1, keepdims=True)
    return rstd * (dxh - m1 - xhat * m2)


def _colsum(x):
    return jnp.sum(x, axis=0, keepdims=True)


def _row_ids(tile, tm):
    return lax.broadcasted_iota(jnp.int32, (tm, 1), 0) + tile * tm


def _ffn_fwd(r_prev, g, b, w13g, w2g, alpha):
    lp, d = r_prev.shape
    nf = N_DEV // 2
    fc = w13g.shape[2]
    w2c = w2g.reshape(nf, fc, d)
    tm = _pick_tm(lp, DENSE_TM_FWD)

    def body(r_ref, g_ref, b_ref, w1_ref, w3_ref, w2_ref, out_ref, a_ref, u_ref, hb, acc):
        f = pl.program_id(1)

        @pl.when(f == 0)
        def _():
            xhat, _ = _ln_stats(r_ref[...])
            hb[...] = (xhat * g_ref[...] + b_ref[...]).astype(BF16)
            acc[...] = jnp.zeros_like(acc)

        h = hb[...]
        a = _dot(h, w1_ref[...])
        u = _dot(h, w3_ref[...])
        a_ref[...] = a.astype(BF16)
        u_ref[...] = u.astype(BF16)
        s = a * _sigmoid(a) * u
        acc[...] += _dot(s.astype(BF16), w2_ref[...])

        @pl.when(f == nf - 1)
        def _():
            xhat, _ = _ln_stats(r_ref[...])
            out_ref[...] = alpha * (xhat * g_ref[...] + b_ref[...]) + 0.5 * acc[...]

    return pl.pallas_call(
        body,
        grid=(lp // tm, nf),
        in_specs=[
            pl.BlockSpec((tm, d), lambda i, f: (i, 0)),
            pl.BlockSpec((1, d), lambda i, f: (0, 0)),
            pl.BlockSpec((1, d), lambda i, f: (0, 0)),
            pl.BlockSpec((None, d, fc), lambda i, f: (f, 0, 0)),
            pl.BlockSpec((None, d, fc), lambda i, f: (nf + f, 0, 0)),
            pl.BlockSpec((None, fc, d), lambda i, f: (f, 0, 0)),
        ],
        out_specs=[
            pl.BlockSpec((tm, d), lambda i, f: (i, 0)),
            pl.BlockSpec((tm, fc), lambda i, f: (i, f)),
            pl.BlockSpec((tm, fc), lambda i, f: (i, f)),
        ],
        out_shape=[
            jax.ShapeDtypeStruct((lp, d), F32),
            jax.ShapeDtypeStruct((lp, nf * fc), BF16),
            jax.ShapeDtypeStruct((lp, nf * fc), BF16),
        ],
        scratch_shapes=[pltpu.VMEM((tm, d), BF16), pltpu.VMEM((tm, d), F32)],
        compiler_params=_cparams(("parallel", "arbitrary")),
        name="ffn_fwd",
    )(r_prev, g, b, w13g, w13g, w2c)


def _ffn_bwd(dr_next, r_prev, a, u, g, b, w13g, w2g, alpha):
    lp, d = r_prev.shape
    nf = N_DEV // 2
    fc = w13g.shape[2]
    w2c = w2g.reshape(nf, fc, d)
    tm = _pick_tm(lp, DENSE_TM_BWD)

    def body(dr_ref, r_ref, a_ref, u_ref, g_ref, b_ref, w1_ref, w3_ref, w2_ref,
             drp_ref, hb_ref, s_ref, dz_ref, dg_ref, db_ref, dyb, dhacc):
        i = pl.program_id(0)
        f = pl.program_id(1)

        @pl.when(jnp.logical_and(i == 0, f == 0))
        def _():
            dg_ref[...] = jnp.zeros_like(dg_ref)
            db_ref[...] = jnp.zeros_like(db_ref)

        @pl.when(f == 0)
        def _():
            dyb[...] = (0.5 * dr_ref[...]).astype(BF16)
            dhacc[...] = jnp.zeros_like(dhacc)
            xhat, _ = _ln_stats(r_ref[...])
            hb_ref[...] = (xhat * g_ref[...] + b_ref[...]).astype(BF16)

        ds = _dot_nt(dyb[...], w2_ref[...])
        av = a_ref[...].astype(F32)
        uv = u_ref[...].astype(F32)
        sig = _sigmoid(av)
        sl = av * sig
        da = (ds * uv * (sig * (1.0 + av * (1.0 - sig)))).astype(BF16)
        du = (ds * sl).astype(BF16)
        s_ref[...] = (sl * uv).astype(BF16)
        dz_ref[:, 0:fc] = da
        dz_ref[:, fc:2 * fc] = du
        dhacc[...] += _dot_nt(da, w1_ref[...]) + _dot_nt(du, w3_ref[...])

        @pl.when(f == nf - 1)
        def _():
            dh = alpha * dr_ref[...] + dhacc[...]
            xhat, rstd = _ln_stats(r_ref[...])
            drp_ref[...] = _ln_bwd(dh, xhat, rstd, g_ref[...])
            dg_ref[...] += _colsum(dh * xhat)
            db_ref[...] += _colsum(dh)

    row = lambda i, f: (i, 0)
    const = lambda i, f: (0, 0)
    chunk = lambda i, f: (i, f)
    return pl.pallas_call(
        body,
        grid=(lp // tm, nf),
        in_specs=[
            pl.BlockSpec((tm, d), row),
            pl.BlockSpec((tm, d), row),
            pl.BlockSpec((tm, fc), chunk),
            pl.BlockSpec((tm, fc), chunk),
            pl.BlockSpec((1, d), const),
            pl.BlockSpec((1, d), const),
            pl.BlockSpec((None, d, fc), lambda i, f: (f, 0, 0)),
            pl.BlockSpec((None, d, fc), lambda i, f: (nf + f, 0, 0)),
            pl.BlockSpec((None, fc, d), lambda i, f: (f, 0, 0)),
        ],
        out_specs=[
            pl.BlockSpec((tm, d), row),
            pl.BlockSpec((tm, d), row),
            pl.BlockSpec((tm, fc), chunk),
            pl.BlockSpec((tm, 2 * fc), chunk),
            pl.BlockSpec((1, d), const),
            pl.BlockSpec((1, d), const),
        ],
        out_shape=[
            jax.ShapeDtypeStruct((lp, d), F32),
            jax.ShapeDtypeStruct((lp, d), BF16),
            jax.ShapeDtypeStruct((lp, nf * fc), BF16),
            jax.ShapeDtypeStruct((lp, 2 * nf * fc), BF16),
            jax.ShapeDtypeStruct((1, d), F32),
            jax.ShapeDtypeStruct((1, d), F32),
        ],
        scratch_shapes=[pltpu.VMEM((tm, d), BF16), pltpu.VMEM((tm, d), F32)],
        compiler_params=_cparams(("arbitrary", "arbitrary")),
        name="ffn_bwd",
    )(dr_next, r_prev, a, u, g, b, w13g, w13g, w2c)


def _mm_tn(a, b, scale=1.0, bn=None, slot_of=None):
    t, m = a.shape
    n = b.shape[1]
    if bn is None:
        bn = n if n <= 1408 else _pick_tm(n, 1408, LANE)
    bm = m if m <= 1536 else _pick_tm(m, 1536, LANE)
    tk = _pick_tm(t, 768)
    nt = t // tk

    def body(a_ref, b_ref, o_ref, acc):
        k = pl.program_id(2)

        @pl.when(k == 0)
        def _():
            acc[...] = jnp.zeros_like(acc)

        acc[...] += _dot_tn(a_ref[...].astype(BF16), b_ref[...].astype(BF16))

        @pl.when(k == nt - 1)
        def _():
            o_ref[...] = (acc[...] * scale).astype(o_ref.dtype)

    if slot_of is None:
        out_spec = pl.BlockSpec((bm, bn), lambda i, j, k: (i, j))
        out_shape = jax.ShapeDtypeStruct((m, n), WIRE)
    else:
        out_spec = pl.BlockSpec((None, bm, bn), lambda i, j, k: (slot_of(j), i, 0))
        out_shape = jax.ShapeDtypeStruct((n // bn, m, bn), WIRE)
    return pl.pallas_call(
        body,
        grid=(m // bm, n // bn, nt),
        in_specs=[
            pl.BlockSpec((tk, bm), lambda i, j, k: (k, i)),
            pl.BlockSpec((tk, bn), lambda i, j, k: (k, j)),
        ],
        out_specs=out_spec,
        out_shape=out_shape,
        scratch_shapes=[pltpu.VMEM((bm, bn), F32)],
        compiler_params=_cparams(("parallel", "parallel", "arbitrary")),
        name="mm_tn",
    )(a, b)


def _proj_fwd(r_prev, g, b, w_in):
    lp, d = r_prev.shape
    n = w_in.shape[1]
    tm = _pick_tm(lp, DENSE_TM_FWD)

    def body(r_ref, g_ref, b_ref, w_ref, z_ref):
        xhat, _ = _ln_stats(r_ref[...])
        h = (xhat * g_ref[...] + b_ref[...]).astype(BF16)
        z = _dot(h, w_ref[...])
        rows = _row_ids(pl.program_id(0), tm)
        z_ref[...] = jnp.where(rows >= PAD_ROWS, z, 0.0)

    return pl.pallas_call(
        body,
        grid=(lp // tm,),
        in_specs=[
            pl.BlockSpec((tm, d), lambda i: (i, 0)),
            pl.BlockSpec((1, d), lambda i: (0, 0)),
            pl.BlockSpec((1, d), lambda i: (0, 0)),
            pl.BlockSpec((d, n), lambda i: (0, 0)),
        ],
        out_specs=pl.BlockSpec((tm, n), lambda i: (i, 0)),
        out_shape=jax.ShapeDtypeStruct((lp, n), F32),
        compiler_params=_cparams(("parallel",)),
        name="proj_fwd",
    )(r_prev, g, b, w_in)


def _out_fwd(r_prev, g, b, cat, w_out, alpha):
    lp, d = r_prev.shape
    k = cat.shape[1]
    tm = _pick_tm(lp, DENSE_TM_FWD)

    def body(r_ref, g_ref, b_ref, c_ref, w_ref, o_ref):
        xhat, _ = _ln_stats(r_ref[...])
        o_ref[...] = alpha * (xhat * g_ref[...] + b_ref[...]) + _dot(c_ref[...], w_ref[...])

    return pl.pallas_call(
        body,
        grid=(lp // tm,),
        in_specs=[
            pl.BlockSpec((tm, d), lambda i: (i, 0)),
            pl.BlockSpec((1, d), lambda i: (0, 0)),
            pl.BlockSpec((1, d), lambda i: (0, 0)),
            pl.BlockSpec((tm, k), lambda i: (i, 0)),
            pl.BlockSpec((k, d), lambda i: (0, 0)),
        ],
        out_specs=pl.BlockSpec((tm, d), lambda i: (i, 0)),
        out_shape=jax.ShapeDtypeStruct((lp, d), F32),
        compiler_params=_cparams(("parallel",)),
        name="out_fwd",
    )(r_prev, g, b, cat, w_out)


def _mm_nt(x, w):
    lp, n = x.shape
    k = w.shape[0]
    tm = _pick_tm(lp, DENSE_TM_FWD)

    def body(x_ref, w_ref, o_ref):
        o_ref[...] = _dot_nt(x_ref[...].astype(BF16), w_ref[...])

    return pl.pallas_call(
        body,
        grid=(lp // tm,),
        in_specs=[pl.BlockSpec((tm, n), lambda i: (i, 0)), pl.BlockSpec((k, n), lambda i: (0, 0))],
        out_specs=pl.BlockSpec((tm, k), lambda i: (i, 0)),
        out_shape=jax.ShapeDtypeStruct((lp, k), F32),
        compiler_params=_cparams(("parallel",)),
        name="mm_nt",
    )(x, w)


def _in_bwd(dr_next, r_prev, dz, g, b, w_in, alpha):
    lp, d = r_prev.shape
    n = w_in.shape[1]
    tm = _pick_tm(lp, DENSE_TM_FWD)

    def body(dr_ref, r_ref, dz_ref, g_ref, b_ref, w_ref, drp_ref, hb_ref, dg_ref, db_ref):
        @pl.when(pl.program_id(0) == 0)
        def _():
            dg_ref[...] = jnp.zeros_like(dg_ref)
            db_ref[...] = jnp.zeros_like(db_ref)

        dh = alpha * dr_ref[...] + _dot_nt(dz_ref[...], w_ref[...])
        xhat, rstd = _ln_stats(r_ref[...])
        hb_ref[...] = (xhat * g_ref[...] + b_ref[...]).astype(BF16)
        drp_ref[...] = _ln_bwd(dh, xhat, rstd, g_ref[...])
        dg_ref[...] += _colsum(dh * xhat)
        db_ref[...] += _colsum(dh)

    row = lambda i: (i, 0)
    const = lambda i: (0, 0)
    return pl.pallas_call(
        body,
        grid=(lp // tm,),
        in_specs=[
            pl.BlockSpec((tm, d), row),
            pl.BlockSpec((tm, d), row),
            pl.BlockSpec((tm, n), row),
            pl.BlockSpec((1, d), const),
            pl.BlockSpec((1, d), const),
            pl.BlockSpec((d, n), const),
        ],
        out_specs=[
            pl.BlockSpec((tm, d), row),
            pl.BlockSpec((tm, d), row),
            pl.BlockSpec((1, d), const),
            pl.BlockSpec((1, d), const),
        ],
        out_shape=[
            jax.ShapeDtypeStruct((lp, d), F32),
            jax.ShapeDtypeStruct((lp, d), BF16),
            jax.ShapeDtypeStruct((1, d), F32),
            jax.ShapeDtypeStruct((1, d), F32),
        ],
        compiler_params=_cparams(("arbitrary",)),
        name="in_bwd",
    )(dr_next, r_prev, dz, g, b, w_in)


def _loss_bwd(r_last, target, g, b):
    lp, d = r_last.shape
    tm = _pick_tm(lp, DENSE_TM_FWD)

    def body(r_ref, t_ref, g_ref, b_ref, dr_ref, dg_ref, db_ref, ls_ref):
        i = pl.program_id(0)

        @pl.when(i == 0)
        def _():
            dg_ref[...] = jnp.zeros_like(dg_ref)
            db_ref[...] = jnp.zeros_like(db_ref)
            ls_ref[...] = jnp.zeros_like(ls_ref)

        xhat, rstd = _ln_stats(r_ref[...])
        y = xhat * g_ref[...] + b_ref[...]
        rows = _row_ids(i, tm)
        err = jnp.where(rows >= FRONT, y - t_ref[...], 0.0)
        ls_ref[...] += _colsum(err * err)
        dy = err * (1.0 / d)
        dr_ref[...] = _ln_bwd(dy, xhat, rstd, g_ref[...])
        dg_ref[...] += _colsum(dy * xhat)
        db_ref[...] += _colsum(dy)

    row = lambda i: (i, 0)
    const = lambda i: (0, 0)
    return pl.pallas_call(
        body,
        grid=(lp // tm,),
        in_specs=[
            pl.BlockSpec((tm, d), row),
            pl.BlockSpec((tm, d), row),
            pl.BlockSpec((1, d), const),
            pl.BlockSpec((1, d), const),
        ],
        out_specs=[
            pl.BlockSpec((tm, d), row),
            pl.BlockSpec((1, d), const),
            pl.BlockSpec((1, d), const),
            pl.BlockSpec((1, d), const),
        ],
        out_shape=[
            jax.ShapeDtypeStruct((lp, d), F32),
            jax.ShapeDtypeStruct((1, d), F32),
            jax.ShapeDtypeStruct((1, d), F32),
            jax.ShapeDtypeStruct((1, d), F32),
        ],
        compiler_params=_cparams(("arbitrary",)),
        name="loss_bwd",
    )(r_last, target, g, b)


def _pool_counts(tile, tm, width):
    pos = _row_ids(tile, tm) - PAD_ROWS
    lane = lax.broadcasted_iota(jnp.int32, (1, width), 1)
    group = width // len(POOL_WINDOWS)
    win = jnp.full((1, width), POOL_WINDOWS[-1], jnp.int32)
    for gi in range(len(POOL_WINDOWS) - 2, -1, -1):
        win = jnp.where(lane < (gi + 1) * group, POOL_WINDOWS[gi], win)
    cnt = jnp.clip(pos + 1, 1, win)
    return cnt.astype(F32), lane, group


def _pool_select(sums, lane, group):
    out = sums[-1]
    for gi in range(len(POOL_WINDOWS) - 2, -1, -1):
        out = jnp.where(lane < (gi + 1) * group, sums[gi], out)
    return out


def _pool_window_sums(ext, tm, sign):
    base = POOL_HALO if sign < 0 else 0
    acc = ext[pl.ds(base, tm), :]
    sums, k = [], 1
    for w in POOL_WINDOWS:
        while k < w:
            acc = acc + ext[pl.ds(base + sign * k, tm), :]
            k += 1
        sums.append(acc)
    return sums


def _pool_fwd(z, wbd, scale):
    lp = z.shape[0]
    c = wbd.shape[0]
    tm = _pick_tm(lp, MIX_TM, CHUNK)

    def body(x_ref, w_ref, s_ref, y_ref, ext):
        i = pl.program_id(0)

        @pl.when(i == 0)
        def _():
            ext[0:POOL_HALO, :] = jnp.zeros((POOL_HALO, c), F32)

        x = x_ref[...]
        ext[POOL_HALO:POOL_HALO + tm, :] = x
        cnt, lane, group = _pool_counts(i, tm, c)
        sums = _pool_window_sums(ext, tm, -1)
        y = _pool_select(sums, lane, group) / cnt - x
        ext[0:POOL_HALO, :] = ext[tm:tm + POOL_HALO, :]
        y_ref[...] = (_dot(y.astype(BF16), w_ref[...]) * s_ref[...]).astype(BF16)

    return pl.pallas_call(
        body,
        grid=(lp // tm,),
        in_specs=[
            pl.BlockSpec((tm, c), lambda i: (i, 0)),
            pl.BlockSpec((c, c), lambda i: (0, 0)),
            pl.BlockSpec((1, c), lambda i: (0, 0)),
        ],
        out_specs=pl.BlockSpec((tm, c), lambda i: (i, 0)),
        out_shape=jax.ShapeDtypeStruct((lp, c), BF16),
        scratch_shapes=[pltpu.VMEM((tm + POOL_HALO, c), F32)],
        compiler_params=_cparams(("arbitrary",)),
        name="pool_fwd",
    )(z, wbd, scale)


def _pool_bwd(z, dcat, wbd, scale):
    lp = z.shape[0]
    c = wbd.shape[0]
    tm = _pick_tm(lp, MIX_TM, CHUNK)
    ni = lp // tm
    hpt = tm // POOL_HALO

    def body(x_ref, xh_ref, dy_ref, w_ref, s_ref, dx_ref, dw_ref, ds_ref, ext, ext2):
        i = pl.program_id(0)
        t = ni - 1 - i

        @pl.when(i == 0)
        def _():
            dw_ref[...] = jnp.zeros_like(dw_ref)
            ds_ref[...] = jnp.zeros_like(ds_ref)
            ext2[tm:tm + POOL_HALO, :] = jnp.zeros((POOL_HALO, c), F32)

        x = x_ref[...]
        ext[0:POOL_HALO, :] = jnp.where(t > 0, xh_ref[...], 0.0)
        ext[POOL_HALO:POOL_HALO + tm, :] = x
        cnt, lane, group = _pool_counts(t, tm, c)
        y = (_pool_select(_pool_window_sums(ext, tm, -1), lane, group) / cnt - x).astype(BF16)
        w = w_ref[...]
        dyv = dy_ref[...]
        ds_ref[...] += _colsum(_dot(y, w) * dyv)
        do = (dyv * s_ref[...]).astype(BF16)
        dw_ref[...] += _dot_tn(y, do)
        dyp = _dot_nt(do, w)
        ext2[0:tm, :] = dyp / cnt
        dx = _pool_select(_pool_window_sums(ext2, tm, 1), lane, group) - dyp
        ext2[tm:tm + POOL_HALO, :] = ext2[0:POOL_HALO, :]
        rows = _row_ids(t, tm)
        dx_ref[...] = jnp.where(rows >= PAD_ROWS, dx, 0.0).astype(BF16)

    rev = lambda i: (ni - 1 - i, 0)
    return pl.pallas_call(
        body,
        grid=(ni,),
        in_specs=[
            pl.BlockSpec((tm, c), rev),
            pl.BlockSpec((POOL_HALO, c), lambda i: (jnp.maximum((ni - 1 - i) * hpt - 1, 0), 0)),
            pl.BlockSpec((tm, c), rev),
            pl.BlockSpec((c, c), lambda i: (0, 0)),
            pl.BlockSpec((1, c), lambda i: (0, 0)),
        ],
        out_specs=[
            pl.BlockSpec((tm, c), rev),
            pl.BlockSpec((c, c), lambda i: (0, 0)),
            pl.BlockSpec((1, c), lambda i: (0, 0)),
        ],
        out_shape=[
            jax.ShapeDtypeStruct((lp, c), BF16),
            jax.ShapeDtypeStruct((c, c), F32),
            jax.ShapeDtypeStruct((1, c), F32),
        ],
        scratch_shapes=[pltpu.VMEM((tm + POOL_HALO, c), F32), pltpu.VMEM((tm + POOL_HALO, c), F32)],
        compiler_params=_cparams(("arbitrary",)),
        name="pool_bwd",
    )(z, z, dcat, wbd, scale)


def _conv_taps(ext, w_ref, tm, first):
    acc = w_ref[0:1, :] * ext[pl.ds(first, tm), :]
    for k in range(1, CONV_WIDTH):
        acc = acc + w_ref[k:k + 1, :] * ext[pl.ds(first + k, tm), :]
    return acc


def _conv_fwd(z, w_dw, b_dw, ln_g, ln_b, w_pw):
    lp = z.shape[0]
    c = w_pw.shape[0]
    tm = _pick_tm(lp, MIX_TM, CHUNK)

    def body(a_ref, gt_ref, w_ref, bd_ref, g_ref, b_ref, pw_ref, y_ref, ext):
        i = pl.program_id(0)

        @pl.when(i == 0)
        def _():
            ext[0:CONV_HALO, :] = jnp.zeros((CONV_HALO, c), F32)

        ext[CONV_HALO:CONV_HALO + tm, :] = a_ref[...] * _sigmoid(gt_ref[...])
        acc = _conv_taps(ext, w_ref, tm, CONV_HALO - CONV_WIDTH + 1) + bd_ref[...]
        ext[0:CONV_HALO, :] = ext[tm:tm + CONV_HALO, :]
        xhat, _ = _ln_stats(acc)
        n = xhat * g_ref[...] + b_ref[...]
        act = n * _sigmoid(n)
        y_ref[...] = _dot(act.astype(BF16), pw_ref[...]).astype(BF16)

    const = lambda i: (0, 0)
    return pl.pallas_call(
        body,
        grid=(lp // tm,),
        in_specs=[
            pl.BlockSpec((tm, c), lambda i: (i, 1)),
            pl.BlockSpec((tm, c), lambda i: (i, 2)),
            pl.BlockSpec((CONV_HALO, c), const),
            pl.BlockSpec((1, c), const),
            pl.BlockSpec((1, c), const),
            pl.BlockSpec((1, c), const),
            pl.BlockSpec((c, c), const),
        ],
        out_specs=pl.BlockSpec((tm, c), lambda i: (i, 0)),
        out_shape=jax.ShapeDtypeStruct((lp, c), BF16),
        scratch_shapes=[pltpu.VMEM((tm + CONV_HALO, c), F32)],
        compiler_params=_cparams(("arbitrary",)),
        name="conv_fwd",
    )(z, z, w_dw, b_dw, ln_g, ln_b, w_pw)


def _conv_bwd(z, dcat, w_dw, b_dw, ln_g, ln_b, w_pw):
    lp = z.shape[0]
    c = w_pw.shape[0]
    tm = _pick_tm(lp, MIX_TM, CHUNK)
    ni = lp // tm
    hpt = tm // CONV_HALO
    first = CONV_HALO - CONV_WIDTH + 1

    def body(a_ref, gt_ref, ah_ref, gh_ref, dy_ref, w_ref, bd_ref, g_ref, b_ref, pw_ref,
             dca_ref, dcg_ref, dw_ref, dbd_ref, dg_ref, db_ref, dpw_ref, extu, extd):
        i = pl.program_id(0)
        t = ni - 1 - i

        @pl.when(i == 0)
        def _():
            dw_ref[...] = jnp.zeros_like(dw_ref)
            dbd_ref[...] = jnp.zeros_like(dbd_ref)
            dg_ref[...] = jnp.zeros_like(dg_ref)
            db_ref[...] = jnp.zeros_like(db_ref)
            dpw_ref[...] = jnp.zeros_like(dpw_ref)
            extd[tm:tm + CONV_HALO, :] = jnp.zeros((CONV_HALO, c), F32)

        ca = a_ref[...]
        sg = _sigmoid(gt_ref[...])
        extu[0:CONV_HALO, :] = jnp.where(t > 0, ah_ref[...] * _sigmoid(gh_ref[...]), 0.0)
        extu[CONV_HALO:CONV_HALO + tm, :] = ca * sg
        acc = _conv_taps(extu, w_ref, tm, first) + bd_ref[...]
        xhat, rstd = _ln_stats(acc)
        gam = g_ref[...]
        n = xhat * gam + b_ref[...]
        sn = _sigmoid(n)
        act = (n * sn).astype(BF16)
        do = dy_ref[...].astype(BF16)
        dpw_ref[...] += _dot_tn(act, do)
        dn = _dot_nt(do, pw_ref[...]) * (sn * (1.0 + n * (1.0 - sn)))
        dg_ref[...] += _colsum(dn * xhat)
        db_ref[...] += _colsum(dn)
        dyc = _ln_bwd(dn, xhat, rstd, gam)
        dbd_ref[...] += _colsum(dyc)
        extd[0:tm, :] = dyc
        du = None
        for k in range(CONV_WIDTH):
            dw_ref[k:k + 1, :] += _colsum(dyc * extu[pl.ds(first + k, tm), :])
            term = w_ref[k:k + 1, :] * extd[pl.ds(CONV_WIDTH - 1 - k, tm), :]
            du = term if du is None else du + term
        extd[tm:tm + CONV_HALO, :] = extd[0:CONV_HALO, :]
        du = jnp.where(_row_ids(t, tm) >= PAD_ROWS, du, 0.0)
        dca_ref[...] = (du * sg).astype(BF16)
        dcg_ref[...] = (du * ca * sg * (1.0 - sg)).astype(BF16)

    const = lambda i: (0, 0)
    rev = lambda col: (lambda i: (ni - 1 - i, col))
    halo = lambda col: (lambda i: (jnp.maximum((ni - 1 - i) * hpt - 1, 0), col))
    return pl.pallas_call(
        body,
        grid=(ni,),
        in_specs=[
            pl.BlockSpec((tm, c), rev(1)),
            pl.BlockSpec((tm, c), rev(2)),
            pl.BlockSpec((CONV_HALO, c), halo(1)),
            pl.BlockSpec((CONV_HALO, c), halo(2)),
            pl.BlockSpec((tm, c), rev(1)),
            pl.BlockSpec((CONV_HALO, c), const),
            pl.BlockSpec((1, c), const),
            pl.BlockSpec((1, c), const),
            pl.BlockSpec((1, c), const),
            pl.BlockSpec((c, c), const),
        ],
        out_specs=[
            pl.BlockSpec((tm, c), rev(0)),
            pl.BlockSpec((tm, c), rev(0)),
            pl.BlockSpec((CONV_HALO, c), const),
            pl.BlockSpec((1, c), const),
            pl.BlockSpec((1, c), const),
            pl.BlockSpec((1, c), const),
            pl.BlockSpec((c, c), const),
        ],
        out_shape=[
            jax.ShapeDtypeStruct((lp, c), BF16),
            jax.ShapeDtypeStruct((lp, c), BF16),
            jax.ShapeDtypeStruct((CONV_HALO, c), F32),
            jax.ShapeDtypeStruct((1, c), F32),
            jax.ShapeDtypeStruct((1, c), F32),
            jax.ShapeDtypeStruct((1, c), F32),
            jax.ShapeDtypeStruct((c, c), F32),
        ],
        scratch_shapes=[pltpu.VMEM((tm + CONV_HALO, c), F32), pltpu.VMEM((tm + CONV_HALO, c), F32)],
        compiler_params=_cparams(("arbitrary",)),
        name="conv_bwd",
    )(z, z, z, z, dcat, w_dw, b_dw, ln_g, ln_b, w_pw)


def _rope(x, cos, sgn_sin):
    return x * cos + pltpu.roll(x, LANE // 2, 1) * sgn_sin


def _rope_t(dy, cos, sgn_sin):
    return dy * cos + pltpu.roll(dy * sgn_sin, LANE // 2, 1)


def _ret_chunk_fwd(q, k, v, cos, sn, state, dmv, qdv, kdv):
    qr = _rope(q, cos, sn)
    kr = _rope(k, cos, sn) * (LANE ** -0.5)
    am = _dot_nt(qr.astype(BF16), kr.astype(BF16)) * dmv
    o = _dot(am.astype(BF16), v.astype(BF16)) + _dot((qr * qdv).astype(BF16), state.astype(BF16))
    return qr, kr, am, o


def _ret_specs(tm, q_blk, tile_of):
    def mk(col):
        return pl.BlockSpec((tm, LANE), lambda i: (tile_of(i), col))
    return [mk(q_blk + j) for j in range(4 * RET_HEADS)]


def _ret_tables(h, dm_ref, qd_ref, kd_ref, cd_ref, gn_ref):
    return dm_ref[h], qd_ref[h], kd_ref[h], cd_ref[h, 0:1, :], gn_ref[:, h * LANE:(h + 1) * LANE]


def _ret_fwd(z, rope_c, rope_s, dm, qd, kd, cd, gn, q_blk):
    lp = z.shape[0]
    tm = _pick_tm(lp, MIX_TM, CHUNK)
    nc = tm // CHUNK
    ni = lp // tm
    nh = RET_HEADS

    def body(*refs):
        q_refs, k_refs, v_refs, g_refs = refs[0:nh], refs[nh:2 * nh], refs[2 * nh:3 * nh], refs[3 * nh:4 * nh]
        c_ref, s_ref, dm_ref, qd_ref, kd_ref, cd_ref, gn_ref, y_ref, st_ref, state = refs[4 * nh:]

        @pl.when(pl.program_id(0) == 0)
        def _():
            state[...] = jnp.zeros_like(state)

        def chunk(c, carry):
            rows = pl.ds(pl.multiple_of(c * CHUNK, CHUNK), CHUNK)
            cos, sn = c_ref[rows, :], s_ref[rows, :]
            loaded = [(q_refs[h][rows, :], k_refs[h][rows, :], v_refs[h][rows, :], g_refs[h][rows, :], state[h])
                      for h in range(nh)]
            results = []
            for h, (q, k, v, gg, s0) in enumerate(loaded):
                dmv, qdv, kdv, cdv, gnv = _ret_tables(h, dm_ref, qd_ref, kd_ref, cd_ref, gn_ref)
                _, kr, _, o = _ret_chunk_fwd(q, k, v, cos, sn, s0, dmv, qdv, kdv)
                s1 = s0 * cdv + _dot_tn((kr * kdv).astype(BF16), v.astype(BF16))
                on, _ = _ln_stats(o)
                results.append((s0, s1, (gg * _sigmoid(gg) * (on * gnv)).astype(BF16)))
            for h, (s0, s1, y) in enumerate(results):
                st_ref[c, h, :, :] = s0
                state[h] = s1
                y_ref[rows, h * LANE:(h + 1) * LANE] = y
            return carry

        lax.fori_loop(0, nc, chunk, 0)

    const3 = lambda s: pl.BlockSpec(s, lambda i: (0, 0, 0))
    return pl.pallas_call(
        body,
        grid=(ni,),
        in_specs=_ret_specs(tm, q_blk, lambda i: i) + [
            pl.BlockSpec((tm, LANE), lambda i: (i, 0)),
            pl.BlockSpec((tm, LANE), lambda i: (i, 0)),
            const3((nh, CHUNK, CHUNK)), const3((nh, CHUNK, LANE)), const3((nh, CHUNK, LANE)), const3((nh, 8, LANE)),
            pl.BlockSpec((1, nh * LANE), lambda i: (0, 0)),
        ],
        out_specs=[
            pl.BlockSpec((tm, nh * LANE), lambda i: (i, 0)),
            pl.BlockSpec((nc, nh, LANE, LANE), lambda i: (i, 0, 0, 0)),
        ],
        out_shape=[
            jax.ShapeDtypeStruct((lp, nh * LANE), BF16),
            jax.ShapeDtypeStruct((lp // CHUNK, nh, LANE, LANE), F32),
        ],
        scratch_shapes=[pltpu.VMEM((nh, LANE, LANE), F32)],
        compiler_params=_cparams(("arbitrary",)),
        name="ret_fwd",
    )(*([z] * (4 * nh)), rope_c, rope_s, dm, qd, kd, cd, gn)


def _ret_bwd(z, dcat, states, rope_c, rope_s, dm, qd, kd, cd, gn, q_blk, dy_blk):
    lp = z.shape[0]
    tm = _pick_tm(lp, MIX_TM, CHUNK)
    nc = tm // CHUNK
    ni = lp // tm

    nh = RET_HEADS

    def body(*refs):
        q_refs, k_refs, v_refs, g_refs = refs[0:nh], refs[nh:2 * nh], refs[2 * nh:3 * nh], refs[3 * nh:4 * nh]
        (dy_ref, st_ref, c_ref, s_ref, dm_ref, qd_ref, kd_ref, cd_ref, gn_ref,
         dq_ref, dk_ref, dv_ref, dgt_ref, dgn_ref, dstate) = refs[4 * nh:]
        i = pl.program_id(0)
        t = ni - 1 - i

        @pl.when(i == 0)
        def _():
            dstate[...] = jnp.zeros_like(dstate)
            dgn_ref[...] = jnp.zeros_like(dgn_ref)

        def chunk(cc, carry):
            c = nc - 1 - cc
            rows = pl.ds(pl.multiple_of(c * CHUNK, CHUNK), CHUNK)
            cos, sn = c_ref[rows, :], s_ref[rows, :]
            keep = (lax.broadcasted_iota(jnp.int32, (CHUNK, 1), 0) + (t * tm + c * CHUNK)) >= PAD_ROWS
            loaded = [(q_refs[h][rows, :], k_refs[h][rows, :], v_refs[h][rows, :], g_refs[h][rows, :],
                       dy_ref[rows, h * LANE:(h + 1) * LANE], st_ref[c, h, :, :], dstate[h], dgn_ref[:, h * LANE:(h + 1) * LANE])
                      for h in range(nh)]
            results = []
            for h, (q, k, v, gg, dyv, s0, ds1, dgn) in enumerate(loaded):
                dmv, qdv, kdv, cdv, gnv = _ret_tables(h, dm_ref, qd_ref, kd_ref, cd_ref, gn_ref)
                vb = v.astype(BF16)
                qr, kr, am, o = _ret_chunk_fwd(q, k, v, cos, sn, s0, dmv, qdv, kdv)
                on, rstd = _ln_stats(o)
                sg = _sigmoid(gg)
                sl = gg * sg
                dgn = dgn + _colsum(dyv * sl * on)
                dgt = dyv * (on * gnv) * (sg * (1.0 + gg * (1.0 - sg)))
                dob = _ln_bwd(dyv * sl, on, rstd, gnv).astype(BF16)
                ds1b = ds1.astype(BF16)
                qdb = (qr * qdv).astype(BF16)
                kdb = (kr * kdv).astype(BF16)
                da = (_dot_nt(dob, vb) * dmv).astype(BF16)
                dv = _dot_tn(am.astype(BF16), dob) + _dot(kdb, ds1b)
                dqr = _dot(da, kr.astype(BF16)) + _dot_nt(dob, s0.astype(BF16)) * qdv
                dkr = _dot_tn(da, qr.astype(BF16)) + _dot_nt(vb, ds1b) * kdv
                ds0 = ds1 * cdv + _dot_tn(qdb, dob)
                dq = _rope_t(dqr, cos, sn)
                dk = _rope_t(dkr * (LANE ** -0.5), cos, sn)
                masked = [jnp.where(keep, t, 0.0).astype(BF16) for t in (dq, dk, dv, dgt)]
                results.append((masked, ds0, dgn))
            for h, ((dq, dk, dv, dgt), ds0, dgn) in enumerate(results):
                lanes = slice(h * LANE, (h + 1) * LANE)
                dq_ref[rows, lanes] = dq
                dk_ref[rows, lanes] = dk
                dv_ref[rows, lanes] = dv
                dgt_ref[rows, lanes] = dgt
                dstate[h] = ds0
                dgn_ref[:, lanes] = dgn
            return carry

        lax.fori_loop(0, nc, chunk, 0)

    rev = lambda i: ni - 1 - i
    const3 = lambda s: pl.BlockSpec(s, lambda i: (0, 0, 0))
    out_blk = pl.BlockSpec((tm, nh * LANE), lambda i: (rev(i), 0))
    out_sds = jax.ShapeDtypeStruct((lp, nh * LANE), BF16)
    return pl.pallas_call(
        body,
        grid=(ni,),
        in_specs=_ret_specs(tm, q_blk, rev) + [
            pl.BlockSpec((tm, nh * LANE), lambda i: (rev(i), dy_blk // nh)),
            pl.BlockSpec((nc, nh, LANE, LANE), lambda i: (rev(i), 0, 0, 0)),
            pl.BlockSpec((tm, LANE), lambda i: (rev(i), 0)),
            pl.BlockSpec((tm, LANE), lambda i: (rev(i), 0)),
            const3((nh, CHUNK, CHUNK)), const3((nh, CHUNK, LANE)), const3((nh, CHUNK, LANE)), const3((nh, 8, LANE)),
            pl.BlockSpec((1, nh * LANE), lambda i: (0, 0)),
        ],
        out_specs=[out_blk, out_blk, out_blk, out_blk, pl.BlockSpec((1, nh * LANE), lambda i: (0, 0))],
        out_shape=[out_sds, out_sds, out_sds, out_sds, jax.ShapeDtypeStruct((1, nh * LANE), F32)],
        scratch_shapes=[pltpu.VMEM((nh, LANE, LANE), F32)],
        compiler_params=_cparams(("arbitrary",)),
        name="ret_bwd",
    )(*([z] * (4 * nh)), dcat, states, rope_c, rope_s, dm, qd, kd, cd, gn)


def _adamw(parts_list, w, m, v):
    nl, r, c = w.shape
    assert len(parts_list) == nl
    tr = _pick_tm(r, max(8, (1 << 17) // c), 8)
    nr = r // tr

    def body(*refs):
        p_refs = refs[:nl]
        w_ref, m_ref, v_ref, g_ref, d_ref, nm_ref, nv_ref = refs[nl:]
        layer = pl.program_id(0)
        for k in range(nl):
            @pl.when(layer == k)
            def _(k=k):
                g = p_refs[k][0].astype(F32)
                for j in range(1, N_DEV):
                    g = g + p_refs[k][j].astype(F32)
                m1 = ADAM_B1 * m_ref[...] + (1.0 - ADAM_B1) * g
                v1 = ADAM_B2 * v_ref[...] + (1.0 - ADAM_B2) * (g * g)
                m_hat = m1 / (1.0 - ADAM_B1 ** ADAM_STEP)
                v_hat = v1 / (1.0 - ADAM_B2 ** ADAM_STEP)
                g_ref[...] = g
                d_ref[...] = -ADAM_LR * (m_hat / (jnp.sqrt(v_hat) + ADAM_EPS) + ADAM_WD * w_ref[...])
                nm_ref[...] = m1
                nv_ref[...] = v1

    def parts_spec(k):
        return pl.BlockSpec((N_DEV, tr, c), lambda l, i: (0, jnp.where(l == k, i, jnp.where(l < k, 0, nr - 1)), 0))

    blk = pl.BlockSpec((None, tr, c), lambda l, i: (l, i, 0))
    sds = jax.ShapeDtypeStruct((nl, r, c), F32)
    return pl.pallas_call(
        body,
        grid=(nl, nr),
        in_specs=[parts_spec(k) for k in range(nl)] + [blk, blk, blk],
        out_specs=[blk, blk, blk, blk],
        out_shape=[sds, sds, sds, sds],
        compiler_params=_cparams(("arbitrary", "arbitrary")),
        name="adamw",
    )(*parts_list, w, m, v)


def _flip(v, bit):
    return 1 - v if bit else v


def _exchange(arrs, scatter):
    n = len(arrs)
    shapes = [tuple(a.shape[1:] if scatter else a.shape) for a in arrs]

    def body(*refs):
        x_refs, o_refs = refs[:n], refs[n:2 * n]
        send_sems, recv_sems, local_sems = refs[2 * n:]
        mx, my, mc = lax.axis_index("x"), lax.axis_index("y"), lax.axis_index("c")
        me = 4 * mx + 2 * my + mc

        def peer_of(k):
            return (_flip(mx, (k >> 2) & 1), _flip(my, (k >> 1) & 1), _flip(mc, k & 1))

        def copy(a, k):
            peer = peer_of(k)
            src = x_refs[a].at[4 * peer[0] + 2 * peer[1] + peer[2]] if scatter else x_refs[a]
            return pltpu.make_async_remote_copy(
                src_ref=src, dst_ref=o_refs[a].at[me], send_sem=send_sems.at[a, k - 1],
                recv_sem=recv_sems.at[a, k - 1], device_id=peer, device_id_type=pl.DeviceIdType.MESH)

        def arrival(a, k):
            peer = peer_of(k)
            slot = o_refs[a].at[4 * peer[0] + 2 * peer[1] + peer[2]]
            return pltpu.make_async_remote_copy(
                src_ref=slot, dst_ref=slot, send_sem=send_sems.at[a, k - 1], recv_sem=recv_sems.at[a, k - 1],
                device_id=peer, device_id_type=pl.DeviceIdType.MESH)

        locals_ = [pltpu.make_async_copy(x_refs[a].at[me] if scatter else x_refs[a], o_refs[a].at[me],
                                         local_sems.at[a]) for a in range(n)]
        sends = [copy(a, k) for a in range(n) for k in range(1, N_DEV)]
        for cp in locals_ + sends:
            cp.start()
        for a in range(n):
            for k in range(1, N_DEV):
                arrival(a, k).wait_recv()
        for cp in sends:
            cp.wait_send()
        for cp in locals_:
            cp.wait()

    hbm = pl.BlockSpec(memory_space=pltpu.HBM)
    return pl.pallas_call(
        body,
        in_specs=[hbm] * n,
        out_specs=[hbm] * n,
        out_shape=[jax.ShapeDtypeStruct((N_DEV,) + s, a.dtype) for s, a in zip(shapes, arrs)],
        scratch_shapes=[
            pltpu.SemaphoreType.DMA((n, N_DEV - 1)),
            pltpu.SemaphoreType.DMA((n, N_DEV - 1)),
            pltpu.SemaphoreType.DMA((n,)),
        ],
        name="reduce_scatter_parts" if scatter else "all_gather",
    )(*arrs)


def _exchange_descriptors(x_refs, land_refs, send_sems, recv_sems, scatter):
    mx, my, mc = lax.axis_index("x"), lax.axis_index("y"), lax.axis_index("c")
    me = 4 * mx + 2 * my + mc
    sends, arrivals = [], []
    for a in range(len(x_refs)):
        for k in range(1, N_DEV):
            peer = (_flip(mx, (k >> 2) & 1), _flip(my, (k >> 1) & 1), _flip(mc, k & 1))
            slot = 4 * peer[0] + 2 * peer[1] + peer[2]
            si = a * (N_DEV - 1) + k - 1
            sems = dict(send_sem=send_sems.at[si], recv_sem=recv_sems.at[si],
                        device_id=peer, device_id_type=pl.DeviceIdType.MESH)
            sends.append(pltpu.make_async_remote_copy(
                src_ref=x_refs[a].at[slot] if scatter else x_refs[a], dst_ref=land_refs[a].at[me], **sems))
            arrivals.append(pltpu.make_async_remote_copy(
                src_ref=land_refs[a].at[slot], dst_ref=land_refs[a].at[slot], **sems))
    return sends, arrivals


def _exchange_start(arrs, scatter, name, after=None):
    n = len(arrs)
    shapes = [tuple(a.shape[1:] if scatter else a.shape) for a in arrs]
    lands = [lax.empty((N_DEV,) + s, a.dtype) for s, a in zip(shapes, arrs)]
    extra = [] if after is None else [after]

    def body(*refs):
        x_refs, land_refs = refs[:n], refs[n:2 * n]
        send_sems, recv_sems = refs[2 * n + len(extra)], refs[2 * n + len(extra) + 1]
        token = refs[-1]
        sends, _ = _exchange_descriptors(x_refs, land_refs, send_sems, recv_sems, scatter)
        for cp in sends:
            cp.start()
        token[...] = jnp.zeros_like(token)

    hbm = pl.BlockSpec(memory_space=pltpu.HBM)
    sem = pl.BlockSpec(memory_space=pltpu.SEMAPHORE)
    sem_type = pltpu.SemaphoreType.DMA((n * (N_DEV - 1),))
    operands = [pltpu.with_memory_space_constraint(a, pltpu.HBM) for a in list(arrs) + lands]
    out = pl.pallas_call(
        body,
        in_specs=[hbm] * (2 * n) + [pl.BlockSpec(memory_space=pl.ANY)] * len(extra),
        out_specs=[sem, sem] + [hbm] * (2 * n) + [pl.BlockSpec(memory_space=pltpu.VMEM)],
        out_shape=[sem_type, sem_type] + [pltpu.HBM(a.shape, a.dtype) for a in operands]
        + [jax.ShapeDtypeStruct((8, LANE), F32)],
        input_output_aliases={i: 2 + i for i in range(2 * n)},
        compiler_params=pltpu.CompilerParams(has_side_effects=pltpu.SideEffectType.DATAFLOW_SIDE_EFFECTING),
        name=name,
    )(*operands, *extra)
    return (out[0], out[1], list(out[2:2 + n]), list(out[2 + n:2 + 2 * n]), scatter), out[-1]


def _exchange_wait(handle, after, name):
    send_sems, recv_sems, x_thru, land_thru, scatter = handle
    n = len(x_thru)

    def body(*refs):
        x_refs, land_refs = refs[:n], refs[n:2 * n]
        sends, arrivals = _exchange_descriptors(x_refs, land_refs, refs[2 * n], refs[2 * n + 1], scatter)
        for cp in sends:
            cp.wait_send()
        for cp in arrivals:
            cp.wait_recv()

    hbm = pl.BlockSpec(memory_space=pltpu.HBM)
    sem = pl.BlockSpec(memory_space=pltpu.SEMAPHORE)
    out = pl.pallas_call(
        body,
        in_specs=[hbm] * (2 * n) + [sem, sem, pl.BlockSpec(memory_space=pl.ANY)],
        out_specs=[hbm] * (2 * n),
        out_shape=[pltpu.HBM(a.shape, a.dtype) for a in x_thru + land_thru],
        input_output_aliases={i: i for i in range(2 * n)},
        compiler_params=pltpu.CompilerParams(has_side_effects=pltpu.SideEffectType.DATAFLOW_SIDE_EFFECTING),
        name=name,
    )(*x_thru, *land_thru, send_sems, recv_sems, after)
    me = 4 * lax.axis_index("x") + 2 * lax.axis_index("y") + lax.axis_index("c")
    got = []
    for x, land in zip(out[:n], out[n:]):
        own = lax.dynamic_index_in_dim(x, me, 0, keepdims=False) if scatter else x
        got.append(lax.dynamic_update_index_in_dim(land, own, me, 0))
    return got


PACK_ALIGN = 2048


def _padded(n):
    return -(-n // PACK_ALIGN) * PACK_ALIGN


def _pad_to(a, axis, size):
    pad = [(0, 0)] * a.ndim
    pad[axis] = (0, size - a.shape[axis])
    return jnp.pad(a, pad)


def _pack(arrs, lead=0):
    flat = []
    for a in arrs:
        v = a.reshape(a.shape[:lead] + (-1,))
        flat.append(_pad_to(v, lead, _padded(v.shape[lead])))
    out = jnp.concatenate(flat, axis=lead)
    return out.reshape(out.shape[:lead] + (-1, LANE))


def _unpack(slab, shapes, lead=0):
    flat = slab.reshape(slab.shape[:lead] + (-1,))
    out, off = [], 0
    for s in shapes:
        n = math.prod(s)
        out.append(flat[..., off:off + n].reshape(slab.shape[:lead] + tuple(s)))
        off += _padded(n)
    return out


def _unshard(parts, ax):
    return jnp.concatenate([parts[j] for j in range(N_DEV)], axis=ax)


def _to_shards(full, ax):
    n = full.shape[ax] // N_DEV
    return jnp.stack([lax.slice_in_dim(full, j * n, (j + 1) * n, axis=ax) for j in range(N_DEV)])


def _pad_halves(w, hp):
    h = w.shape[-1] // 2
    zeros = jnp.zeros(w.shape[:-1] + (hp - h,), w.dtype)
    return jnp.concatenate([w[..., :h], zeros, w[..., h:], zeros], axis=-1)


def _unpad_halves(w, h):
    hp = w.shape[-1] // 2
    return jnp.concatenate([w[..., :h], w[..., hp:hp + h]], axis=-1)


SMALL_SHARDED = (("meta", 1), ("conv_dw", 2), ("ln_g", 2), ("ln_b", 2))
MATMUL_WEIGHTS = ("ffn1_w13", "ffn1_w2", "w_in", "conv_pw", "w_out", "ffn2_w13", "ffn2_w2")
REPLICATED = ("ln_in_g", "ln_in_b", "pool_w", "pool_scale", "conv_db", "conv_ln_g", "conv_ln_b", "ret_gn_g")
WEIGHT_ORDER = ("meta", "ln_in_g", "ln_in_b", "ffn1_w13", "ffn1_w2", "w_in", "pool_w", "pool_scale", "conv_dw",
                "conv_db", "conv_ln_g", "conv_ln_b", "conv_pw", "ret_gn_g", "w_out", "ffn2_w13", "ffn2_w2",
                "ln_g", "ln_b")


def _retention_tables(lp, heads):
    pos = jnp.arange(lp, dtype=F32) - PAD_ROWS
    inv_freq = ROPE_BASE ** (-jnp.arange(0, LANE, 2, dtype=F32) / LANE)
    ang = pos[:, None] * inv_freq[None, :]
    cos, sin = jnp.cos(ang), jnp.sin(ang)
    rope_c = jnp.concatenate([cos, cos], axis=1)
    rope_s = jnp.concatenate([-sin, sin], axis=1)
    log_gamma = jnp.log(1.0 - 2.0 ** (-5.0 - jnp.arange(heads, dtype=F32)))
    i = jnp.arange(CHUNK, dtype=F32)
    dm = jnp.exp(log_gamma[:, None, None] * jnp.abs(i[:, None] - i[None, :]))
    lanes = lambda t: jnp.broadcast_to(t[:, :, None], t.shape + (LANE,))
    qd = lanes(jnp.exp(log_gamma[:, None] * (i + 1.0)))
    kd = lanes(jnp.exp(log_gamma[:, None] * (CHUNK - 1.0 - i)))
    cd = lanes(jnp.broadcast_to(jnp.exp(log_gamma * CHUNK)[:, None], (heads, 8)))
    return rope_c, rope_s, dm, qd, kd, cd


def _block_diag(w):
    g, n, _ = w.shape
    rows = []
    for i in range(g):
        rows.append(jnp.concatenate([w[i] if j == i else jnp.zeros((n, n), w.dtype) for j in range(g)], axis=1))
    return jnp.concatenate(rows, axis=0)


def kernel(x, meta, ln_in_g, ln_in_b, ffn1_w13, ffn1_w2, w_in, pool_w, pool_scale, conv_dw, conv_db, conv_ln_g, conv_ln_b, conv_pw, ret_gn_g, w_out, ffn2_w13, ffn2_w2, ln_g, ln_b, loss_target, m_meta, m_ln_in_g, m_ln_in_b, m_ffn1_w13, m_ffn1_w2, m_w_in, m_pool_w, m_pool_scale, m_conv_dw, m_conv_db, m_conv_ln_g, m_conv_ln_b, m_conv_pw, m_ret_gn_g, m_w_out, m_ffn2_w13, m_ffn2_w2, m_ln_g, m_ln_b, v_meta, v_ln_in_g, v_ln_in_b, v_ffn1_w13, v_ffn1_w2, v_w_in, v_pool_w, v_pool_scale, v_conv_dw, v_conv_db, v_conv_ln_g, v_conv_ln_b, v_conv_pw, v_ret_gn_g, v_w_out, v_ffn2_w13, v_ffn2_w2, v_ln_g, v_ln_b):
    local = dict(meta=meta, ln_in_g=ln_in_g, ln_in_b=ln_in_b, ffn1_w13=ffn1_w13, ffn1_w2=ffn1_w2, w_in=w_in,
                 pool_w=pool_w, pool_scale=pool_scale, conv_dw=conv_dw, conv_db=conv_db, conv_ln_g=conv_ln_g,
                 conv_ln_b=conv_ln_b, conv_pw=conv_pw, ret_gn_g=ret_gn_g, w_out=w_out, ffn2_w13=ffn2_w13,
                 ffn2_w2=ffn2_w2, ln_g=ln_g, ln_b=ln_b)
    mom1 = dict(meta=m_meta, ln_in_g=m_ln_in_g, ln_in_b=m_ln_in_b, ffn1_w13=m_ffn1_w13, ffn1_w2=m_ffn1_w2,
                w_in=m_w_in, pool_w=m_pool_w, pool_scale=m_pool_scale, conv_dw=m_conv_dw, conv_db=m_conv_db,
                conv_ln_g=m_conv_ln_g, conv_ln_b=m_conv_ln_b, conv_pw=m_conv_pw, ret_gn_g=m_ret_gn_g,
                w_out=m_w_out, ffn2_w13=m_ffn2_w13, ffn2_w2=m_ffn2_w2, ln_g=m_ln_g, ln_b=m_ln_b)
    mom2 = dict(meta=v_meta, ln_in_g=v_ln_in_g, ln_in_b=v_ln_in_b, ffn1_w13=v_ffn1_w13, ffn1_w2=v_ffn1_w2,
                w_in=v_w_in, pool_w=v_pool_w, pool_scale=v_pool_scale, conv_dw=v_conv_dw, conv_db=v_conv_db,
                conv_ln_g=v_conv_ln_g, conv_ln_b=v_conv_ln_b, conv_pw=v_conv_pw, ret_gn_g=v_ret_gn_g,
                w_out=v_w_out, ffn2_w13=v_ffn2_w13, ffn2_w2=v_ffn2_w2, ln_g=v_ln_g, ln_b=v_ln_b)

    depth = ffn1_w13.shape[0]
    alpha = (2.0 * depth) ** 0.25
    seq, d = x.shape[1], x.shape[2]
    lp = FRONT + seq
    h_loc = ffn1_w2.shape[1]
    hp = -(-h_loc // LANE) * LANE
    c_pool = pool_scale.shape[1]
    c_conv = conv_db.shape[1]
    q_blk = (c_pool + 2 * c_conv) // LANE
    dy_blk = (c_pool + c_conv) // LANE
    heads = ret_gn_g.shape[1] // LANE
    assert meta.shape[0] == N_META and heads == RET_HEADS and conv_dw.shape[1] == CONV_WIDTH
    assert ffn1_w13.shape[2] == 2 * h_loc

    def to_wire(name, w):
        if name in ("ffn1_w13", "ffn2_w13"):
            return _pad_halves(w, hp)
        if name in ("ffn1_w2", "ffn2_w2"):
            return _pad_to(w, w.ndim - 2, hp)
        return w

    def from_wire(name, w):
        if name in ("ffn1_w13", "ffn2_w13"):
            return _unpad_halves(w, h_loc)
        if name in ("ffn1_w2", "ffn2_w2"):
            return w[..., :h_loc, :]
        return w

    small_names = [n for n, _ in SMALL_SHARDED]
    small_ax = dict(SMALL_SHARDED)
    n_mm = len(MATMUL_WEIGHTS)
    first = ("ffn1_w13", "ffn1_w2")
    wire = lambda l, names: [to_wire(n, local[n][l]).astype(BF16) for n in names]
    rest = [n for n in MATMUL_WEIGHTS if n not in first]
    groups = [(wire(0, first) + [_pack([local[n] for n in small_names])]), wire(0, rest)]
    groups += [wire(l, MATMUL_WEIGHTS) for l in range(1, depth)]
    handles, token = [], None
    for gi, arrs in enumerate(groups):
        handle, token = _exchange_start(arrs, False, f"gather_start_{gi}", after=token)
        handles.append(handle)
    got = _exchange_wait(handles[0], token, "gather_wait_0")
    gw_first = dict(zip(first, got))
    small_parts = _unpack(got[-1], [local[n].shape for n in small_names], 1)
    small_full = {n: _unshard(p, small_ax[n]) for n, p in zip(small_names, small_parts)}

    row = lambda v: v.reshape(1, -1)
    ln_params = [(row(ln_in_g), row(ln_in_b))]
    for l in range(depth):
        ln_params += [(row(small_full["ln_g"][l, j]), row(small_full["ln_b"][l, j])) for j in range(3)]
    rope_c, rope_s, dm, qd, kd, cd = _retention_tables(lp, heads)

    def layer_params(l, gw):
        return dict(
            ffn1=(gw["ffn1_w13"], gw["ffn1_w2"]),
            ffn2=(gw["ffn2_w13"], gw["ffn2_w2"]),
            w_in=_unshard(gw["w_in"], 1),
            w_out=gw["w_out"].reshape(-1, d),
            pool=(_block_diag(pool_w[l]).astype(BF16), row(pool_scale[l])),
            conv=(_pad_to(small_full["conv_dw"][l], 0, CONV_HALO), row(conv_db[l]), row(conv_ln_g[l]),
                  row(conv_ln_b[l]), gw["conv_pw"].reshape(c_conv, c_conv)),
            gn=row(ret_gn_g[l]),
        )

    r = [jnp.concatenate([jnp.zeros((PAD_ROWS, d), F32), small_full["meta"], x[0]], axis=0)]
    saved, layers = [], []
    for l in range(depth):
        k = 3 * l
        if l == 0:
            r1, a1, u1 = _ffn_fwd(r[k], *ln_params[k], gw_first["ffn1_w13"], gw_first["ffn1_w2"], alpha)
            gw = dict(gw_first, **dict(zip(rest, _exchange_wait(handles[1], r1, "gather_wait_1"))))
            p = layer_params(l, gw)
        else:
            gw = dict(zip(MATMUL_WEIGHTS, _exchange_wait(handles[l + 1], r[k], f"gather_wait_{l + 1}")))
            p = layer_params(l, gw)
            r1, a1, u1 = _ffn_fwd(r[k], *ln_params[k], *p["ffn1"], alpha)
        layers.append(p)
        z = _proj_fwd(r1, *ln_params[k + 1], p["w_in"])
        y_pool = _pool_fwd(z, *p["pool"])
        y_conv = _conv_fwd(z, *p["conv"])
        y_ret, states = _ret_fwd(z, rope_c, rope_s, dm, qd, kd, cd, p["gn"], q_blk)
        cat = jnp.concatenate([y_pool, y_conv, y_ret], axis=1)
        r2 = _out_fwd(r1, *ln_params[k + 1], cat, p["w_out"], alpha)
        r3, a2, u2 = _ffn_fwd(r2, *ln_params[k + 2], *p["ffn2"], alpha)
        r += [r1, r2, r3]
        saved.append((a1, u1, z, states, cat, a2, u2))

    target = jnp.concatenate([jnp.zeros((FRONT, d), F32), loss_target[0]], axis=0)
    dr, dg, db, loss_cols = _loss_bwd(r[-1], target, *ln_params[-1])
    loss = lax.psum(0.5 * jnp.sum(loss_cols) / d, MESH_AXES)
    ln_grads = {3 * depth: (dg, db)}
    w13_slot = lambda j: (j % 2) * (N_DEV // 2) + j // 2
    g_parts = [None] * depth
    g_rep = [None] * depth
    scatter_jobs = []

    def start_scatter(l, gp, names, extra=()):
        parts = [gp[n].astype(WIRE) for n in names] + list(extra)
        handle, token = _exchange_start(parts, True, f"scatter_start_{len(scatter_jobs)}")
        scatter_jobs.append((handle, l, names))
        return token

    def ffn_grads(dr_out, r_in, a, u, ln_p, wts):
        dr_in, hb, s, dz, dg, db = _ffn_bwd(dr_out, r_in, a, u, *ln_p, *wts, alpha)
        g13 = _mm_tn(hb, dz, bn=2 * hp, slot_of=w13_slot)
        g2 = _mm_tn(s, dr_out, 0.5).reshape(N_DEV, hp, d)
        return dr_in, (dg, db), g13, g2

    for l in reversed(range(depth)):
        p = layers[l]
        k = 3 * l
        a1, u1, z, states, cat, a2, u2 = saved[l]
        gp, gr = {}, {}
        dr, ln_grads[k + 2], gp["ffn2_w13"], gp["ffn2_w2"] = ffn_grads(dr, r[k + 2], a2, u2, ln_params[k + 2], p["ffn2"])
        zero = start_scatter(l, gp, ("ffn2_w13", "ffn2_w2"))[0:1, 0:1]

        dr2 = dr
        dcat = _mm_nt(dr2, p["w_out"])
        gp["w_out"] = _mm_tn(cat, dr2).reshape(N_DEV, -1, d)
        dxp, dwbd, gr["pool_scale"] = _pool_bwd(z, dcat, p["pool"][0], p["pool"][1] + zero)
        dca, dcg, ddw, gr["conv_db"], gr["conv_ln_g"], gr["conv_ln_b"], dpw = _conv_bwd(z, dcat, *p["conv"])
        dq, dk, dv, dgt, gr["ret_gn_g"] = _ret_bwd(z, dcat, states, rope_c, rope_s, dm, qd, kd, cd, p["gn"],
                                                  q_blk, dy_blk)
        dz = jnp.concatenate([dxp, dca, dcg, dq, dk, dv, dgt], axis=1)
        dr, hb, dg, db = _in_bwd(dr2, r[k + 1], dz, *ln_params[k + 1], p["w_in"], alpha)
        ln_grads[k + 1] = (dg, db)
        gp["w_in"] = _to_shards(_mm_tn(hb, dz), 1)
        gp["conv_pw"] = dpw.reshape(N_DEV, -1, c_conv)
        n_grp, grp = pool_w.shape[1], pool_w.shape[2]
        gr["pool_w"] = jnp.stack([dwbd[i * grp:(i + 1) * grp, i * grp:(i + 1) * grp] for i in range(n_grp)])
        gr["conv_dw"] = ddw[:CONV_WIDTH]

        zero = start_scatter(l, gp, ("w_in", "conv_pw", "w_out"))[0:1, 0:1]
        ln_g_in, ln_b_in = ln_params[k]
        dr, ln_grads[k], gp["ffn1_w13"], gp["ffn1_w2"] = ffn_grads(dr, r[k], a1, u1, (ln_g_in + zero, ln_b_in), p["ffn1"])
        g_parts[l], g_rep[l] = gp, gr
        if l > 0:
            zero = start_scatter(l, gp, ("ffn1_w13", "ffn1_w2"))[0:1, 0:1]
            g_prev, b_prev = ln_params[k - 1]
            ln_params[k - 1] = (g_prev + zero, b_prev)

    grad_x = dr[FRONT:][None]
    local_shape = lambda name: ((CONV_WIDTH, c_conv) if name == "conv_dw" else local[name].shape[1:])
    stack_layers = lambda name: jnp.stack([g_rep[l][name].reshape(local_shape(name)) for l in range(depth)])
    small_grad = dict(
        meta=dr[PAD_ROWS:FRONT],
        conv_dw=stack_layers("conv_dw"),
        ln_g=jnp.stack([jnp.stack([ln_grads[3 * l + j + 1][0][0] for j in range(3)]) for l in range(depth)]),
        ln_b=jnp.stack([jnp.stack([ln_grads[3 * l + j + 1][1][0] for j in range(3)]) for l in range(depth)]),
    )
    rep_grad = {n: stack_layers(n) for n in REPLICATED if n not in ("ln_in_g", "ln_in_b")}
    rep_grad["ln_in_g"], rep_grad["ln_in_b"] = ln_grads[0][0][0], ln_grads[0][1][0]

    small_pack8 = _pack([_to_shards(small_grad[n], small_ax[n]) for n in small_names], 1)
    after = start_scatter(0, g_parts[0], ("ffn1_w13", "ffn1_w2"), extra=[small_pack8])
    rep_parts = _exchange([_pack([rep_grad[n] for n in REPLICATED])], False)[0]
    scattered = {}
    for ji, (handle, l, names) in enumerate(scatter_jobs):
        got = _exchange_wait(handle, after, f"scatter_wait_{ji}")
        scattered.update({(l, n): t for n, t in zip(names, got)})
        after = got[0]
    small_scattered = got[-1]

    outs_by_name = {}
    for n in MATMUL_WEIGHTS:
        res = _adamw([scattered[l, n] for l in range(depth)], *[to_wire(n, src[n]) for src in (local, mom1, mom2)])
        outs_by_name[n] = [from_wire(n, t) for t in res]
    for names, parts in ((small_names, small_scattered), (REPLICATED, rep_parts)):
        shapes = [local[n].shape for n in names]
        res = _adamw([parts], *[_pack([src[n] for n in names])[None] for src in (local, mom1, mom2)])
        for n, vals in zip(names, zip(*[_unpack(t[0], shapes) for t in res])):
            outs_by_name[n] = list(vals)

    outs = [loss, grad_x]
    for kind in range(4):
        outs += [outs_by_name[n][kind] for n in WEIGHT_ORDER]
    return tuple(outs)
```

```python
import math

import jax
import jax.numpy as jnp
from jax import lax
from jax.experimental import pallas as pl
from jax.experimental.pallas import tpu as pltpu

F32 = jnp.float32
BF16 = jnp.bfloat16
WIRE = jnp.bfloat16

N_DEV = 8
MESH_AXES = ("x", "y", "c")
CHUNK = 64
N_META = 16
PAD_ROWS = 240
FRONT = PAD_ROWS + N_META
LN_EPS = 1e-5
LANE = 128
POOL_WINDOWS = (2, 4, 8, 16)
CONV_WIDTH = 31
CONV_HALO = 32
POOL_HALO = 16
RET_HEADS = 4
ROPE_BASE = 10000.0

ADAM_LR = 0.001
ADAM_B1 = 0.9
ADAM_B2 = 0.999
ADAM_EPS = 1e-08
ADAM_WD = 0.01
ADAM_STEP = 10

VMEM_LIMIT = 56 * 1024 * 1024
DENSE_TM_FWD = 768
DENSE_TM_BWD = 528
MIX_TM = 768
MM_TN_TK = 1408


def _cparams(sem):
    return pltpu.CompilerParams(dimension_semantics=sem, vmem_limit_bytes=VMEM_LIMIT)


def _pick_tm(lp, target, mult=16):
    best = None
    for t in range(mult, min(lp, target) + 1, mult):
        if lp % t == 0:
            best = t
    assert best is not None, (lp, target, mult)
    return best


def _dot(a, b):
    return jnp.dot(a, b, preferred_element_type=F32)


def _dot_nt(a, b):
    return lax.dot_general(a, b, (((1,), (1,)), ((), ())), preferred_element_type=F32)


def _dot_tn(a, b):
    return lax.dot_general(a, b, (((0,), (0,)), ((), ())), preferred_element_type=F32)


def _sigmoid(x):
    return 0.5 * jnp.tanh(0.5 * x) + 0.5


def _ln_stats(r):
    mu = jnp.mean(r, axis=-1, keepdims=True)
    xc = r - mu
    var = jnp.mean(xc * xc, axis=-1, keepdims=True)
    rstd = lax.rsqrt(var + LN_EPS)
    return xc * rstd, rstd


def _ln_bwd(dh, xhat, rstd, g):
    dxh = dh * g
    m1 = jnp.mean(dxh, axis=-1, keepdims=True)
    m2 = jnp.mean(dxh * xhat, axis=-1, keepdims=True)
    return rstd * (dxh - m1 - xhat * m2)


def _colsum(x):
    return jnp.sum(x, axis=0, keepdims=True)


def _row_ids(tile, tm):
    return lax.broadcasted_iota(jnp.int32, (tm, 1), 0) + tile * tm


def _ffn_fwd(r_prev, g, b, w13g, w2g, alpha):
    lp, d = r_prev.shape
    nf = N_DEV // 2
    fc = w13g.shape[2]
    w2c = w2g.reshape(nf, fc, d)
    tm = _pick_tm(lp, DENSE_TM_FWD)

    def body(r_ref, g_ref, b_ref, w1_ref, w3_ref, w2_ref, out_ref, a_ref, u_ref, hb, acc):
        f = pl.program_id(1)

        @pl.when(f == 0)
        def _():
            xhat, _ = _ln_stats(r_ref[...])
            hb[...] = (xhat * g_ref[...] + b_ref[...]).astype(BF16)
            acc[...] = jnp.zeros_like(acc)

        h = hb[...]
        a = _dot(h, w1_ref[...])
        u = _dot(h, w3_ref[...])
        a_ref[...] = a.astype(BF16)
        u_ref[...] = u.astype(BF16)
        s = a * _sigmoid(a) * u
        acc[...] += _dot(s.astype(BF16), w2_ref[...])

        @pl.when(f == nf - 1)
        def _():
            xhat, _ = _ln_stats(r_ref[...])
            out_ref[...] = alpha * (xhat * g_ref[...] + b_ref[...]) + 0.5 * acc[...]

    return pl.pallas_call(
        body,
        grid=(lp // tm, nf),
        in_specs=[
            pl.BlockSpec((tm, d), lambda i, f: (i, 0)),
            pl.BlockSpec((1, d), lambda i, f: (0, 0)),
            pl.BlockSpec((1, d), lambda i, f: (0, 0)),
            pl.BlockSpec((None, d, fc), lambda i, f: (f, 0, 0)),
            pl.BlockSpec((None, d, fc), lambda i, f: (nf + f, 0, 0)),
            pl.BlockSpec((None, fc, d), lambda i, f: (f, 0, 0)),
        ],
        out_specs=[
            pl.BlockSpec((tm, d), lambda i, f: (i, 0)),
            pl.BlockSpec((tm, fc), lambda i, f: (i, f)),
            pl.BlockSpec((tm, fc), lambda i, f: (i, f)),
        ],
        out_shape=[
            jax.ShapeDtypeStruct((lp, d), F32),
            jax.ShapeDtypeStruct((lp, nf * fc), BF16),
            jax.ShapeDtypeStruct((lp, nf * fc), BF16),
        ],
        scratch_shapes=[pltpu.VMEM((tm, d), BF16), pltpu.VMEM((tm, d), F32)],
        compiler_params=_cparams(("parallel", "arbitrary")),
        name="ffn_fwd",
    )(r_prev, g, b, w13g, w13g, w2c)


def _ffn_bwd(dr_next, r_prev, a, u, g, b, w13g, w2g, alpha):
    lp, d = r_prev.shape
    nf = N_DEV // 2
    fc = w13g.shape[2]
    w2c = w2g.reshape(nf, fc, d)
    tm = _pick_tm(lp, DENSE_TM_BWD)

    def body(dr_ref, r_ref, a_ref, u_ref, g_ref, b_ref, w1_ref, w3_ref, w2_ref,
             drp_ref, hb_ref, s_ref, dz_ref, dg_ref, db_ref, dyb, dhacc):
        i = pl.program_id(0)
        f = pl.program_id(1)

        @pl.when(jnp.logical_and(i == 0, f == 0))
        def _():
            dg_ref[...] = jnp.zeros_like(dg_ref)
            db_ref[...] = jnp.zeros_like(db_ref)

        @pl.when(f == 0)
        def _():
            dyb[...] = (0.5 * dr_ref[...]).astype(BF16)
            dhacc[...] = jnp.zeros_like(dhacc)
            xhat, _ = _ln_stats(r_ref[...])
            hb_ref[...] = (xhat * g_ref[...] + b_ref[...]).astype(BF16)

        ds = _dot_nt(dyb[...], w2_ref[...])
        av = a_ref[...].astype(F32)
        uv = u_ref[...].astype(F32)
        sig = _sigmoid(av)
        sl = av * sig
        da = (ds * uv * (sig * (1.0 + av * (1.0 - sig)))).astype(BF16)
        du = (ds * sl).astype(BF16)
        s_ref[...] = (sl * uv).astype(BF16)
        dz_ref[:, 0:fc] = da
        dz_ref[:, fc:2 * fc] = du
        dhacc[...] += _dot_nt(da, w1_ref[...]) + _dot_nt(du, w3_ref[...])

        @pl.when(f == nf - 1)
        def _():
            dh = alpha * dr_ref[...] + dhacc[...]
            xhat, rstd = _ln_stats(r_ref[...])
            drp_ref[...] = _ln_bwd(dh, xhat, rstd, g_ref[...])
            dg_ref[...] += _colsum(dh * xhat)
            db_ref[...] += _colsum(dh)

    row = lambda i, f: (i, 0)
    const = lambda i, f: (0, 0)
    chunk = lambda i, f: (i, f)
    return pl.pallas_call(
        body,
        grid=(lp // tm, nf),
        in_specs=[
            pl.BlockSpec((tm, d), row),
            pl.BlockSpec((tm, d), row),
            pl.BlockSpec((tm, fc), chunk),
            pl.BlockSpec((tm, fc), chunk),
            pl.BlockSpec((1, d), const),
            pl.BlockSpec((1, d), const),
            pl.BlockSpec((None, d, fc), lambda i, f: (f, 0, 0)),
            pl.BlockSpec((None, d, fc), lambda i, f: (nf + f, 0, 0)),
            pl.BlockSpec((None, fc, d), lambda i, f: (f, 0, 0)),
        ],
        out_specs=[
            pl.BlockSpec((tm, d), row),
            pl.BlockSpec((tm, d), row),
            pl.BlockSpec((tm, fc), chunk),
            pl.BlockSpec((tm, 2 * fc), chunk),
            pl.BlockSpec((1, d), const),
            pl.BlockSpec((1, d), const),
        ],
        out_shape=[
            jax.ShapeDtypeStruct((lp, d), F32),
            jax.ShapeDtypeStruct((lp, d), BF16),
            jax.ShapeDtypeStruct((lp, nf * fc), BF16),
            jax.ShapeDtypeStruct((lp, 2 * nf * fc), BF16),
            jax.ShapeDtypeStruct((1, d), F32),
            jax.ShapeDtypeStruct((1, d), F32),
        ],
        scratch_shapes=[pltpu.VMEM((tm, d), BF16), pltpu.VMEM((tm, d), F32)],
        compiler_params=_cparams(("arbitrary", "arbitrary")),
        name="ffn_bwd",
    )(dr_next, r_prev, a, u, g, b, w13g, w13g, w2c)


def _mm_tn(a, b, scale=1.0, bn=None, slot_of=None):
    t, m = a.shape
    n = b.shape[1]
    if bn is None:
        bn = n if n <= 1408 else _pick_tm(n, 1408, LANE)
    bm = m if m <= 1536 else _pick_tm(m, 1536, LANE)
    tk = _pick_tm(t, MM_TN_TK)
    nt = t // tk

    def body(a_ref, b_ref, o_ref, acc):
        k = pl.program_id(2)

        @pl.when(k == 0)
        def _():
            acc[...] = jnp.zeros_like(acc)

        acc[...] += _dot_tn(a_ref[...].astype(BF16), b_ref[...].astype(BF16))

        @pl.when(k == nt - 1)
        def _():
            o_ref[...] = (acc[...] * scale).astype(o_ref.dtype)

    if slot_of is None:
        out_spec = pl.BlockSpec((bm, bn), lambda i, j, k: (i, j))
        out_shape = jax.ShapeDtypeStruct((m, n), WIRE)
    else:
        out_spec = pl.BlockSpec((None, bm, bn), lambda i, j, k: (slot_of(j), i, 0))
        out_shape = jax.ShapeDtypeStruct((n // bn, m, bn), WIRE)
    return pl.pallas_call(
        body,
        grid=(m // bm, n // bn, nt),
        in_specs=[
            pl.BlockSpec((tk, bm), lambda i, j, k: (k, i)),
            pl.BlockSpec((tk, bn), lambda i, j, k: (k, j)),
        ],
        out_specs=out_spec,
        out_shape=out_shape,
        scratch_shapes=[pltpu.VMEM((bm, bn), F32)],
        compiler_params=_cparams(("parallel", "parallel", "arbitrary")),
        name="mm_tn",
    )(a, b)


def _proj_fwd(r_prev, g, b, w_in):
    lp, d = r_prev.shape
    n = w_in.shape[1]
    tm = _pick_tm(lp, DENSE_TM_FWD)

    def body(r_ref, g_ref, b_ref, w_ref, z_ref):
        xhat, _ = _ln_stats(r_ref[...])
        h = (xhat * g_ref[...] + b_ref[...]).astype(BF16)
        z = _dot(h, w_ref[...])
        rows = _row_ids(pl.program_id(0), tm)
        z_ref[...] = jnp.where(rows >= PAD_ROWS, z, 0.0)

    return pl.pallas_call(
        body,
        grid=(lp // tm,),
        in_specs=[
            pl.BlockSpec((tm, d), lambda i: (i, 0)),
            pl.BlockSpec((1, d), lambda i: (0, 0)),
            pl.BlockSpec((1, d), lambda i: (0, 0)),
            pl.BlockSpec((d, n), lambda i: (0, 0)),
        ],
        out_specs=pl.BlockSpec((tm, n), lambda i: (i, 0)),
        out_shape=jax.ShapeDtypeStruct((lp, n), F32),
        compiler_params=_cparams(("parallel",)),
        name="proj_fwd",
    )(r_prev, g, b, w_in)


def _out_fwd(r_prev, g, b, cat, w_out, alpha):
    lp, d = r_prev.shape
    k = cat.shape[1]
    tm = _pick_tm(lp, DENSE_TM_FWD)

    def body(r_ref, g_ref, b_ref, c_ref, w_ref, o_ref):
        xhat, _ = _ln_stats(r_ref[...])
        o_ref[...] = alpha * (xhat * g_ref[...] + b_ref[...]) + _dot(c_ref[...], w_ref[...])

    return pl.pallas_call(
        body,
        grid=(lp // tm,),
        in_specs=[
            pl.BlockSpec((tm, d), lambda i: (i, 0)),
            pl.BlockSpec((1, d), lambda i: (0, 0)),
            pl.BlockSpec((1, d), lambda i: (0, 0)),
            pl.BlockSpec((tm, k), lambda i: (i, 0)),
            pl.BlockSpec((k, d), lambda i: (0, 0)),
        ],
        out_specs=pl.BlockSpec((tm, d), lambda i: (i, 0)),
        out_shape=jax.ShapeDtypeStruct((lp, d), F32),
        compiler_params=_cparams(("parallel",)),
        name="out_fwd",
    )(r_prev, g, b, cat, w_out)


def _mm_nt(x, w):
    lp, n = x.shape
    k = w.shape[0]
    tm = _pick_tm(lp, DENSE_TM_FWD)

    def body(x_ref, w_ref, o_ref):
        o_ref[...] = _dot_nt(x_ref[...].astype(BF16), w_ref[...])

    return pl.pallas_call(
        body,
        grid=(lp // tm,),
        in_specs=[pl.BlockSpec((tm, n), lambda i: (i, 0)), pl.BlockSpec((k, n), lambda i: (0, 0))],
        out_specs=pl.BlockSpec((tm, k), lambda i: (i, 0)),
        out_shape=jax.ShapeDtypeStruct((lp, k), F32),
        compiler_params=_cparams(("parallel",)),
        name="mm_nt",
    )(x, w)


def _in_bwd(dr_next, r_prev, dz, g, b, w_in, alpha):
    lp, d = r_prev.shape
    n = w_in.shape[1]
    tm = _pick_tm(lp, DENSE_TM_FWD)

    def body(dr_ref, r_ref, dz_ref, g_ref, b_ref, w_ref, drp_ref, hb_ref, dg_ref, db_ref):
        @pl.when(pl.program_id(0) == 0)
        def _():
            dg_ref[...] = jnp.zeros_like(dg_ref)
            db_ref[...] = jnp.zeros_like(db_ref)

        dh = alpha * dr_ref[...] + _dot_nt(dz_ref[...], w_ref[...])
        xhat, rstd = _ln_stats(r_ref[...])
        hb_ref[...] = (xhat * g_ref[...] + b_ref[...]).astype(BF16)
        drp_ref[...] = _ln_bwd(dh, xhat, rstd, g_ref[...])
        dg_ref[...] += _colsum(dh * xhat)
        db_ref[...] += _colsum(dh)

    row = lambda i: (i, 0)
    const = lambda i: (0, 0)
    return pl.pallas_call(
        body,
        grid=(lp // tm,),
        in_specs=[
            pl.BlockSpec((tm, d), row),
            pl.BlockSpec((tm, d), row),
            pl.BlockSpec((tm, n), row),
            pl.BlockSpec((1, d), const),
            pl.BlockSpec((1, d), const),
            pl.BlockSpec((d, n), const),
        ],
        out_specs=[
            pl.BlockSpec((tm, d), row),
            pl.BlockSpec((tm, d), row),
            pl.BlockSpec((1, d), const),
            pl.BlockSpec((1, d), const),
        ],
        out_shape=[
            jax.ShapeDtypeStruct((lp, d), F32),
            jax.ShapeDtypeStruct((lp, d), BF16),
            jax.ShapeDtypeStruct((1, d), F32),
            jax.ShapeDtypeStruct((1, d), F32),
        ],
        compiler_params=_cparams(("arbitrary",)),
        name="in_bwd",
    )(dr_next, r_prev, dz, g, b, w_in)


def _loss_bwd(r_last, target, g, b):
    lp, d = r_last.shape
    tm = _pick_tm(lp, DENSE_TM_FWD)

    def body(r_ref, t_ref, g_ref, b_ref, dr_ref, dg_ref, db_ref, ls_ref):
        i = pl.program_id(0)

        @pl.when(i == 0)
        def _():
            dg_ref[...] = jnp.zeros_like(dg_ref)
            db_ref[...] = jnp.zeros_like(db_ref)
            ls_ref[...] = jnp.zeros_like(ls_ref)

        xhat, rstd = _ln_stats(r_ref[...])
        y = xhat * g_ref[...] + b_ref[...]
        rows = _row_ids(i, tm)
        err = jnp.where(rows >= FRONT, y - t_ref[...], 0.0)
        ls_ref[...] += _colsum(err * err)
        dy = err * (1.0 / d)
        dr_ref[...] = _ln_bwd(dy, xhat, rstd, g_ref[...])
        dg_ref[...] += _colsum(dy * xhat)
        db_ref[...] += _colsum(dy)

    row = lambda i: (i, 0)
    const = lambda i: (0, 0)
    return pl.pallas_call(
        body,
        grid=(lp // tm,),
        in_specs=[
            pl.BlockSpec((tm, d), row),
            pl.BlockSpec((tm, d), row),
            pl.BlockSpec((1, d), const),
            pl.BlockSpec((1, d), const),
        ],
        out_specs=[
            pl.BlockSpec((tm, d), row),
            pl.BlockSpec((1, d), const),
            pl.BlockSpec((1, d), const),
            pl.BlockSpec((1, d), const),
        ],
        out_shape=[
            jax.ShapeDtypeStruct((lp, d), F32),
            jax.ShapeDtypeStruct((1, d), F32),
            jax.ShapeDtypeStruct((1, d), F32),
            jax.ShapeDtypeStruct((1, d), F32),
        ],
        compiler_params=_cparams(("arbitrary",)),
        name="loss_bwd",
    )(r_last, target, g, b)


def _pool_counts(tile, tm, width):
    pos = _row_ids(tile, tm) - PAD_ROWS
    lane = lax.broadcasted_iota(jnp.int32, (1, width), 1)
    group = width // len(POOL_WINDOWS)
    win = jnp.full((1, width), POOL_WINDOWS[-1], jnp.int32)
    for gi in range(len(POOL_WINDOWS) - 2, -1, -1):
        win = jnp.where(lane < (gi + 1) * group, POOL_WINDOWS[gi], win)
    cnt = jnp.clip(pos + 1, 1, win)
    return cnt.astype(F32), lane, group


def _pool_select(sums, lane, group):
    out = sums[-1]
    for gi in range(len(POOL_WINDOWS) - 2, -1, -1):
        out = jnp.where(lane < (gi + 1) * group, sums[gi], out)
    return out


def _pool_window_sums(ext, tm, sign):
    base = POOL_HALO if sign < 0 else 0
    acc = ext[pl.ds(base, tm), :]
    sums, k = [], 1
    for w in POOL_WINDOWS:
        while k < w:
            acc = acc + ext[pl.ds(base + sign * k, tm), :]
            k += 1
        sums.append(acc)
    return sums


def _pool_fwd(z, wbd, scale):
    lp = z.shape[0]
    c = wbd.shape[0]
    tm = _pick_tm(lp, MIX_TM, CHUNK)

    def body(x_ref, w_ref, s_ref, y_ref, ext):
        i = pl.program_id(0)

        @pl.when(i == 0)
        def _():
            ext[0:POOL_HALO, :] = jnp.zeros((POOL_HALO, c), F32)

        x = x_ref[...]
        ext[POOL_HALO:POOL_HALO + tm, :] = x
        cnt, lane, group = _pool_counts(i, tm, c)
        sums = _pool_window_sums(ext, tm, -1)
        y = _pool_select(sums, lane, group) / cnt - x
        ext[0:POOL_HALO, :] = ext[tm:tm + POOL_HALO, :]
        y_ref[...] = (_dot(y.astype(BF16), w_ref[...]) * s_ref[...]).astype(BF16)

    return pl.pallas_call(
        body,
        grid=(lp // tm,),
        in_specs=[
            pl.BlockSpec((tm, c), lambda i: (i, 0)),
            pl.BlockSpec((c, c), lambda i: (0, 0)),
            pl.BlockSpec((1, c), lambda i: (0, 0)),
        ],
        out_specs=pl.BlockSpec((tm, c), lambda i: (i, 0)),
        out_shape=jax.ShapeDtypeStruct((lp, c), BF16),
        scratch_shapes=[pltpu.VMEM((tm + POOL_HALO, c), F32)],
        compiler_params=_cparams(("arbitrary",)),
        name="pool_fwd",
    )(z, wbd, scale)


def _pool_bwd(z, dcat, wbd, scale):
    lp = z.shape[0]
    c = wbd.shape[0]
    tm = _pick_tm(lp, MIX_TM, CHUNK)
    ni = lp // tm
    hpt = tm // POOL_HALO

    def body(x_ref, xh_ref, dy_ref, w_ref, s_ref, dx_ref, dw_ref, ds_ref, ext, ext2):
        i = pl.program_id(0)
        t = ni - 1 - i

        @pl.when(i == 0)
        def _():
            dw_ref[...] = jnp.zeros_like(dw_ref)
            ds_ref[...] = jnp.zeros_like(ds_ref)
            ext2[tm:tm + POOL_HALO, :] = jnp.zeros((POOL_HALO, c), F32)

        x = x_ref[...]
        ext[0:POOL_HALO, :] = jnp.where(t > 0, xh_ref[...], 0.0)
        ext[POOL_HALO:POOL_HALO + tm, :] = x
        cnt, lane, group = _pool_counts(t, tm, c)
        y = (_pool_select(_pool_window_sums(ext, tm, -1), lane, group) / cnt - x).astype(BF16)
        w = w_ref[...]
        dyv = dy_ref[...]
        ds_ref[...] += _colsum(_dot(y, w) * dyv)
        do = (dyv * s_ref[...]).astype(BF16)
        dw_ref[...] += _dot_tn(y, do)
        dyp = _dot_nt(do, w)
        ext2[0:tm, :] = dyp / cnt
        dx = _pool_select(_pool_window_sums(ext2, tm, 1), lane, group) - dyp
        ext2[tm:tm + POOL_HALO, :] = ext2[0:POOL_HALO, :]
        rows = _row_ids(t, tm)
        dx_ref[...] = jnp.where(rows >= PAD_ROWS, dx, 0.0).astype(BF16)

    rev = lambda i: (ni - 1 - i, 0)
    return pl.pallas_call(
        body,
        grid=(ni,),
        in_specs=[
            pl.BlockSpec((tm, c), rev),
            pl.BlockSpec((POOL_HALO, c), lambda i: (jnp.maximum((ni - 1 - i) * hpt - 1, 0), 0)),
            pl.BlockSpec((tm, c), rev),
            pl.BlockSpec((c, c), lambda i: (0, 0)),
            pl.BlockSpec((1, c), lambda i: (0, 0)),
        ],
        out_specs=[
            pl.BlockSpec((tm, c), rev),
            pl.BlockSpec((c, c), lambda i: (0, 0)),
            pl.BlockSpec((1, c), lambda i: (0, 0)),
        ],
        out_shape=[
            jax.ShapeDtypeStruct((lp, c), BF16),
            jax.ShapeDtypeStruct((c, c), F32),
            jax.ShapeDtypeStruct((1, c), F32),
        ],
        scratch_shapes=[pltpu.VMEM((tm + POOL_HALO, c), F32), pltpu.VMEM((tm + POOL_HALO, c), F32)],
        compiler_params=_cparams(("arbitrary",)),
        name="pool_bwd",
    )(z, z, dcat, wbd, scale)


def _conv_taps(ext, w_ref, tm, first):
    acc = w_ref[0:1, :] * ext[pl.ds(first, tm), :]
    for k in range(1, CONV_WIDTH):
        acc = acc + w_ref[k:k + 1, :] * ext[pl.ds(first + k, tm), :]
    return acc


def _conv_fwd(z, w_dw, b_dw, ln_g, ln_b, w_pw):
    lp = z.shape[0]
    c = w_pw.shape[0]
    tm = _pick_tm(lp, MIX_TM, CHUNK)

    def body(a_ref, gt_ref, w_ref, bd_ref, g_ref, b_ref, pw_ref, y_ref, acc_ref, ext):
        i = pl.program_id(0)

        @pl.when(i == 0)
        def _():
            ext[0:CONV_HALO, :] = jnp.zeros((CONV_HALO, c), F32)

        ext[CONV_HALO:CONV_HALO + tm, :] = a_ref[...] * _sigmoid(gt_ref[...])
        acc = _conv_taps(ext, w_ref, tm, CONV_HALO - CONV_WIDTH + 1) + bd_ref[...]
        acc_ref[...] = acc
        ext[0:CONV_HALO, :] = ext[tm:tm + CONV_HALO, :]
        xhat, _ = _ln_stats(acc)
        n = xhat * g_ref[...] + b_ref[...]
        act = n * _sigmoid(n)
        y_ref[...] = _dot(act.astype(BF16), pw_ref[...]).astype(BF16)

    const = lambda i: (0, 0)
    return pl.pallas_call(
        body,
        grid=(lp // tm,),
        in_specs=[
            pl.BlockSpec((tm, c), lambda i: (i, 1)),
            pl.BlockSpec((tm, c), lambda i: (i, 2)),
            pl.BlockSpec((CONV_HALO, c), const),
            pl.BlockSpec((1, c), const),
            pl.BlockSpec((1, c), const),
            pl.BlockSpec((1, c), const),
            pl.BlockSpec((c, c), const),
        ],
        out_specs=[pl.BlockSpec((tm, c), lambda i: (i, 0)), pl.BlockSpec((tm, c), lambda i: (i, 0))],
        out_shape=[jax.ShapeDtypeStruct((lp, c), BF16), jax.ShapeDtypeStruct((lp, c), F32)],
        scratch_shapes=[pltpu.VMEM((tm + CONV_HALO, c), F32)],
        compiler_params=_cparams(("arbitrary",)),
        name="conv_fwd",
    )(z, z, w_dw, b_dw, ln_g, ln_b, w_pw)


def _conv_bwd(z, dcat, acc_fwd, w_dw, ln_g, ln_b, w_pw):
    lp = z.shape[0]
    c = w_pw.shape[0]
    tm = _pick_tm(lp, MIX_TM, CHUNK)
    ni = lp // tm
    hpt = tm // CONV_HALO
    first = CONV_HALO - CONV_WIDTH + 1

    def body(a_ref, gt_ref, ah_ref, gh_ref, dy_ref, acc_ref, w_ref, g_ref, b_ref, pw_ref,
             dca_ref, dcg_ref, dw_ref, dbd_ref, dg_ref, db_ref, dpw_ref, extu, extd):
        i = pl.program_id(0)
        t = ni - 1 - i

        @pl.when(i == 0)
        def _():
            dw_ref[...] = jnp.zeros_like(dw_ref)
            dbd_ref[...] = jnp.zeros_like(dbd_ref)
            dg_ref[...] = jnp.zeros_like(dg_ref)
            db_ref[...] = jnp.zeros_like(db_ref)
            dpw_ref[...] = jnp.zeros_like(dpw_ref)
            extd[tm:tm + CONV_HALO, :] = jnp.zeros((CONV_HALO, c), F32)

        ca = a_ref[...]
        sg = _sigmoid(gt_ref[...])
        extu[0:CONV_HALO, :] = jnp.where(t > 0, ah_ref[...] * _sigmoid(gh_ref[...]), 0.0)
        extu[CONV_HALO:CONV_HALO + tm, :] = ca * sg
        xhat, rstd = _ln_stats(acc_ref[...])
        gam = g_ref[...]
        n = xhat * gam + b_ref[...]
        sn = _sigmoid(n)
        act = (n * sn).astype(BF16)
        do = dy_ref[...].astype(BF16)
        dpw_ref[...] += _dot_tn(act, do)
        dn = _dot_nt(do, pw_ref[...]) * (sn * (1.0 + n * (1.0 - sn)))
        dg_ref[...] += _colsum(dn * xhat)
        db_ref[...] += _colsum(dn)
        dyc = _ln_bwd(dn, xhat, rstd, gam)
        dbd_ref[...] += _colsum(dyc)
        extd[0:tm, :] = dyc
        du = None
        for k in range(CONV_WIDTH):
            dw_ref[k:k + 1, :] += _colsum(dyc * extu[pl.ds(first + k, tm), :])
            term = w_ref[k:k + 1, :] * extd[pl.ds(CONV_WIDTH - 1 - k, tm), :]
            du = term if du is None else du + term
        extd[tm:tm + CONV_HALO, :] = extd[0:CONV_HALO, :]
        du = jnp.where(_row_ids(t, tm) >= PAD_ROWS, du, 0.0)
        dca_ref[...] = (du * sg).astype(BF16)
        dcg_ref[...] = (du * ca * sg * (1.0 - sg)).astype(BF16)

    const = lambda i: (0, 0)
    rev = lambda col: (lambda i: (ni - 1 - i, col))
    halo = lambda col: (lambda i: (jnp.maximum((ni - 1 - i) * hpt - 1, 0), col))
    return pl.pallas_call(
        body,
        grid=(ni,),
        in_specs=[
            pl.BlockSpec((tm, c), rev(1)),
            pl.BlockSpec((tm, c), rev(2)),
            pl.BlockSpec((CONV_HALO, c), halo(1)),
            pl.BlockSpec((CONV_HALO, c), halo(2)),
            pl.BlockSpec((tm, c), rev(1)),
            pl.BlockSpec((tm, c), rev(0)),
            pl.BlockSpec((CONV_HALO, c), const),
            pl.BlockSpec((1, c), const),
            pl.BlockSpec((1, c), const),
            pl.BlockSpec((c, c), const),
        ],
        out_specs=[
            pl.BlockSpec((tm, c), rev(0)),
            pl.BlockSpec((tm, c), rev(0)),
            pl.BlockSpec((CONV_HALO, c), const),
            pl.BlockSpec((1, c), const),
            pl.BlockSpec((1, c), const),
            pl.BlockSpec((1, c), const),
            pl.BlockSpec((c, c), const),
        ],
        out_shape=[
            jax.ShapeDtypeStruct((lp, c), BF16),
            jax.ShapeDtypeStruct((lp, c), BF16),
            jax.ShapeDtypeStruct((CONV_HALO, c), F32),
            jax.ShapeDtypeStruct((1, c), F32),
            jax.ShapeDtypeStruct((1, c), F32),
            jax.ShapeDtypeStruct((1, c), F32),
            jax.ShapeDtypeStruct((c, c), F32),
        ],
        scratch_shapes=[pltpu.VMEM((tm + CONV_HALO, c), F32), pltpu.VMEM((tm + CONV_HALO, c), F32)],
        compiler_params=_cparams(("arbitrary",)),
        name="conv_bwd",
    )(z, z, z, z, dcat, acc_fwd, w_dw, ln_g, ln_b, w_pw)


def _rope(x, cos, sgn_sin):
    return x * cos + pltpu.roll(x, LANE // 2, 1) * sgn_sin


def _rope_t(dy, cos, sgn_sin):
    return dy * cos + pltpu.roll(dy * sgn_sin, LANE // 2, 1)


def _ret_chunk_fwd(q, k, v, cos, sn, state, dmv, qdv, kdv):
    qr = _rope(q, cos, sn)
    kr = _rope(k, cos, sn) * (LANE ** -0.5)
    am = _dot_nt(qr.astype(BF16), kr.astype(BF16)) * dmv
    o = _dot(am.astype(BF16), v.astype(BF16)) + _dot((qr * qdv).astype(BF16), state.astype(BF16))
    return qr, kr, am, o


def _ret_specs(tm, q_blk, tile_of):
    def mk(col):
        return pl.BlockSpec((tm, LANE), lambda i: (tile_of(i), col))
    return [mk(q_blk + j) for j in range(4 * RET_HEADS)]


def _ret_tables(h, dm_ref, qd_ref, kd_ref, cd_ref, gn_ref):
    return dm_ref[h], qd_ref[h], kd_ref[h], cd_ref[h, 0:1, :], gn_ref[:, h * LANE:(h + 1) * LANE]


def _ret_fwd(z, rope_c, rope_s, dm, qd, kd, cd, gn, q_blk):
    lp = z.shape[0]
    tm = _pick_tm(lp, MIX_TM, CHUNK)
    nc = tm // CHUNK
    ni = lp // tm
    nh = RET_HEADS

    def body(*refs):
        q_refs, k_refs, v_refs, g_refs = refs[0:nh], refs[nh:2 * nh], refs[2 * nh:3 * nh], refs[3 * nh:4 * nh]
        c_ref, s_ref, dm_ref, qd_ref, kd_ref, cd_ref, gn_ref, y_ref, st_ref, state = refs[4 * nh:]

        @pl.when(pl.program_id(0) == 0)
        def _():
            state[...] = jnp.zeros_like(state)

        def chunk(c, carry):
            rows = pl.ds(pl.multiple_of(c * CHUNK, CHUNK), CHUNK)
            cos, sn = c_ref[rows, :], s_ref[rows, :]
            loaded = [(q_refs[h][rows, :], k_refs[h][rows, :], v_refs[h][rows, :], g_refs[h][rows, :], state[h])
                      for h in range(nh)]
            results = []
            for h, (q, k, v, gg, s0) in enumerate(loaded):
                dmv, qdv, kdv, cdv, gnv = _ret_tables(h, dm_ref, qd_ref, kd_ref, cd_ref, gn_ref)
                _, kr, _, o = _ret_chunk_fwd(q, k, v, cos, sn, s0, dmv, qdv, kdv)
                s1 = s0 * cdv + _dot_tn((kr * kdv).astype(BF16), v.astype(BF16))
                on, _ = _ln_stats(o)
                results.append((s0, s1, (gg * _sigmoid(gg) * (on * gnv)).astype(BF16)))
            for h, (s0, s1, y) in enumerate(results):
                st_ref[c, h, :, :] = s0
                state[h] = s1
                y_ref[rows, h * LANE:(h + 1) * LANE] = y
            return carry

        lax.fori_loop(0, nc, chunk, 0, unroll=2)

    const3 = lambda s: pl.BlockSpec(s, lambda i: (0, 0, 0))
    return pl.pallas_call(
        body,
        grid=(ni,),
        in_specs=_ret_specs(tm, q_blk, lambda i: i) + [
            pl.BlockSpec((tm, LANE), lambda i: (i, 0)),
            pl.BlockSpec((tm, LANE), lambda i: (i, 0)),
            const3((nh, CHUNK, CHUNK)), const3((nh, CHUNK, LANE)), const3((nh, CHUNK, LANE)), const3((nh, 8, LANE)),
            pl.BlockSpec((1, nh * LANE), lambda i: (0, 0)),
        ],
        out_specs=[
            pl.BlockSpec((tm, nh * LANE), lambda i: (i, 0)),
            pl.BlockSpec((nc, nh, LANE, LANE), lambda i: (i, 0, 0, 0)),
        ],
        out_shape=[
            jax.ShapeDtypeStruct((lp, nh * LANE), BF16),
            jax.ShapeDtypeStruct((lp // CHUNK, nh, LANE, LANE), F32),
        ],
        scratch_shapes=[pltpu.VMEM((nh, LANE, LANE), F32)],
        compiler_params=_cparams(("arbitrary",)),
        name="ret_fwd",
    )(*([z] * (4 * nh)), rope_c, rope_s, dm, qd, kd, cd, gn)


def _ret_bwd(z, dcat, states, rope_c, rope_s, dm, qd, kd, cd, gn, q_blk, dy_blk):
    lp = z.shape[0]
    tm = _pick_tm(lp, MIX_TM, CHUNK)
    nc = tm // CHUNK
    ni = lp // tm

    nh = RET_HEADS

    def body(*refs):
        q_refs, k_refs, v_refs, g_refs = refs[0:nh], refs[nh:2 * nh], refs[2 * nh:3 * nh], refs[3 * nh:4 * nh]
        (dy_ref, st_ref, c_ref, s_ref, dm_ref, qd_ref, kd_ref, cd_ref, gn_ref,
         dq_ref, dk_ref, dv_ref, dgt_ref, dgn_ref, dstate) = refs[4 * nh:]
        i = pl.program_id(0)
        t = ni - 1 - i

        @pl.when(i == 0)
        def _():
            dstate[...] = jnp.zeros_like(dstate)
            dgn_ref[...] = jnp.zeros_like(dgn_ref)

        def chunk(cc, carry):
            c = nc - 1 - cc
            rows = pl.ds(pl.multiple_of(c * CHUNK, CHUNK), CHUNK)
            cos, sn = c_ref[rows, :], s_ref[rows, :]
            keep = (lax.broadcasted_iota(jnp.int32, (CHUNK, 1), 0) + (t * tm + c * CHUNK)) >= PAD_ROWS
            loaded = [(q_refs[h][rows, :], k_refs[h][rows, :], v_refs[h][rows, :], g_refs[h][rows, :],
                       dy_ref[rows, h * LANE:(h + 1) * LANE], st_ref[c, h, :, :], dstate[h], dgn_ref[:, h * LANE:(h + 1) * LANE])
                      for h in range(nh)]
            results = []
            for h, (q, k, v, gg, dyv, s0, ds1, dgn) in enumerate(loaded):
                dmv, qdv, kdv, cdv, gnv = _ret_tables(h, dm_ref, qd_ref, kd_ref, cd_ref, gn_ref)
                vb = v.astype(BF16)
                qr, kr, am, o = _ret_chunk_fwd(q, k, v, cos, sn, s0, dmv, qdv, kdv)
                on, rstd = _ln_stats(o)
                sg = _sigmoid(gg)
                sl = gg * sg
                dgn = dgn + _colsum(dyv * sl * on)
                dgt = dyv * (on * gnv) * (sg * (1.0 + gg * (1.0 - sg)))
                dob = _ln_bwd(dyv * sl, on, rstd, gnv).astype(BF16)
                ds1b = ds1.astype(BF16)
                qdb = (qr * qdv).astype(BF16)
                kdb = (kr * kdv).astype(BF16)
                da = (_dot_nt(dob, vb) * dmv).astype(BF16)
                dv = _dot_tn(am.astype(BF16), dob) + _dot(kdb, ds1b)
                dqr = _dot(da, kr.astype(BF16)) + _dot_nt(dob, s0.astype(BF16)) * qdv
                dkr = _dot_tn(da, qr.astype(BF16)) + _dot_nt(vb, ds1b) * kdv
                ds0 = ds1 * cdv + _dot_tn(qdb, dob)
                dq = _rope_t(dqr, cos, sn)
                dk = _rope_t(dkr * (LANE ** -0.5), cos, sn)
                masked = [jnp.where(keep, t, 0.0).astype(BF16) for t in (dq, dk, dv, dgt)]
                results.append((masked, ds0, dgn))
            for h, ((dq, dk, dv, dgt), ds0, dgn) in enumerate(results):
                lanes = slice(h * LANE, (h + 1) * LANE)
                dq_ref[rows, lanes] = dq
                dk_ref[rows, lanes] = dk
                dv_ref[rows, lanes] = dv
                dgt_ref[rows, lanes] = dgt
                dstate[h] = ds0
                dgn_ref[:, lanes] = dgn
            return carry

        lax.fori_loop(0, nc, chunk, 0, unroll=2)

    rev = lambda i: ni - 1 - i
    const3 = lambda s: pl.BlockSpec(s, lambda i: (0, 0, 0))
    out_blk = pl.BlockSpec((tm, nh * LANE), lambda i: (rev(i), 0))
    out_sds = jax.ShapeDtypeStruct((lp, nh * LANE), BF16)
    return pl.pallas_call(
        body,
        grid=(ni,),
        in_specs=_ret_specs(tm, q_blk, rev) + [
            pl.BlockSpec((tm, nh * LANE), lambda i: (rev(i), dy_blk // nh)),
            pl.BlockSpec((nc, nh, LANE, LANE), lambda i: (rev(i), 0, 0, 0)),
            pl.BlockSpec((tm, LANE), lambda i: (rev(i), 0)),
            pl.BlockSpec((tm, LANE), lambda i: (rev(i), 0)),
            const3((nh, CHUNK, CHUNK)), const3((nh, CHUNK, LANE)), const3((nh, CHUNK, LANE)), const3((nh, 8, LANE)),
            pl.BlockSpec((1, nh * LANE), lambda i: (0, 0)),
        ],
        out_specs=[out_blk, out_blk, out_blk, out_blk, pl.BlockSpec((1, nh * LANE), lambda i: (0, 0))],
        out_shape=[out_sds, out_sds, out_sds, out_sds, jax.ShapeDtypeStruct((1, nh * LANE), F32)],
        scratch_shapes=[pltpu.VMEM((nh, LANE, LANE), F32)],
        compiler_params=_cparams(("arbitrary",)),
        name="ret_bwd",
    )(*([z] * (4 * nh)), dcat, states, rope_c, rope_s, dm, qd, kd, cd, gn)


def _adamw(parts_list, w, m, v):
    nl, r, c = w.shape
    assert len(parts_list) == nl
    tr = _pick_tm(r, max(8, (1 << 17) // c), 8)
    nr = r // tr

    def body(*refs):
        p_refs = refs[:nl]
        w_ref, m_ref, v_ref, g_ref, d_ref, nm_ref, nv_ref = refs[nl:]
        layer = pl.program_id(0)
        for k in range(nl):
            @pl.when(layer == k)
            def _(k=k):
                g = p_refs[k][0].astype(F32)
                for j in range(1, N_DEV):
                    g = g + p_refs[k][j].astype(F32)
                m1 = ADAM_B1 * m_ref[...] + (1.0 - ADAM_B1) * g
                v1 = ADAM_B2 * v_ref[...] + (1.0 - ADAM_B2) * (g * g)
                m_hat = m1 / (1.0 - ADAM_B1 ** ADAM_STEP)
                v_hat = v1 / (1.0 - ADAM_B2 ** ADAM_STEP)
                g_ref[...] = g
                d_ref[...] = -ADAM_LR * (m_hat / (jnp.sqrt(v_hat) + ADAM_EPS) + ADAM_WD * w_ref[...])
                nm_ref[...] = m1
                nv_ref[...] = v1

    def parts_spec(k):
        return pl.BlockSpec((N_DEV, tr, c), lambda l, i: (0, jnp.where(l == k, i, jnp.where(l < k, 0, nr - 1)), 0))

    blk = pl.BlockSpec((None, tr, c), lambda l, i: (l, i, 0))
    sds = jax.ShapeDtypeStruct((nl, r, c), F32)
    return pl.pallas_call(
        body,
        grid=(nl, nr),
        in_specs=[parts_spec(k) for k in range(nl)] + [blk, blk, blk],
        out_specs=[blk, blk, blk, blk],
        out_shape=[sds, sds, sds, sds],
        compiler_params=_cparams(("arbitrary", "arbitrary")),
        name="adamw",
    )(*parts_list, w, m, v)


def _flip(v, bit):
    return 1 - v if bit else v


def _exchange(arrs, scatter):
    n = len(arrs)
    shapes = [tuple(a.shape[1:] if scatter else a.shape) for a in arrs]

    def body(*refs):
        x_refs, o_refs = refs[:n], refs[n:2 * n]
        send_sems, recv_sems, local_sems = refs[2 * n:]
        mx, my, mc = lax.axis_index("x"), lax.axis_index("y"), lax.axis_index("c")
        me = 4 * mx + 2 * my + mc

        def peer_of(k):
            return (_flip(mx, (k >> 2) & 1), _flip(my, (k >> 1) & 1), _flip(mc, k & 1))

        def copy(a, k):
            peer = peer_of(k)
            src = x_refs[a].at[4 * peer[0] + 2 * peer[1] + peer[2]] if scatter else x_refs[a]
            return pltpu.make_async_remote_copy(
                src_ref=src, dst_ref=o_refs[a].at[me], send_sem=send_sems.at[a, k - 1],
                recv_sem=recv_sems.at[a, k - 1], device_id=peer, device_id_type=pl.DeviceIdType.MESH)

        def arrival(a, k):
            peer = peer_of(k)
            slot = o_refs[a].at[4 * peer[0] + 2 * peer[1] + peer[2]]
            return pltpu.make_async_remote_copy(
                src_ref=slot, dst_ref=slot, send_sem=send_sems.at[a, k - 1], recv_sem=recv_sems.at[a, k - 1],
                device_id=peer, device_id_type=pl.DeviceIdType.MESH)

        locals_ = [pltpu.make_async_copy(x_refs[a].at[me] if scatter else x_refs[a], o_refs[a].at[me],
                                         local_sems.at[a]) for a in range(n)]
        sends = [copy(a, k) for a in range(n) for k in range(1, N_DEV)]
        for cp in locals_ + sends:
            cp.start()
        for a in range(n):
            for k in range(1, N_DEV):
                arrival(a, k).wait_recv()
        for cp in sends:
            cp.wait_send()
        for cp in locals_:
            cp.wait()

    hbm = pl.BlockSpec(memory_space=pltpu.HBM)
    return pl.pallas_call(
        body,
        in_specs=[hbm] * n,
        out_specs=[hbm] * n,
        out_shape=[jax.ShapeDtypeStruct((N_DEV,) + s, a.dtype) for s, a in zip(shapes, arrs)],
        scratch_shapes=[
            pltpu.SemaphoreType.DMA((n, N_DEV - 1)),
            pltpu.SemaphoreType.DMA((n, N_DEV - 1)),
            pltpu.SemaphoreType.DMA((n,)),
        ],
        name="reduce_scatter_parts" if scatter else "all_gather",
    )(*arrs)


def _exchange_descriptors(x_refs, land_refs, send_sems, recv_sems, scatter):
    mx, my, mc = lax.axis_index("x"), lax.axis_index("y"), lax.axis_index("c")
    me = 4 * mx + 2 * my + mc
    sends, arrivals = [], []
    for a in range(len(x_refs)):
        for k in range(1, N_DEV):
            peer = (_flip(mx, (k >> 2) & 1), _flip(my, (k >> 1) & 1), _flip(mc, k & 1))
            slot = 4 * peer[0] + 2 * peer[1] + peer[2]
            si = a * (N_DEV - 1) + k - 1
            sems = dict(send_sem=send_sems.at[si], recv_sem=recv_sems.at[si],
                        device_id=peer, device_id_type=pl.DeviceIdType.MESH)
            sends.append(pltpu.make_async_remote_copy(
                src_ref=x_refs[a].at[slot] if scatter else x_refs[a], dst_ref=land_refs[a].at[me], **sems))
            arrivals.append(pltpu.make_async_remote_copy(
                src_ref=land_refs[a].at[slot], dst_ref=land_refs[a].at[slot], **sems))
    return sends, arrivals


def _own_slot_copies(x_refs, land_refs, local_sems, scatter):
    me = 4 * lax.axis_index("x") + 2 * lax.axis_index("y") + lax.axis_index("c")
    return [pltpu.make_async_copy(x_refs[a].at[me] if scatter else x_refs[a], land_refs[a].at[me], local_sems.at[a])
            for a in range(len(x_refs))]


def _exchange_start(arrs, scatter, name, after=None):
    n = len(arrs)
    shapes = [tuple(a.shape[1:] if scatter else a.shape) for a in arrs]
    lands = [lax.empty((N_DEV,) + s, a.dtype) for s, a in zip(shapes, arrs)]
    extra = [] if after is None else [after]

    def body(*refs):
        x_refs, land_refs = refs[:n], refs[n:2 * n]
        send_sems, recv_sems, local_sems = refs[2 * n + len(extra):2 * n + len(extra) + 3]
        token = refs[-1]
        sends, _ = _exchange_descriptors(x_refs, land_refs, send_sems, recv_sems, scatter)
        for cp in sends + _own_slot_copies(x_refs, land_refs, local_sems, scatter):
            cp.start()
        token[...] = jnp.zeros_like(token)

    hbm = pl.BlockSpec(memory_space=pltpu.HBM)
    sem = pl.BlockSpec(memory_space=pltpu.SEMAPHORE)
    sem_type = pltpu.SemaphoreType.DMA((n * (N_DEV - 1),))
    operands = [pltpu.with_memory_space_constraint(a, pltpu.HBM) for a in list(arrs) + lands]
    out = pl.pallas_call(
        body,
        in_specs=[hbm] * (2 * n) + [pl.BlockSpec(memory_space=pl.ANY)] * len(extra),
        out_specs=[sem, sem, sem] + [hbm] * (2 * n) + [pl.BlockSpec(memory_space=pltpu.VMEM)],
        out_shape=[sem_type, sem_type, pltpu.SemaphoreType.DMA((n,))] + [pltpu.HBM(a.shape, a.dtype) for a in operands]
        + [jax.ShapeDtypeStruct((8, LANE), F32)],
        input_output_aliases={i: 3 + i for i in range(2 * n)},
        compiler_params=pltpu.CompilerParams(has_side_effects=pltpu.SideEffectType.DATAFLOW_SIDE_EFFECTING),
        name=name,
    )(*operands, *extra)
    return (out[0:3], list(out[3:3 + n]), list(out[3 + n:3 + 2 * n]), scatter), out[-1]


def _exchange_wait(handle, after, name):
    sems, x_thru, land_thru, scatter = handle
    n = len(x_thru)

    def body(*refs):
        x_refs, land_refs = refs[:n], refs[n:2 * n]
        send_sems, recv_sems, local_sems = refs[2 * n:2 * n + 3]
        sends, arrivals = _exchange_descriptors(x_refs, land_refs, send_sems, recv_sems, scatter)
        for cp in sends:
            cp.wait_send()
        for cp in arrivals:
            cp.wait_recv()
        for cp in _own_slot_copies(x_refs, land_refs, local_sems, scatter):
            cp.wait()

    hbm = pl.BlockSpec(memory_space=pltpu.HBM)
    sem = pl.BlockSpec(memory_space=pltpu.SEMAPHORE)
    out = pl.pallas_call(
        body,
        in_specs=[hbm] * (2 * n) + [sem, sem, sem, pl.BlockSpec(memory_space=pl.ANY)],
        out_specs=[hbm] * (2 * n),
        out_shape=[pltpu.HBM(a.shape, a.dtype) for a in x_thru + land_thru],
        input_output_aliases={i: i for i in range(2 * n)},
        compiler_params=pltpu.CompilerParams(has_side_effects=pltpu.SideEffectType.DATAFLOW_SIDE_EFFECTING),
        name=name,
    )(*x_thru, *land_thru, *sems, after)
    return list(out[n:])


PACK_ALIGN = 2048


def _padded(n):
    return -(-n // PACK_ALIGN) * PACK_ALIGN


def _pad_to(a, axis, size):
    pad = [(0, 0)] * a.ndim
    pad[axis] = (0, size - a.shape[axis])
    return jnp.pad(a, pad)


def _pack(arrs, lead=0):
    flat = []
    for a in arrs:
        v = a.reshape(a.shape[:lead] + (-1,))
        flat.append(_pad_to(v, lead, _padded(v.shape[lead])))
    out = jnp.concatenate(flat, axis=lead)
    return out.reshape(out.shape[:lead] + (-1, LANE))


def _unpack(slab, shapes, lead=0):
    flat = slab.reshape(slab.shape[:lead] + (-1,))
    out, off = [], 0
    for s in shapes:
        n = math.prod(s)
        out.append(flat[..., off:off + n].reshape(slab.shape[:lead] + tuple(s)))
        off += _padded(n)
    return out


def _unshard(parts, ax):
    return jnp.concatenate([parts[j] for j in range(N_DEV)], axis=ax)


def _to_shards(full, ax):
    n = full.shape[ax] // N_DEV
    return jnp.stack([lax.slice_in_dim(full, j * n, (j + 1) * n, axis=ax) for j in range(N_DEV)])


def _pad_halves(w, hp):
    h = w.shape[-1] // 2
    zeros = jnp.zeros(w.shape[:-1] + (hp - h,), w.dtype)
    return jnp.concatenate([w[..., :h], zeros, w[..., h:], zeros], axis=-1)


def _unpad_halves(w, h):
    hp = w.shape[-1] // 2
    return jnp.concatenate([w[..., :h], w[..., hp:hp + h]], axis=-1)


SMALL_SHARDED = (("meta", 1), ("conv_dw", 2), ("ln_g", 2), ("ln_b", 2))
MATMUL_WEIGHTS = ("ffn1_w13", "ffn1_w2", "w_in", "conv_pw", "w_out", "ffn2_w13", "ffn2_w2")
REPLICATED = ("ln_in_g", "ln_in_b", "pool_w", "pool_scale", "conv_db", "conv_ln_g", "conv_ln_b", "ret_gn_g")
WEIGHT_ORDER = ("meta", "ln_in_g", "ln_in_b", "ffn1_w13", "ffn1_w2", "w_in", "pool_w", "pool_scale", "conv_dw",
                "conv_db", "conv_ln_g", "conv_ln_b", "conv_pw", "ret_gn_g", "w_out", "ffn2_w13", "ffn2_w2",
                "ln_g", "ln_b")


def _retention_tables(lp, heads):
    pos = jnp.arange(lp, dtype=F32) - PAD_ROWS
    inv_freq = ROPE_BASE ** (-jnp.arange(0, LANE, 2, dtype=F32) / LANE)
    ang = pos[:, None] * inv_freq[None, :]
    cos, sin = jnp.cos(ang), jnp.sin(ang)
    rope_c = jnp.concatenate([cos, cos], axis=1)
    rope_s = jnp.concatenate([-sin, sin], axis=1)
    log_gamma = jnp.log(1.0 - 2.0 ** (-5.0 - jnp.arange(heads, dtype=F32)))
    i = jnp.arange(CHUNK, dtype=F32)
    dm = jnp.exp(log_gamma[:, None, None] * jnp.abs(i[:, None] - i[None, :]))
    lanes = lambda t: jnp.broadcast_to(t[:, :, None], t.shape + (LANE,))
    qd = lanes(jnp.exp(log_gamma[:, None] * (i + 1.0)))
    kd = lanes(jnp.exp(log_gamma[:, None] * (CHUNK - 1.0 - i)))
    cd = lanes(jnp.broadcast_to(jnp.exp(log_gamma * CHUNK)[:, None], (heads, 8)))
    return rope_c, rope_s, dm, qd, kd, cd


def _block_diag(w):
    g, n, _ = w.shape
    rows = []
    for i in range(g):
        rows.append(jnp.concatenate([w[i] if j == i else jnp.zeros((n, n), w.dtype) for j in range(g)], axis=1))
    return jnp.concatenate(rows, axis=0)


def kernel(x, meta, ln_in_g, ln_in_b, ffn1_w13, ffn1_w2, w_in, pool_w, pool_scale, conv_dw, conv_db, conv_ln_g, conv_ln_b, conv_pw, ret_gn_g, w_out, ffn2_w13, ffn2_w2, ln_g, ln_b, loss_target, m_meta, m_ln_in_g, m_ln_in_b, m_ffn1_w13, m_ffn1_w2, m_w_in, m_pool_w, m_pool_scale, m_conv_dw, m_conv_db, m_conv_ln_g, m_conv_ln_b, m_conv_pw, m_ret_gn_g, m_w_out, m_ffn2_w13, m_ffn2_w2, m_ln_g, m_ln_b, v_meta, v_ln_in_g, v_ln_in_b, v_ffn1_w13, v_ffn1_w2, v_w_in, v_pool_w, v_pool_scale, v_conv_dw, v_conv_db, v_conv_ln_g, v_conv_ln_b, v_conv_pw, v_ret_gn_g, v_w_out, v_ffn2_w13, v_ffn2_w2, v_ln_g, v_ln_b):
    local = dict(meta=meta, ln_in_g=ln_in_g, ln_in_b=ln_in_b, ffn1_w13=ffn1_w13, ffn1_w2=ffn1_w2, w_in=w_in,
                 pool_w=pool_w, pool_scale=pool_scale, conv_dw=conv_dw, conv_db=conv_db, conv_ln_g=conv_ln_g,
                 conv_ln_b=conv_ln_b, conv_pw=conv_pw, ret_gn_g=ret_gn_g, w_out=w_out, ffn2_w13=ffn2_w13,
                 ffn2_w2=ffn2_w2, ln_g=ln_g, ln_b=ln_b)
    mom1 = dict(meta=m_meta, ln_in_g=m_ln_in_g, ln_in_b=m_ln_in_b, ffn1_w13=m_ffn1_w13, ffn1_w2=m_ffn1_w2,
                w_in=m_w_in, pool_w=m_pool_w, pool_scale=m_pool_scale, conv_dw=m_conv_dw, conv_db=m_conv_db,
                conv_ln_g=m_conv_ln_g, conv_ln_b=m_conv_ln_b, conv_pw=m_conv_pw, ret_gn_g=m_ret_gn_g,
                w_out=m_w_out, ffn2_w13=m_ffn2_w13, ffn2_w2=m_ffn2_w2, ln_g=m_ln_g, ln_b=m_ln_b)
    mom2 = dict(meta=v_meta, ln_in_g=v_ln_in_g, ln_in_b=v_ln_in_b, ffn1_w13=v_ffn1_w13, ffn1_w2=v_ffn1_w2,
                w_in=v_w_in, pool_w=v_pool_w, pool_scale=v_pool_scale, conv_dw=v_conv_dw, conv_db=v_conv_db,
                conv_ln_g=v_conv_ln_g, conv_ln_b=v_conv_ln_b, conv_pw=v_conv_pw, ret_gn_g=v_ret_gn_g,
                w_out=v_w_out, ffn2_w13=v_ffn2_w13, ffn2_w2=v_ffn2_w2, ln_g=v_ln_g, ln_b=v_ln_b)

    depth = ffn1_w13.shape[0]
    alpha = (2.0 * depth) ** 0.25
    seq, d = x.shape[1], x.shape[2]
    lp = FRONT + seq
    h_loc = ffn1_w2.shape[1]
    hp = -(-h_loc // LANE) * LANE
    c_pool = pool_scale.shape[1]
    c_conv = conv_db.shape[1]
    q_blk = (c_pool + 2 * c_conv) // LANE
    dy_blk = (c_pool + c_conv) // LANE
    heads = ret_gn_g.shape[1] // LANE
    assert meta.shape[0] == N_META and heads == RET_HEADS and conv_dw.shape[1] == CONV_WIDTH
    assert ffn1_w13.shape[2] == 2 * h_loc

    def to_wire(name, w):
        if name in ("ffn1_w13", "ffn2_w13"):
            return _pad_halves(w, hp)
        if name in ("ffn1_w2", "ffn2_w2"):
            return _pad_to(w, w.ndim - 2, hp)
        return w

    def from_wire(name, w):
        if name in ("ffn1_w13", "ffn2_w13"):
            return _unpad_halves(w, h_loc)
        if name in ("ffn1_w2", "ffn2_w2"):
            return w[..., :h_loc, :]
        return w

    small_names = [n for n, _ in SMALL_SHARDED]
    small_ax = dict(SMALL_SHARDED)
    n_mm = len(MATMUL_WEIGHTS)
    first = ("ffn1_w13", "ffn1_w2")
    wire = lambda l, names: [to_wire(n, local[n][l]).astype(BF16) for n in names]
    rest = [n for n in MATMUL_WEIGHTS if n not in first]
    groups = [(wire(0, first) + [_pack([local[n] for n in small_names])]), wire(0, rest)]
    groups += [wire(l, MATMUL_WEIGHTS) for l in range(1, depth)]
    handles, token = [], None
    for gi, arrs in enumerate(groups):
        handle, token = _exchange_start(arrs, False, f"gather_start_{gi}", after=token)
        handles.append(handle)
    got = _exchange_wait(handles[0], token, "gather_wait_0")
    gw_first = dict(zip(first, got))
    small_parts = _unpack(got[-1], [local[n].shape for n in small_names], 1)
    small_full = {n: _unshard(p, small_ax[n]) for n, p in zip(small_names, small_parts)}

    row = lambda v: v.reshape(1, -1)
    ln_params = [(row(ln_in_g), row(ln_in_b))]
    for l in range(depth):
        ln_params += [(row(small_full["ln_g"][l, j]), row(small_full["ln_b"][l, j])) for j in range(3)]
    rope_c, rope_s, dm, qd, kd, cd = _retention_tables(lp, heads)

    def layer_params(l, gw):
        return dict(
            ffn1=(gw["ffn1_w13"], gw["ffn1_w2"]),
            ffn2=(gw["ffn2_w13"], gw["ffn2_w2"]),
            w_in=_unshard(gw["w_in"], 1),
            w_out=gw["w_out"].reshape(-1, d),
            pool=(_block_diag(pool_w[l]).astype(BF16), row(pool_scale[l])),
            conv=(_pad_to(small_full["conv_dw"][l], 0, CONV_HALO), row(conv_db[l]), row(conv_ln_g[l]),
                  row(conv_ln_b[l]), gw["conv_pw"].reshape(c_conv, c_conv)),
            gn=row(ret_gn_g[l]),
        )

    r = [jnp.concatenate([jnp.zeros((PAD_ROWS, d), F32), small_full["meta"], x[0]], axis=0)]
    saved, layers = [], []
    for l in range(depth):
        k = 3 * l
        if l == 0:
            r1, a1, u1 = _ffn_fwd(r[k], *ln_params[k], gw_first["ffn1_w13"], gw_first["ffn1_w2"], alpha)
            gw = dict(gw_first, **dict(zip(rest, _exchange_wait(handles[1], r1, "gather_wait_1"))))
            p = layer_params(l, gw)
        else:
            gw = dict(zip(MATMUL_WEIGHTS, _exchange_wait(handles[l + 1], r[k], f"gather_wait_{l + 1}")))
            p = layer_params(l, gw)
            r1, a1, u1 = _ffn_fwd(r[k], *ln_params[k], *p["ffn1"], alpha)
        layers.append(p)
        z = _proj_fwd(r1, *ln_params[k + 1], p["w_in"])
        y_pool = _pool_fwd(z, *p["pool"])
        y_conv, conv_acc = _conv_fwd(z, *p["conv"])
        y_ret, states = _ret_fwd(z, rope_c, rope_s, dm, qd, kd, cd, p["gn"], q_blk)
        cat = jnp.concatenate([y_pool, y_conv, y_ret], axis=1)
        r2 = _out_fwd(r1, *ln_params[k + 1], cat, p["w_out"], alpha)
        r3, a2, u2 = _ffn_fwd(r2, *ln_params[k + 2], *p["ffn2"], alpha)
        r += [r1, r2, r3]
        saved.append((a1, u1, z, states, cat, a2, u2, conv_acc))

    target = jnp.concatenate([jnp.zeros((FRONT, d), F32), loss_target[0]], axis=0)
    dr, dg, db, loss_cols = _loss_bwd(r[-1], target, *ln_params[-1])
    loss = lax.psum(0.5 * jnp.sum(loss_cols) / d, MESH_AXES)
    ln_grads = {3 * depth: (dg, db)}
    w13_slot = lambda j: (j % 2) * (N_DEV // 2) + j // 2
    g_parts = [None] * depth
    g_rep = [None] * depth
    scatter_jobs = []

    def start_scatter(l, gp, names, extra=()):
        parts = [gp[n].astype(WIRE) for n in names] + list(extra)
        handle, token = _exchange_start(parts, True, f"scatter_start_{len(scatter_jobs)}")
        scatter_jobs.append((handle, l, names))
        return token

    def ffn_grads(dr_out, r_in, a, u, ln_p, wts):
        dr_in, hb, s, dz, dg, db = _ffn_bwd(dr_out, r_in, a, u, *ln_p, *wts, alpha)
        g13 = _mm_tn(hb, dz, bn=2 * hp, slot_of=w13_slot)
        g2 = _mm_tn(s, dr_out, 0.5).reshape(N_DEV, hp, d)
        return dr_in, (dg, db), g13, g2

    for l in reversed(range(depth)):
        p = layers[l]
        k = 3 * l
        a1, u1, z, states, cat, a2, u2, conv_acc = saved[l]
        gp, gr = {}, {}
        dr, ln_grads[k + 2], gp["ffn2_w13"], gp["ffn2_w2"] = ffn_grads(dr, r[k + 2], a2, u2, ln_params[k + 2], p["ffn2"])
        zero = start_scatter(l, gp, ("ffn2_w13", "ffn2_w2"))[0:1, 0:1]

        dr2 = dr
        dcat = _mm_nt(dr2, p["w_out"])
        gp["w_out"] = _mm_tn(cat, dr2).reshape(N_DEV, -1, d)
        dxp, dwbd, gr["pool_scale"] = _pool_bwd(z, dcat, p["pool"][0], p["pool"][1] + zero)
        dca, dcg, ddw, gr["conv_db"], gr["conv_ln_g"], gr["conv_ln_b"], dpw = _conv_bwd(
            z, dcat, conv_acc, p["conv"][0], p["conv"][2], p["conv"][3], p["conv"][4])
        dq, dk, dv, dgt, gr["ret_gn_g"] = _ret_bwd(z, dcat, states, rope_c, rope_s, dm, qd, kd, cd, p["gn"],
                                                  q_blk, dy_blk)
        dz = jnp.concatenate([dxp, dca, dcg, dq, dk, dv, dgt], axis=1)
        dr, hb, dg, db = _in_bwd(dr2, r[k + 1], dz, *ln_params[k + 1], p["w_in"], alpha)
        ln_grads[k + 1] = (dg, db)
        gp["w_in"] = _to_shards(_mm_tn(hb, dz), 1)
        gp["conv_pw"] = dpw.reshape(N_DEV, -1, c_conv)
        n_grp, grp = pool_w.shape[1], pool_w.shape[2]
        gr["pool_w"] = jnp.stack([dwbd[i * grp:(i + 1) * grp, i * grp:(i + 1) * grp] for i in range(n_grp)])
        gr["conv_dw"] = ddw[:CONV_WIDTH]

        zero = start_scatter(l, gp, ("w_in", "conv_pw", "w_out"))[0:1, 0:1]
        ln_g_in, ln_b_in = ln_params[k]
        dr, ln_grads[k], gp["ffn1_w13"], gp["ffn1_w2"] = ffn_grads(dr, r[k], a1, u1, (ln_g_in + zero, ln_b_in), p["ffn1"])
        g_parts[l], g_rep[l] = gp, gr
        if l > 0:
            zero = start_scatter(l, gp, ("ffn1_w13", "ffn1_w2"))[0:1, 0:1]
            g_prev, b_prev = ln_params[k - 1]
            ln_params[k - 1] = (g_prev + zero, b_prev)

    grad_x = dr[FRONT:][None]
    local_shape = lambda name: ((CONV_WIDTH, c_conv) if name == "conv_dw" else local[name].shape[1:])
    stack_layers = lambda name: jnp.stack([g_rep[l][name].reshape(local_shape(name)) for l in range(depth)])
    small_grad = dict(
        meta=dr[PAD_ROWS:FRONT],
        conv_dw=stack_layers("conv_dw"),
        ln_g=jnp.stack([jnp.stack([ln_grads[3 * l + j + 1][0][0] for j in range(3)]) for l in range(depth)]),
        ln_b=jnp.stack([jnp.stack([ln_grads[3 * l + j + 1][1][0] for j in range(3)]) for l in range(depth)]),
    )
    rep_grad = {n: stack_layers(n) for n in REPLICATED if n not in ("ln_in_g", "ln_in_b")}
    rep_grad["ln_in_g"], rep_grad["ln_in_b"] = ln_grads[0][0][0], ln_grads[0][1][0]

    small_pack8 = _pack([_to_shards(small_grad[n], small_ax[n]) for n in small_names], 1)
    after = start_scatter(0, g_parts[0], ("ffn1_w13", "ffn1_w2"), extra=[small_pack8])
    rep_parts = _exchange([_pack([rep_grad[n] for n in REPLICATED])], False)[0]
    scattered, outs_by_name = {}, {}

    def update(names):
        nonlocal after
        for n in names:
            res = _adamw([scattered[l, n] for l in range(depth)], *[to_wire(n, src[n]) for src in (local, mom1, mom2)])
            outs_by_name[n] = [from_wire(n, t) for t in res]
            after = res[0]

    for ji, (handle, l, names) in enumerate(scatter_jobs):
        if ji == len(scatter_jobs) - 1:
            update([n for n in MATMUL_WEIGHTS if n not in first])
        got = _exchange_wait(handle, after, f"scatter_wait_{ji}")
        scattered.update({(l, n): t for n, t in zip(names, got)})
        after = got[0]
    small_scattered = got[-1]
    update(first)
    for names, parts in ((small_names, small_scattered), (REPLICATED, rep_parts)):
        shapes = [local[n].shape for n in names]
        res = _adamw([parts], *[_pack([src[n] for n in names])[None] for src in (local, mom1, mom2)])
        for n, vals in zip(names, zip(*[_unpack(t[0], shapes) for t in res])):
            outs_by_name[n] = list(vals)

    outs = [loss, grad_x]
    for kind in range(4):
        outs += [outs_by_name[n][kind] for n in WEIGHT_ORDER]
    return tuple(outs)
```

```python
import math

import jax
import jax.numpy as jnp
from jax import lax
from jax.experimental import pallas as pl
from jax.experimental.pallas import tpu as pltpu

F32 = jnp.float32
BF16 = jnp.bfloat16
WIRE = jnp.bfloat16

N_DEV = 8
MESH_AXES = ("x", "y", "c")
CHUNK = 64
N_META = 16
PAD_ROWS = 240
FRONT = PAD_ROWS + N_META
LN_EPS = 1e-5
LANE = 128
POOL_WINDOWS = (2, 4, 8, 16)
CONV_WIDTH = 31
CONV_HALO = 32
POOL_HALO = 16
RET_HEADS = 4
ROPE_BASE = 10000.0

ADAM_LR = 0.001
ADAM_B1 = 0.9
ADAM_B2 = 0.999
ADAM_EPS = 1e-08
ADAM_WD = 0.01
ADAM_STEP = 10

VMEM_LIMIT = 56 * 1024 * 1024
DENSE_TM_FWD = 768
DENSE_TM_BWD = 528
MIX_TM = 768
MM_TN_TK = 1408


def _cparams(sem):
    return pltpu.CompilerParams(dimension_semantics=sem, vmem_limit_bytes=VMEM_LIMIT)


def _pick_tm(lp, target, mult=16):
    best = None
    for t in range(mult, min(lp, target) + 1, mult):
        if lp % t == 0:
            best = t
    assert best is not None, (lp, target, mult)
    return best


def _dot(a, b):
    return jnp.dot(a, b, preferred_element_type=F32)


def _dot_nt(a, b):
    return lax.dot_general(a, b, (((1,), (1,)), ((), ())), preferred_element_type=F32)


def _dot_tn(a, b):
    return lax.dot_general(a, b, (((0,), (0,)), ((), ())), preferred_element_type=F32)


def _sigmoid(x):
    return 0.5 * jnp.tanh(0.5 * x) + 0.5


def _ln_stats(r):
    mu = jnp.mean(r, axis=-1, keepdims=True)
    xc = r - mu
    var = jnp.mean(xc * xc, axis=-1, keepdims=True)
    rstd = lax.rsqrt(var + LN_EPS)
    return xc * rstd, rstd


def _ln_bwd(dh, xhat, rstd, g):
    dxh = dh * g
    m1 = jnp.mean(dxh, axis=-1, keepdims=True)
    m2 = jnp.mean(dxh * xhat, axis=-1, keepdims=True)
    return rstd * (dxh - m1 - xhat * m2)


def _colsum(x):
    return jnp.sum(x, axis=0, keepdims=True)


def _row_ids(tile, tm):
    return lax.broadcasted_iota(jnp.int32, (tm, 1), 0) + tile * tm


def _ffn_fwd(r_prev, g, b, w13g, w2g, alpha):
    lp, d = r_prev.shape
    nf = N_DEV // 2
    fc = w13g.shape[2]
    w2c = w2g.reshape(nf, fc, d)
    tm = _pick_tm(lp, DENSE_TM_FWD)

    def body(r_ref, g_ref, b_ref, w1_ref, w3_ref, w2_ref, out_ref, a_ref, u_ref, hb, acc):
        f = pl.program_id(1)

        @pl.when(f == 0)
        def _():
            xhat, _ = _ln_stats(r_ref[...])
            hb[...] = (xhat * g_ref[...] + b_ref[...]).astype(BF16)
            acc[...] = jnp.zeros_like(acc)

        h = hb[...]
        a = _dot(h, w1_ref[...])
        u = _dot(h, w3_ref[...])
        a_ref[...] = a.astype(BF16)
        u_ref[...] = u.astype(BF16)
        s = a * _sigmoid(a) * u
        acc[...] += _dot(s.astype(BF16), w2_ref[...])

        @pl.when(f == nf - 1)
        def _():
            xhat, _ = _ln_stats(r_ref[...])
            out_ref[...] = alpha * (xhat * g_ref[...] + b_ref[...]) + 0.5 * acc[...]

    return pl.pallas_call(
        body,
        grid=(lp // tm, nf),
        in_specs=[
            pl.BlockSpec((tm, d), lambda i, f: (i, 0)),
            pl.BlockSpec((1, d), lambda i, f: (0, 0)),
            pl.BlockSpec((1, d), lambda i, f: (0, 0)),
            pl.BlockSpec((None, d, fc), lambda i, f: (f, 0, 0)),
            pl.BlockSpec((None, d, fc), lambda i, f: (nf + f, 0, 0)),
            pl.BlockSpec((None, fc, d), lambda i, f: (f, 0, 0)),
        ],
        out_specs=[
            pl.BlockSpec((tm, d), lambda i, f: (i, 0)),
            pl.BlockSpec((tm, fc), lambda i, f: (i, f)),
            pl.BlockSpec((tm, fc), lambda i, f: (i, f)),
        ],
        out_shape=[
            jax.ShapeDtypeStruct((lp, d), F32),
            jax.ShapeDtypeStruct((lp, nf * fc), BF16),
            jax.ShapeDtypeStruct((lp, nf * fc), BF16),
        ],
        scratch_shapes=[pltpu.VMEM((tm, d), BF16), pltpu.VMEM((tm, d), F32)],
        compiler_params=_cparams(("parallel", "arbitrary")),
        name="ffn_fwd",
    )(r_prev, g, b, w13g, w13g, w2c)


def _ffn_bwd(dr_next, r_prev, a, u, g, b, w13g, w2g, alpha):
    lp, d = r_prev.shape
    nf = N_DEV // 2
    fc = w13g.shape[2]
    w2c = w2g.reshape(nf, fc, d)
    tm = _pick_tm(lp, DENSE_TM_BWD)

    def body(dr_ref, r_ref, a_ref, u_ref, g_ref, b_ref, w1_ref, w3_ref, w2_ref,
             drp_ref, hb_ref, s_ref, dz_ref, dg_ref, db_ref, dyb, dhacc):
        i = pl.program_id(0)
        f = pl.program_id(1)

        @pl.when(jnp.logical_and(i == 0, f == 0))
        def _():
            dg_ref[...] = jnp.zeros_like(dg_ref)
            db_ref[...] = jnp.zeros_like(db_ref)

        @pl.when(f == 0)
        def _():
            dyb[...] = (0.5 * dr_ref[...]).astype(BF16)
            dhacc[...] = jnp.zeros_like(dhacc)
            xhat, _ = _ln_stats(r_ref[...])
            hb_ref[...] = (xhat * g_ref[...] + b_ref[...]).astype(BF16)

        ds = _dot_nt(dyb[...], w2_ref[...])
        av = a_ref[...].astype(F32)
        uv = u_ref[...].astype(F32)
        sig = _sigmoid(av)
        sl = av * sig
        da = (ds * uv * (sig * (1.0 + av * (1.0 - sig)))).astype(BF16)
        du = (ds * sl).astype(BF16)
        s_ref[...] = (sl * uv).astype(BF16)
        dz_ref[:, 0:fc] = da
        dz_ref[:, fc:2 * fc] = du
        dhacc[...] += _dot_nt(da, w1_ref[...]) + _dot_nt(du, w3_ref[...])

        @pl.when(f == nf - 1)
        def _():
            dh = alpha * dr_ref[...] + dhacc[...]
            xhat, rstd = _ln_stats(r_ref[...])
            drp_ref[...] = _ln_bwd(dh, xhat, rstd, g_ref[...])
            dg_ref[...] += _colsum(dh * xhat)
            db_ref[...] += _colsum(dh)

    row = lambda i, f: (i, 0)
    const = lambda i, f: (0, 0)
    chunk = lambda i, f: (i, f)
    return pl.pallas_call(
        body,
        grid=(lp // tm, nf),
        in_specs=[
            pl.BlockSpec((tm, d), row),
            pl.BlockSpec((tm, d), row),
            pl.BlockSpec((tm, fc), chunk),
            pl.BlockSpec((tm, fc), chunk),
            pl.BlockSpec((1, d), const),
            pl.BlockSpec((1, d), const),
            pl.BlockSpec((None, d, fc), lambda i, f: (f, 0, 0)),
            pl.BlockSpec((None, d, fc), lambda i, f: (nf + f, 0, 0)),
            pl.BlockSpec((None, fc, d), lambda i, f: (f, 0, 0)),
        ],
        out_specs=[
            pl.BlockSpec((tm, d), row),
            pl.BlockSpec((tm, d), row),
            pl.BlockSpec((tm, fc), chunk),
            pl.BlockSpec((tm, 2 * fc), chunk),
            pl.BlockSpec((1, d), const),
            pl.BlockSpec((1, d), const),
        ],
        out_shape=[
            jax.ShapeDtypeStruct((lp, d), F32),
            jax.ShapeDtypeStruct((lp, d), BF16),
            jax.ShapeDtypeStruct((lp, nf * fc), BF16),
            jax.ShapeDtypeStruct((lp, 2 * nf * fc), BF16),
            jax.ShapeDtypeStruct((1, d), F32),
            jax.ShapeDtypeStruct((1, d), F32),
        ],
        scratch_shapes=[pltpu.VMEM((tm, d), BF16), pltpu.VMEM((tm, d), F32)],
        compiler_params=_cparams(("arbitrary", "arbitrary")),
        name="ffn_bwd",
    )(dr_next, r_prev, a, u, g, b, w13g, w13g, w2c)


def _mm_tn(a, b, scale=1.0, bn=None, slot_of=None):
    t, m = a.shape
    n = b.shape[1]
    if bn is None:
        bn = n if n <= 1408 else _pick_tm(n, 1408, LANE)
    bm = m if m <= 1536 else _pick_tm(m, 1536, LANE)
    tk = _pick_tm(t, MM_TN_TK)
    nt = t // tk

    def body(a_ref, b_ref, o_ref, acc):
        k = pl.program_id(2)

        @pl.when(k == 0)
        def _():
            acc[...] = jnp.zeros_like(acc)

        acc[...] += _dot_tn(a_ref[...].astype(BF16), b_ref[...].astype(BF16))

        @pl.when(k == nt - 1)
        def _():
            o_ref[...] = (acc[...] * scale).astype(o_ref.dtype)

    if slot_of is None:
        out_spec = pl.BlockSpec((bm, bn), lambda i, j, k: (i, j))
        out_shape = jax.ShapeDtypeStruct((m, n), WIRE)
    else:
        out_spec = pl.BlockSpec((None, bm, bn), lambda i, j, k: (slot_of(j), i, 0))
        out_shape = jax.ShapeDtypeStruct((n // bn, m, bn), WIRE)
    return pl.pallas_call(
        body,
        grid=(m // bm, n // bn, nt),
        in_specs=[
            pl.BlockSpec((tk, bm), lambda i, j, k: (k, i)),
            pl.BlockSpec((tk, bn), lambda i, j, k: (k, j)),
        ],
        out_specs=out_spec,
        out_shape=out_shape,
        scratch_shapes=[pltpu.VMEM((bm, bn), F32)],
        compiler_params=_cparams(("parallel", "parallel", "arbitrary")),
        name="mm_tn",
    )(a, b)


def _proj_fwd(r_prev, g, b, w_in):
    lp, d = r_prev.shape
    n = w_in.shape[1]
    tm = _pick_tm(lp, DENSE_TM_FWD)

    def body(r_ref, g_ref, b_ref, w_ref, z_ref):
        xhat, _ = _ln_stats(r_ref[...])
        h = (xhat * g_ref[...] + b_ref[...]).astype(BF16)
        z = _dot(h, w_ref[...])
        rows = _row_ids(pl.program_id(0), tm)
        z_ref[...] = jnp.where(rows >= PAD_ROWS, z, 0.0)

    return pl.pallas_call(
        body,
        grid=(lp // tm,),
        in_specs=[
            pl.BlockSpec((tm, d), lambda i: (i, 0)),
            pl.BlockSpec((1, d), lambda i: (0, 0)),
            pl.BlockSpec((1, d), lambda i: (0, 0)),
            pl.BlockSpec((d, n), lambda i: (0, 0)),
        ],
        out_specs=pl.BlockSpec((tm, n), lambda i: (i, 0)),
        out_shape=jax.ShapeDtypeStruct((lp, n), F32),
        compiler_params=_cparams(("parallel",)),
        name="proj_fwd",
    )(r_prev, g, b, w_in)


def _out_fwd(r_prev, g, b, cat, w_out, alpha):
    lp, d = r_prev.shape
    k = cat.shape[1]
    tm = _pick_tm(lp, DENSE_TM_FWD)

    def body(r_ref, g_ref, b_ref, c_ref, w_ref, o_ref):
        xhat, _ = _ln_stats(r_ref[...])
        o_ref[...] = alpha * (xhat * g_ref[...] + b_ref[...]) + _dot(c_ref[...], w_ref[...])

    return pl.pallas_call(
        body,
        grid=(lp // tm,),
        in_specs=[
            pl.BlockSpec((tm, d), lambda i: (i, 0)),
            pl.BlockSpec((1, d), lambda i: (0, 0)),
            pl.BlockSpec((1, d), lambda i: (0, 0)),
            pl.BlockSpec((tm, k), lambda i: (i, 0)),
            pl.BlockSpec((k, d), lambda i: (0, 0)),
        ],
        out_specs=pl.BlockSpec((tm, d), lambda i: (i, 0)),
        out_shape=jax.ShapeDtypeStruct((lp, d), F32),
        compiler_params=_cparams(("parallel",)),
        name="out_fwd",
    )(r_prev, g, b, cat, w_out)


def _mm_nt(x, w):
    lp, n = x.shape
    k = w.shape[0]
    tm = _pick_tm(lp, DENSE_TM_FWD)

    def body(x_ref, w_ref, o_ref):
        o_ref[...] = _dot_nt(x_ref[...].astype(BF16), w_ref[...])

    return pl.pallas_call(
        body,
        grid=(lp // tm,),
        in_specs=[pl.BlockSpec((tm, n), lambda i: (i, 0)), pl.BlockSpec((k, n), lambda i: (0, 0))],
        out_specs=pl.BlockSpec((tm, k), lambda i: (i, 0)),
        out_shape=jax.ShapeDtypeStruct((lp, k), F32),
        compiler_params=_cparams(("parallel",)),
        name="mm_nt",
    )(x, w)


def _in_bwd(dr_next, r_prev, dz, g, b, w_in, alpha):
    lp, d = r_prev.shape
    n = w_in.shape[1]
    tm = _pick_tm(lp, DENSE_TM_FWD)

    def body(dr_ref, r_ref, dz_ref, g_ref, b_ref, w_ref, drp_ref, hb_ref, dg_ref, db_ref):
        @pl.when(pl.program_id(0) == 0)
        def _():
            dg_ref[...] = jnp.zeros_like(dg_ref)
            db_ref[...] = jnp.zeros_like(db_ref)

        dh = alpha * dr_ref[...] + _dot_nt(dz_ref[...], w_ref[...])
        xhat, rstd = _ln_stats(r_ref[...])
        hb_ref[...] = (xhat * g_ref[...] + b_ref[...]).astype(BF16)
        drp_ref[...] = _ln_bwd(dh, xhat, rstd, g_ref[...])
        dg_ref[...] += _colsum(dh * xhat)
        db_ref[...] += _colsum(dh)

    row = lambda i: (i, 0)
    const = lambda i: (0, 0)
    return pl.pallas_call(
        body,
        grid=(lp // tm,),
        in_specs=[
            pl.BlockSpec((tm, d), row),
            pl.BlockSpec((tm, d), row),
            pl.BlockSpec((tm, n), row),
            pl.BlockSpec((1, d), const),
            pl.BlockSpec((1, d), const),
            pl.BlockSpec((d, n), const),
        ],
        out_specs=[
            pl.BlockSpec((tm, d), row),
            pl.BlockSpec((tm, d), row),
            pl.BlockSpec((1, d), const),
            pl.BlockSpec((1, d), const),
        ],
        out_shape=[
            jax.ShapeDtypeStruct((lp, d), F32),
            jax.ShapeDtypeStruct((lp, d), BF16),
            jax.ShapeDtypeStruct((1, d), F32),
            jax.ShapeDtypeStruct((1, d), F32),
        ],
        compiler_params=_cparams(("arbitrary",)),
        name="in_bwd",
    )(dr_next, r_prev, dz, g, b, w_in)


def _loss_bwd(r_last, target, g, b):
    lp, d = r_last.shape
    tm = _pick_tm(lp, DENSE_TM_FWD)

    def body(r_ref, t_ref, g_ref, b_ref, dr_ref, dg_ref, db_ref, ls_ref):
        i = pl.program_id(0)

        @pl.when(i == 0)
        def _():
            dg_ref[...] = jnp.zeros_like(dg_ref)
            db_ref[...] = jnp.zeros_like(db_ref)
            ls_ref[...] = jnp.zeros_like(ls_ref)

        xhat, rstd = _ln_stats(r_ref[...])
        y = xhat * g_ref[...] + b_ref[...]
        rows = _row_ids(i, tm)
        err = jnp.where(rows >= FRONT, y - t_ref[...], 0.0)
        ls_ref[...] += _colsum(err * err)
        dy = err * (1.0 / d)
        dr_ref[...] = _ln_bwd(dy, xhat, rstd, g_ref[...])
        dg_ref[...] += _colsum(dy * xhat)
        db_ref[...] += _colsum(dy)

    row = lambda i: (i, 0)
    const = lambda i: (0, 0)
    return pl.pallas_call(
        body,
        grid=(lp // tm,),
        in_specs=[
            pl.BlockSpec((tm, d), row),
            pl.BlockSpec((tm, d), row),
            pl.BlockSpec((1, d), const),
            pl.BlockSpec((1, d), const),
        ],
        out_specs=[
            pl.BlockSpec((tm, d), row),
            pl.BlockSpec((1, d), const),
            pl.BlockSpec((1, d), const),
            pl.BlockSpec((1, d), const),
        ],
        out_shape=[
            jax.ShapeDtypeStruct((lp, d), F32),
            jax.ShapeDtypeStruct((1, d), F32),
            jax.ShapeDtypeStruct((1, d), F32),
            jax.ShapeDtypeStruct((1, d), F32),
        ],
        compiler_params=_cparams(("arbitrary",)),
        name="loss_bwd",
    )(r_last, target, g, b)


def _pool_counts(tile, tm, width):
    pos = _row_ids(tile, tm) - PAD_ROWS
    lane = lax.broadcasted_iota(jnp.int32, (1, width), 1)
    group = width // len(POOL_WINDOWS)
    win = jnp.full((1, width), POOL_WINDOWS[-1], jnp.int32)
    for gi in range(len(POOL_WINDOWS) - 2, -1, -1):
        win = jnp.where(lane < (gi + 1) * group, POOL_WINDOWS[gi], win)
    cnt = jnp.clip(pos + 1, 1, win)
    return cnt.astype(F32), lane, group


def _pool_select(sums, lane, group):
    out = sums[-1]
    for gi in range(len(POOL_WINDOWS) - 2, -1, -1):
        out = jnp.where(lane < (gi + 1) * group, sums[gi], out)
    return out


def _pool_window_sums(ext, tm, sign):
    base = POOL_HALO if sign < 0 else 0
    acc = ext[pl.ds(base, tm), :]
    sums, k = [], 1
    for w in POOL_WINDOWS:
        while k < w:
            acc = acc + ext[pl.ds(base + sign * k, tm), :]
            k += 1
        sums.append(acc)
    return sums


def _pool_fwd(z, wbd, scale):
    lp = z.shape[0]
    c = wbd.shape[0]
    tm = _pick_tm(lp, MIX_TM, CHUNK)

    def body(x_ref, w_ref, s_ref, y_ref, ext):
        i = pl.program_id(0)

        @pl.when(i == 0)
        def _():
            ext[0:POOL_HALO, :] = jnp.zeros((POOL_HALO, c), F32)

        x = x_ref[...]
        ext[POOL_HALO:POOL_HALO + tm, :] = x
        cnt, lane, group = _pool_counts(i, tm, c)
        sums = _pool_window_sums(ext, tm, -1)
        y = _pool_select(sums, lane, group) / cnt - x
        ext[0:POOL_HALO, :] = ext[tm:tm + POOL_HALO, :]
        y_ref[...] = (_dot(y.astype(BF16), w_ref[...]) * s_ref[...]).astype(BF16)

    return pl.pallas_call(
        body,
        grid=(lp // tm,),
        in_specs=[
            pl.BlockSpec((tm, c), lambda i: (i, 0)),
            pl.BlockSpec((c, c), lambda i: (0, 0)),
            pl.BlockSpec((1, c), lambda i: (0, 0)),
        ],
        out_specs=pl.BlockSpec((tm, c), lambda i: (i, 0)),
        out_shape=jax.ShapeDtypeStruct((lp, c), BF16),
        scratch_shapes=[pltpu.VMEM((tm + POOL_HALO, c), F32)],
        compiler_params=_cparams(("arbitrary",)),
        name="pool_fwd",
    )(z, wbd, scale)


def _pool_bwd(z, dcat, wbd, scale):
    lp = z.shape[0]
    c = wbd.shape[0]
    tm = _pick_tm(lp, MIX_TM, CHUNK)
    ni = lp // tm
    hpt = tm // POOL_HALO

    def body(x_ref, xh_ref, dy_ref, w_ref, s_ref, dx_ref, dw_ref, ds_ref, ext, ext2):
        i = pl.program_id(0)
        t = ni - 1 - i

        @pl.when(i == 0)
        def _():
            dw_ref[...] = jnp.zeros_like(dw_ref)
            ds_ref[...] = jnp.zeros_like(ds_ref)
            ext2[tm:tm + POOL_HALO, :] = jnp.zeros((POOL_HALO, c), F32)

        x = x_ref[...]
        ext[0:POOL_HALO, :] = jnp.where(t > 0, xh_ref[...], 0.0)
        ext[POOL_HALO:POOL_HALO + tm, :] = x
        cnt, lane, group = _pool_counts(t, tm, c)
        y = (_pool_select(_pool_window_sums(ext, tm, -1), lane, group) / cnt - x).astype(BF16)
        w = w_ref[...]
        dyv = dy_ref[...]
        ds_ref[...] += _colsum(_dot(y, w) * dyv)
        do = (dyv * s_ref[...]).astype(BF16)
        dw_ref[...] += _dot_tn(y, do)
        dyp = _dot_nt(do, w)
        ext2[0:tm, :] = dyp / cnt
        dx = _pool_select(_pool_window_sums(ext2, tm, 1), lane, group) - dyp
        ext2[tm:tm + POOL_HALO, :] = ext2[0:POOL_HALO, :]
        rows = _row_ids(t, tm)
        dx_ref[...] = jnp.where(rows >= PAD_ROWS, dx, 0.0).astype(BF16)

    rev = lambda i: (ni - 1 - i, 0)
    return pl.pallas_call(
        body,
        grid=(ni,),
        in_specs=[
            pl.BlockSpec((tm, c), rev),
            pl.BlockSpec((POOL_HALO, c), lambda i: (jnp.maximum((ni - 1 - i) * hpt - 1, 0), 0)),
            pl.BlockSpec((tm, c), rev),
            pl.BlockSpec((c, c), lambda i: (0, 0)),
            pl.BlockSpec((1, c), lambda i: (0, 0)),
        ],
        out_specs=[
            pl.BlockSpec((tm, c), rev),
            pl.BlockSpec((c, c), lambda i: (0, 0)),
            pl.BlockSpec((1, c), lambda i: (0, 0)),
        ],
        out_shape=[
            jax.ShapeDtypeStruct((lp, c), BF16),
            jax.ShapeDtypeStruct((c, c), F32),
            jax.ShapeDtypeStruct((1, c), F32),
        ],
        scratch_shapes=[pltpu.VMEM((tm + POOL_HALO, c), F32), pltpu.VMEM((tm + POOL_HALO, c), F32)],
        compiler_params=_cparams(("arbitrary",)),
        name="pool_bwd",
    )(z, z, dcat, wbd, scale)


def _conv_taps(ext, w_ref, tm, first):
    acc = w_ref[0:1, :] * ext[pl.ds(first, tm), :]
    for k in range(1, CONV_WIDTH):
        acc = acc + w_ref[k:k + 1, :] * ext[pl.ds(first + k, tm), :]
    return acc


def _conv_fwd(z, w_dw, b_dw, ln_g, ln_b, w_pw):
    lp = z.shape[0]
    c = w_pw.shape[0]
    tm = _pick_tm(lp, MIX_TM, CHUNK)

    def body(a_ref, gt_ref, w_ref, bd_ref, g_ref, b_ref, pw_ref, y_ref, acc_ref, ext):
        i = pl.program_id(0)

        @pl.when(i == 0)
        def _():
            ext[0:CONV_HALO, :] = jnp.zeros((CONV_HALO, c), F32)

        ext[CONV_HALO:CONV_HALO + tm, :] = a_ref[...] * _sigmoid(gt_ref[...])
        acc = _conv_taps(ext, w_ref, tm, CONV_HALO - CONV_WIDTH + 1) + bd_ref[...]
        acc_ref[...] = acc
        ext[0:CONV_HALO, :] = ext[tm:tm + CONV_HALO, :]
        xhat, _ = _ln_stats(acc)
        n = xhat * g_ref[...] + b_ref[...]
        act = n * _sigmoid(n)
        y_ref[...] = _dot(act.astype(BF16), pw_ref[...]).astype(BF16)

    const = lambda i: (0, 0)
    return pl.pallas_call(
        body,
        grid=(lp // tm,),
        in_specs=[
            pl.BlockSpec((tm, c), lambda i: (i, 1)),
            pl.BlockSpec((tm, c), lambda i: (i, 2)),
            pl.BlockSpec((CONV_HALO, c), const),
            pl.BlockSpec((1, c), const),
            pl.BlockSpec((1, c), const),
            pl.BlockSpec((1, c), const),
            pl.BlockSpec((c, c), const),
        ],
        out_specs=[pl.BlockSpec((tm, c), lambda i: (i, 0)), pl.BlockSpec((tm, c), lambda i: (i, 0))],
        out_shape=[jax.ShapeDtypeStruct((lp, c), BF16), jax.ShapeDtypeStruct((lp, c), F32)],
        scratch_shapes=[pltpu.VMEM((tm + CONV_HALO, c), F32)],
        compiler_params=_cparams(("arbitrary",)),
        name="conv_fwd",
    )(z, z, w_dw, b_dw, ln_g, ln_b, w_pw)


def _conv_bwd(z, dcat, acc_fwd, w_dw, ln_g, ln_b, w_pw):
    lp = z.shape[0]
    c = w_pw.shape[0]
    tm = _pick_tm(lp, MIX_TM, CHUNK)
    ni = lp // tm
    hpt = tm // CONV_HALO
    first = CONV_HALO - CONV_WIDTH + 1

    def body(a_ref, gt_ref, ah_ref, gh_ref, dy_ref, acc_ref, w_ref, g_ref, b_ref, pw_ref,
             dca_ref, dcg_ref, dw_ref, dbd_ref, dg_ref, db_ref, dpw_ref, extu, extd):
        i = pl.program_id(0)
        t = ni - 1 - i

        @pl.when(i == 0)
        def _():
            dw_ref[...] = jnp.zeros_like(dw_ref)
            dbd_ref[...] = jnp.zeros_like(dbd_ref)
            dg_ref[...] = jnp.zeros_like(dg_ref)
            db_ref[...] = jnp.zeros_like(db_ref)
            dpw_ref[...] = jnp.zeros_like(dpw_ref)
            extd[tm:tm + CONV_HALO, :] = jnp.zeros((CONV_HALO, c), F32)

        ca = a_ref[...]
        sg = _sigmoid(gt_ref[...])
        extu[0:CONV_HALO, :] = jnp.where(t > 0, ah_ref[...] * _sigmoid(gh_ref[...]), 0.0)
        extu[CONV_HALO:CONV_HALO + tm, :] = ca * sg
        xhat, rstd = _ln_stats(acc_ref[...])
        gam = g_ref[...]
        n = xhat * gam + b_ref[...]
        sn = _sigmoid(n)
        act = (n * sn).astype(BF16)
        do = dy_ref[...].astype(BF16)
        dpw_ref[...] += _dot_tn(act, do)
        dn = _dot_nt(do, pw_ref[...]) * (sn * (1.0 + n * (1.0 - sn)))
        dg_ref[...] += _colsum(dn * xhat)
        db_ref[...] += _colsum(dn)
        dyc = _ln_bwd(dn, xhat, rstd, gam)
        dbd_ref[...] += _colsum(dyc)
        extd[0:tm, :] = dyc
        du = None
        for k in range(CONV_WIDTH):
            dw_ref[k:k + 1, :] += _colsum(dyc * extu[pl.ds(first + k, tm), :])
            term = w_ref[k:k + 1, :] * extd[pl.ds(CONV_WIDTH - 1 - k, tm), :]
            du = term if du is None else du + term
        extd[tm:tm + CONV_HALO, :] = extd[0:CONV_HALO, :]
        du = jnp.where(_row_ids(t, tm) >= PAD_ROWS, du, 0.0)
        dca_ref[...] = (du * sg).astype(BF16)
        dcg_ref[...] = (du * ca * sg * (1.0 - sg)).astype(BF16)

    const = lambda i: (0, 0)
    rev = lambda col: (lambda i: (ni - 1 - i, col))
    halo = lambda col: (lambda i: (jnp.maximum((ni - 1 - i) * hpt - 1, 0), col))
    return pl.pallas_call(
        body,
        grid=(ni,),
        in_specs=[
            pl.BlockSpec((tm, c), rev(1)),
            pl.BlockSpec((tm, c), rev(2)),
            pl.BlockSpec((CONV_HALO, c), halo(1)),
            pl.BlockSpec((CONV_HALO, c), halo(2)),
            pl.BlockSpec((tm, c), rev(1)),
            pl.BlockSpec((tm, c), rev(0)),
            pl.BlockSpec((CONV_HALO, c), const),
            pl.BlockSpec((1, c), const),
            pl.BlockSpec((1, c), const),
            pl.BlockSpec((c, c), const),
        ],
        out_specs=[
            pl.BlockSpec((tm, c), rev(0)),
            pl.BlockSpec((tm, c), rev(0)),
            pl.BlockSpec((CONV_HALO, c), const),
            pl.BlockSpec((1, c), const),
            pl.BlockSpec((1, c), const),
            pl.BlockSpec((1, c), const),
            pl.BlockSpec((c, c), const),
        ],
        out_shape=[
            jax.ShapeDtypeStruct((lp, c), BF16),
            jax.ShapeDtypeStruct((lp, c), BF16),
            jax.ShapeDtypeStruct((CONV_HALO, c), F32),
            jax.ShapeDtypeStruct((1, c), F32),
            jax.ShapeDtypeStruct((1, c), F32),
            jax.ShapeDtypeStruct((1, c), F32),
            jax.ShapeDtypeStruct((c, c), F32),
        ],
        scratch_shapes=[pltpu.VMEM((tm + CONV_HALO, c), F32), pltpu.VMEM((tm + CONV_HALO, c), F32)],
        compiler_params=_cparams(("arbitrary",)),
        name="conv_bwd",
    )(z, z, z, z, dcat, acc_fwd, w_dw, ln_g, ln_b, w_pw)


def _rope(x, cos, sgn_sin):
    return x * cos + pltpu.roll(x, LANE // 2, 1) * sgn_sin


def _rope_t(dy, cos, sgn_sin):
    return dy * cos + pltpu.roll(dy * sgn_sin, LANE // 2, 1)


def _ret_chunk_fwd(q, k, v, s0, cos, sn, tabs):
    heads = range(len(q))
    bf = lambda xs: [x.astype(BF16) for x in xs]
    qr = [_rope(q[h], cos, sn) for h in heads]
    kr = [_rope(k[h], cos, sn) * (LANE ** -0.5) for h in heads]
    qrb, krb, vb, s0b = bf(qr), bf(kr), bf(v), bf(s0)
    amb = bf([_dot_nt(qrb[h], krb[h]) * tabs[h][0] for h in heads])
    qdb = bf([qr[h] * tabs[h][1] for h in heads])
    intra = [_dot(amb[h], vb[h]) for h in heads]
    cross = [_dot(qdb[h], s0b[h]) for h in heads]
    o = [intra[h] + cross[h] for h in heads]
    return qr, kr, qrb, krb, vb, s0b, amb, qdb, o


def _ret_specs(tm, q_blk, tile_of):
    def mk(col):
        return pl.BlockSpec((tm, LANE), lambda i: (tile_of(i), col))
    return [mk(q_blk + j) for j in range(4 * RET_HEADS)]


def _ret_tables(h, dm_ref, qd_ref, kd_ref, cd_ref, gn_ref):
    return dm_ref[h], qd_ref[h], kd_ref[h], cd_ref[h, 0:1, :], gn_ref[:, h * LANE:(h + 1) * LANE]


def _ret_fwd(z, rope_c, rope_s, dm, qd, kd, cd, gn, q_blk):
    lp = z.shape[0]
    tm = _pick_tm(lp, MIX_TM, CHUNK)
    nc = tm // CHUNK
    ni = lp // tm
    nh = RET_HEADS

    def body(*refs):
        q_refs, k_refs, v_refs, g_refs = refs[0:nh], refs[nh:2 * nh], refs[2 * nh:3 * nh], refs[3 * nh:4 * nh]
        c_ref, s_ref, dm_ref, qd_ref, kd_ref, cd_ref, gn_ref, y_ref, st_ref, state = refs[4 * nh:]

        @pl.when(pl.program_id(0) == 0)
        def _():
            state[...] = jnp.zeros_like(state)

        def chunk(c, carry):
            rows = pl.ds(pl.multiple_of(c * CHUNK, CHUNK), CHUNK)
            cos, sn = c_ref[rows, :], s_ref[rows, :]
            heads = range(nh)
            tabs = [_ret_tables(h, dm_ref, qd_ref, kd_ref, cd_ref, gn_ref) for h in heads]
            q = [q_refs[h][rows, :] for h in heads]
            k = [k_refs[h][rows, :] for h in heads]
            v = [v_refs[h][rows, :] for h in heads]
            gg = [g_refs[h][rows, :] for h in heads]
            s0 = [state[h] for h in heads]
            _, kr, _, _, vb, _, _, _, o = _ret_chunk_fwd(q, k, v, s0, cos, sn, tabs)
            kv = [_dot_tn((kr[h] * tabs[h][2]).astype(BF16), vb[h]) for h in heads]
            for h in heads:
                on, _ = _ln_stats(o[h])
                st_ref[c, h, :, :] = s0[h]
                state[h] = s0[h] * tabs[h][3] + kv[h]
                y_ref[rows, h * LANE:(h + 1) * LANE] = (gg[h] * _sigmoid(gg[h]) * (on * tabs[h][4])).astype(BF16)
            return carry

        lax.fori_loop(0, nc, chunk, 0, unroll=2)

    const3 = lambda s: pl.BlockSpec(s, lambda i: (0, 0, 0))
    return pl.pallas_call(
        body,
        grid=(ni,),
        in_specs=_ret_specs(tm, q_blk, lambda i: i) + [
            pl.BlockSpec((tm, LANE), lambda i: (i, 0)),
            pl.BlockSpec((tm, LANE), lambda i: (i, 0)),
            const3((nh, CHUNK, CHUNK)), const3((nh, CHUNK, LANE)), const3((nh, CHUNK, LANE)), const3((nh, 8, LANE)),
            pl.BlockSpec((1, nh * LANE), lambda i: (0, 0)),
        ],
        out_specs=[
            pl.BlockSpec((tm, nh * LANE), lambda i: (i, 0)),
            pl.BlockSpec((nc, nh, LANE, LANE), lambda i: (i, 0, 0, 0)),
        ],
        out_shape=[
            jax.ShapeDtypeStruct((lp, nh * LANE), BF16),
            jax.ShapeDtypeStruct((lp // CHUNK, nh, LANE, LANE), F32),
        ],
        scratch_shapes=[pltpu.VMEM((nh, LANE, LANE), F32)],
        compiler_params=_cparams(("arbitrary",)),
        name="ret_fwd",
    )(*([z] * (4 * nh)), rope_c, rope_s, dm, qd, kd, cd, gn)


def _ret_bwd(z, dcat, states, rope_c, rope_s, dm, qd, kd, cd, gn, q_blk, dy_blk):
    lp = z.shape[0]
    tm = _pick_tm(lp, MIX_TM, CHUNK)
    nc = tm // CHUNK
    ni = lp // tm

    nh = RET_HEADS

    def body(*refs):
        q_refs, k_refs, v_refs, g_refs = refs[0:nh], refs[nh:2 * nh], refs[2 * nh:3 * nh], refs[3 * nh:4 * nh]
        (dy_ref, st_ref, c_ref, s_ref, dm_ref, qd_ref, kd_ref, cd_ref, gn_ref,
         dq_ref, dk_ref, dv_ref, dgt_ref, dgn_ref, dstate) = refs[4 * nh:]
        i = pl.program_id(0)
        t = ni - 1 - i

        @pl.when(i == 0)
        def _():
            dstate[...] = jnp.zeros_like(dstate)
            dgn_ref[...] = jnp.zeros_like(dgn_ref)

        def chunk(cc, carry):
            c = nc - 1 - cc
            rows = pl.ds(pl.multiple_of(c * CHUNK, CHUNK), CHUNK)
            cos, sn = c_ref[rows, :], s_ref[rows, :]
            keep = (lax.broadcasted_iota(jnp.int32, (CHUNK, 1), 0) + (t * tm + c * CHUNK)) >= PAD_ROWS
            heads = range(nh)
            lanes = [slice(h * LANE, (h + 1) * LANE) for h in heads]
            tabs = [_ret_tables(h, dm_ref, qd_ref, kd_ref, cd_ref, gn_ref) for h in heads]
            q = [q_refs[h][rows, :] for h in heads]
            k = [k_refs[h][rows, :] for h in heads]
            v = [v_refs[h][rows, :] for h in heads]
            gg = [g_refs[h][rows, :] for h in heads]
            dyv = [dy_ref[rows, lanes[h]] for h in heads]
            s0 = [st_ref[c, h, :, :] for h in heads]
            ds1 = [dstate[h] for h in heads]
            dgn = [dgn_ref[:, lanes[h]] for h in heads]
            qr, kr, qrb, krb, vb, s0b, amb, qdb, o = _ret_chunk_fwd(q, k, v, s0, cos, sn, tabs)
            stats = [_ln_stats(o[h]) for h in heads]
            sg = [_sigmoid(gg[h]) for h in heads]
            sl = [gg[h] * sg[h] for h in heads]
            dgt = [dyv[h] * (stats[h][0] * tabs[h][4]) * (sg[h] * (1.0 + gg[h] * (1.0 - sg[h]))) for h in heads]
            dob = [_ln_bwd(dyv[h] * sl[h], stats[h][0], stats[h][1], tabs[h][4]).astype(BF16) for h in heads]
            ds1b = [ds1[h].astype(BF16) for h in heads]
            kdb = [(kr[h] * tabs[h][2]).astype(BF16) for h in heads]
            da = [(_dot_nt(dob[h], vb[h]) * tabs[h][0]).astype(BF16) for h in heads]
            dv_intra = [_dot_tn(amb[h], dob[h]) for h in heads]
            dv_cross = [_dot(kdb[h], ds1b[h]) for h in heads]
            dq_cross = [_dot_nt(dob[h], s0b[h]) for h in heads]
            dk_cross = [_dot_nt(vb[h], ds1b[h]) for h in heads]
            ds_inc = [_dot_tn(qdb[h], dob[h]) for h in heads]
            dq_intra = [_dot(da[h], krb[h]) for h in heads]
            dk_intra = [_dot_tn(da[h], qrb[h]) for h in heads]
            for h in heads:
                dqr = dq_intra[h] + dq_cross[h] * tabs[h][1]
                dkr = dk_intra[h] + dk_cross[h] * tabs[h][2]
                dq = _rope_t(dqr, cos, sn)
                dk = _rope_t(dkr * (LANE ** -0.5), cos, sn)
                dq_ref[rows, lanes[h]] = jnp.where(keep, dq, 0.0).astype(BF16)
                dk_ref[rows, lanes[h]] = jnp.where(keep, dk, 0.0).astype(BF16)
                dv_ref[rows, lanes[h]] = jnp.where(keep, dv_intra[h] + dv_cross[h], 0.0).astype(BF16)
                dgt_ref[rows, lanes[h]] = jnp.where(keep, dgt[h], 0.0).astype(BF16)
                dstate[h] = ds1[h] * tabs[h][3] + ds_inc[h]
                dgn_ref[:, lanes[h]] = dgn[h] + _colsum(dyv[h] * sl[h] * stats[h][0])
            return carry

        lax.fori_loop(0, nc, chunk, 0, unroll=2)

    rev = lambda i: ni - 1 - i
    const3 = lambda s: pl.BlockSpec(s, lambda i: (0, 0, 0))
    out_blk = pl.BlockSpec((tm, nh * LANE), lambda i: (rev(i), 0))
    out_sds = jax.ShapeDtypeStruct((lp, nh * LANE), BF16)
    return pl.pallas_call(
        body,
        grid=(ni,),
        in_specs=_ret_specs(tm, q_blk, rev) + [
            pl.BlockSpec((tm, nh * LANE), lambda i: (rev(i), dy_blk // nh)),
            pl.BlockSpec((nc, nh, LANE, LANE), lambda i: (rev(i), 0, 0, 0)),
            pl.BlockSpec((tm, LANE), lambda i: (rev(i), 0)),
            pl.BlockSpec((tm, LANE), lambda i: (rev(i), 0)),
            const3((nh, CHUNK, CHUNK)), const3((nh, CHUNK, LANE)), const3((nh, CHUNK, LANE)), const3((nh, 8, LANE)),
            pl.BlockSpec((1, nh * LANE), lambda i: (0, 0)),
        ],
        out_specs=[out_blk, out_blk, out_blk, out_blk, pl.BlockSpec((1, nh * LANE), lambda i: (0, 0))],
        out_shape=[out_sds, out_sds, out_sds, out_sds, jax.ShapeDtypeStruct((1, nh * LANE), F32)],
        scratch_shapes=[pltpu.VMEM((nh, LANE, LANE), F32)],
        compiler_params=_cparams(("arbitrary",)),
        name="ret_bwd",
    )(*([z] * (4 * nh)), dcat, states, rope_c, rope_s, dm, qd, kd, cd, gn)


def _adamw(parts_list, w, m, v):
    nl, r, c = w.shape
    assert len(parts_list) == nl
    tr = _pick_tm(r, max(8, (1 << 17) // c), 8)
    nr = r // tr

    def body(*refs):
        p_refs = refs[:nl]
        w_ref, m_ref, v_ref, g_ref, d_ref, nm_ref, nv_ref = refs[nl:]
        layer = pl.program_id(0)
        for k in range(nl):
            @pl.when(layer == k)
            def _(k=k):
                g = p_refs[k][0].astype(F32)
                for j in range(1, N_DEV):
                    g = g + p_refs[k][j].astype(F32)
                m1 = ADAM_B1 * m_ref[...] + (1.0 - ADAM_B1) * g
                v1 = ADAM_B2 * v_ref[...] + (1.0 - ADAM_B2) * (g * g)
                m_hat = m1 / (1.0 - ADAM_B1 ** ADAM_STEP)
                v_hat = v1 / (1.0 - ADAM_B2 ** ADAM_STEP)
                g_ref[...] = g
                d_ref[...] = -ADAM_LR * (m_hat / (jnp.sqrt(v_hat) + ADAM_EPS) + ADAM_WD * w_ref[...])
                nm_ref[...] = m1
                nv_ref[...] = v1

    def parts_spec(k):
        return pl.BlockSpec((N_DEV, tr, c), lambda l, i: (0, jnp.where(l == k, i, jnp.where(l < k, 0, nr - 1)), 0))

    blk = pl.BlockSpec((None, tr, c), lambda l, i: (l, i, 0))
    sds = jax.ShapeDtypeStruct((nl, r, c), F32)
    return pl.pallas_call(
        body,
        grid=(nl, nr),
        in_specs=[parts_spec(k) for k in range(nl)] + [blk, blk, blk],
        out_specs=[blk, blk, blk, blk],
        out_shape=[sds, sds, sds, sds],
        compiler_params=_cparams(("arbitrary", "arbitrary")),
        name="adamw",
    )(*parts_list, w, m, v)


def _flip(v, bit):
    return 1 - v if bit else v


def _exchange(arrs, scatter):
    n = len(arrs)
    shapes = [tuple(a.shape[1:] if scatter else a.shape) for a in arrs]

    def body(*refs):
        x_refs, o_refs = refs[:n], refs[n:2 * n]
        send_sems, recv_sems, local_sems = refs[2 * n:]
        mx, my, mc = lax.axis_index("x"), lax.axis_index("y"), lax.axis_index("c")
        me = 4 * mx + 2 * my + mc

        def peer_of(k):
            return (_flip(mx, (k >> 2) & 1), _flip(my, (k >> 1) & 1), _flip(mc, k & 1))

        def copy(a, k):
            peer = peer_of(k)
            src = x_refs[a].at[4 * peer[0] + 2 * peer[1] + peer[2]] if scatter else x_refs[a]
            return pltpu.make_async_remote_copy(
                src_ref=src, dst_ref=o_refs[a].at[me], send_sem=send_sems.at[a, k - 1],
                recv_sem=recv_sems.at[a, k - 1], device_id=peer, device_id_type=pl.DeviceIdType.MESH)

        def arrival(a, k):
            peer = peer_of(k)
            slot = o_refs[a].at[4 * peer[0] + 2 * peer[1] + peer[2]]
            return pltpu.make_async_remote_copy(
                src_ref=slot, dst_ref=slot, send_sem=send_sems.at[a, k - 1], recv_sem=recv_sems.at[a, k - 1],
                device_id=peer, device_id_type=pl.DeviceIdType.MESH)

        locals_ = [pltpu.make_async_copy(x_refs[a].at[me] if scatter else x_refs[a], o_refs[a].at[me],
                                         local_sems.at[a]) for a in range(n)]
        sends = [copy(a, k) for a in range(n) for k in range(1, N_DEV)]
        for cp in locals_ + sends:
            cp.start()
        for a in range(n):
            for k in range(1, N_DEV):
                arrival(a, k).wait_recv()
        for cp in sends:
            cp.wait_send()
        for cp in locals_:
            cp.wait()

    hbm = pl.BlockSpec(memory_space=pltpu.HBM)
    return pl.pallas_call(
        body,
        in_specs=[hbm] * n,
        out_specs=[hbm] * n,
        out_shape=[jax.ShapeDtypeStruct((N_DEV,) + s, a.dtype) for s, a in zip(shapes, arrs)],
        scratch_shapes=[
            pltpu.SemaphoreType.DMA((n, N_DEV - 1)),
            pltpu.SemaphoreType.DMA((n, N_DEV - 1)),
            pltpu.SemaphoreType.DMA((n,)),
        ],
        name="reduce_scatter_parts" if scatter else "all_gather",
    )(*arrs)


def _exchange_descriptors(x_refs, land_refs, send_sems, recv_sems, scatter):
    mx, my, mc = lax.axis_index("x"), lax.axis_index("y"), lax.axis_index("c")
    me = 4 * mx + 2 * my + mc
    sends, arrivals = [], []
    for a in range(len(x_refs)):
        for k in range(1, N_DEV):
            peer = (_flip(mx, (k >> 2) & 1), _flip(my, (k >> 1) & 1), _flip(mc, k & 1))
            slot = 4 * peer[0] + 2 * peer[1] + peer[2]
            si = a * (N_DEV - 1) + k - 1
            sems = dict(send_sem=send_sems.at[si], recv_sem=recv_sems.at[si],
                        device_id=peer, device_id_type=pl.DeviceIdType.MESH)
            sends.append(pltpu.make_async_remote_copy(
                src_ref=x_refs[a].at[slot] if scatter else x_refs[a], dst_ref=land_refs[a].at[me], **sems))
            arrivals.append(pltpu.make_async_remote_copy(
                src_ref=land_refs[a].at[slot], dst_ref=land_refs[a].at[slot], **sems))
    return sends, arrivals


def _own_slot_copies(x_refs, land_refs, local_sems, scatter):
    me = 4 * lax.axis_index("x") + 2 * lax.axis_index("y") + lax.axis_index("c")
    return [pltpu.make_async_copy(x_refs[a].at[me] if scatter else x_refs[a], land_refs[a].at[me], local_sems.at[a])
            for a in range(len(x_refs))]


def _exchange_start(arrs, scatter, name, after=None):
    n = len(arrs)
    shapes = [tuple(a.shape[1:] if scatter else a.shape) for a in arrs]
    lands = [lax.empty((N_DEV,) + s, a.dtype) for s, a in zip(shapes, arrs)]
    extra = [] if after is None else [after]

    def body(*refs):
        x_refs, land_refs = refs[:n], refs[n:2 * n]
        send_sems, recv_sems, local_sems = refs[2 * n + len(extra):2 * n + len(extra) + 3]
        token = refs[-1]
        sends, _ = _exchange_descriptors(x_refs, land_refs, send_sems, recv_sems, scatter)
        for cp in sends + _own_slot_copies(x_refs, land_refs, local_sems, scatter):
            cp.start()
        token[...] = jnp.zeros_like(token)

    hbm = pl.BlockSpec(memory_space=pltpu.HBM)
    sem = pl.BlockSpec(memory_space=pltpu.SEMAPHORE)
    sem_type = pltpu.SemaphoreType.DMA((n * (N_DEV - 1),))
    operands = [pltpu.with_memory_space_constraint(a, pltpu.HBM) for a in list(arrs) + lands]
    out = pl.pallas_call(
        body,
        in_specs=[hbm] * (2 * n) + [pl.BlockSpec(memory_space=pl.ANY)] * len(extra),
        out_specs=[sem, sem, sem] + [hbm] * (2 * n) + [pl.BlockSpec(memory_space=pltpu.VMEM)],
        out_shape=[sem_type, sem_type, pltpu.SemaphoreType.DMA((n,))] + [pltpu.HBM(a.shape, a.dtype) for a in operands]
        + [jax.ShapeDtypeStruct((8, LANE), F32)],
        input_output_aliases={i: 3 + i for i in range(2 * n)},
        compiler_params=pltpu.CompilerParams(has_side_effects=pltpu.SideEffectType.DATAFLOW_SIDE_EFFECTING),
        name=name,
    )(*operands, *extra)
    return (out[0:3], list(out[3:3 + n]), list(out[3 + n:3 + 2 * n]), scatter), out[-1]


def _exchange_wait(handle, after, name):
    sems, x_thru, land_thru, scatter = handle
    n = len(x_thru)

    def body(*refs):
        x_refs, land_refs = refs[:n], refs[n:2 * n]
        send_sems, recv_sems, local_sems = refs[2 * n:2 * n + 3]
        sends, arrivals = _exchange_descriptors(x_refs, land_refs, send_sems, recv_sems, scatter)
        for cp in sends:
            cp.wait_send()
        for cp in arrivals:
            cp.wait_recv()
        for cp in _own_slot_copies(x_refs, land_refs, local_sems, scatter):
            cp.wait()

    hbm = pl.BlockSpec(memory_space=pltpu.HBM)
    sem = pl.BlockSpec(memory_space=pltpu.SEMAPHORE)
    out = pl.pallas_call(
        body,
        in_specs=[hbm] * (2 * n) + [sem, sem, sem, pl.BlockSpec(memory_space=pl.ANY)],
        out_specs=[hbm] * (2 * n),
        out_shape=[pltpu.HBM(a.shape, a.dtype) for a in x_thru + land_thru],
        input_output_aliases={i: i for i in range(2 * n)},
        compiler_params=pltpu.CompilerParams(has_side_effects=pltpu.SideEffectType.DATAFLOW_SIDE_EFFECTING),
        name=name,
    )(*x_thru, *land_thru, *sems, after)
    return list(out[n:])


PACK_ALIGN = 2048


def _padded(n):
    return -(-n // PACK_ALIGN) * PACK_ALIGN


def _pad_to(a, axis, size):
    pad = [(0, 0)] * a.ndim
    pad[axis] = (0, size - a.shape[axis])
    return jnp.pad(a, pad)


def _pack(arrs, lead=0):
    flat = []
    for a in arrs:
        v = a.reshape(a.shape[:lead] + (-1,))
        flat.append(_pad_to(v, lead, _padded(v.shape[lead])))
    out = jnp.concatenate(flat, axis=lead)
    return out.reshape(out.shape[:lead] + (-1, LANE))


def _unpack(slab, shapes, lead=0):
    flat = slab.reshape(slab.shape[:lead] + (-1,))
    out, off = [], 0
    for s in shapes:
        n = math.prod(s)
        out.append(flat[..., off:off + n].reshape(slab.shape[:lead] + tuple(s)))
        off += _padded(n)
    return out


def _unshard(parts, ax):
    return jnp.concatenate([parts[j] for j in range(N_DEV)], axis=ax)


def _to_shards(full, ax):
    n = full.shape[ax] // N_DEV
    return jnp.stack([lax.slice_in_dim(full, j * n, (j + 1) * n, axis=ax) for j in range(N_DEV)])


def _pad_halves(w, hp):
    h = w.shape[-1] // 2
    zeros = jnp.zeros(w.shape[:-1] + (hp - h,), w.dtype)
    return jnp.concatenate([w[..., :h], zeros, w[..., h:], zeros], axis=-1)


def _unpad_halves(w, h):
    hp = w.shape[-1] // 2
    return jnp.concatenate([w[..., :h], w[..., hp:hp + h]], axis=-1)


SMALL_SHARDED = (("meta", 1), ("conv_dw", 2), ("ln_g", 2), ("ln_b", 2))
MATMUL_WEIGHTS = ("ffn1_w13", "ffn1_w2", "w_in", "conv_pw", "w_out", "ffn2_w13", "ffn2_w2")
REPLICATED = ("ln_in_g", "ln_in_b", "pool_w", "pool_scale", "conv_db", "conv_ln_g", "conv_ln_b", "ret_gn_g")
WEIGHT_ORDER = ("meta", "ln_in_g", "ln_in_b", "ffn1_w13", "ffn1_w2", "w_in", "pool_w", "pool_scale", "conv_dw",
                "conv_db", "conv_ln_g", "conv_ln_b", "conv_pw", "ret_gn_g", "w_out", "ffn2_w13", "ffn2_w2",
                "ln_g", "ln_b")


def _retention_tables(lp, heads):
    pos = jnp.arange(lp, dtype=F32) - PAD_ROWS
    inv_freq = ROPE_BASE ** (-jnp.arange(0, LANE, 2, dtype=F32) / LANE)
    ang = pos[:, None] * inv_freq[None, :]
    cos, sin = jnp.cos(ang), jnp.sin(ang)
    rope_c = jnp.concatenate([cos, cos], axis=1)
    rope_s = jnp.concatenate([-sin, sin], axis=1)
    log_gamma = jnp.log(1.0 - 2.0 ** (-5.0 - jnp.arange(heads, dtype=F32)))
    i = jnp.arange(CHUNK, dtype=F32)
    dm = jnp.exp(log_gamma[:, None, None] * jnp.abs(i[:, None] - i[None, :]))
    lanes = lambda t: jnp.broadcast_to(t[:, :, None], t.shape + (LANE,))
    qd = lanes(jnp.exp(log_gamma[:, None] * (i + 1.0)))
    kd = lanes(jnp.exp(log_gamma[:, None] * (CHUNK - 1.0 - i)))
    cd = lanes(jnp.broadcast_to(jnp.exp(log_gamma * CHUNK)[:, None], (heads, 8)))
    return rope_c, rope_s, dm, qd, kd, cd


def _block_diag(w):
    g, n, _ = w.shape
    rows = []
    for i in range(g):
        rows.append(jnp.concatenate([w[i] if j == i else jnp.zeros((n, n), w.dtype) for j in range(g)], axis=1))
    return jnp.concatenate(rows, axis=0)


def kernel(x, meta, ln_in_g, ln_in_b, ffn1_w13, ffn1_w2, w_in, pool_w, pool_scale, conv_dw, conv_db, conv_ln_g, conv_ln_b, conv_pw, ret_gn_g, w_out, ffn2_w13, ffn2_w2, ln_g, ln_b, loss_target, m_meta, m_ln_in_g, m_ln_in_b, m_ffn1_w13, m_ffn1_w2, m_w_in, m_pool_w, m_pool_scale, m_conv_dw, m_conv_db, m_conv_ln_g, m_conv_ln_b, m_conv_pw, m_ret_gn_g, m_w_out, m_ffn2_w13, m_ffn2_w2, m_ln_g, m_ln_b, v_meta, v_ln_in_g, v_ln_in_b, v_ffn1_w13, v_ffn1_w2, v_w_in, v_pool_w, v_pool_scale, v_conv_dw, v_conv_db, v_conv_ln_g, v_conv_ln_b, v_conv_pw, v_ret_gn_g, v_w_out, v_ffn2_w13, v_ffn2_w2, v_ln_g, v_ln_b):
    local = dict(meta=meta, ln_in_g=ln_in_g, ln_in_b=ln_in_b, ffn1_w13=ffn1_w13, ffn1_w2=ffn1_w2, w_in=w_in,
                 pool_w=pool_w, pool_scale=pool_scale, conv_dw=conv_dw, conv_db=conv_db, conv_ln_g=conv_ln_g,
                 conv_ln_b=conv_ln_b, conv_pw=conv_pw, ret_gn_g=ret_gn_g, w_out=w_out, ffn2_w13=ffn2_w13,
                 ffn2_w2=ffn2_w2, ln_g=ln_g, ln_b=ln_b)
    mom1 = dict(meta=m_meta, ln_in_g=m_ln_in_g, ln_in_b=m_ln_in_b, ffn1_w13=m_ffn1_w13, ffn1_w2=m_ffn1_w2,
                w_in=m_w_in, pool_w=m_pool_w, pool_scale=m_pool_scale, conv_dw=m_conv_dw, conv_db=m_conv_db,
                conv_ln_g=m_conv_ln_g, conv_ln_b=m_conv_ln_b, conv_pw=m_conv_pw, ret_gn_g=m_ret_gn_g,
                w_out=m_w_out, ffn2_w13=m_ffn2_w13, ffn2_w2=m_ffn2_w2, ln_g=m_ln_g, ln_b=m_ln_b)
    mom2 = dict(meta=v_meta, ln_in_g=v_ln_in_g, ln_in_b=v_ln_in_b, ffn1_w13=v_ffn1_w13, ffn1_w2=v_ffn1_w2,
                w_in=v_w_in, pool_w=v_pool_w, pool_scale=v_pool_scale, conv_dw=v_conv_dw, conv_db=v_conv_db,
                conv_ln_g=v_conv_ln_g, conv_ln_b=v_conv_ln_b, conv_pw=v_conv_pw, ret_gn_g=v_ret_gn_g,
                w_out=v_w_out, ffn2_w13=v_ffn2_w13, ffn2_w2=v_ffn2_w2, ln_g=v_ln_g, ln_b=v_ln_b)

    depth = ffn1_w13.shape[0]
    alpha = (2.0 * depth) ** 0.25
    seq, d = x.shape[1], x.shape[2]
    lp = FRONT + seq
    h_loc = ffn1_w2.shape[1]
    hp = -(-h_loc // LANE) * LANE
    c_pool = pool_scale.shape[1]
    c_conv = conv_db.shape[1]
    q_blk = (c_pool + 2 * c_conv) // LANE
    dy_blk = (c_pool + c_conv) // LANE
    heads = ret_gn_g.shape[1] // LANE
    assert meta.shape[0] == N_META and heads == RET_HEADS and conv_dw.shape[1] == CONV_WIDTH
    assert ffn1_w13.shape[2] == 2 * h_loc

    def to_wire(name, w):
        if name in ("ffn1_w13", "ffn2_w13"):
            return _pad_halves(w, hp)
        if name in ("ffn1_w2", "ffn2_w2"):
            return _pad_to(w, w.ndim - 2, hp)
        return w

    def from_wire(name, w):
        if name in ("ffn1_w13", "ffn2_w13"):
            return _unpad_halves(w, h_loc)
        if name in ("ffn1_w2", "ffn2_w2"):
            return w[..., :h_loc, :]
        return w

    small_names = [n for n, _ in SMALL_SHARDED]
    small_ax = dict(SMALL_SHARDED)
    n_mm = len(MATMUL_WEIGHTS)
    first = ("ffn1_w13", "ffn1_w2")
    wire = lambda l, names: [to_wire(n, local[n][l]).astype(BF16) for n in names]
    rest = [n for n in MATMUL_WEIGHTS if n not in first]
    groups = [[_pack([local[n] for n in small_names])], wire(0, first), wire(0, rest)]
    groups += [wire(l, MATMUL_WEIGHTS) for l in range(1, depth)]
    handles, token = [], None
    for gi, arrs in enumerate(groups):
        handle, token = _exchange_start(arrs, False, f"gather_start_{gi}", after=token)
        handles.append(handle)
    got = _exchange_wait(handles[0], token, "gather_wait_0")
    small_parts = _unpack(got[0], [local[n].shape for n in small_names], 1)
    small_full = {n: _unshard(p, small_ax[n]) for n, p in zip(small_names, small_parts)}

    row = lambda v: v.reshape(1, -1)
    ln_params = [(row(ln_in_g), row(ln_in_b))]
    for l in range(depth):
        ln_params += [(row(small_full["ln_g"][l, j]), row(small_full["ln_b"][l, j])) for j in range(3)]
    rope_c, rope_s, dm, qd, kd, cd = _retention_tables(lp, heads)

    def layer_params(l, gw):
        return dict(
            ffn1=(gw["ffn1_w13"], gw["ffn1_w2"]),
            ffn2=(gw["ffn2_w13"], gw["ffn2_w2"]),
            w_in=_unshard(gw["w_in"], 1),
            w_out=gw["w_out"].reshape(-1, d),
            pool=(_block_diag(pool_w[l]).astype(BF16), row(pool_scale[l])),
            conv=(_pad_to(small_full["conv_dw"][l], 0, CONV_HALO), row(conv_db[l]), row(conv_ln_g[l]),
                  row(conv_ln_b[l]), gw["conv_pw"].reshape(c_conv, c_conv)),
            gn=row(ret_gn_g[l]),
        )

    r = [jnp.concatenate([jnp.zeros((PAD_ROWS, d), F32), small_full["meta"], x[0]], axis=0)]
    saved, layers = [], []
    for l in range(depth):
        k = 3 * l
        if l == 0:
            gw_first = dict(zip(first, _exchange_wait(handles[1], r[0], "gather_wait_1")))
            r1, a1, u1 = _ffn_fwd(r[k], *ln_params[k], gw_first["ffn1_w13"], gw_first["ffn1_w2"], alpha)
            gw = dict(gw_first, **dict(zip(rest, _exchange_wait(handles[2], r1, "gather_wait_2"))))
            p = layer_params(l, gw)
        else:
            gw = dict(zip(MATMUL_WEIGHTS, _exchange_wait(handles[l + 2], r[k], f"gather_wait_{l + 2}")))
            p = layer_params(l, gw)
            r1, a1, u1 = _ffn_fwd(r[k], *ln_params[k], *p["ffn1"], alpha)
        layers.append(p)
        z = _proj_fwd(r1, *ln_params[k + 1], p["w_in"])
        y_pool = _pool_fwd(z, *p["pool"])
        y_conv, conv_acc = _conv_fwd(z, *p["conv"])
        y_ret, states = _ret_fwd(z, rope_c, rope_s, dm, qd, kd, cd, p["gn"], q_blk)
        cat = jnp.concatenate([y_pool, y_conv, y_ret], axis=1)
        r2 = _out_fwd(r1, *ln_params[k + 1], cat, p["w_out"], alpha)
        r3, a2, u2 = _ffn_fwd(r2, *ln_params[k + 2], *p["ffn2"], alpha)
        r += [r1, r2, r3]
        saved.append((a1, u1, z, states, cat, a2, u2, conv_acc))

    target = jnp.concatenate([jnp.zeros((FRONT, d), F32), loss_target[0]], axis=0)
    dr, dg, db, loss_cols = _loss_bwd(r[-1], target, *ln_params[-1])
    loss = lax.psum(0.5 * jnp.sum(loss_cols) / d, MESH_AXES)
    ln_grads = {3 * depth: (dg, db)}
    w13_slot = lambda j: (j % 2) * (N_DEV // 2) + j // 2
    g_parts = [None] * depth
    g_rep = [None] * depth
    scatter_jobs = []

    def start_scatter(l, gp, names, extra=()):
        parts = [gp[n].astype(WIRE) for n in names] + list(extra)
        handle, token = _exchange_start(parts, True, f"scatter_start_{len(scatter_jobs)}")
        scatter_jobs.append((handle, l, names))
        return token

    def ffn_grads(dr_out, r_in, a, u, ln_p, wts, scatter_w13_of=None):
        dr_in, hb, s, dz, dg, db = _ffn_bwd(dr_out, r_in, a, u, *ln_p, *wts, alpha)
        g13 = _mm_tn(hb, dz, bn=2 * hp, slot_of=w13_slot)
        if scatter_w13_of is not None:
            start_scatter(scatter_w13_of, {"ffn1_w13": g13}, ("ffn1_w13",))
        g2 = _mm_tn(s, dr_out, 0.5).reshape(N_DEV, hp, d)
        return dr_in, (dg, db), g13, g2

    for l in reversed(range(depth)):
        p = layers[l]
        k = 3 * l
        a1, u1, z, states, cat, a2, u2, conv_acc = saved[l]
        gp, gr = {}, {}
        dr, ln_grads[k + 2], gp["ffn2_w13"], gp["ffn2_w2"] = ffn_grads(dr, r[k + 2], a2, u2, ln_params[k + 2], p["ffn2"])
        zero = start_scatter(l, gp, ("ffn2_w13", "ffn2_w2"))[0:1, 0:1]

        dr2 = dr
        dcat = _mm_nt(dr2, p["w_out"])
        gp["w_out"] = _mm_tn(cat, dr2).reshape(N_DEV, -1, d)
        dxp, dwbd, gr["pool_scale"] = _pool_bwd(z, dcat, p["pool"][0], p["pool"][1] + zero)
        dca, dcg, ddw, gr["conv_db"], gr["conv_ln_g"], gr["conv_ln_b"], dpw = _conv_bwd(
            z, dcat, conv_acc, p["conv"][0], p["conv"][2], p["conv"][3], p["conv"][4])
        dq, dk, dv, dgt, gr["ret_gn_g"] = _ret_bwd(z, dcat, states, rope_c, rope_s, dm, qd, kd, cd, p["gn"],
                                                  q_blk, dy_blk)
        dz = jnp.concatenate([dxp, dca, dcg, dq, dk, dv, dgt], axis=1)
        dr, hb, dg, db = _in_bwd(dr2, r[k + 1], dz, *ln_params[k + 1], p["w_in"], alpha)
        ln_grads[k + 1] = (dg, db)
        gp["w_in"] = _to_shards(_mm_tn(hb, dz), 1)
        gp["conv_pw"] = dpw.reshape(N_DEV, -1, c_conv)
        n_grp, grp = pool_w.shape[1], pool_w.shape[2]
        gr["pool_w"] = jnp.stack([dwbd[i * grp:(i + 1) * grp, i * grp:(i + 1) * grp] for i in range(n_grp)])
        gr["conv_dw"] = ddw[:CONV_WIDTH]

        zero = start_scatter(l, gp, ("w_in", "conv_pw", "w_out"))[0:1, 0:1]
        ln_g_in, ln_b_in = ln_params[k]
        dr, ln_grads[k], gp["ffn1_w13"], gp["ffn1_w2"] = ffn_grads(
            dr, r[k], a1, u1, (ln_g_in + zero, ln_b_in), p["ffn1"], scatter_w13_of=0 if l == 0 else None)
        g_parts[l], g_rep[l] = gp, gr
        if l > 0:
            zero = start_scatter(l, gp, ("ffn1_w13", "ffn1_w2"))[0:1, 0:1]
            g_prev, b_prev = ln_params[k - 1]
            ln_params[k - 1] = (g_prev + zero, b_prev)

    grad_x = dr[FRONT:][None]
    local_shape = lambda name: ((CONV_WIDTH, c_conv) if name == "conv_dw" else local[name].shape[1:])
    stack_layers = lambda name: jnp.stack([g_rep[l][name].reshape(local_shape(name)) for l in range(depth)])
    small_grad = dict(
        meta=dr[PAD_ROWS:FRONT],
        conv_dw=stack_layers("conv_dw"),
        ln_g=jnp.stack([jnp.stack([ln_grads[3 * l + j + 1][0][0] for j in range(3)]) for l in range(depth)]),
        ln_b=jnp.stack([jnp.stack([ln_grads[3 * l + j + 1][1][0] for j in range(3)]) for l in range(depth)]),
    )
    rep_grad = {n: stack_layers(n) for n in REPLICATED if n not in ("ln_in_g", "ln_in_b")}
    rep_grad["ln_in_g"], rep_grad["ln_in_b"] = ln_grads[0][0][0], ln_grads[0][1][0]

    small_pack8 = _pack([_to_shards(small_grad[n], small_ax[n]) for n in small_names], 1)
    after = start_scatter(0, g_parts[0], ("ffn1_w2",), extra=[small_pack8])
    rep_parts = _exchange([_pack([rep_grad[n] for n in REPLICATED])], False)[0]
    scattered, outs_by_name = {}, {}

    def update(names):
        nonlocal after
        for n in names:
            res = _adamw([scattered[l, n] for l in range(depth)], *[to_wire(n, src[n]) for src in (local, mom1, mom2)])
            outs_by_name[n] = [from_wire(n, t) for t in res]
            after = res[0]

    for ji, (handle, l, names) in enumerate(scatter_jobs):
        if ji == len(scatter_jobs) - 1:
            update([n for n in MATMUL_WEIGHTS if n not in names])
        got = _exchange_wait(handle, after, f"scatter_wait_{ji}")
        scattered.update({(l, n): t for n, t in zip(names, got)})
        after = got[0]
    small_scattered = got[-1]
    update(names)
    for names, parts in ((small_names, small_scattered), (REPLICATED, rep_parts)):
        shapes = [local[n].shape for n in names]
        res = _adamw([parts], *[_pack([src[n] for n in names])[None] for src in (local, mom1, mom2)])
        for n, vals in zip(names, zip(*[_unpack(t[0], shapes) for t in res])):
            outs_by_name[n] = list(vals)

    outs = [loss, grad_x]
    for kind in range(4):
        outs += [outs_by_name[n][kind] for n in WEIGHT_ORDER]
    return tuple(outs)
```

```python
import math

import jax
import jax.numpy as jnp
from jax import lax
from jax.experimental import pallas as pl
from jax.experimental.pallas import tpu as pltpu

F32 = jnp.float32
BF16 = jnp.bfloat16
WIRE = jnp.bfloat16

N_DEV = 8
MESH_AXES = ("x", "y", "c")
CHUNK = 64
N_META = 16
PAD_ROWS = 240
FRONT = PAD_ROWS + N_META
LN_EPS = 1e-5
LANE = 128
POOL_WINDOWS = (2, 4, 8, 16)
CONV_WIDTH = 31
CONV_HALO = 32
POOL_HALO = 16
RET_HEADS = 4
ROPE_BASE = 10000.0

ADAM_LR = 0.001
ADAM_B1 = 0.9
ADAM_B2 = 0.999
ADAM_EPS = 1e-08
ADAM_WD = 0.01
ADAM_STEP = 10

VMEM_LIMIT = 56 * 1024 * 1024
DENSE_TM_FWD = 768
DENSE_TM_BWD = 528
MIX_TM = 768
FFN_ROW_PARTS = 2
MM_TN_TK = 1408


def _cparams(sem):
    return pltpu.CompilerParams(dimension_semantics=sem, vmem_limit_bytes=VMEM_LIMIT)


def _pick_tm(lp, target, mult=16):
    best = None
    for t in range(mult, min(lp, target) + 1, mult):
        if lp % t == 0:
            best = t
    assert best is not None, (lp, target, mult)
    return best


def _row_parts(tm, n):
    step = -(-tm // (16 * n)) * 16
    return [slice(r0, min(r0 + step, tm)) for r0 in range(0, tm, step)]


def _dot(a, b):
    return jnp.dot(a, b, preferred_element_type=F32)


def _dot_nt(a, b):
    return lax.dot_general(a, b, (((1,), (1,)), ((), ())), preferred_element_type=F32)


def _dot_tn(a, b):
    return lax.dot_general(a, b, (((0,), (0,)), ((), ())), preferred_element_type=F32)


def _sigmoid(x):
    return 0.5 * jnp.tanh(0.5 * x) + 0.5


def _ln_stats(r):
    mu = jnp.mean(r, axis=-1, keepdims=True)
    xc = r - mu
    var = jnp.mean(xc * xc, axis=-1, keepdims=True)
    rstd = lax.rsqrt(var + LN_EPS)
    return xc * rstd, rstd


def _ln_bwd(dh, xhat, rstd, g):
    dxh = dh * g
    m1 = jnp.mean(dxh, axis=-1, keepdims=True)
    m2 = jnp.mean(dxh * xhat, axis=-1, keepdims=True)
    return rstd * (dxh - m1 - xhat * m2)


def _colsum(x):
    return jnp.sum(x, axis=0, keepdims=True)


def _row_ids(tile, tm):
    return lax.broadcasted_iota(jnp.int32, (tm, 1), 0) + tile * tm


def _ffn_fwd(r_prev, g, b, w13g, w2g, alpha):
    lp, d = r_prev.shape
    nf = N_DEV // 2
    fc = w13g.shape[2]
    w2c = w2g.reshape(nf, fc, d)
    tm = _pick_tm(lp, DENSE_TM_FWD)

    def body(r_ref, g_ref, b_ref, w1_ref, w3_ref, w2_ref, out_ref, a_ref, u_ref, hb, acc):
        f = pl.program_id(1)

        @pl.when(f == 0)
        def _():
            xhat, _ = _ln_stats(r_ref[...])
            hb[...] = (xhat * g_ref[...] + b_ref[...]).astype(BF16)
            acc[...] = jnp.zeros_like(acc)

        h = hb[...]
        a = _dot(h, w1_ref[...])
        u = _dot(h, w3_ref[...])
        a_ref[...] = a.astype(BF16)
        u_ref[...] = u.astype(BF16)
        s = a * _sigmoid(a) * u
        acc[...] += _dot(s.astype(BF16), w2_ref[...])

        @pl.when(f == nf - 1)
        def _():
            xhat, _ = _ln_stats(r_ref[...])
            out_ref[...] = alpha * (xhat * g_ref[...] + b_ref[...]) + 0.5 * acc[...]

    return pl.pallas_call(
        body,
        grid=(lp // tm, nf),
        in_specs=[
            pl.BlockSpec((tm, d), lambda i, f: (i, 0)),
            pl.BlockSpec((1, d), lambda i, f: (0, 0)),
            pl.BlockSpec((1, d), lambda i, f: (0, 0)),
            pl.BlockSpec((None, d, fc), lambda i, f: (f, 0, 0)),
            pl.BlockSpec((None, d, fc), lambda i, f: (nf + f, 0, 0)),
            pl.BlockSpec((None, fc, d), lambda i, f: (f, 0, 0)),
        ],
        out_specs=[
            pl.BlockSpec((tm, d), lambda i, f: (i, 0)),
            pl.BlockSpec((tm, fc), lambda i, f: (i, f)),
            pl.BlockSpec((tm, fc), lambda i, f: (i, f)),
        ],
        out_shape=[
            jax.ShapeDtypeStruct((lp, d), F32),
            jax.ShapeDtypeStruct((lp, nf * fc), BF16),
            jax.ShapeDtypeStruct((lp, nf * fc), BF16),
        ],
        scratch_shapes=[pltpu.VMEM((tm, d), BF16), pltpu.VMEM((tm, d), F32)],
        compiler_params=_cparams(("parallel", "arbitrary")),
        name="ffn_fwd",
    )(r_prev, g, b, w13g, w13g, w2c)


def _ffn_bwd(dr_next, r_prev, a, u, g, b, w13g, w2g, alpha):
    lp, d = r_prev.shape
    nf = N_DEV // 2
    fc = w13g.shape[2]
    w2c = w2g.reshape(nf, fc, d)
    tm = _pick_tm(lp, DENSE_TM_BWD)
    parts = _row_parts(tm, FFN_ROW_PARTS)

    def body(dr_ref, r_ref, a_ref, u_ref, g_ref, b_ref, w1_ref, w3_ref, w2_ref,
             drp_ref, hb_ref, s_ref, dz_ref, dg_ref, db_ref, dyb, dhacc):
        i = pl.program_id(0)
        f = pl.program_id(1)

        @pl.when(jnp.logical_and(i == 0, f == 0))
        def _():
            dg_ref[...] = jnp.zeros_like(dg_ref)
            db_ref[...] = jnp.zeros_like(db_ref)

        @pl.when(f == 0)
        def _():
            dyb[...] = (0.5 * dr_ref[...]).astype(BF16)
            dhacc[...] = jnp.zeros_like(dhacc)
            xhat, _ = _ln_stats(r_ref[...])
            hb_ref[...] = (xhat * g_ref[...] + b_ref[...]).astype(BF16)

        ds = [_dot_nt(dyb[rows, :], w2_ref[...]) for rows in parts]
        da, du = [], []
        for rows, dsv in zip(parts, ds):
            av = a_ref[rows, :].astype(F32)
            uv = u_ref[rows, :].astype(F32)
            sig = _sigmoid(av)
            sl = av * sig
            da.append((dsv * uv * (sig * (1.0 + av * (1.0 - sig)))).astype(BF16))
            du.append((dsv * sl).astype(BF16))
            s_ref[rows, :] = (sl * uv).astype(BF16)
        dh = [_dot_nt(da[p], w1_ref[...]) + _dot_nt(du[p], w3_ref[...]) for p in range(len(parts))]
        for p, rows in enumerate(parts):
            dz_ref[rows, 0:fc] = da[p]
            dz_ref[rows, fc:2 * fc] = du[p]
            dhacc[rows, :] += dh[p]

        @pl.when(f == nf - 1)
        def _():
            dh = alpha * dr_ref[...] + dhacc[...]
            xhat, rstd = _ln_stats(r_ref[...])
            drp_ref[...] = _ln_bwd(dh, xhat, rstd, g_ref[...])
            dg_ref[...] += _colsum(dh * xhat)
            db_ref[...] += _colsum(dh)

    row = lambda i, f: (i, 0)
    const = lambda i, f: (0, 0)
    chunk = lambda i, f: (i, f)
    return pl.pallas_call(
        body,
        grid=(lp // tm, nf),
        in_specs=[
            pl.BlockSpec((tm, d), row),
            pl.BlockSpec((tm, d), row),
            pl.BlockSpec((tm, fc), chunk),
            pl.BlockSpec((tm, fc), chunk),
            pl.BlockSpec((1, d), const),
            pl.BlockSpec((1, d), const),
            pl.BlockSpec((None, d, fc), lambda i, f: (f, 0, 0)),
            pl.BlockSpec((None, d, fc), lambda i, f: (nf + f, 0, 0)),
            pl.BlockSpec((None, fc, d), lambda i, f: (f, 0, 0)),
        ],
        out_specs=[
            pl.BlockSpec((tm, d), row),
            pl.BlockSpec((tm, d), row),
            pl.BlockSpec((tm, fc), chunk),
            pl.BlockSpec((tm, 2 * fc), chunk),
            pl.BlockSpec((1, d), const),
            pl.BlockSpec((1, d), const),
        ],
        out_shape=[
            jax.ShapeDtypeStruct((lp, d), F32),
            jax.ShapeDtypeStruct((lp, d), BF16),
            jax.ShapeDtypeStruct((lp, nf * fc), BF16),
            jax.ShapeDtypeStruct((lp, 2 * nf * fc), BF16),
            jax.ShapeDtypeStruct((1, d), F32),
            jax.ShapeDtypeStruct((1, d), F32),
        ],
        scratch_shapes=[pltpu.VMEM((tm, d), BF16), pltpu.VMEM((tm, d), F32)],
        compiler_params=_cparams(("arbitrary", "arbitrary")),
        name="ffn_bwd",
    )(dr_next, r_prev, a, u, g, b, w13g, w13g, w2c)


def _mm_tn(a, b, scale=1.0, bn=None, slot_of=None):
    t, m = a.shape
    n = b.shape[1]
    if bn is None:
        bn = n if n <= 1408 else _pick_tm(n, 1408, LANE)
    bm = m if m <= 1536 else _pick_tm(m, 1536, LANE)
    tk = _pick_tm(t, MM_TN_TK)
    nt = t // tk

    def body(a_ref, b_ref, o_ref, acc):
        k = pl.program_id(2)

        @pl.when(k == 0)
        def _():
            acc[...] = jnp.zeros_like(acc)

        acc[...] += _dot_tn(a_ref[...].astype(BF16), b_ref[...].astype(BF16))

        @pl.when(k == nt - 1)
        def _():
            o_ref[...] = (acc[...] * scale).astype(o_ref.dtype)

    if slot_of is None:
        out_spec = pl.BlockSpec((bm, bn), lambda i, j, k: (i, j))
        out_shape = jax.ShapeDtypeStruct((m, n), WIRE)
    else:
        out_spec = pl.BlockSpec((None, bm, bn), lambda i, j, k: (slot_of(j), i, 0))
        out_shape = jax.ShapeDtypeStruct((n // bn, m, bn), WIRE)
    return pl.pallas_call(
        body,
        grid=(m // bm, n // bn, nt),
        in_specs=[
            pl.BlockSpec((tk, bm), lambda i, j, k: (k, i)),
            pl.BlockSpec((tk, bn), lambda i, j, k: (k, j)),
        ],
        out_specs=out_spec,
        out_shape=out_shape,
        scratch_shapes=[pltpu.VMEM((bm, bn), F32)],
        compiler_params=_cparams(("parallel", "parallel", "arbitrary")),
        name="mm_tn",
    )(a, b)


def _proj_fwd(r_prev, g, b, w_in):
    lp, d = r_prev.shape
    n = w_in.shape[1]
    tm = _pick_tm(lp, DENSE_TM_FWD)

    def body(r_ref, g_ref, b_ref, w_ref, z_ref):
        xhat, _ = _ln_stats(r_ref[...])
        h = (xhat * g_ref[...] + b_ref[...]).astype(BF16)
        z = _dot(h, w_ref[...])
        rows = _row_ids(pl.program_id(0), tm)
        z_ref[...] = jnp.where(rows >= PAD_ROWS, z, 0.0)

    return pl.pallas_call(
        body,
        grid=(lp // tm,),
        in_specs=[
            pl.BlockSpec((tm, d), lambda i: (i, 0)),
            pl.BlockSpec((1, d), lambda i: (0, 0)),
            pl.BlockSpec((1, d), lambda i: (0, 0)),
            pl.BlockSpec((d, n), lambda i: (0, 0)),
        ],
        out_specs=pl.BlockSpec((tm, n), lambda i: (i, 0)),
        out_shape=jax.ShapeDtypeStruct((lp, n), F32),
        compiler_params=_cparams(("parallel",)),
        name="proj_fwd",
    )(r_prev, g, b, w_in)


def _out_fwd(r_prev, g, b, cat, w_out, alpha):
    lp, d = r_prev.shape
    k = cat.shape[1]
    tm = _pick_tm(lp, DENSE_TM_FWD)

    def body(r_ref, g_ref, b_ref, c_ref, w_ref, o_ref):
        xhat, _ = _ln_stats(r_ref[...])
        o_ref[...] = alpha * (xhat * g_ref[...] + b_ref[...]) + _dot(c_ref[...], w_ref[...])

    return pl.pallas_call(
        body,
        grid=(lp // tm,),
        in_specs=[
            pl.BlockSpec((tm, d), lambda i: (i, 0)),
            pl.BlockSpec((1, d), lambda i: (0, 0)),
            pl.BlockSpec((1, d), lambda i: (0, 0)),
            pl.BlockSpec((tm, k), lambda i: (i, 0)),
            pl.BlockSpec((k, d), lambda i: (0, 0)),
        ],
        out_specs=pl.BlockSpec((tm, d), lambda i: (i, 0)),
        out_shape=jax.ShapeDtypeStruct((lp, d), F32),
        compiler_params=_cparams(("parallel",)),
        name="out_fwd",
    )(r_prev, g, b, cat, w_out)


def _mm_nt(x, w):
    lp, n = x.shape
    k = w.shape[0]
    tm = _pick_tm(lp, DENSE_TM_FWD)

    def body(x_ref, w_ref, o_ref):
        o_ref[...] = _dot_nt(x_ref[...].astype(BF16), w_ref[...])

    return pl.pallas_call(
        body,
        grid=(lp // tm,),
        in_specs=[pl.BlockSpec((tm, n), lambda i: (i, 0)), pl.BlockSpec((k, n), lambda i: (0, 0))],
        out_specs=pl.BlockSpec((tm, k), lambda i: (i, 0)),
        out_shape=jax.ShapeDtypeStruct((lp, k), F32),
        compiler_params=_cparams(("parallel",)),
        name="mm_nt",
    )(x, w)


def _in_bwd(dr_next, r_prev, dz, g, b, w_in, alpha):
    lp, d = r_prev.shape
    n = w_in.shape[1]
    tm = _pick_tm(lp, DENSE_TM_FWD)

    def body(dr_ref, r_ref, dz_ref, g_ref, b_ref, w_ref, drp_ref, hb_ref, dg_ref, db_ref):
        @pl.when(pl.program_id(0) == 0)
        def _():
            dg_ref[...] = jnp.zeros_like(dg_ref)
            db_ref[...] = jnp.zeros_like(db_ref)

        dh = alpha * dr_ref[...] + _dot_nt(dz_ref[...], w_ref[...])
        xhat, rstd = _ln_stats(r_ref[...])
        hb_ref[...] = (xhat * g_ref[...] + b_ref[...]).astype(BF16)
        drp_ref[...] = _ln_bwd(dh, xhat, rstd, g_ref[...])
        dg_ref[...] += _colsum(dh * xhat)
        db_ref[...] += _colsum(dh)

    row = lambda i: (i, 0)
    const = lambda i: (0, 0)
    return pl.pallas_call(
        body,
        grid=(lp // tm,),
        in_specs=[
            pl.BlockSpec((tm, d), row),
            pl.BlockSpec((tm, d), row),
            pl.BlockSpec((tm, n), row),
            pl.BlockSpec((1, d), const),
            pl.BlockSpec((1, d), const),
            pl.BlockSpec((d, n), const),
        ],
        out_specs=[
            pl.BlockSpec((tm, d), row),
            pl.BlockSpec((tm, d), row),
            pl.BlockSpec((1, d), const),
            pl.BlockSpec((1, d), const),
        ],
        out_shape=[
            jax.ShapeDtypeStruct((lp, d), F32),
            jax.ShapeDtypeStruct((lp, d), BF16),
            jax.ShapeDtypeStruct((1, d), F32),
            jax.ShapeDtypeStruct((1, d), F32),
        ],
        compiler_params=_cparams(("arbitrary",)),
        name="in_bwd",
    )(dr_next, r_prev, dz, g, b, w_in)


def _loss_bwd(r_last, target, g, b):
    lp, d = r_last.shape
    tm = _pick_tm(lp, DENSE_TM_FWD)

    def body(r_ref, t_ref, g_ref, b_ref, dr_ref, dg_ref, db_ref, ls_ref):
        i = pl.program_id(0)

        @pl.when(i == 0)
        def _():
            dg_ref[...] = jnp.zeros_like(dg_ref)
            db_ref[...] = jnp.zeros_like(db_ref)
            ls_ref[...] = jnp.zeros_like(ls_ref)

        xhat, rstd = _ln_stats(r_ref[...])
        y = xhat * g_ref[...] + b_ref[...]
        rows = _row_ids(i, tm)
        err = jnp.where(rows >= FRONT, y - t_ref[...], 0.0)
        ls_ref[...] += _colsum(err * err)
        dy = err * (1.0 / d)
        dr_ref[...] = _ln_bwd(dy, xhat, rstd, g_ref[...])
        dg_ref[...] += _colsum(dy * xhat)
        db_ref[...] += _colsum(dy)

    row = lambda i: (i, 0)
    const = lambda i: (0, 0)
    return pl.pallas_call(
        body,
        grid=(lp // tm,),
        in_specs=[
            pl.BlockSpec((tm, d), row),
            pl.BlockSpec((tm, d), row),
            pl.BlockSpec((1, d), const),
            pl.BlockSpec((1, d), const),
        ],
        out_specs=[
            pl.BlockSpec((tm, d), row),
            pl.BlockSpec((1, d), const),
            pl.BlockSpec((1, d), const),
            pl.BlockSpec((1, d), const),
        ],
        out_shape=[
            jax.ShapeDtypeStruct((lp, d), F32),
            jax.ShapeDtypeStruct((1, d), F32),
            jax.ShapeDtypeStruct((1, d), F32),
            jax.ShapeDtypeStruct((1, d), F32),
        ],
        compiler_params=_cparams(("arbitrary",)),
        name="loss_bwd",
    )(r_last, target, g, b)


def _pool_counts(tile, tm, width):
    pos = _row_ids(tile, tm) - PAD_ROWS
    lane = lax.broadcasted_iota(jnp.int32, (1, width), 1)
    group = width // len(POOL_WINDOWS)
    win = jnp.full((1, width), POOL_WINDOWS[-1], jnp.int32)
    for gi in range(len(POOL_WINDOWS) - 2, -1, -1):
        win = jnp.where(lane < (gi + 1) * group, POOL_WINDOWS[gi], win)
    cnt = jnp.clip(pos + 1, 1, win)
    return cnt.astype(F32), lane, group


def _pool_select(sums, lane, group):
    out = sums[-1]
    for gi in range(len(POOL_WINDOWS) - 2, -1, -1):
        out = jnp.where(lane < (gi + 1) * group, sums[gi], out)
    return out


def _pool_window_sums(ext, tm, sign):
    base = POOL_HALO if sign < 0 else 0
    acc = ext[pl.ds(base, tm), :]
    sums, k = [], 1
    for w in POOL_WINDOWS:
        while k < w:
            acc = acc + ext[pl.ds(base + sign * k, tm), :]
            k += 1
        sums.append(acc)
    return sums


def _pool_fwd(z, wbd, scale):
    lp = z.shape[0]
    c = wbd.shape[0]
    tm = _pick_tm(lp, MIX_TM, CHUNK)

    def body(x_ref, w_ref, s_ref, y_ref, ext):
        i = pl.program_id(0)

        @pl.when(i == 0)
        def _():
            ext[0:POOL_HALO, :] = jnp.zeros((POOL_HALO, c), F32)

        x = x_ref[...]
        ext[POOL_HALO:POOL_HALO + tm, :] = x
        cnt, lane, group = _pool_counts(i, tm, c)
        sums = _pool_window_sums(ext, tm, -1)
        y = _pool_select(sums, lane, group) / cnt - x
        ext[0:POOL_HALO, :] = ext[tm:tm + POOL_HALO, :]
        y_ref[...] = (_dot(y.astype(BF16), w_ref[...]) * s_ref[...]).astype(BF16)

    return pl.pallas_call(
        body,
        grid=(lp // tm,),
        in_specs=[
            pl.BlockSpec((tm, c), lambda i: (i, 0)),
            pl.BlockSpec((c, c), lambda i: (0, 0)),
            pl.BlockSpec((1, c), lambda i: (0, 0)),
        ],
        out_specs=pl.BlockSpec((tm, c), lambda i: (i, 0)),
        out_shape=jax.ShapeDtypeStruct((lp, c), BF16),
        scratch_shapes=[pltpu.VMEM((tm + POOL_HALO, c), F32)],
        compiler_params=_cparams(("arbitrary",)),
        name="pool_fwd",
    )(z, wbd, scale)


def _pool_bwd(z, dcat, wbd, scale):
    lp = z.shape[0]
    c = wbd.shape[0]
    tm = _pick_tm(lp, MIX_TM, CHUNK)
    ni = lp // tm
    hpt = tm // POOL_HALO

    def body(x_ref, xh_ref, dy_ref, w_ref, s_ref, dx_ref, dw_ref, ds_ref, ext, ext2):
        i = pl.program_id(0)
        t = ni - 1 - i

        @pl.when(i == 0)
        def _():
            dw_ref[...] = jnp.zeros_like(dw_ref)
            ds_ref[...] = jnp.zeros_like(ds_ref)
            ext2[tm:tm + POOL_HALO, :] = jnp.zeros((POOL_HALO, c), F32)

        x = x_ref[...]
        ext[0:POOL_HALO, :] = jnp.where(t > 0, xh_ref[...], 0.0)
        ext[POOL_HALO:POOL_HALO + tm, :] = x
        cnt, lane, group = _pool_counts(t, tm, c)
        y = (_pool_select(_pool_window_sums(ext, tm, -1), lane, group) / cnt - x).astype(BF16)
        w = w_ref[...]
        dyv = dy_ref[...]
        ds_ref[...] += _colsum(_dot(y, w) * dyv)
        do = (dyv * s_ref[...]).astype(BF16)
        dw_ref[...] += _dot_tn(y, do)
        dyp = _dot_nt(do, w)
        ext2[0:tm, :] = dyp / cnt
        dx = _pool_select(_pool_window_sums(ext2, tm, 1), lane, group) - dyp
        ext2[tm:tm + POOL_HALO, :] = ext2[0:POOL_HALO, :]
        rows = _row_ids(t, tm)
        dx_ref[...] = jnp.where(rows >= PAD_ROWS, dx, 0.0).astype(BF16)

    rev = lambda i: (ni - 1 - i, 0)
    return pl.pallas_call(
        body,
        grid=(ni,),
        in_specs=[
            pl.BlockSpec((tm, c), rev),
            pl.BlockSpec((POOL_HALO, c), lambda i: (jnp.maximum((ni - 1 - i) * hpt - 1, 0), 0)),
            pl.BlockSpec((tm, c), rev),
            pl.BlockSpec((c, c), lambda i: (0, 0)),
            pl.BlockSpec((1, c), lambda i: (0, 0)),
        ],
        out_specs=[
            pl.BlockSpec((tm, c), rev),
            pl.BlockSpec((c, c), lambda i: (0, 0)),
            pl.BlockSpec((1, c), lambda i: (0, 0)),
        ],
        out_shape=[
            jax.ShapeDtypeStruct((lp, c), BF16),
            jax.ShapeDtypeStruct((c, c), F32),
            jax.ShapeDtypeStruct((1, c), F32),
        ],
        scratch_shapes=[pltpu.VMEM((tm + POOL_HALO, c), F32), pltpu.VMEM((tm + POOL_HALO, c), F32)],
        compiler_params=_cparams(("arbitrary",)),
        name="pool_bwd",
    )(z, z, dcat, wbd, scale)


def _conv_taps(ext, w_ref, tm, first):
    acc = w_ref[0:1, :] * ext[pl.ds(first, tm), :]
    for k in range(1, CONV_WIDTH):
        acc = acc + w_ref[k:k + 1, :] * ext[pl.ds(first + k, tm), :]
    return acc


def _conv_fwd(z, w_dw, b_dw, ln_g, ln_b, w_pw):
    lp = z.shape[0]
    c = w_pw.shape[0]
    tm = _pick_tm(lp, MIX_TM, CHUNK)

    def body(a_ref, gt_ref, w_ref, bd_ref, g_ref, b_ref, pw_ref, y_ref, acc_ref, ext):
        i = pl.program_id(0)

        @pl.when(i == 0)
        def _():
            ext[0:CONV_HALO, :] = jnp.zeros((CONV_HALO, c), F32)

        ext[CONV_HALO:CONV_HALO + tm, :] = a_ref[...] * _sigmoid(gt_ref[...])
        acc = _conv_taps(ext, w_ref, tm, CONV_HALO - CONV_WIDTH + 1) + bd_ref[...]
        acc_ref[...] = acc
        ext[0:CONV_HALO, :] = ext[tm:tm + CONV_HALO, :]
        xhat, _ = _ln_stats(acc)
        n = xhat * g_ref[...] + b_ref[...]
        act = n * _sigmoid(n)
        y_ref[...] = _dot(act.astype(BF16), pw_ref[...]).astype(BF16)

    const = lambda i: (0, 0)
    return pl.pallas_call(
        body,
        grid=(lp // tm,),
        in_specs=[
            pl.BlockSpec((tm, c), lambda i: (i, 1)),
            pl.BlockSpec((tm, c), lambda i: (i, 2)),
            pl.BlockSpec((CONV_HALO, c), const),
            pl.BlockSpec((1, c), const),
            pl.BlockSpec((1, c), const),
            pl.BlockSpec((1, c), const),
            pl.BlockSpec((c, c), const),
        ],
        out_specs=[pl.BlockSpec((tm, c), lambda i: (i, 0)), pl.BlockSpec((tm, c), lambda i: (i, 0))],
        out_shape=[jax.ShapeDtypeStruct((lp, c), BF16), jax.ShapeDtypeStruct((lp, c), F32)],
        scratch_shapes=[pltpu.VMEM((tm + CONV_HALO, c), F32)],
        compiler_params=_cparams(("arbitrary",)),
        name="conv_fwd",
    )(z, z, w_dw, b_dw, ln_g, ln_b, w_pw)


def _conv_bwd(z, dcat, acc_fwd, w_dw, ln_g, ln_b, w_pw):
    lp = z.shape[0]
    c = w_pw.shape[0]
    tm = _pick_tm(lp, MIX_TM, CHUNK)
    ni = lp // tm
    hpt = tm // CONV_HALO
    first = CONV_HALO - CONV_WIDTH + 1

    def body(a_ref, gt_ref, ah_ref, gh_ref, dy_ref, acc_ref, w_ref, g_ref, b_ref, pw_ref,
             dca_ref, dcg_ref, dw_ref, dbd_ref, dg_ref, db_ref, dpw_ref, extu, extd):
        i = pl.program_id(0)
        t = ni - 1 - i

        @pl.when(i == 0)
        def _():
            dw_ref[...] = jnp.zeros_like(dw_ref)
            dbd_ref[...] = jnp.zeros_like(dbd_ref)
            dg_ref[...] = jnp.zeros_like(dg_ref)
            db_ref[...] = jnp.zeros_like(db_ref)
            dpw_ref[...] = jnp.zeros_like(dpw_ref)
            extd[tm:tm + CONV_HALO, :] = jnp.zeros((CONV_HALO, c), F32)

        ca = a_ref[...]
        sg = _sigmoid(gt_ref[...])
        extu[0:CONV_HALO, :] = jnp.where(t > 0, ah_ref[...] * _sigmoid(gh_ref[...]), 0.0)
        extu[CONV_HALO:CONV_HALO + tm, :] = ca * sg
        xhat, rstd = _ln_stats(acc_ref[...])
        gam = g_ref[...]
        n = xhat * gam + b_ref[...]
        sn = _sigmoid(n)
        act = (n * sn).astype(BF16)
        do = dy_ref[...].astype(BF16)
        dpw_ref[...] += _dot_tn(act, do)
        dn = _dot_nt(do, pw_ref[...]) * (sn * (1.0 + n * (1.0 - sn)))
        dg_ref[...] += _colsum(dn * xhat)
        db_ref[...] += _colsum(dn)
        dyc = _ln_bwd(dn, xhat, rstd, gam)
        dbd_ref[...] += _colsum(dyc)
        extd[0:tm, :] = dyc
        du = None
        for k in range(CONV_WIDTH):
            dw_ref[k:k + 1, :] += _colsum(dyc * extu[pl.ds(first + k, tm), :])
            term = w_ref[k:k + 1, :] * extd[pl.ds(CONV_WIDTH - 1 - k, tm), :]
            du = term if du is None else du + term
        extd[tm:tm + CONV_HALO, :] = extd[0:CONV_HALO, :]
        du = jnp.where(_row_ids(t, tm) >= PAD_ROWS, du, 0.0)
        dca_ref[...] = (du * sg).astype(BF16)
        dcg_ref[...] = (du * ca * sg * (1.0 - sg)).astype(BF16)

    const = lambda i: (0, 0)
    rev = lambda col: (lambda i: (ni - 1 - i, col))
    halo = lambda col: (lambda i: (jnp.maximum((ni - 1 - i) * hpt - 1, 0), col))
    return pl.pallas_call(
        body,
        grid=(ni,),
        in_specs=[
            pl.BlockSpec((tm, c), rev(1)),
            pl.BlockSpec((tm, c), rev(2)),
            pl.BlockSpec((CONV_HALO, c), halo(1)),
            pl.BlockSpec((CONV_HALO, c), halo(2)),
            pl.BlockSpec((tm, c), rev(1)),
            pl.BlockSpec((tm, c), rev(0)),
            pl.BlockSpec((CONV_HALO, c), const),
            pl.BlockSpec((1, c), const),
            pl.BlockSpec((1, c), const),
            pl.BlockSpec((c, c), const),
        ],
        out_specs=[
            pl.BlockSpec((tm, c), rev(0)),
            pl.BlockSpec((tm, c), rev(0)),
            pl.BlockSpec((CONV_HALO, c), const),
            pl.BlockSpec((1, c), const),
            pl.BlockSpec((1, c), const),
            pl.BlockSpec((1, c), const),
            pl.BlockSpec((c, c), const),
        ],
        out_shape=[
            jax.ShapeDtypeStruct((lp, c), BF16),
            jax.ShapeDtypeStruct((lp, c), BF16),
            jax.ShapeDtypeStruct((CONV_HALO, c), F32),
            jax.ShapeDtypeStruct((1, c), F32),
            jax.ShapeDtypeStruct((1, c), F32),
            jax.ShapeDtypeStruct((1, c), F32),
            jax.ShapeDtypeStruct((c, c), F32),
        ],
        scratch_shapes=[pltpu.VMEM((tm + CONV_HALO, c), F32), pltpu.VMEM((tm + CONV_HALO, c), F32)],
        compiler_params=_cparams(("arbitrary",)),
        name="conv_bwd",
    )(z, z, z, z, dcat, acc_fwd, w_dw, ln_g, ln_b, w_pw)


def _rope(x, cos, sgn_sin):
    return x * cos + pltpu.roll(x, LANE // 2, 1) * sgn_sin


def _rope_t(dy, cos, sgn_sin):
    return dy * cos + pltpu.roll(dy * sgn_sin, LANE // 2, 1)


def _ret_chunk_fwd(q, k, v, s0, cos, sn, tabs):
    heads = range(len(q))
    bf = lambda xs: [x.astype(BF16) for x in xs]
    qr = [_rope(q[h], cos, sn) for h in heads]
    kr = [_rope(k[h], cos, sn) * (LANE ** -0.5) for h in heads]
    qrb, krb, vb, s0b = bf(qr), bf(kr), bf(v), bf(s0)
    amb = bf([_dot_nt(qrb[h], krb[h]) * tabs[h][0] for h in heads])
    qdb = bf([qr[h] * tabs[h][1] for h in heads])
    intra = [_dot(amb[h], vb[h]) for h in heads]
    cross = [_dot(qdb[h], s0b[h]) for h in heads]
    o = [intra[h] + cross[h] for h in heads]
    return qr, kr, qrb, krb, vb, s0b, amb, qdb, o


def _ret_specs(tm, q_blk, tile_of):
    def mk(col):
        return pl.BlockSpec((tm, LANE), lambda i: (tile_of(i), col))
    return [mk(q_blk + j) for j in range(4 * RET_HEADS)]


def _ret_tables(h, dm_ref, qd_ref, kd_ref, cd_ref, gn_ref):
    return dm_ref[h], qd_ref[h], kd_ref[h], cd_ref[h, 0:1, :], gn_ref[:, h * LANE:(h + 1) * LANE]


def _ret_fwd(z, rope_c, rope_s, dm, qd, kd, cd, gn, q_blk):
    lp = z.shape[0]
    tm = _pick_tm(lp, MIX_TM, CHUNK)
    nc = tm // CHUNK
    ni = lp // tm
    nh = RET_HEADS

    def body(*refs):
        q_refs, k_refs, v_refs, g_refs = refs[0:nh], refs[nh:2 * nh], refs[2 * nh:3 * nh], refs[3 * nh:4 * nh]
        c_ref, s_ref, dm_ref, qd_ref, kd_ref, cd_ref, gn_ref, y_ref, st_ref, state = refs[4 * nh:]

        @pl.when(pl.program_id(0) == 0)
        def _():
            state[...] = jnp.zeros_like(state)

        def chunk(c, carry):
            rows = pl.ds(pl.multiple_of(c * CHUNK, CHUNK), CHUNK)
            cos, sn = c_ref[rows, :], s_ref[rows, :]
            heads = range(nh)
            tabs = [_ret_tables(h, dm_ref, qd_ref, kd_ref, cd_ref, gn_ref) for h in heads]
            q = [q_refs[h][rows, :] for h in heads]
            k = [k_refs[h][rows, :] for h in heads]
            v = [v_refs[h][rows, :] for h in heads]
            gg = [g_refs[h][rows, :] for h in heads]
            s0 = [state[h] for h in heads]
            _, kr, _, _, vb, _, _, _, o = _ret_chunk_fwd(q, k, v, s0, cos, sn, tabs)
            kv = [_dot_tn((kr[h] * tabs[h][2]).astype(BF16), vb[h]) for h in heads]
            for h in heads:
                on, _ = _ln_stats(o[h])
                st_ref[c, h, :, :] = s0[h]
                state[h] = s0[h] * tabs[h][3] + kv[h]
                y_ref[rows, h * LANE:(h + 1) * LANE] = (gg[h] * _sigmoid(gg[h]) * (on * tabs[h][4])).astype(BF16)
            return carry

        lax.fori_loop(0, nc, chunk, 0, unroll=2)

    const3 = lambda s: pl.BlockSpec(s, lambda i: (0, 0, 0))
    return pl.pallas_call(
        body,
        grid=(ni,),
        in_specs=_ret_specs(tm, q_blk, lambda i: i) + [
            pl.BlockSpec((tm, LANE), lambda i: (i, 0)),
            pl.BlockSpec((tm, LANE), lambda i: (i, 0)),
            const3((nh, CHUNK, CHUNK)), const3((nh, CHUNK, LANE)), const3((nh, CHUNK, LANE)), const3((nh, 8, LANE)),
            pl.BlockSpec((1, nh * LANE), lambda i: (0, 0)),
        ],
        out_specs=[
            pl.BlockSpec((tm, nh * LANE), lambda i: (i, 0)),
            pl.BlockSpec((nc, nh, LANE, LANE), lambda i: (i, 0, 0, 0)),
        ],
        out_shape=[
            jax.ShapeDtypeStruct((lp, nh * LANE), BF16),
            jax.ShapeDtypeStruct((lp // CHUNK, nh, LANE, LANE), F32),
        ],
        scratch_shapes=[pltpu.VMEM((nh, LANE, LANE), F32)],
        compiler_params=_cparams(("arbitrary",)),
        name="ret_fwd",
    )(*([z] * (4 * nh)), rope_c, rope_s, dm, qd, kd, cd, gn)


def _ret_bwd(z, dcat, states, rope_c, rope_s, dm, qd, kd, cd, gn, q_blk, dy_blk):
    lp = z.shape[0]
    tm = _pick_tm(lp, MIX_TM, CHUNK)
    nc = tm // CHUNK
    ni = lp // tm

    nh = RET_HEADS

    def body(*refs):
        q_refs, k_refs, v_refs, g_refs = refs[0:nh], refs[nh:2 * nh], refs[2 * nh:3 * nh], refs[3 * nh:4 * nh]
        (dy_ref, st_ref, c_ref, s_ref, dm_ref, qd_ref, kd_ref, cd_ref, gn_ref,
         dq_ref, dk_ref, dv_ref, dgt_ref, dgn_ref, dstate) = refs[4 * nh:]
        i = pl.program_id(0)
        t = ni - 1 - i

        @pl.when(i == 0)
        def _():
            dstate[...] = jnp.zeros_like(dstate)
            dgn_ref[...] = jnp.zeros_like(dgn_ref)

        def chunk(cc, carry):
            c = nc - 1 - cc
            rows = pl.ds(pl.multiple_of(c * CHUNK, CHUNK), CHUNK)
            cos, sn = c_ref[rows, :], s_ref[rows, :]
            keep = (lax.broadcasted_iota(jnp.int32, (CHUNK, 1), 0) + (t * tm + c * CHUNK)) >= PAD_ROWS
            heads = range(nh)
            lanes = [slice(h * LANE, (h + 1) * LANE) for h in heads]
            tabs = [_ret_tables(h, dm_ref, qd_ref, kd_ref, cd_ref, gn_ref) for h in heads]
            q = [q_refs[h][rows, :] for h in heads]
            k = [k_refs[h][rows, :] for h in heads]
            v = [v_refs[h][rows, :] for h in heads]
            gg = [g_refs[h][rows, :] for h in heads]
            dyv = [dy_ref[rows, lanes[h]] for h in heads]
            s0 = [st_ref[c, h, :, :] for h in heads]
            ds1 = [dstate[h] for h in heads]
            dgn = [dgn_ref[:, lanes[h]] for h in heads]
            qr, kr, qrb, krb, vb, s0b, amb, qdb, o = _ret_chunk_fwd(q, k, v, s0, cos, sn, tabs)
            stats = [_ln_stats(o[h]) for h in heads]
            sg = [_sigmoid(gg[h]) for h in heads]
            sl = [gg[h] * sg[h] for h in heads]
            dgt = [dyv[h] * (stats[h][0] * tabs[h][4]) * (sg[h] * (1.0 + gg[h] * (1.0 - sg[h]))) for h in heads]
            dob = [_ln_bwd(dyv[h] * sl[h], stats[h][0], stats[h][1], tabs[h][4]).astype(BF16) for h in heads]
            ds1b = [ds1[h].astype(BF16) for h in heads]
            kdb = [(kr[h] * tabs[h][2]).astype(BF16) for h in heads]
            da = [(_dot_nt(dob[h], vb[h]) * tabs[h][0]).astype(BF16) for h in heads]
            dv_intra = [_dot_tn(amb[h], dob[h]) for h in heads]
            dv_cross = [_dot(kdb[h], ds1b[h]) for h in heads]
            dq_cross = [_dot_nt(dob[h], s0b[h]) for h in heads]
            dk_cross = [_dot_nt(vb[h], ds1b[h]) for h in heads]
            ds_inc = [_dot_tn(qdb[h], dob[h]) for h in heads]
            dq_intra = [_dot(da[h], krb[h]) for h in heads]
            dk_intra = [_dot_tn(da[h], qrb[h]) for h in heads]
            for h in heads:
                dqr = dq_intra[h] + dq_cross[h] * tabs[h][1]
                dkr = dk_intra[h] + dk_cross[h] * tabs[h][2]
                dq = _rope_t(dqr, cos, sn)
                dk = _rope_t(dkr * (LANE ** -0.5), cos, sn)
                dq_ref[rows, lanes[h]] = jnp.where(keep, dq, 0.0).astype(BF16)
                dk_ref[rows, lanes[h]] = jnp.where(keep, dk, 0.0).astype(BF16)
                dv_ref[rows, lanes[h]] = jnp.where(keep, dv_intra[h] + dv_cross[h], 0.0).astype(BF16)
                dgt_ref[rows, lanes[h]] = jnp.where(keep, dgt[h], 0.0).astype(BF16)
                dstate[h] = ds1[h] * tabs[h][3] + ds_inc[h]
                dgn_ref[:, lanes[h]] = dgn[h] + _colsum(dyv[h] * sl[h] * stats[h][0])
            return carry

        lax.fori_loop(0, nc, chunk, 0, unroll=2)

    rev = lambda i: ni - 1 - i
    const3 = lambda s: pl.BlockSpec(s, lambda i: (0, 0, 0))
    out_blk = pl.BlockSpec((tm, nh * LANE), lambda i: (rev(i), 0))
    out_sds = jax.ShapeDtypeStruct((lp, nh * LANE), BF16)
    return pl.pallas_call(
        body,
        grid=(ni,),
        in_specs=_ret_specs(tm, q_blk, rev) + [
            pl.BlockSpec((tm, nh * LANE), lambda i: (rev(i), dy_blk // nh)),
            pl.BlockSpec((nc, nh, LANE, LANE), lambda i: (rev(i), 0, 0, 0)),
            pl.BlockSpec((tm, LANE), lambda i: (rev(i), 0)),
            pl.BlockSpec((tm, LANE), lambda i: (rev(i), 0)),
            const3((nh, CHUNK, CHUNK)), const3((nh, CHUNK, LANE)), const3((nh, CHUNK, LANE)), const3((nh, 8, LANE)),
            pl.BlockSpec((1, nh * LANE), lambda i: (0, 0)),
        ],
        out_specs=[out_blk, out_blk, out_blk, out_blk, pl.BlockSpec((1, nh * LANE), lambda i: (0, 0))],
        out_shape=[out_sds, out_sds, out_sds, out_sds, jax.ShapeDtypeStruct((1, nh * LANE), F32)],
        scratch_shapes=[pltpu.VMEM((nh, LANE, LANE), F32)],
        compiler_params=_cparams(("arbitrary",)),
        name="ret_bwd",
    )(*([z] * (4 * nh)), dcat, states, rope_c, rope_s, dm, qd, kd, cd, gn)


def _adamw(parts_list, w, m, v):
    nl, r, c = w.shape
    assert len(parts_list) == nl
    tr = _pick_tm(r, max(8, (1 << 17) // c), 8)
    nr = r // tr

    def body(*refs):
        p_refs = refs[:nl]
        w_ref, m_ref, v_ref, g_ref, d_ref, nm_ref, nv_ref = refs[nl:]
        layer = pl.program_id(0)
        for k in range(nl):
            @pl.when(layer == k)
            def _(k=k):
                g = p_refs[k][0].astype(F32)
                for j in range(1, N_DEV):
                    g = g + p_refs[k][j].astype(F32)
                m1 = ADAM_B1 * m_ref[...] + (1.0 - ADAM_B1) * g
                v1 = ADAM_B2 * v_ref[...] + (1.0 - ADAM_B2) * (g * g)
                m_hat = m1 / (1.0 - ADAM_B1 ** ADAM_STEP)
                v_hat = v1 / (1.0 - ADAM_B2 ** ADAM_STEP)
                g_ref[...] = g
                d_ref[...] = -ADAM_LR * (m_hat / (jnp.sqrt(v_hat) + ADAM_EPS) + ADAM_WD * w_ref[...])
                nm_ref[...] = m1
                nv_ref[...] = v1

    def parts_spec(k):
        return pl.BlockSpec((N_DEV, tr, c), lambda l, i: (0, jnp.where(l == k, i, jnp.where(l < k, 0, nr - 1)), 0))

    blk = pl.BlockSpec((None, tr, c), lambda l, i: (l, i, 0))
    sds = jax.ShapeDtypeStruct((nl, r, c), F32)
    return pl.pallas_call(
        body,
        grid=(nl, nr),
        in_specs=[parts_spec(k) for k in range(nl)] + [blk, blk, blk],
        out_specs=[blk, blk, blk, blk],
        out_shape=[sds, sds, sds, sds],
        compiler_params=_cparams(("arbitrary", "arbitrary")),
        name="adamw",
    )(*parts_list, w, m, v)


def _flip(v, bit):
    return 1 - v if bit else v


def _exchange(arrs, scatter):
    n = len(arrs)
    shapes = [tuple(a.shape[1:] if scatter else a.shape) for a in arrs]

    def body(*refs):
        x_refs, o_refs = refs[:n], refs[n:2 * n]
        send_sems, recv_sems, local_sems = refs[2 * n:]
        mx, my, mc = lax.axis_index("x"), lax.axis_index("y"), lax.axis_index("c")
        me = 4 * mx + 2 * my + mc

        def peer_of(k):
            return (_flip(mx, (k >> 2) & 1), _flip(my, (k >> 1) & 1), _flip(mc, k & 1))

        def copy(a, k):
            peer = peer_of(k)
            src = x_refs[a].at[4 * peer[0] + 2 * peer[1] + peer[2]] if scatter else x_refs[a]
            return pltpu.make_async_remote_copy(
                src_ref=src, dst_ref=o_refs[a].at[me], send_sem=send_sems.at[a, k - 1],
                recv_sem=recv_sems.at[a, k - 1], device_id=peer, device_id_type=pl.DeviceIdType.MESH)

        def arrival(a, k):
            peer = peer_of(k)
            slot = o_refs[a].at[4 * peer[0] + 2 * peer[1] + peer[2]]
            return pltpu.make_async_remote_copy(
                src_ref=slot, dst_ref=slot, send_sem=send_sems.at[a, k - 1], recv_sem=recv_sems.at[a, k - 1],
                device_id=peer, device_id_type=pl.DeviceIdType.MESH)

        locals_ = [pltpu.make_async_copy(x_refs[a].at[me] if scatter else x_refs[a], o_refs[a].at[me],
                                         local_sems.at[a]) for a in range(n)]
        sends = [copy(a, k) for a in range(n) for k in range(1, N_DEV)]
        for cp in locals_ + sends:
            cp.start()
        for a in range(n):
            for k in range(1, N_DEV):
                arrival(a, k).wait_recv()
        for cp in sends:
            cp.wait_send()
        for cp in locals_:
            cp.wait()

    hbm = pl.BlockSpec(memory_space=pltpu.HBM)
    return pl.pallas_call(
        body,
        in_specs=[hbm] * n,
        out_specs=[hbm] * n,
        out_shape=[jax.ShapeDtypeStruct((N_DEV,) + s, a.dtype) for s, a in zip(shapes, arrs)],
        scratch_shapes=[
            pltpu.SemaphoreType.DMA((n, N_DEV - 1)),
            pltpu.SemaphoreType.DMA((n, N_DEV - 1)),
            pltpu.SemaphoreType.DMA((n,)),
        ],
        name="reduce_scatter_parts" if scatter else "all_gather",
    )(*arrs)


def _exchange_descriptors(x_refs, land_refs, send_sems, recv_sems, scatter):
    mx, my, mc = lax.axis_index("x"), lax.axis_index("y"), lax.axis_index("c")
    me = 4 * mx + 2 * my + mc
    sends, arrivals = [], []
    for a in range(len(x_refs)):
        for k in range(1, N_DEV):
            peer = (_flip(mx, (k >> 2) & 1), _flip(my, (k >> 1) & 1), _flip(mc, k & 1))
            slot = 4 * peer[0] + 2 * peer[1] + peer[2]
            si = a * (N_DEV - 1) + k - 1
            sems = dict(send_sem=send_sems.at[si], recv_sem=recv_sems.at[si],
                        device_id=peer, device_id_type=pl.DeviceIdType.MESH)
            sends.append(pltpu.make_async_remote_copy(
                src_ref=x_refs[a].at[slot] if scatter else x_refs[a], dst_ref=land_refs[a].at[me], **sems))
            arrivals.append(pltpu.make_async_remote_copy(
                src_ref=land_refs[a].at[slot], dst_ref=land_refs[a].at[slot], **sems))
    return sends, arrivals


def _own_slot_copies(x_refs, land_refs, local_sems, scatter):
    me = 4 * lax.axis_index("x") + 2 * lax.axis_index("y") + lax.axis_index("c")
    return [pltpu.make_async_copy(x_refs[a].at[me] if scatter else x_refs[a], land_refs[a].at[me], local_sems.at[a])
            for a in range(len(x_refs))]


def _exchange_start(arrs, scatter, name, after=None):
    n = len(arrs)
    shapes = [tuple(a.shape[1:] if scatter else a.shape) for a in arrs]
    lands = [lax.empty((N_DEV,) + s, a.dtype) for s, a in zip(shapes, arrs)]
    extra = [] if after is None else [after]

    def body(*refs):
        x_refs, land_refs = refs[:n], refs[n:2 * n]
        send_sems, recv_sems, local_sems = refs[2 * n + len(extra):2 * n + len(extra) + 3]
        token = refs[-1]
        sends, _ = _exchange_descriptors(x_refs, land_refs, send_sems, recv_sems, scatter)
        for cp in sends + _own_slot_copies(x_refs, land_refs, local_sems, scatter):
            cp.start()
        token[...] = jnp.zeros_like(token)

    hbm = pl.BlockSpec(memory_space=pltpu.HBM)
    sem = pl.BlockSpec(memory_space=pltpu.SEMAPHORE)
    sem_type = pltpu.SemaphoreType.DMA((n * (N_DEV - 1),))
    operands = [pltpu.with_memory_space_constraint(a, pltpu.HBM) for a in list(arrs) + lands]
    out = pl.pallas_call(
        body,
        in_specs=[hbm] * (2 * n) + [pl.BlockSpec(memory_space=pl.ANY)] * len(extra),
        out_specs=[sem, sem, sem] + [hbm] * (2 * n) + [pl.BlockSpec(memory_space=pltpu.VMEM)],
        out_shape=[sem_type, sem_type, pltpu.SemaphoreType.DMA((n,))] + [pltpu.HBM(a.shape, a.dtype) for a in operands]
        + [jax.ShapeDtypeStruct((8, LANE), F32)],
        input_output_aliases={i: 3 + i for i in range(2 * n)},
        compiler_params=pltpu.CompilerParams(has_side_effects=pltpu.SideEffectType.DATAFLOW_SIDE_EFFECTING),
        name=name,
    )(*operands, *extra)
    return (out[0:3], list(out[3:3 + n]), list(out[3 + n:3 + 2 * n]), scatter), out[-1]


def _exchange_wait(handle, after, name):
    sems, x_thru, land_thru, scatter = handle
    n = len(x_thru)

    def body(*refs):
        x_refs, land_refs = refs[:n], refs[n:2 * n]
        send_sems, recv_sems, local_sems = refs[2 * n:2 * n + 3]
        sends, arrivals = _exchange_descriptors(x_refs, land_refs, send_sems, recv_sems, scatter)
        for cp in sends:
            cp.wait_send()
        for cp in arrivals:
            cp.wait_recv()
        for cp in _own_slot_copies(x_refs, land_refs, local_sems, scatter):
            cp.wait()

    hbm = pl.BlockSpec(memory_space=pltpu.HBM)
    sem = pl.BlockSpec(memory_space=pltpu.SEMAPHORE)
    out = pl.pallas_call(
        body,
        in_specs=[hbm] * (2 * n) + [sem, sem, sem, pl.BlockSpec(memory_space=pl.ANY)],
        out_specs=[hbm] * (2 * n),
        out_shape=[pltpu.HBM(a.shape, a.dtype) for a in x_thru + land_thru],
        input_output_aliases={i: i for i in range(2 * n)},
        compiler_params=pltpu.CompilerParams(has_side_effects=pltpu.SideEffectType.DATAFLOW_SIDE_EFFECTING),
        name=name,
    )(*x_thru, *land_thru, *sems, after)
    return list(out[n:])


PACK_ALIGN = 2048


def _padded(n):
    return -(-n // PACK_ALIGN) * PACK_ALIGN


def _pad_to(a, axis, size):
    pad = [(0, 0)] * a.ndim
    pad[axis] = (0, size - a.shape[axis])
    return jnp.pad(a, pad)


def _pack(arrs, lead=0):
    flat = []
    for a in arrs:
        v = a.reshape(a.shape[:lead] + (-1,))
        flat.append(_pad_to(v, lead, _padded(v.shape[lead])))
    out = jnp.concatenate(flat, axis=lead)
    return out.reshape(out.shape[:lead] + (-1, LANE))


def _unpack(slab, shapes, lead=0):
    flat = slab.reshape(slab.shape[:lead] + (-1,))
    out, off = [], 0
    for s in shapes:
        n = math.prod(s)
        out.append(flat[..., off:off + n].reshape(slab.shape[:lead] + tuple(s)))
        off += _padded(n)
    return out


def _unshard(parts, ax):
    return jnp.concatenate([parts[j] for j in range(N_DEV)], axis=ax)


def _to_shards(full, ax):
    n = full.shape[ax] // N_DEV
    return jnp.stack([lax.slice_in_dim(full, j * n, (j + 1) * n, axis=ax) for j in range(N_DEV)])


def _pad_halves(w, hp):
    h = w.shape[-1] // 2
    zeros = jnp.zeros(w.shape[:-1] + (hp - h,), w.dtype)
    return jnp.concatenate([w[..., :h], zeros, w[..., h:], zeros], axis=-1)


def _unpad_halves(w, h):
    hp = w.shape[-1] // 2
    return jnp.concatenate([w[..., :h], w[..., hp:hp + h]], axis=-1)


SMALL_SHARDED = (("meta", 1), ("conv_dw", 2), ("ln_g", 2), ("ln_b", 2))
MATMUL_WEIGHTS = ("ffn1_w13", "ffn1_w2", "w_in", "conv_pw", "w_out", "ffn2_w13", "ffn2_w2")
REPLICATED = ("ln_in_g", "ln_in_b", "pool_w", "pool_scale", "conv_db", "conv_ln_g", "conv_ln_b", "ret_gn_g")
WEIGHT_ORDER = ("meta", "ln_in_g", "ln_in_b", "ffn1_w13", "ffn1_w2", "w_in", "pool_w", "pool_scale", "conv_dw",
                "conv_db", "conv_ln_g", "conv_ln_b", "conv_pw", "ret_gn_g", "w_out", "ffn2_w13", "ffn2_w2",
                "ln_g", "ln_b")


def _retention_tables(lp, heads):
    pos = jnp.arange(lp, dtype=F32) - PAD_ROWS
    inv_freq = ROPE_BASE ** (-jnp.arange(0, LANE, 2, dtype=F32) / LANE)
    ang = pos[:, None] * inv_freq[None, :]
    cos, sin = jnp.cos(ang), jnp.sin(ang)
    rope_c = jnp.concatenate([cos, cos], axis=1)
    rope_s = jnp.concatenate([-sin, sin], axis=1)
    log_gamma = jnp.log(1.0 - 2.0 ** (-5.0 - jnp.arange(heads, dtype=F32)))
    i = jnp.arange(CHUNK, dtype=F32)
    dm = jnp.exp(log_gamma[:, None, None] * jnp.abs(i[:, None] - i[None, :]))
    lanes = lambda t: jnp.broadcast_to(t[:, :, None], t.shape + (LANE,))
    qd = lanes(jnp.exp(log_gamma[:, None] * (i + 1.0)))
    kd = lanes(jnp.exp(log_gamma[:, None] * (CHUNK - 1.0 - i)))
    cd = lanes(jnp.broadcast_to(jnp.exp(log_gamma * CHUNK)[:, None], (heads, 8)))
    return rope_c, rope_s, dm, qd, kd, cd


def _block_diag(w):
    g, n, _ = w.shape
    rows = []
    for i in range(g):
        rows.append(jnp.concatenate([w[i] if j == i else jnp.zeros((n, n), w.dtype) for j in range(g)], axis=1))
    return jnp.concatenate(rows, axis=0)


def kernel(x, meta, ln_in_g, ln_in_b, ffn1_w13, ffn1_w2, w_in, pool_w, pool_scale, conv_dw, conv_db, conv_ln_g, conv_ln_b, conv_pw, ret_gn_g, w_out, ffn2_w13, ffn2_w2, ln_g, ln_b, loss_target, m_meta, m_ln_in_g, m_ln_in_b, m_ffn1_w13, m_ffn1_w2, m_w_in, m_pool_w, m_pool_scale, m_conv_dw, m_conv_db, m_conv_ln_g, m_conv_ln_b, m_conv_pw, m_ret_gn_g, m_w_out, m_ffn2_w13, m_ffn2_w2, m_ln_g, m_ln_b, v_meta, v_ln_in_g, v_ln_in_b, v_ffn1_w13, v_ffn1_w2, v_w_in, v_pool_w, v_pool_scale, v_conv_dw, v_conv_db, v_conv_ln_g, v_conv_ln_b, v_conv_pw, v_ret_gn_g, v_w_out, v_ffn2_w13, v_ffn2_w2, v_ln_g, v_ln_b):
    local = dict(meta=meta, ln_in_g=ln_in_g, ln_in_b=ln_in_b, ffn1_w13=ffn1_w13, ffn1_w2=ffn1_w2, w_in=w_in,
                 pool_w=pool_w, pool_scale=pool_scale, conv_dw=conv_dw, conv_db=conv_db, conv_ln_g=conv_ln_g,
                 conv_ln_b=conv_ln_b, conv_pw=conv_pw, ret_gn_g=ret_gn_g, w_out=w_out, ffn2_w13=ffn2_w13,
                 ffn2_w2=ffn2_w2, ln_g=ln_g, ln_b=ln_b)
    mom1 = dict(meta=m_meta, ln_in_g=m_ln_in_g, ln_in_b=m_ln_in_b, ffn1_w13=m_ffn1_w13, ffn1_w2=m_ffn1_w2,
                w_in=m_w_in, pool_w=m_pool_w, pool_scale=m_pool_scale, conv_dw=m_conv_dw, conv_db=m_conv_db,
                conv_ln_g=m_conv_ln_g, conv_ln_b=m_conv_ln_b, conv_pw=m_conv_pw, ret_gn_g=m_ret_gn_g,
                w_out=m_w_out, ffn2_w13=m_ffn2_w13, ffn2_w2=m_ffn2_w2, ln_g=m_ln_g, ln_b=m_ln_b)
    mom2 = dict(meta=v_meta, ln_in_g=v_ln_in_g, ln_in_b=v_ln_in_b, ffn1_w13=v_ffn1_w13, ffn1_w2=v_ffn1_w2,
                w_in=v_w_in, pool_w=v_pool_w, pool_scale=v_pool_scale, conv_dw=v_conv_dw, conv_db=v_conv_db,
                conv_ln_g=v_conv_ln_g, conv_ln_b=v_conv_ln_b, conv_pw=v_conv_pw, ret_gn_g=v_ret_gn_g,
                w_out=v_w_out, ffn2_w13=v_ffn2_w13, ffn2_w2=v_ffn2_w2, ln_g=v_ln_g, ln_b=v_ln_b)

    depth = ffn1_w13.shape[0]
    alpha = (2.0 * depth) ** 0.25
    seq, d = x.shape[1], x.shape[2]
    lp = FRONT + seq
    h_loc = ffn1_w2.shape[1]
    hp = -(-h_loc // LANE) * LANE
    c_pool = pool_scale.shape[1]
    c_conv = conv_db.shape[1]
    q_blk = (c_pool + 2 * c_conv) // LANE
    dy_blk = (c_pool + c_conv) // LANE
    heads = ret_gn_g.shape[1] // LANE
    assert meta.shape[0] == N_META and heads == RET_HEADS and conv_dw.shape[1] == CONV_WIDTH
    assert ffn1_w13.shape[2] == 2 * h_loc

    def to_wire(name, w):
        if name in ("ffn1_w13", "ffn2_w13"):
            return _pad_halves(w, hp)
        if name in ("ffn1_w2", "ffn2_w2"):
            return _pad_to(w, w.ndim - 2, hp)
        return w

    def from_wire(name, w):
        if name in ("ffn1_w13", "ffn2_w13"):
            return _unpad_halves(w, h_loc)
        if name in ("ffn1_w2", "ffn2_w2"):
            return w[..., :h_loc, :]
        return w

    small_names = [n for n, _ in SMALL_SHARDED]
    small_ax = dict(SMALL_SHARDED)
    n_mm = len(MATMUL_WEIGHTS)
    first = ("ffn1_w13", "ffn1_w2")
    wire = lambda l, names: [to_wire(n, local[n][l]).astype(BF16) for n in names]
    rest = [n for n in MATMUL_WEIGHTS if n not in first]
    groups = [[_pack([local[n] for n in small_names])], wire(0, first), wire(0, rest)]
    groups += [wire(l, MATMUL_WEIGHTS) for l in range(1, depth)]
    handles, token = [], None
    for gi, arrs in enumerate(groups):
        handle, token = _exchange_start(arrs, False, f"gather_start_{gi}", after=token)
        handles.append(handle)
    got = _exchange_wait(handles[0], token, "gather_wait_0")
    small_parts = _unpack(got[0], [local[n].shape for n in small_names], 1)
    small_full = {n: _unshard(p, small_ax[n]) for n, p in zip(small_names, small_parts)}

    row = lambda v: v.reshape(1, -1)
    ln_params = [(row(ln_in_g), row(ln_in_b))]
    for l in range(depth):
        ln_params += [(row(small_full["ln_g"][l, j]), row(small_full["ln_b"][l, j])) for j in range(3)]
    rope_c, rope_s, dm, qd, kd, cd = _retention_tables(lp, heads)

    def layer_params(l, gw):
        return dict(
            ffn1=(gw["ffn1_w13"], gw["ffn1_w2"]),
            ffn2=(gw["ffn2_w13"], gw["ffn2_w2"]),
            w_in=_unshard(gw["w_in"], 1),
            w_out=gw["w_out"].reshape(-1, d),
            pool=(_block_diag(pool_w[l]).astype(BF16), row(pool_scale[l])),
            conv=(_pad_to(small_full["conv_dw"][l], 0, CONV_HALO), row(conv_db[l]), row(conv_ln_g[l]),
                  row(conv_ln_b[l]), gw["conv_pw"].reshape(c_conv, c_conv)),
            gn=row(ret_gn_g[l]),
        )

    r = [jnp.concatenate([jnp.zeros((PAD_ROWS, d), F32), small_full["meta"], x[0]], axis=0)]
    saved, layers = [], []
    for l in range(depth):
        k = 3 * l
        if l == 0:
            gw_first = dict(zip(first, _exchange_wait(handles[1], r[0], "gather_wait_1")))
            r1, a1, u1 = _ffn_fwd(r[k], *ln_params[k], gw_first["ffn1_w13"], gw_first["ffn1_w2"], alpha)
            gw = dict(gw_first, **dict(zip(rest, _exchange_wait(handles[2], r1, "gather_wait_2"))))
            p = layer_params(l, gw)
        else:
            gw = dict(zip(MATMUL_WEIGHTS, _exchange_wait(handles[l + 2], r[k], f"gather_wait_{l + 2}")))
            p = layer_params(l, gw)
            r1, a1, u1 = _ffn_fwd(r[k], *ln_params[k], *p["ffn1"], alpha)
        layers.append(p)
        z = _proj_fwd(r1, *ln_params[k + 1], p["w_in"])
        y_pool = _pool_fwd(z, *p["pool"])
        y_conv, conv_acc = _conv_fwd(z, *p["conv"])
        y_ret, states = _ret_fwd(z, rope_c, rope_s, dm, qd, kd, cd, p["gn"], q_blk)
        cat = jnp.concatenate([y_pool, y_conv, y_ret], axis=1)
        r2 = _out_fwd(r1, *ln_params[k + 1], cat, p["w_out"], alpha)
        r3, a2, u2 = _ffn_fwd(r2, *ln_params[k + 2], *p["ffn2"], alpha)
        r += [r1, r2, r3]
        saved.append((a1, u1, z, states, cat, a2, u2, conv_acc))

    target = jnp.concatenate([jnp.zeros((FRONT, d), F32), loss_target[0]], axis=0)
    dr, dg, db, loss_cols = _loss_bwd(r[-1], target, *ln_params[-1])
    loss = lax.psum(0.5 * jnp.sum(loss_cols) / d, MESH_AXES)
    ln_grads = {3 * depth: (dg, db)}
    w13_slot = lambda j: (j % 2) * (N_DEV // 2) + j // 2
    g_parts = [None] * depth
    g_rep = [None] * depth
    scatter_jobs = []

    def start_scatter(l, gp, names, extra=()):
        parts = [gp[n].astype(WIRE) for n in names] + list(extra)
        handle, token = _exchange_start(parts, True, f"scatter_start_{len(scatter_jobs)}")
        scatter_jobs.append((handle, l, names))
        return token

    def ffn_grads(dr_out, r_in, a, u, ln_p, wts, scatter_w13_of=None):
        dr_in, hb, s, dz, dg, db = _ffn_bwd(dr_out, r_in, a, u, *ln_p, *wts, alpha)
        g13 = _mm_tn(hb, dz, bn=2 * hp, slot_of=w13_slot)
        if scatter_w13_of is not None:
            start_scatter(scatter_w13_of, {"ffn1_w13": g13}, ("ffn1_w13",))
        g2 = _mm_tn(s, dr_out, 0.5).reshape(N_DEV, hp, d)
        return dr_in, (dg, db), g13, g2

    for l in reversed(range(depth)):
        p = layers[l]
        k = 3 * l
        a1, u1, z, states, cat, a2, u2, conv_acc = saved[l]
        gp, gr = {}, {}
        dr, ln_grads[k + 2], gp["ffn2_w13"], gp["ffn2_w2"] = ffn_grads(dr, r[k + 2], a2, u2, ln_params[k + 2], p["ffn2"])
        zero = start_scatter(l, gp, ("ffn2_w13", "ffn2_w2"))[0:1, 0:1]

        dr2 = dr
        dcat = _mm_nt(dr2, p["w_out"])
        gp["w_out"] = _mm_tn(cat, dr2).reshape(N_DEV, -1, d)
        dxp, dwbd, gr["pool_scale"] = _pool_bwd(z, dcat, p["pool"][0], p["pool"][1] + zero)
        dca, dcg, ddw, gr["conv_db"], gr["conv_ln_g"], gr["conv_ln_b"], dpw = _conv_bwd(
            z, dcat, conv_acc, p["conv"][0], p["conv"][2], p["conv"][3], p["conv"][4])
        dq, dk, dv, dgt, gr["ret_gn_g"] = _ret_bwd(z, dcat, states, rope_c, rope_s, dm, qd, kd, cd, p["gn"],
                                                  q_blk, dy_blk)
        dz = jnp.concatenate([dxp, dca, dcg, dq, dk, dv, dgt], axis=1)
        dr, hb, dg, db = _in_bwd(dr2, r[k + 1], dz, *ln_params[k + 1], p["w_in"], alpha)
        ln_grads[k + 1] = (dg, db)
        gp["w_in"] = _to_shards(_mm_tn(hb, dz), 1)
        gp["conv_pw"] = dpw.reshape(N_DEV, -1, c_conv)
        n_grp, grp = pool_w.shape[1], pool_w.shape[2]
        gr["pool_w"] = jnp.stack([dwbd[i * grp:(i + 1) * grp, i * grp:(i + 1) * grp] for i in range(n_grp)])
        gr["conv_dw"] = ddw[:CONV_WIDTH]

        zero = start_scatter(l, gp, ("w_in", "conv_pw", "w_out"))[0:1, 0:1]
        ln_g_in, ln_b_in = ln_params[k]
        dr, ln_grads[k], gp["ffn1_w13"], gp["ffn1_w2"] = ffn_grads(
            dr, r[k], a1, u1, (ln_g_in + zero, ln_b_in), p["ffn1"], scatter_w13_of=0 if l == 0 else None)
        g_parts[l], g_rep[l] = gp, gr
        if l > 0:
            zero = start_scatter(l, gp, ("ffn1_w13", "ffn1_w2"))[0:1, 0:1]
            g_prev, b_prev = ln_params[k - 1]
            ln_params[k - 1] = (g_prev + zero, b_prev)

    grad_x = dr[FRONT:][None]
    local_shape = lambda name: ((CONV_WIDTH, c_conv) if name == "conv_dw" else local[name].shape[1:])
    stack_layers = lambda name: jnp.stack([g_rep[l][name].reshape(local_shape(name)) for l in range(depth)])
    small_grad = dict(
        meta=dr[PAD_ROWS:FRONT],
        conv_dw=stack_layers("conv_dw"),
        ln_g=jnp.stack([jnp.stack([ln_grads[3 * l + j + 1][0][0] for j in range(3)]) for l in range(depth)]),
        ln_b=jnp.stack([jnp.stack([ln_grads[3 * l + j + 1][1][0] for j in range(3)]) for l in range(depth)]),
    )
    rep_grad = {n: stack_layers(n) for n in REPLICATED if n not in ("ln_in_g", "ln_in_b")}
    rep_grad["ln_in_g"], rep_grad["ln_in_b"] = ln_grads[0][0][0], ln_grads[0][1][0]

    small_pack8 = _pack([_to_shards(small_grad[n], small_ax[n]) for n in small_names], 1)
    after = start_scatter(0, g_parts[0], ("ffn1_w2",), extra=[small_pack8])
    rep_parts = _exchange([_pack([rep_grad[n] for n in REPLICATED])], False)[0]
    scattered, outs_by_name = {}, {}

    def update(names):
        nonlocal after
        for n in names:
            res = _adamw([scattered[l, n] for l in range(depth)], *[to_wire(n, src[n]) for src in (local, mom1, mom2)])
            outs_by_name[n] = [from_wire(n, t) for t in res]
            after = res[0]

    for ji, (handle, l, names) in enumerate(scatter_jobs):
        if ji == len(scatter_jobs) - 1:
            update([n for n in MATMUL_WEIGHTS if n not in names])
        got = _exchange_wait(handle, after, f"scatter_wait_{ji}")
        scattered.update({(l, n): t for n, t in zip(names, got)})
        after = got[0]
    small_scattered = got[-1]
    update(names)
    for names, parts in ((small_names, small_scattered), (REPLICATED, rep_parts)):
        shapes = [local[n].shape for n in names]
        res = _adamw([parts], *[_pack([src[n] for n in names])[None] for src in (local, mom1, mom2)])
        for n, vals in zip(names, zip(*[_unpack(t[0], shapes) for t in res])):
            outs_by_name[n] = list(vals)

    outs = [loss, grad_x]
    for kind in range(4):
        outs += [outs_by_name[n][kind] for n in WEIGHT_ORDER]
    return tuple(outs)
```

```python
import math

import jax
import jax.numpy as jnp
from jax import lax
from jax.experimental import pallas as pl
from jax.experimental.pallas import tpu as pltpu

F32 = jnp.float32
BF16 = jnp.bfloat16
WIRE = jnp.bfloat16

N_DEV = 8
MESH_AXES = ("x", "y", "c")
CHUNK = 64
N_META = 16
PAD_ROWS = 240
FRONT = PAD_ROWS + N_META
LN_EPS = 1e-5
LANE = 128
POOL_WINDOWS = (2, 4, 8, 16)
CONV_WIDTH = 31
CONV_HALO = 32
POOL_HALO = 16
RET_HEADS = 4
ROPE_BASE = 10000.0

ADAM_LR = 0.001
ADAM_B1 = 0.9
ADAM_B2 = 0.999
ADAM_EPS = 1e-08
ADAM_WD = 0.01
ADAM_STEP = 10

VMEM_LIMIT = 56 * 1024 * 1024
DENSE_TM_FWD = 768
DENSE_TM_BWD = 528
MIX_TM = 768
FFN_ROW_PARTS = 2
MM_TN_TK = 1408


def _cparams(sem):
    return pltpu.CompilerParams(dimension_semantics=sem, vmem_limit_bytes=VMEM_LIMIT)


def _pick_tm(lp, target, mult=16):
    best = None
    for t in range(mult, min(lp, target) + 1, mult):
        if lp % t == 0:
            best = t
    assert best is not None, (lp, target, mult)
    return best


def _row_parts(tm, n):
    step = -(-tm // (16 * n)) * 16
    return [slice(r0, min(r0 + step, tm)) for r0 in range(0, tm, step)]


def _dot(a, b):
    return jnp.dot(a, b, preferred_element_type=F32)


def _dot_nt(a, b):
    return lax.dot_general(a, b, (((1,), (1,)), ((), ())), preferred_element_type=F32)


def _dot_tn(a, b):
    return lax.dot_general(a, b, (((0,), (0,)), ((), ())), preferred_element_type=F32)


def _sigmoid(x):
    return 0.5 * jnp.tanh(0.5 * x) + 0.5


def _ln_stats(r):
    mu = jnp.mean(r, axis=-1, keepdims=True)
    xc = r - mu
    var = jnp.mean(xc * xc, axis=-1, keepdims=True)
    rstd = lax.rsqrt(var + LN_EPS)
    return xc * rstd, rstd


def _ln_bwd(dh, xhat, rstd, g):
    dxh = dh * g
    m1 = jnp.mean(dxh, axis=-1, keepdims=True)
    m2 = jnp.mean(dxh * xhat, axis=-1, keepdims=True)
    return rstd * (dxh - m1 - xhat * m2)


def _colsum(x):
    return jnp.sum(x, axis=0, keepdims=True)


def _row_ids(tile, tm):
    return lax.broadcasted_iota(jnp.int32, (tm, 1), 0) + tile * tm


def _ffn_fwd(r_prev, g, b, w13g, w2g, alpha):
    lp, d = r_prev.shape
    nf = N_DEV // 2
    fc = w13g.shape[2]
    w2c = w2g.reshape(nf, fc, d)
    tm = _pick_tm(lp, DENSE_TM_FWD)

    def body(r_ref, g_ref, b_ref, w1_ref, w3_ref, w2_ref, out_ref, a_ref, u_ref, hb, acc):
        f = pl.program_id(1)

        @pl.when(f == 0)
        def _():
            xhat, _ = _ln_stats(r_ref[...])
            hb[...] = (xhat * g_ref[...] + b_ref[...]).astype(BF16)
            acc[...] = jnp.zeros_like(acc)

        h = hb[...]
        a = _dot(h, w1_ref[...])
        u = _dot(h, w3_ref[...])
        a_ref[...] = a.astype(BF16)
        u_ref[...] = u.astype(BF16)
        s = a * _sigmoid(a) * u
        acc[...] += _dot(s.astype(BF16), w2_ref[...])

        @pl.when(f == nf - 1)
        def _():
            xhat, _ = _ln_stats(r_ref[...])
            out_ref[...] = alpha * (xhat * g_ref[...] + b_ref[...]) + 0.5 * acc[...]

    return pl.pallas_call(
        body,
        grid=(lp // tm, nf),
        in_specs=[
            pl.BlockSpec((tm, d), lambda i, f: (i, 0)),
            pl.BlockSpec((1, d), lambda i, f: (0, 0)),
            pl.BlockSpec((1, d), lambda i, f: (0, 0)),
            pl.BlockSpec((None, d, fc), lambda i, f: (f, 0, 0)),
            pl.BlockSpec((None, d, fc), lambda i, f: (nf + f, 0, 0)),
            pl.BlockSpec((None, fc, d), lambda i, f: (f, 0, 0)),
        ],
        out_specs=[
            pl.BlockSpec((tm, d), lambda i, f: (i, 0)),
            pl.BlockSpec((tm, fc), lambda i, f: (i, f)),
            pl.BlockSpec((tm, fc), lambda i, f: (i, f)),
        ],
        out_shape=[
            jax.ShapeDtypeStruct((lp, d), F32),
            jax.ShapeDtypeStruct((lp, nf * fc), BF16),
            jax.ShapeDtypeStruct((lp, nf * fc), BF16),
        ],
        scratch_shapes=[pltpu.VMEM((tm, d), BF16), pltpu.VMEM((tm, d), F32)],
        compiler_params=_cparams(("parallel", "arbitrary")),
        name="ffn_fwd",
    )(r_prev, g, b, w13g, w13g, w2c)


def _ffn_bwd(dr_next, r_prev, a, u, g, b, w13g, w2g, alpha):
    lp, d = r_prev.shape
    nf = N_DEV // 2
    fc = w13g.shape[2]
    w2c = w2g.reshape(nf, fc, d)
    tm = _pick_tm(lp, DENSE_TM_BWD)
    parts = _row_parts(tm, FFN_ROW_PARTS)

    def body(dr_ref, r_ref, a_ref, u_ref, g_ref, b_ref, w1_ref, w3_ref, w2_ref,
             drp_ref, hb_ref, s_ref, dz_ref, dg_ref, db_ref, dyb, dhacc):
        i = pl.program_id(0)
        f = pl.program_id(1)

        @pl.when(jnp.logical_and(i == 0, f == 0))
        def _():
            dg_ref[...] = jnp.zeros_like(dg_ref)
            db_ref[...] = jnp.zeros_like(db_ref)

        @pl.when(f == 0)
        def _():
            dyb[...] = (0.5 * dr_ref[...]).astype(BF16)
            dhacc[...] = jnp.zeros_like(dhacc)
            xhat, _ = _ln_stats(r_ref[...])
            hb_ref[...] = (xhat * g_ref[...] + b_ref[...]).astype(BF16)

        ds = [_dot_nt(dyb[rows, :], w2_ref[...]) for rows in parts]
        da, du = [], []
        for rows, dsv in zip(parts, ds):
            av = a_ref[rows, :].astype(F32)
            uv = u_ref[rows, :].astype(F32)
            sig = _sigmoid(av)
            sl = av * sig
            da.append((dsv * uv * (sig * (1.0 + av * (1.0 - sig)))).astype(BF16))
            du.append((dsv * sl).astype(BF16))
            s_ref[rows, :] = (sl * uv).astype(BF16)
        dh = [_dot_nt(da[p], w1_ref[...]) + _dot_nt(du[p], w3_ref[...]) for p in range(len(parts))]
        for p, rows in enumerate(parts):
            dz_ref[rows, 0:fc] = da[p]
            dz_ref[rows, fc:2 * fc] = du[p]
            dhacc[rows, :] += dh[p]

        @pl.when(f == nf - 1)
        def _():
            dh = alpha * dr_ref[...] + dhacc[...]
            xhat, rstd = _ln_stats(r_ref[...])
            drp_ref[...] = _ln_bwd(dh, xhat, rstd, g_ref[...])
            dg_ref[...] += _colsum(dh * xhat)
            db_ref[...] += _colsum(dh)

    row = lambda i, f: (i, 0)
    const = lambda i, f: (0, 0)
    chunk = lambda i, f: (i, f)
    return pl.pallas_call(
        body,
        grid=(lp // tm, nf),
        in_specs=[
            pl.BlockSpec((tm, d), row),
            pl.BlockSpec((tm, d), row),
            pl.BlockSpec((tm, fc), chunk),
            pl.BlockSpec((tm, fc), chunk),
            pl.BlockSpec((1, d), const),
            pl.BlockSpec((1, d), const),
            pl.BlockSpec((None, d, fc), lambda i, f: (f, 0, 0)),
            pl.BlockSpec((None, d, fc), lambda i, f: (nf + f, 0, 0)),
            pl.BlockSpec((None, fc, d), lambda i, f: (f, 0, 0)),
        ],
        out_specs=[
            pl.BlockSpec((tm, d), row),
            pl.BlockSpec((tm, d), row),
            pl.BlockSpec((tm, fc), chunk),
            pl.BlockSpec((tm, 2 * fc), chunk),
            pl.BlockSpec((1, d), const),
            pl.BlockSpec((1, d), const),
        ],
        out_shape=[
            jax.ShapeDtypeStruct((lp, d), F32),
            jax.ShapeDtypeStruct((lp, d), BF16),
            jax.ShapeDtypeStruct((lp, nf * fc), BF16),
            jax.ShapeDtypeStruct((lp, 2 * nf * fc), BF16),
            jax.ShapeDtypeStruct((1, d), F32),
            jax.ShapeDtypeStruct((1, d), F32),
        ],
        scratch_shapes=[pltpu.VMEM((tm, d), BF16), pltpu.VMEM((tm, d), F32)],
        compiler_params=_cparams(("arbitrary", "arbitrary")),
        name="ffn_bwd",
    )(dr_next, r_prev, a, u, g, b, w13g, w13g, w2c)


def _mm_tn(a, b, scale=1.0, bn=None, slot_of=None):
    t, m = a.shape
    n = b.shape[1]
    if bn is None:
        bn = n if n <= 1408 else _pick_tm(n, 1408, LANE)
    bm = m if m <= 1536 else _pick_tm(m, 1536, LANE)
    tk = _pick_tm(t, MM_TN_TK)
    nt = t // tk

    def body(a_ref, b_ref, o_ref, acc):
        k = pl.program_id(2)

        @pl.when(k == 0)
        def _():
            acc[...] = jnp.zeros_like(acc)

        acc[...] += _dot_tn(a_ref[...].astype(BF16), b_ref[...].astype(BF16))

        @pl.when(k == nt - 1)
        def _():
            o_ref[...] = (acc[...] * scale).astype(o_ref.dtype)

    if slot_of is None:
        out_spec = pl.BlockSpec((bm, bn), lambda i, j, k: (i, j))
        out_shape = jax.ShapeDtypeStruct((m, n), WIRE)
    else:
        out_spec = pl.BlockSpec((None, bm, bn), lambda i, j, k: (slot_of(j), i, 0))
        out_shape = jax.ShapeDtypeStruct((n // bn, m, bn), WIRE)
    return pl.pallas_call(
        body,
        grid=(m // bm, n // bn, nt),
        in_specs=[
            pl.BlockSpec((tk, bm), lambda i, j, k: (k, i)),
            pl.BlockSpec((tk, bn), lambda i, j, k: (k, j)),
        ],
        out_specs=out_spec,
        out_shape=out_shape,
        scratch_shapes=[pltpu.VMEM((bm, bn), F32)],
        compiler_params=_cparams(("parallel", "parallel", "arbitrary")),
        name="mm_tn",
    )(a, b)


def _proj_fwd(r_prev, g, b, w_in):
    lp, d = r_prev.shape
    n = w_in.shape[1]
    tm = _pick_tm(lp, DENSE_TM_FWD)

    def body(r_ref, g_ref, b_ref, w_ref, z_ref):
        xhat, _ = _ln_stats(r_ref[...])
        h = (xhat * g_ref[...] + b_ref[...]).astype(BF16)
        z = _dot(h, w_ref[...])
        rows = _row_ids(pl.program_id(0), tm)
        z_ref[...] = jnp.where(rows >= PAD_ROWS, z, 0.0)

    return pl.pallas_call(
        body,
        grid=(lp // tm,),
        in_specs=[
            pl.BlockSpec((tm, d), lambda i: (i, 0)),
            pl.BlockSpec((1, d), lambda i: (0, 0)),
            pl.BlockSpec((1, d), lambda i: (0, 0)),
            pl.BlockSpec((d, n), lambda i: (0, 0)),
        ],
        out_specs=pl.BlockSpec((tm, n), lambda i: (i, 0)),
        out_shape=jax.ShapeDtypeStruct((lp, n), F32),
        compiler_params=_cparams(("parallel",)),
        name="proj_fwd",
    )(r_prev, g, b, w_in)


def _out_fwd(r_prev, g, b, cat, w_out, alpha):
    lp, d = r_prev.shape
    k = cat.shape[1]
    tm = _pick_tm(lp, DENSE_TM_FWD)

    def body(r_ref, g_ref, b_ref, c_ref, w_ref, o_ref):
        xhat, _ = _ln_stats(r_ref[...])
        o_ref[...] = alpha * (xhat * g_ref[...] + b_ref[...]) + _dot(c_ref[...], w_ref[...])

    return pl.pallas_call(
        body,
        grid=(lp // tm,),
        in_specs=[
            pl.BlockSpec((tm, d), lambda i: (i, 0)),
            pl.BlockSpec((1, d), lambda i: (0, 0)),
            pl.BlockSpec((1, d), lambda i: (0, 0)),
            pl.BlockSpec((tm, k), lambda i: (i, 0)),
            pl.BlockSpec((k, d), lambda i: (0, 0)),
        ],
        out_specs=pl.BlockSpec((tm, d), lambda i: (i, 0)),
        out_shape=jax.ShapeDtypeStruct((lp, d), F32),
        compiler_params=_cparams(("parallel",)),
        name="out_fwd",
    )(r_prev, g, b, cat, w_out)


def _mm_nt(x, w):
    lp, n = x.shape
    k = w.shape[0]
    tm = _pick_tm(lp, DENSE_TM_FWD)

    def body(x_ref, w_ref, o_ref):
        o_ref[...] = _dot_nt(x_ref[...].astype(BF16), w_ref[...])

    return pl.pallas_call(
        body,
        grid=(lp // tm,),
        in_specs=[pl.BlockSpec((tm, n), lambda i: (i, 0)), pl.BlockSpec((k, n), lambda i: (0, 0))],
        out_specs=pl.BlockSpec((tm, k), lambda i: (i, 0)),
        out_shape=jax.ShapeDtypeStruct((lp, k), F32),
        compiler_params=_cparams(("parallel",)),
        name="mm_nt",
    )(x, w)


def _in_bwd(dr_next, r_prev, dz, g, b, w_in, alpha):
    lp, d = r_prev.shape
    n = w_in.shape[1]
    tm = _pick_tm(lp, DENSE_TM_FWD)

    def body(dr_ref, r_ref, dz_ref, g_ref, b_ref, w_ref, drp_ref, hb_ref, dg_ref, db_ref):
        @pl.when(pl.program_id(0) == 0)
        def _():
            dg_ref[...] = jnp.zeros_like(dg_ref)
            db_ref[...] = jnp.zeros_like(db_ref)

        dh = alpha * dr_ref[...] + _dot_nt(dz_ref[...], w_ref[...])
        xhat, rstd = _ln_stats(r_ref[...])
        hb_ref[...] = (xhat * g_ref[...] + b_ref[...]).astype(BF16)
        drp_ref[...] = _ln_bwd(dh, xhat, rstd, g_ref[...])
        dg_ref[...] += _colsum(dh * xhat)
        db_ref[...] += _colsum(dh)

    row = lambda i: (i, 0)
    const = lambda i: (0, 0)
    return pl.pallas_call(
        body,
        grid=(lp // tm,),
        in_specs=[
            pl.BlockSpec((tm, d), row),
            pl.BlockSpec((tm, d), row),
            pl.BlockSpec((tm, n), row),
            pl.BlockSpec((1, d), const),
            pl.BlockSpec((1, d), const),
            pl.BlockSpec((d, n), const),
        ],
        out_specs=[
            pl.BlockSpec((tm, d), row),
            pl.BlockSpec((tm, d), row),
            pl.BlockSpec((1, d), const),
            pl.BlockSpec((1, d), const),
        ],
        out_shape=[
            jax.ShapeDtypeStruct((lp, d), F32),
            jax.ShapeDtypeStruct((lp, d), BF16),
            jax.ShapeDtypeStruct((1, d), F32),
            jax.ShapeDtypeStruct((1, d), F32),
        ],
        compiler_params=_cparams(("arbitrary",)),
        name="in_bwd",
    )(dr_next, r_prev, dz, g, b, w_in)


def _loss_bwd(r_last, target, g, b):
    lp, d = r_last.shape
    tm = _pick_tm(lp, DENSE_TM_FWD)

    def body(r_ref, t_ref, g_ref, b_ref, dr_ref, dg_ref, db_ref, ls_ref):
        i = pl.program_id(0)

        @pl.when(i == 0)
        def _():
            dg_ref[...] = jnp.zeros_like(dg_ref)
            db_ref[...] = jnp.zeros_like(db_ref)
            ls_ref[...] = jnp.zeros_like(ls_ref)

        xhat, rstd = _ln_stats(r_ref[...])
        y = xhat * g_ref[...] + b_ref[...]
        rows = _row_ids(i, tm)
        err = jnp.where(rows >= FRONT, y - t_ref[...], 0.0)
        ls_ref[...] += _colsum(err * err)
        dy = err * (1.0 / d)
        dr_ref[...] = _ln_bwd(dy, xhat, rstd, g_ref[...])
        dg_ref[...] += _colsum(dy * xhat)
        db_ref[...] += _colsum(dy)

    row = lambda i: (i, 0)
    const = lambda i: (0, 0)
    return pl.pallas_call(
        body,
        grid=(lp // tm,),
        in_specs=[
            pl.BlockSpec((tm, d), row),
            pl.BlockSpec((tm, d), row),
            pl.BlockSpec((1, d), const),
            pl.BlockSpec((1, d), const),
        ],
        out_specs=[
            pl.BlockSpec((tm, d), row),
            pl.BlockSpec((1, d), const),
            pl.BlockSpec((1, d), const),
            pl.BlockSpec((1, d), const),
        ],
        out_shape=[
            jax.ShapeDtypeStruct((lp, d), F32),
            jax.ShapeDtypeStruct((1, d), F32),
            jax.ShapeDtypeStruct((1, d), F32),
            jax.ShapeDtypeStruct((1, d), F32),
        ],
        compiler_params=_cparams(("arbitrary",)),
        name="loss_bwd",
    )(r_last, target, g, b)


def _pool_counts(tile, tm, width):
    pos = _row_ids(tile, tm) - PAD_ROWS
    lane = lax.broadcasted_iota(jnp.int32, (1, width), 1)
    group = width // len(POOL_WINDOWS)
    win = jnp.full((1, width), POOL_WINDOWS[-1], jnp.int32)
    for gi in range(len(POOL_WINDOWS) - 2, -1, -1):
        win = jnp.where(lane < (gi + 1) * group, POOL_WINDOWS[gi], win)
    cnt = jnp.clip(pos + 1, 1, win)
    return cnt.astype(F32), lane, group


def _pool_select(sums, lane, group):
    out = sums[-1]
    for gi in range(len(POOL_WINDOWS) - 2, -1, -1):
        out = jnp.where(lane < (gi + 1) * group, sums[gi], out)
    return out


def _pool_window_sums(ext, tm, sign):
    base = POOL_HALO if sign < 0 else 0
    acc = ext[pl.ds(base, tm), :]
    sums, k = [], 1
    for w in POOL_WINDOWS:
        while k < w:
            acc = acc + ext[pl.ds(base + sign * k, tm), :]
            k += 1
        sums.append(acc)
    return sums


def _pool_fwd(z, wbd, scale):
    lp = z.shape[0]
    c = wbd.shape[0]
    tm = _pick_tm(lp, MIX_TM, CHUNK)

    def body(x_ref, w_ref, s_ref, y_ref, ext):
        i = pl.program_id(0)

        @pl.when(i == 0)
        def _():
            ext[0:POOL_HALO, :] = jnp.zeros((POOL_HALO, c), F32)

        x = x_ref[...]
        ext[POOL_HALO:POOL_HALO + tm, :] = x
        cnt, lane, group = _pool_counts(i, tm, c)
        sums = _pool_window_sums(ext, tm, -1)
        y = _pool_select(sums, lane, group) / cnt - x
        ext[0:POOL_HALO, :] = ext[tm:tm + POOL_HALO, :]
        y_ref[...] = (_dot(y.astype(BF16), w_ref[...]) * s_ref[...]).astype(BF16)

    return pl.pallas_call(
        body,
        grid=(lp // tm,),
        in_specs=[
            pl.BlockSpec((tm, c), lambda i: (i, 0)),
            pl.BlockSpec((c, c), lambda i: (0, 0)),
            pl.BlockSpec((1, c), lambda i: (0, 0)),
        ],
        out_specs=pl.BlockSpec((tm, c), lambda i: (i, 0)),
        out_shape=jax.ShapeDtypeStruct((lp, c), BF16),
        scratch_shapes=[pltpu.VMEM((tm + POOL_HALO, c), F32)],
        compiler_params=_cparams(("arbitrary",)),
        name="pool_fwd",
    )(z, wbd, scale)


def _pool_bwd(z, dcat, wbd, scale):
    lp = z.shape[0]
    c = wbd.shape[0]
    tm = _pick_tm(lp, MIX_TM, CHUNK)
    ni = lp // tm
    hpt = tm // POOL_HALO

    def body(x_ref, xh_ref, dy_ref, w_ref, s_ref, dx_ref, dw_ref, ds_ref, ext, ext2):
        i = pl.program_id(0)
        t = ni - 1 - i

        @pl.when(i == 0)
        def _():
            dw_ref[...] = jnp.zeros_like(dw_ref)
            ds_ref[...] = jnp.zeros_like(ds_ref)
            ext2[tm:tm + POOL_HALO, :] = jnp.zeros((POOL_HALO, c), F32)

        x = x_ref[...]
        ext[0:POOL_HALO, :] = jnp.where(t > 0, xh_ref[...], 0.0)
        ext[POOL_HALO:POOL_HALO + tm, :] = x
        cnt, lane, group = _pool_counts(t, tm, c)
        y = (_pool_select(_pool_window_sums(ext, tm, -1), lane, group) / cnt - x).astype(BF16)
        w = w_ref[...]
        dyv = dy_ref[...]
        ds_ref[...] += _colsum(_dot(y, w) * dyv)
        do = (dyv * s_ref[...]).astype(BF16)
        dw_ref[...] += _dot_tn(y, do)
        dyp = _dot_nt(do, w)
        ext2[0:tm, :] = dyp / cnt
        dx = _pool_select(_pool_window_sums(ext2, tm, 1), lane, group) - dyp
        ext2[tm:tm + POOL_HALO, :] = ext2[0:POOL_HALO, :]
        rows = _row_ids(t, tm)
        dx_ref[...] = jnp.where(rows >= PAD_ROWS, dx, 0.0).astype(BF16)

    rev = lambda i: (ni - 1 - i, 0)
    return pl.pallas_call(
        body,
        grid=(ni,),
        in_specs=[
            pl.BlockSpec((tm, c), rev),
            pl.BlockSpec((POOL_HALO, c), lambda i: (jnp.maximum((ni - 1 - i) * hpt - 1, 0), 0)),
            pl.BlockSpec((tm, c), rev),
            pl.BlockSpec((c, c), lambda i: (0, 0)),
            pl.BlockSpec((1, c), lambda i: (0, 0)),
        ],
        out_specs=[
            pl.BlockSpec((tm, c), rev),
            pl.BlockSpec((c, c), lambda i: (0, 0)),
            pl.BlockSpec((1, c), lambda i: (0, 0)),
        ],
        out_shape=[
            jax.ShapeDtypeStruct((lp, c), BF16),
            jax.ShapeDtypeStruct((c, c), F32),
            jax.ShapeDtypeStruct((1, c), F32),
        ],
        scratch_shapes=[pltpu.VMEM((tm + POOL_HALO, c), F32), pltpu.VMEM((tm + POOL_HALO, c), F32)],
        compiler_params=_cparams(("arbitrary",)),
        name="pool_bwd",
    )(z, z, dcat, wbd, scale)


def _conv_taps(ext, w_ref, tm, first):
    acc = w_ref[0:1, :] * ext[pl.ds(first, tm), :]
    for k in range(1, CONV_WIDTH):
        acc = acc + w_ref[k:k + 1, :] * ext[pl.ds(first + k, tm), :]
    return acc


def _conv_fwd(z, w_dw, b_dw, ln_g, ln_b, w_pw):
    lp = z.shape[0]
    c = w_pw.shape[0]
    tm = _pick_tm(lp, MIX_TM, CHUNK)

    def body(a_ref, gt_ref, w_ref, bd_ref, g_ref, b_ref, pw_ref, y_ref, acc_ref, ext):
        i = pl.program_id(0)

        @pl.when(i == 0)
        def _():
            ext[0:CONV_HALO, :] = jnp.zeros((CONV_HALO, c), F32)

        ext[CONV_HALO:CONV_HALO + tm, :] = a_ref[...] * _sigmoid(gt_ref[...])
        acc = _conv_taps(ext, w_ref, tm, CONV_HALO - CONV_WIDTH + 1) + bd_ref[...]
        acc_ref[...] = acc
        ext[0:CONV_HALO, :] = ext[tm:tm + CONV_HALO, :]
        xhat, _ = _ln_stats(acc)
        n = xhat * g_ref[...] + b_ref[...]
        act = n * _sigmoid(n)
        y_ref[...] = _dot(act.astype(BF16), pw_ref[...]).astype(BF16)

    const = lambda i: (0, 0)
    return pl.pallas_call(
        body,
        grid=(lp // tm,),
        in_specs=[
            pl.BlockSpec((tm, c), lambda i: (i, 1)),
            pl.BlockSpec((tm, c), lambda i: (i, 2)),
            pl.BlockSpec((CONV_HALO, c), const),
            pl.BlockSpec((1, c), const),
            pl.BlockSpec((1, c), const),
            pl.BlockSpec((1, c), const),
            pl.BlockSpec((c, c), const),
        ],
        out_specs=[pl.BlockSpec((tm, c), lambda i: (i, 0)), pl.BlockSpec((tm, c), lambda i: (i, 0))],
        out_shape=[jax.ShapeDtypeStruct((lp, c), BF16), jax.ShapeDtypeStruct((lp, c), F32)],
        scratch_shapes=[pltpu.VMEM((tm + CONV_HALO, c), F32)],
        compiler_params=_cparams(("arbitrary",)),
        name="conv_fwd",
    )(z, z, w_dw, b_dw, ln_g, ln_b, w_pw)


def _conv_bwd(z, dcat, acc_fwd, w_dw, ln_g, ln_b, w_pw):
    lp = z.shape[0]
    c = w_pw.shape[0]
    tm = _pick_tm(lp, MIX_TM, CHUNK)
    ni = lp // tm
    hpt = tm // CONV_HALO
    first = CONV_HALO - CONV_WIDTH + 1

    def body(a_ref, gt_ref, ah_ref, gh_ref, dy_ref, acc_ref, w_ref, g_ref, b_ref, pw_ref,
             dca_ref, dcg_ref, dw_ref, dbd_ref, dg_ref, db_ref, dpw_ref, extu, extd):
        i = pl.program_id(0)
        t = ni - 1 - i

        @pl.when(i == 0)
        def _():
            dw_ref[...] = jnp.zeros_like(dw_ref)
            dbd_ref[...] = jnp.zeros_like(dbd_ref)
            dg_ref[...] = jnp.zeros_like(dg_ref)
            db_ref[...] = jnp.zeros_like(db_ref)
            dpw_ref[...] = jnp.zeros_like(dpw_ref)
            extd[tm:tm + CONV_HALO, :] = jnp.zeros((CONV_HALO, c), F32)

        ca = a_ref[...]
        sg = _sigmoid(gt_ref[...])
        extu[0:CONV_HALO, :] = jnp.where(t > 0, ah_ref[...] * _sigmoid(gh_ref[...]), 0.0)
        extu[CONV_HALO:CONV_HALO + tm, :] = ca * sg
        xhat, rstd = _ln_stats(acc_ref[...])
        gam = g_ref[...]
        n = xhat * gam + b_ref[...]
        sn = _sigmoid(n)
        act = (n * sn).astype(BF16)
        do = dy_ref[...].astype(BF16)
        dpw_ref[...] += _dot_tn(act, do)
        dn = _dot_nt(do, pw_ref[...]) * (sn * (1.0 + n * (1.0 - sn)))
        dg_ref[...] += _colsum(dn * xhat)
        db_ref[...] += _colsum(dn)
        dyc = _ln_bwd(dn, xhat, rstd, gam)
        dbd_ref[...] += _colsum(dyc)
        extd[0:tm, :] = dyc
        du = None
        for k in range(CONV_WIDTH):
            dw_ref[k:k + 1, :] += _colsum(dyc * extu[pl.ds(first + k, tm), :])
            term = w_ref[k:k + 1, :] * extd[pl.ds(CONV_WIDTH - 1 - k, tm), :]
            du = term if du is None else du + term
        extd[tm:tm + CONV_HALO, :] = extd[0:CONV_HALO, :]
        du = jnp.where(_row_ids(t, tm) >= PAD_ROWS, du, 0.0)
        dca_ref[...] = (du * sg).astype(BF16)
        dcg_ref[...] = (du * ca * sg * (1.0 - sg)).astype(BF16)

    const = lambda i: (0, 0)
    rev = lambda col: (lambda i: (ni - 1 - i, col))
    halo = lambda col: (lambda i: (jnp.maximum((ni - 1 - i) * hpt - 1, 0), col))
    return pl.pallas_call(
        body,
        grid=(ni,),
        in_specs=[
            pl.BlockSpec((tm, c), rev(1)),
            pl.BlockSpec((tm, c), rev(2)),
            pl.BlockSpec((CONV_HALO, c), halo(1)),
            pl.BlockSpec((CONV_HALO, c), halo(2)),
            pl.BlockSpec((tm, c), rev(1)),
            pl.BlockSpec((tm, c), rev(0)),
            pl.BlockSpec((CONV_HALO, c), const),
            pl.BlockSpec((1, c), const),
            pl.BlockSpec((1, c), const),
            pl.BlockSpec((c, c), const),
        ],
        out_specs=[
            pl.BlockSpec((tm, c), rev(0)),
            pl.BlockSpec((tm, c), rev(0)),
            pl.BlockSpec((CONV_HALO, c), const),
            pl.BlockSpec((1, c), const),
            pl.BlockSpec((1, c), const),
            pl.BlockSpec((1, c), const),
            pl.BlockSpec((c, c), const),
        ],
        out_shape=[
            jax.ShapeDtypeStruct((lp, c), BF16),
            jax.ShapeDtypeStruct((lp, c), BF16),
            jax.ShapeDtypeStruct((CONV_HALO, c), F32),
            jax.ShapeDtypeStruct((1, c), F32),
            jax.ShapeDtypeStruct((1, c), F32),
            jax.ShapeDtypeStruct((1, c), F32),
            jax.ShapeDtypeStruct((c, c), F32),
        ],
        scratch_shapes=[pltpu.VMEM((tm + CONV_HALO, c), F32), pltpu.VMEM((tm + CONV_HALO, c), F32)],
        compiler_params=_cparams(("arbitrary",)),
        name="conv_bwd",
    )(z, z, z, z, dcat, acc_fwd, w_dw, ln_g, ln_b, w_pw)


def _rope(x, cos, sgn_sin):
    return x * cos + pltpu.roll(x, LANE // 2, 1) * sgn_sin


def _rope_t(dy, cos, sgn_sin):
    return dy * cos + pltpu.roll(dy * sgn_sin, LANE // 2, 1)


def _ret_chunk_fwd(q, k, v, s0, cos, sn, tabs):
    heads = range(len(q))
    bf = lambda xs: [x.astype(BF16) for x in xs]
    qr = [_rope(q[h], cos, sn) for h in heads]
    kr = [_rope(k[h], cos, sn) * (LANE ** -0.5) for h in heads]
    qrb, krb, vb, s0b = bf(qr), bf(kr), bf(v), bf(s0)
    amb = bf([_dot_nt(qrb[h], krb[h]) * tabs[h][0] for h in heads])
    qdb = bf([qr[h] * tabs[h][1] for h in heads])
    intra = [_dot(amb[h], vb[h]) for h in heads]
    cross = [_dot(qdb[h], s0b[h]) for h in heads]
    o = [intra[h] + cross[h] for h in heads]
    return qr, kr, qrb, krb, vb, s0b, amb, qdb, o


def _ret_specs(tm, q_blk, tile_of):
    def mk(col):
        return pl.BlockSpec((tm, LANE), lambda i: (tile_of(i), col))
    return [mk(q_blk + j) for j in range(4 * RET_HEADS)]


def _ret_tables(h, dm_ref, qd_ref, kd_ref, cd_ref, gn_ref):
    return dm_ref[h], qd_ref[h], kd_ref[h], cd_ref[h, 0:1, :], gn_ref[:, h * LANE:(h + 1) * LANE]


def _ret_fwd(z, rope_c, rope_s, dm, qd, kd, cd, gn, q_blk):
    lp = z.shape[0]
    tm = _pick_tm(lp, MIX_TM, CHUNK)
    nc = tm // CHUNK
    ni = lp // tm
    nh = RET_HEADS

    def body(*refs):
        q_refs, k_refs, v_refs, g_refs = refs[0:nh], refs[nh:2 * nh], refs[2 * nh:3 * nh], refs[3 * nh:4 * nh]
        c_ref, s_ref, dm_ref, qd_ref, kd_ref, cd_ref, gn_ref, y_ref, st_ref, state = refs[4 * nh:]

        @pl.when(pl.program_id(0) == 0)
        def _():
            state[...] = jnp.zeros_like(state)

        def chunk(c, carry):
            rows = pl.ds(pl.multiple_of(c * CHUNK, CHUNK), CHUNK)
            cos, sn = c_ref[rows, :], s_ref[rows, :]
            heads = range(nh)
            tabs = [_ret_tables(h, dm_ref, qd_ref, kd_ref, cd_ref, gn_ref) for h in heads]
            q = [q_refs[h][rows, :] for h in heads]
            k = [k_refs[h][rows, :] for h in heads]
            v = [v_refs[h][rows, :] for h in heads]
            gg = [g_refs[h][rows, :] for h in heads]
            s0 = [state[h] for h in heads]
            _, kr, _, _, vb, _, _, _, o = _ret_chunk_fwd(q, k, v, s0, cos, sn, tabs)
            kv = [_dot_tn((kr[h] * tabs[h][2]).astype(BF16), vb[h]) for h in heads]
            for h in heads:
                on, _ = _ln_stats(o[h])
                st_ref[c, h, :, :] = s0[h]
                state[h] = s0[h] * tabs[h][3] + kv[h]
                y_ref[rows, h * LANE:(h + 1) * LANE] = (gg[h] * _sigmoid(gg[h]) * (on * tabs[h][4])).astype(BF16)
            return carry

        lax.fori_loop(0, nc, chunk, 0, unroll=2)

    const3 = lambda s: pl.BlockSpec(s, lambda i: (0, 0, 0))
    return pl.pallas_call(
        body,
        grid=(ni,),
        in_specs=_ret_specs(tm, q_blk, lambda i: i) + [
            pl.BlockSpec((tm, LANE), lambda i: (i, 0)),
            pl.BlockSpec((tm, LANE), lambda i: (i, 0)),
            const3((nh, CHUNK, CHUNK)), const3((nh, CHUNK, LANE)), const3((nh, CHUNK, LANE)), const3((nh, 8, LANE)),
            pl.BlockSpec((1, nh * LANE), lambda i: (0, 0)),
        ],
        out_specs=[
            pl.BlockSpec((tm, nh * LANE), lambda i: (i, 0)),
            pl.BlockSpec((nc, nh, LANE, LANE), lambda i: (i, 0, 0, 0)),
        ],
        out_shape=[
            jax.ShapeDtypeStruct((lp, nh * LANE), BF16),
            jax.ShapeDtypeStruct((lp // CHUNK, nh, LANE, LANE), F32),
        ],
        scratch_shapes=[pltpu.VMEM((nh, LANE, LANE), F32)],
        compiler_params=_cparams(("arbitrary",)),
        name="ret_fwd",
    )(*([z] * (4 * nh)), rope_c, rope_s, dm, qd, kd, cd, gn)


def _ret_bwd(z, dcat, states, rope_c, rope_s, dm, qd, kd, cd, gn, q_blk, dy_blk):
    lp = z.shape[0]
    tm = _pick_tm(lp, MIX_TM, CHUNK)
    nc = tm // CHUNK
    ni = lp // tm

    nh = RET_HEADS

    def body(*refs):
        q_refs, k_refs, v_refs, g_refs = refs[0:nh], refs[nh:2 * nh], refs[2 * nh:3 * nh], refs[3 * nh:4 * nh]
        (dy_ref, st_ref, c_ref, s_ref, dm_ref, qd_ref, kd_ref, cd_ref, gn_ref,
         dq_ref, dk_ref, dv_ref, dgt_ref, dgn_ref, dstate) = refs[4 * nh:]
        i = pl.program_id(0)
        t = ni - 1 - i

        @pl.when(i == 0)
        def _():
            dstate[...] = jnp.zeros_like(dstate)
            dgn_ref[...] = jnp.zeros_like(dgn_ref)

        def chunk(cc, carry):
            c = nc - 1 - cc
            rows = pl.ds(pl.multiple_of(c * CHUNK, CHUNK), CHUNK)
            cos, sn = c_ref[rows, :], s_ref[rows, :]
            keep = (lax.broadcasted_iota(jnp.int32, (CHUNK, 1), 0) + (t * tm + c * CHUNK)) >= PAD_ROWS
            heads = range(nh)
            lanes = [slice(h * LANE, (h + 1) * LANE) for h in heads]
            tabs = [_ret_tables(h, dm_ref, qd_ref, kd_ref, cd_ref, gn_ref) for h in heads]
            q = [q_refs[h][rows, :] for h in heads]
            k = [k_refs[h][rows, :] for h in heads]
            v = [v_refs[h][rows, :] for h in heads]
            gg = [g_refs[h][rows, :] for h in heads]
            dyv = [dy_ref[rows, lanes[h]] for h in heads]
            s0 = [st_ref[c, h, :, :] for h in heads]
            ds1 = [dstate[h] for h in heads]
            dgn = [dgn_ref[:, lanes[h]] for h in heads]
            qr, kr, qrb, krb, vb, s0b, amb, qdb, o = _ret_chunk_fwd(q, k, v, s0, cos, sn, tabs)
            stats = [_ln_stats(o[h]) for h in heads]
            sg = [_sigmoid(gg[h]) for h in heads]
            sl = [gg[h] * sg[h] for h in heads]
            dgt = [dyv[h] * (stats[h][0] * tabs[h][4]) * (sg[h] * (1.0 + gg[h] * (1.0 - sg[h]))) for h in heads]
            dob = [_ln_bwd(dyv[h] * sl[h], stats[h][0], stats[h][1], tabs[h][4]).astype(BF16) for h in heads]
            ds1b = [ds1[h].astype(BF16) for h in heads]
            kdb = [(kr[h] * tabs[h][2]).astype(BF16) for h in heads]
            da = [(_dot_nt(dob[h], vb[h]) * tabs[h][0]).astype(BF16) for h in heads]
            dv_intra = [_dot_tn(amb[h], dob[h]) for h in heads]
            dv_cross = [_dot(kdb[h], ds1b[h]) for h in heads]
            dq_cross = [_dot_nt(dob[h], s0b[h]) for h in heads]
            dk_cross = [_dot_nt(vb[h], ds1b[h]) for h in heads]
            ds_inc = [_dot_tn(qdb[h], dob[h]) for h in heads]
            dq_intra = [_dot(da[h], krb[h]) for h in heads]
            dk_intra = [_dot_tn(da[h], qrb[h]) for h in heads]
            for h in heads:
                dqr = dq_intra[h] + dq_cross[h] * tabs[h][1]
                dkr = dk_intra[h] + dk_cross[h] * tabs[h][2]
                dq = _rope_t(dqr, cos, sn)
                dk = _rope_t(dkr * (LANE ** -0.5), cos, sn)
                dq_ref[rows, lanes[h]] = jnp.where(keep, dq, 0.0).astype(BF16)
                dk_ref[rows, lanes[h]] = jnp.where(keep, dk, 0.0).astype(BF16)
                dv_ref[rows, lanes[h]] = jnp.where(keep, dv_intra[h] + dv_cross[h], 0.0).astype(BF16)
                dgt_ref[rows, lanes[h]] = jnp.where(keep, dgt[h], 0.0).astype(BF16)
                dstate[h] = ds1[h] * tabs[h][3] + ds_inc[h]
                dgn_ref[:, lanes[h]] = dgn[h] + _colsum(dyv[h] * sl[h] * stats[h][0])
            return carry

        lax.fori_loop(0, nc, chunk, 0, unroll=2)

    rev = lambda i: ni - 1 - i
    const3 = lambda s: pl.BlockSpec(s, lambda i: (0, 0, 0))
    out_blk = pl.BlockSpec((tm, nh * LANE), lambda i: (rev(i), 0))
    out_sds = jax.ShapeDtypeStruct((lp, nh * LANE), BF16)
    return pl.pallas_call(
        body,
        grid=(ni,),
        in_specs=_ret_specs(tm, q_blk, rev) + [
            pl.BlockSpec((tm, nh * LANE), lambda i: (rev(i), dy_blk // nh)),
            pl.BlockSpec((nc, nh, LANE, LANE), lambda i: (rev(i), 0, 0, 0)),
            pl.BlockSpec((tm, LANE), lambda i: (rev(i), 0)),
            pl.BlockSpec((tm, LANE), lambda i: (rev(i), 0)),
            const3((nh, CHUNK, CHUNK)), const3((nh, CHUNK, LANE)), const3((nh, CHUNK, LANE)), const3((nh, 8, LANE)),
            pl.BlockSpec((1, nh * LANE), lambda i: (0, 0)),
        ],
        out_specs=[out_blk, out_blk, out_blk, out_blk, pl.BlockSpec((1, nh * LANE), lambda i: (0, 0))],
        out_shape=[out_sds, out_sds, out_sds, out_sds, jax.ShapeDtypeStruct((1, nh * LANE), F32)],
        scratch_shapes=[pltpu.VMEM((nh, LANE, LANE), F32)],
        compiler_params=_cparams(("arbitrary",)),
        name="ret_bwd",
    )(*([z] * (4 * nh)), dcat, states, rope_c, rope_s, dm, qd, kd, cd, gn)


def _adamw(parts_list, w, m, v):
    nl, r, c = w.shape
    assert len(parts_list) == nl
    tr = _pick_tm(r, max(8, (1 << 17) // c), 8)
    nr = r // tr

    def body(*refs):
        p_refs = refs[:nl]
        w_ref, m_ref, v_ref, g_ref, d_ref, nm_ref, nv_ref = refs[nl:]
        layer = pl.program_id(0)
        for k in range(nl):
            @pl.when(layer == k)
            def _(k=k):
                g = p_refs[k][0].astype(F32)
                for j in range(1, N_DEV):
                    g = g + p_refs[k][j].astype(F32)
                m1 = ADAM_B1 * m_ref[...] + (1.0 - ADAM_B1) * g
                v1 = ADAM_B2 * v_ref[...] + (1.0 - ADAM_B2) * (g * g)
                m_hat = m1 / (1.0 - ADAM_B1 ** ADAM_STEP)
                v_hat = v1 / (1.0 - ADAM_B2 ** ADAM_STEP)
                g_ref[...] = g
                d_ref[...] = -ADAM_LR * (m_hat / (jnp.sqrt(v_hat) + ADAM_EPS) + ADAM_WD * w_ref[...])
                nm_ref[...] = m1
                nv_ref[...] = v1

    def parts_spec(k):
        return pl.BlockSpec((N_DEV, tr, c), lambda l, i: (0, jnp.where(l == k, i, jnp.where(l < k, 0, nr - 1)), 0))

    blk = pl.BlockSpec((None, tr, c), lambda l, i: (l, i, 0))
    sds = jax.ShapeDtypeStruct((nl, r, c), F32)
    return pl.pallas_call(
        body,
        grid=(nl, nr),
        in_specs=[parts_spec(k) for k in range(nl)] + [blk, blk, blk],
        out_specs=[blk, blk, blk, blk],
        out_shape=[sds, sds, sds, sds],
        compiler_params=_cparams(("arbitrary", "arbitrary")),
        name="adamw",
    )(*parts_list, w, m, v)


def _flip(v, bit):
    return 1 - v if bit else v


def _exchange(arrs, scatter):
    n = len(arrs)
    shapes = [tuple(a.shape[1:] if scatter else a.shape) for a in arrs]

    def body(*refs):
        x_refs, o_refs = refs[:n], refs[n:2 * n]
        send_sems, recv_sems, local_sems = refs[2 * n:]
        mx, my, mc = lax.axis_index("x"), lax.axis_index("y"), lax.axis_index("c")
        me = 4 * mx + 2 * my + mc

        def peer_of(k):
            return (_flip(mx, (k >> 2) & 1), _flip(my, (k >> 1) & 1), _flip(mc, k & 1))

        def copy(a, k):
            peer = peer_of(k)
            src = x_refs[a].at[4 * peer[0] + 2 * peer[1] + peer[2]] if scatter else x_refs[a]
            return pltpu.make_async_remote_copy(
                src_ref=src, dst_ref=o_refs[a].at[me], send_sem=send_sems.at[a, k - 1],
                recv_sem=recv_sems.at[a, k - 1], device_id=peer, device_id_type=pl.DeviceIdType.MESH)

        def arrival(a, k):
            peer = peer_of(k)
            slot = o_refs[a].at[4 * peer[0] + 2 * peer[1] + peer[2]]
            return pltpu.make_async_remote_copy(
                src_ref=slot, dst_ref=slot, send_sem=send_sems.at[a, k - 1], recv_sem=recv_sems.at[a, k - 1],
                device_id=peer, device_id_type=pl.DeviceIdType.MESH)

        locals_ = [pltpu.make_async_copy(x_refs[a].at[me] if scatter else x_refs[a], o_refs[a].at[me],
                                         local_sems.at[a]) for a in range(n)]
        sends = [copy(a, k) for a in range(n) for k in range(1, N_DEV)]
        for cp in locals_ + sends:
            cp.start()
        for a in range(n):
            for k in range(1, N_DEV):
                arrival(a, k).wait_recv()
        for cp in sends:
            cp.wait_send()
        for cp in locals_:
            cp.wait()

    hbm = pl.BlockSpec(memory_space=pltpu.HBM)
    return pl.pallas_call(
        body,
        in_specs=[hbm] * n,
        out_specs=[hbm] * n,
        out_shape=[jax.ShapeDtypeStruct((N_DEV,) + s, a.dtype) for s, a in zip(shapes, arrs)],
        scratch_shapes=[
            pltpu.SemaphoreType.DMA((n, N_DEV - 1)),
            pltpu.SemaphoreType.DMA((n, N_DEV - 1)),
            pltpu.SemaphoreType.DMA((n,)),
        ],
        name="reduce_scatter_parts" if scatter else "all_gather",
    )(*arrs)


def _exchange_descriptors(x_refs, land_refs, send_sems, recv_sems, scatter):
    mx, my, mc = lax.axis_index("x"), lax.axis_index("y"), lax.axis_index("c")
    me = 4 * mx + 2 * my + mc
    sends, arrivals = [], []
    for a in range(len(x_refs)):
        for k in range(1, N_DEV):
            peer = (_flip(mx, (k >> 2) & 1), _flip(my, (k >> 1) & 1), _flip(mc, k & 1))
            slot = 4 * peer[0] + 2 * peer[1] + peer[2]
            si = a * (N_DEV - 1) + k - 1
            sems = dict(send_sem=send_sems.at[si], recv_sem=recv_sems.at[si],
                        device_id=peer, device_id_type=pl.DeviceIdType.MESH)
            sends.append(pltpu.make_async_remote_copy(
                src_ref=x_refs[a].at[slot] if scatter else x_refs[a], dst_ref=land_refs[a].at[me], **sems))
            arrivals.append(pltpu.make_async_remote_copy(
                src_ref=land_refs[a].at[slot], dst_ref=land_refs[a].at[slot], **sems))
    return sends, arrivals


def _own_slot_copies(x_refs, land_refs, local_sems, scatter):
    me = 4 * lax.axis_index("x") + 2 * lax.axis_index("y") + lax.axis_index("c")
    return [pltpu.make_async_copy(x_refs[a].at[me] if scatter else x_refs[a], land_refs[a].at[me], local_sems.at[a])
            for a in range(len(x_refs))]


def _exchange_start(arrs, scatter, name, after=None):
    n = len(arrs)
    shapes = [tuple(a.shape[1:] if scatter else a.shape) for a in arrs]
    lands = [lax.empty((N_DEV,) + s, a.dtype) for s, a in zip(shapes, arrs)]
    extra = [] if after is None else [after]

    def body(*refs):
        x_refs, land_refs = refs[:n], refs[n:2 * n]
        send_sems, recv_sems, local_sems = refs[2 * n + len(extra):2 * n + len(extra) + 3]
        token = refs[-1]
        sends, _ = _exchange_descriptors(x_refs, land_refs, send_sems, recv_sems, scatter)
        for cp in sends + _own_slot_copies(x_refs, land_refs, local_sems, scatter):
            cp.start()
        token[...] = jnp.zeros_like(token)

    hbm = pl.BlockSpec(memory_space=pltpu.HBM)
    sem = pl.BlockSpec(memory_space=pltpu.SEMAPHORE)
    sem_type = pltpu.SemaphoreType.DMA((n * (N_DEV - 1),))
    operands = [pltpu.with_memory_space_constraint(a, pltpu.HBM) for a in list(arrs) + lands]
    out = pl.pallas_call(
        body,
        in_specs=[hbm] * (2 * n) + [pl.BlockSpec(memory_space=pl.ANY)] * len(extra),
        out_specs=[sem, sem, sem] + [hbm] * (2 * n) + [pl.BlockSpec(memory_space=pltpu.VMEM)],
        out_shape=[sem_type, sem_type, pltpu.SemaphoreType.DMA((n,))] + [pltpu.HBM(a.shape, a.dtype) for a in operands]
        + [jax.ShapeDtypeStruct((8, LANE), F32)],
        input_output_aliases={i: 3 + i for i in range(2 * n)},
        compiler_params=pltpu.CompilerParams(has_side_effects=pltpu.SideEffectType.DATAFLOW_SIDE_EFFECTING),
        name=name,
    )(*operands, *extra)
    return (out[0:3], list(out[3:3 + n]), list(out[3 + n:3 + 2 * n]), scatter), out[-1]


def _exchange_wait(handle, after, name):
    sems, x_thru, land_thru, scatter = handle
    n = len(x_thru)

    def body(*refs):
        x_refs, land_refs = refs[:n], refs[n:2 * n]
        send_sems, recv_sems, local_sems = refs[2 * n:2 * n + 3]
        sends, arrivals = _exchange_descriptors(x_refs, land_refs, send_sems, recv_sems, scatter)
        for cp in sends:
            cp.wait_send()
        for cp in arrivals:
            cp.wait_recv()
        for cp in _own_slot_copies(x_refs, land_refs, local_sems, scatter):
            cp.wait()

    hbm = pl.BlockSpec(memory_space=pltpu.HBM)
    sem = pl.BlockSpec(memory_space=pltpu.SEMAPHORE)
    out = pl.pallas_call(
        body,
        in_specs=[hbm] * (2 * n) + [sem, sem, sem, pl.BlockSpec(memory_space=pl.ANY)],
        out_specs=[hbm] * (2 * n),
        out_shape=[pltpu.HBM(a.shape, a.dtype) for a in x_thru + land_thru],
        input_output_aliases={i: i for i in range(2 * n)},
        compiler_params=pltpu.CompilerParams(has_side_effects=pltpu.SideEffectType.DATAFLOW_SIDE_EFFECTING),
        name=name,
    )(*x_thru, *land_thru, *sems, after)
    return list(out[n:])


PACK_ALIGN = 2048


def _padded(n):
    return -(-n // PACK_ALIGN) * PACK_ALIGN


def _pad_to(a, axis, size):
    pad = [(0, 0)] * a.ndim
    pad[axis] = (0, size - a.shape[axis])
    return jnp.pad(a, pad)


def _pack(arrs, lead=0):
    flat = []
    for a in arrs:
        v = a.reshape(a.shape[:lead] + (-1,))
        flat.append(_pad_to(v, lead, _padded(v.shape[lead])))
    out = jnp.concatenate(flat, axis=lead)
    return out.reshape(out.shape[:lead] + (-1, LANE))


def _unpack(slab, shapes, lead=0):
    flat = slab.reshape(slab.shape[:lead] + (-1,))
    out, off = [], 0
    for s in shapes:
        n = math.prod(s)
        out.append(flat[..., off:off + n].reshape(slab.shape[:lead] + tuple(s)))
        off += _padded(n)
    return out


def _unshard(parts, ax):
    return jnp.concatenate([parts[j] for j in range(N_DEV)], axis=ax)


def _to_shards(full, ax):
    n = full.shape[ax] // N_DEV
    return jnp.stack([lax.slice_in_dim(full, j * n, (j + 1) * n, axis=ax) for j in range(N_DEV)])


def _pad_halves(w, hp):
    h = w.shape[-1] // 2
    zeros = jnp.zeros(w.shape[:-1] + (hp - h,), w.dtype)
    return jnp.concatenate([w[..., :h], zeros, w[..., h:], zeros], axis=-1)


def _unpad_halves(w, h):
    hp = w.shape[-1] // 2
    return jnp.concatenate([w[..., :h], w[..., hp:hp + h]], axis=-1)


SMALL_SHARDED = (("meta", 1), ("conv_dw", 2), ("ln_g", 2), ("ln_b", 2))
MATMUL_WEIGHTS = ("ffn1_w13", "ffn1_w2", "w_in", "conv_pw", "w_out", "ffn2_w13", "ffn2_w2")
REPLICATED = ("ln_in_g", "ln_in_b", "pool_w", "pool_scale", "conv_db", "conv_ln_g", "conv_ln_b", "ret_gn_g")
WEIGHT_ORDER = ("meta", "ln_in_g", "ln_in_b", "ffn1_w13", "ffn1_w2", "w_in", "pool_w", "pool_scale", "conv_dw",
                "conv_db", "conv_ln_g", "conv_ln_b", "conv_pw", "ret_gn_g", "w_out", "ffn2_w13", "ffn2_w2",
                "ln_g", "ln_b")


def _retention_tables(lp, heads):
    pos = jnp.arange(lp, dtype=F32) - PAD_ROWS
    inv_freq = ROPE_BASE ** (-jnp.arange(0, LANE, 2, dtype=F32) / LANE)
    ang = pos[:, None] * inv_freq[None, :]
    cos, sin = jnp.cos(ang), jnp.sin(ang)
    rope_c = jnp.concatenate([cos, cos], axis=1)
    rope_s = jnp.concatenate([-sin, sin], axis=1)
    log_gamma = jnp.log(1.0 - 2.0 ** (-5.0 - jnp.arange(heads, dtype=F32)))
    i = jnp.arange(CHUNK, dtype=F32)
    dm = jnp.exp(log_gamma[:, None, None] * jnp.abs(i[:, None] - i[None, :]))
    lanes = lambda t: jnp.broadcast_to(t[:, :, None], t.shape + (LANE,))
    qd = lanes(jnp.exp(log_gamma[:, None] * (i + 1.0)))
    kd = lanes(jnp.exp(log_gamma[:, None] * (CHUNK - 1.0 - i)))
    cd = lanes(jnp.broadcast_to(jnp.exp(log_gamma * CHUNK)[:, None], (heads, 8)))
    return rope_c, rope_s, dm, qd, kd, cd


def _block_diag(w):
    g, n, _ = w.shape
    rows = []
    for i in range(g):
        rows.append(jnp.concatenate([w[i] if j == i else jnp.zeros((n, n), w.dtype) for j in range(g)], axis=1))
    return jnp.concatenate(rows, axis=0)


def kernel(x, meta, ln_in_g, ln_in_b, ffn1_w13, ffn1_w2, w_in, pool_w, pool_scale, conv_dw, conv_db, conv_ln_g, conv_ln_b, conv_pw, ret_gn_g, w_out, ffn2_w13, ffn2_w2, ln_g, ln_b, loss_target, m_meta, m_ln_in_g, m_ln_in_b, m_ffn1_w13, m_ffn1_w2, m_w_in, m_pool_w, m_pool_scale, m_conv_dw, m_conv_db, m_conv_ln_g, m_conv_ln_b, m_conv_pw, m_ret_gn_g, m_w_out, m_ffn2_w13, m_ffn2_w2, m_ln_g, m_ln_b, v_meta, v_ln_in_g, v_ln_in_b, v_ffn1_w13, v_ffn1_w2, v_w_in, v_pool_w, v_pool_scale, v_conv_dw, v_conv_db, v_conv_ln_g, v_conv_ln_b, v_conv_pw, v_ret_gn_g, v_w_out, v_ffn2_w13, v_ffn2_w2, v_ln_g, v_ln_b):
    local = dict(meta=meta, ln_in_g=ln_in_g, ln_in_b=ln_in_b, ffn1_w13=ffn1_w13, ffn1_w2=ffn1_w2, w_in=w_in,
                 pool_w=pool_w, pool_scale=pool_scale, conv_dw=conv_dw, conv_db=conv_db, conv_ln_g=conv_ln_g,
                 conv_ln_b=conv_ln_b, conv_pw=conv_pw, ret_gn_g=ret_gn_g, w_out=w_out, ffn2_w13=ffn2_w13,
                 ffn2_w2=ffn2_w2, ln_g=ln_g, ln_b=ln_b)
    mom1 = dict(meta=m_meta, ln_in_g=m_ln_in_g, ln_in_b=m_ln_in_b, ffn1_w13=m_ffn1_w13, ffn1_w2=m_ffn1_w2,
                w_in=m_w_in, pool_w=m_pool_w, pool_scale=m_pool_scale, conv_dw=m_conv_dw, conv_db=m_conv_db,
                conv_ln_g=m_conv_ln_g, conv_ln_b=m_conv_ln_b, conv_pw=m_conv_pw, ret_gn_g=m_ret_gn_g,
                w_out=m_w_out, ffn2_w13=m_ffn2_w13, ffn2_w2=m_ffn2_w2, ln_g=m_ln_g, ln_b=m_ln_b)
    mom2 = dict(meta=v_meta, ln_in_g=v_ln_in_g, ln_in_b=v_ln_in_b, ffn1_w13=v_ffn1_w13, ffn1_w2=v_ffn1_w2,
                w_in=v_w_in, pool_w=v_pool_w, pool_scale=v_pool_scale, conv_dw=v_conv_dw, conv_db=v_conv_db,
                conv_ln_g=v_conv_ln_g, conv_ln_b=v_conv_ln_b, conv_pw=v_conv_pw, ret_gn_g=v_ret_gn_g,
                w_out=v_w_out, ffn2_w13=v_ffn2_w13, ffn2_w2=v_ffn2_w2, ln_g=v_ln_g, ln_b=v_ln_b)

    depth = ffn1_w13.shape[0]
    alpha = (2.0 * depth) ** 0.25
    seq, d = x.shape[1], x.shape[2]
    lp = FRONT + seq
    h_loc = ffn1_w2.shape[1]
    hp = -(-h_loc // LANE) * LANE
    c_pool = pool_scale.shape[1]
    c_conv = conv_db.shape[1]
    q_blk = (c_pool + 2 * c_conv) // LANE
    dy_blk = (c_pool + c_conv) // LANE
    heads = ret_gn_g.shape[1] // LANE
    assert meta.shape[0] == N_META and heads == RET_HEADS and conv_dw.shape[1] == CONV_WIDTH
    assert ffn1_w13.shape[2] == 2 * h_loc

    def to_wire(name, w):
        if name in ("ffn1_w13", "ffn2_w13"):
            return _pad_halves(w, hp)
        if name in ("ffn1_w2", "ffn2_w2"):
            return _pad_to(w, w.ndim - 2, hp)
        return w

    def from_wire(name, w):
        if name in ("ffn1_w13", "ffn2_w13"):
            return _unpad_halves(w, h_loc)
        if name in ("ffn1_w2", "ffn2_w2"):
            return w[..., :h_loc, :]
        return w

    small_names = [n for n, _ in SMALL_SHARDED]
    small_ax = dict(SMALL_SHARDED)
    first = ("ffn1_w13", "ffn1_w2")
    wire = lambda l, names: [to_wire(n, local[n][l]).astype(BF16) for n in names]
    layer0_groups = [first, ("w_in", "conv_pw"), ("w_out",), ("ffn2_w13", "ffn2_w2")]
    pending = {0: [(1 + i, names) for i, names in enumerate(layer0_groups)]}
    pending.update({l: [(len(layer0_groups) + l, MATMUL_WEIGHTS)] for l in range(1, depth)})
    groups = [[_pack([local[n] for n in small_names])]] + [wire(0, names) for names in layer0_groups]
    groups += [wire(l, MATMUL_WEIGHTS) for l in range(1, depth)]
    handles, token = [], None
    for gi, arrs in enumerate(groups):
        handle, token = _exchange_start(arrs, False, f"gather_start_{gi}", after=token)
        handles.append(handle)
    gathered = {l: {} for l in range(depth)}

    def weight(l, name, after):
        for gi, names in list(pending[l]):
            if name in names:
                gathered[l].update(zip(names, _exchange_wait(handles[gi], after, f"gather_wait_{gi}")))
                pending[l].remove((gi, names))
        return gathered[l][name]
    got = _exchange_wait(handles[0], token, "gather_wait_0")
    small_parts = _unpack(got[0], [local[n].shape for n in small_names], 1)
    small_full = {n: _unshard(p, small_ax[n]) for n, p in zip(small_names, small_parts)}

    row = lambda v: v.reshape(1, -1)
    ln_params = [(row(ln_in_g), row(ln_in_b))]
    for l in range(depth):
        ln_params += [(row(small_full["ln_g"][l, j]), row(small_full["ln_b"][l, j])) for j in range(3)]
    rope_c, rope_s, dm, qd, kd, cd = _retention_tables(lp, heads)

    r = [jnp.concatenate([jnp.zeros((PAD_ROWS, d), F32), small_full["meta"], x[0]], axis=0)]
    saved, layers = [], []
    for l in range(depth):
        k = 3 * l
        p = dict(pool=(_block_diag(pool_w[l]).astype(BF16), row(pool_scale[l])), gn=row(ret_gn_g[l]))
        p["ffn1"] = (weight(l, "ffn1_w13", r[k]), weight(l, "ffn1_w2", r[k]))
        r1, a1, u1 = _ffn_fwd(r[k], *ln_params[k], *p["ffn1"], alpha)
        p["w_in"] = _unshard(weight(l, "w_in", r1), 1)
        p["conv"] = (_pad_to(small_full["conv_dw"][l], 0, CONV_HALO), row(conv_db[l]), row(conv_ln_g[l]),
                     row(conv_ln_b[l]), weight(l, "conv_pw", r1).reshape(c_conv, c_conv))
        z = _proj_fwd(r1, *ln_params[k + 1], p["w_in"])
        y_pool = _pool_fwd(z, *p["pool"])
        y_conv, conv_acc = _conv_fwd(z, *p["conv"])
        y_ret, states = _ret_fwd(z, rope_c, rope_s, dm, qd, kd, cd, p["gn"], q_blk)
        cat = jnp.concatenate([y_pool, y_conv, y_ret], axis=1)
        p["w_out"] = weight(l, "w_out", cat).reshape(-1, d)
        r2 = _out_fwd(r1, *ln_params[k + 1], cat, p["w_out"], alpha)
        p["ffn2"] = (weight(l, "ffn2_w13", r2), weight(l, "ffn2_w2", r2))
        r3, a2, u2 = _ffn_fwd(r2, *ln_params[k + 2], *p["ffn2"], alpha)
        r += [r1, r2, r3]
        layers.append(p)
        saved.append((a1, u1, z, states, cat, a2, u2, conv_acc))

    target = jnp.concatenate([jnp.zeros((FRONT, d), F32), loss_target[0]], axis=0)
    dr, dg, db, loss_cols = _loss_bwd(r[-1], target, *ln_params[-1])
    loss = lax.psum(0.5 * jnp.sum(loss_cols) / d, MESH_AXES)
    ln_grads = {3 * depth: (dg, db)}
    w13_slot = lambda j: (j % 2) * (N_DEV // 2) + j // 2
    g_parts = [None] * depth
    g_rep = [None] * depth
    scatter_jobs = []

    def start_scatter(l, gp, names, extra=()):
        parts = [gp[n].astype(WIRE) for n in names] + list(extra)
        handle, token = _exchange_start(parts, True, f"scatter_start_{len(scatter_jobs)}")
        scatter_jobs.append((handle, l, names))
        return token

    def ffn_grads(dr_out, r_in, a, u, ln_p, wts, scatter_w13_of=None):
        dr_in, hb, s, dz, dg, db = _ffn_bwd(dr_out, r_in, a, u, *ln_p, *wts, alpha)
        g13 = _mm_tn(hb, dz, bn=2 * hp, slot_of=w13_slot)
        if scatter_w13_of is not None:
            start_scatter(scatter_w13_of, {"ffn1_w13": g13}, ("ffn1_w13",))
        g2 = _mm_tn(s, dr_out, 0.5).reshape(N_DEV, hp, d)
        return dr_in, (dg, db), g13, g2

    for l in reversed(range(depth)):
        p = layers[l]
        k = 3 * l
        a1, u1, z, states, cat, a2, u2, conv_acc = saved[l]
        gp, gr = {}, {}
        dr, ln_grads[k + 2], gp["ffn2_w13"], gp["ffn2_w2"] = ffn_grads(dr, r[k + 2], a2, u2, ln_params[k + 2], p["ffn2"])
        zero = start_scatter(l, gp, ("ffn2_w13", "ffn2_w2"))[0:1, 0:1]

        dr2 = dr
        dcat = _mm_nt(dr2, p["w_out"])
        gp["w_out"] = _mm_tn(cat, dr2).reshape(N_DEV, -1, d)
        dxp, dwbd, gr["pool_scale"] = _pool_bwd(z, dcat, p["pool"][0], p["pool"][1] + zero)
        dca, dcg, ddw, gr["conv_db"], gr["conv_ln_g"], gr["conv_ln_b"], dpw = _conv_bwd(
            z, dcat, conv_acc, p["conv"][0], p["conv"][2], p["conv"][3], p["conv"][4])
        dq, dk, dv, dgt, gr["ret_gn_g"] = _ret_bwd(z, dcat, states, rope_c, rope_s, dm, qd, kd, cd, p["gn"],
                                                  q_blk, dy_blk)
        dz = jnp.concatenate([dxp, dca, dcg, dq, dk, dv, dgt], axis=1)
        dr, hb, dg, db = _in_bwd(dr2, r[k + 1], dz, *ln_params[k + 1], p["w_in"], alpha)
        ln_grads[k + 1] = (dg, db)
        gp["w_in"] = _to_shards(_mm_tn(hb, dz), 1)
        gp["conv_pw"] = dpw.reshape(N_DEV, -1, c_conv)
        n_grp, grp = pool_w.shape[1], pool_w.shape[2]
        gr["pool_w"] = jnp.stack([dwbd[i * grp:(i + 1) * grp, i * grp:(i + 1) * grp] for i in range(n_grp)])
        gr["conv_dw"] = ddw[:CONV_WIDTH]

        zero = start_scatter(l, gp, ("w_in", "conv_pw", "w_out"))[0:1, 0:1]
        ln_g_in, ln_b_in = ln_params[k]
        dr, ln_grads[k], gp["ffn1_w13"], gp["ffn1_w2"] = ffn_grads(
            dr, r[k], a1, u1, (ln_g_in + zero, ln_b_in), p["ffn1"], scatter_w13_of=0 if l == 0 else None)
        g_parts[l], g_rep[l] = gp, gr
        if l > 0:
            zero = start_scatter(l, gp, ("ffn1_w13", "ffn1_w2"))[0:1, 0:1]
            g_prev, b_prev = ln_params[k - 1]
            ln_params[k - 1] = (g_prev + zero, b_prev)

    grad_x = dr[FRONT:][None]
    local_shape = lambda name: ((CONV_WIDTH, c_conv) if name == "conv_dw" else local[name].shape[1:])
    stack_layers = lambda name: jnp.stack([g_rep[l][name].reshape(local_shape(name)) for l in range(depth)])
    small_grad = dict(
        meta=dr[PAD_ROWS:FRONT],
        conv_dw=stack_layers("conv_dw"),
        ln_g=jnp.stack([jnp.stack([ln_grads[3 * l + j + 1][0][0] for j in range(3)]) for l in range(depth)]),
        ln_b=jnp.stack([jnp.stack([ln_grads[3 * l + j + 1][1][0] for j in range(3)]) for l in range(depth)]),
    )
    rep_grad = {n: stack_layers(n) for n in REPLICATED if n not in ("ln_in_g", "ln_in_b")}
    rep_grad["ln_in_g"], rep_grad["ln_in_b"] = ln_grads[0][0][0], ln_grads[0][1][0]

    small_pack8 = _pack([_to_shards(small_grad[n], small_ax[n]) for n in small_names], 1)
    after = start_scatter(0, g_parts[0], ("ffn1_w2",), extra=[small_pack8])
    rep_parts = _exchange([_pack([rep_grad[n] for n in REPLICATED])], False)[0]
    scattered, outs_by_name = {}, {}

    def update(names):
        nonlocal after
        for n in names:
            res = _adamw([scattered[l, n] for l in range(depth)], *[to_wire(n, src[n]) for src in (local, mom1, mom2)])
            outs_by_name[n] = [from_wire(n, t) for t in res]
            after = res[0]

    for ji, (handle, l, names) in enumerate(scatter_jobs):
        if ji == len(scatter_jobs) - 1:
            update([n for n in MATMUL_WEIGHTS if n not in names])
        got = _exchange_wait(handle, after, f"scatter_wait_{ji}")
        scattered.update({(l, n): t for n, t in zip(names, got)})
        after = got[0]
    small_scattered = got[-1]
    update(names)
    for names, parts in ((small_names, small_scattered), (REPLICATED, rep_parts)):
        shapes = [local[n].shape for n in names]
        res = _adamw([parts], *[_pack([src[n] for n in names])[None] for src in (local, mom1, mom2)])
        for n, vals in zip(names, zip(*[_unpack(t[0], shapes) for t in res])):
            outs_by_name[n] = list(vals)

    outs = [loss, grad_x]
    for kind in range(4):
        outs += [outs_by_name[n][kind] for n in WEIGHT_ORDER]
    return tuple(outs)
```

```python
import math

import jax
import jax.numpy as jnp
from jax import lax
from jax.experimental import pallas as pl
from jax.experimental.pallas import tpu as pltpu

F32 = jnp.float32
BF16 = jnp.bfloat16
WIRE = jnp.bfloat16

N_DEV = 8
MESH_AXES = ("x", "y", "c")
CHUNK = 64
N_META = 16
PAD_ROWS = 240
FRONT = PAD_ROWS + N_META
LN_EPS = 1e-5
LANE = 128
POOL_WINDOWS = (2, 4, 8, 16)
CONV_WIDTH = 31
CONV_HALO = 32
POOL_HALO = 16
RET_HEADS = 4
ROPE_BASE = 10000.0

ADAM_LR = 0.001
ADAM_B1 = 0.9
ADAM_B2 = 0.999
ADAM_EPS = 1e-08
ADAM_WD = 0.01
ADAM_STEP = 10

VMEM_LIMIT = 56 * 1024 * 1024
DENSE_TM_FWD = 768
DENSE_TM_BWD = 528
MIX_TM = 768
FFN_ROW_PARTS = 2
MM_TN_TK = 1408


def _cparams(sem):
    return pltpu.CompilerParams(dimension_semantics=sem, vmem_limit_bytes=VMEM_LIMIT)


def _pick_tm(lp, target, mult=16):
    best = None
    for t in range(mult, min(lp, target) + 1, mult):
        if lp % t == 0:
            best = t
    assert best is not None, (lp, target, mult)
    return best


def _row_parts(tm, n):
    step = -(-tm // (16 * n)) * 16
    return [slice(r0, min(r0 + step, tm)) for r0 in range(0, tm, step)]


def _dot(a, b):
    return jnp.dot(a, b, preferred_element_type=F32)


def _dot_nt(a, b):
    return lax.dot_general(a, b, (((1,), (1,)), ((), ())), preferred_element_type=F32)


def _dot_tn(a, b):
    return lax.dot_general(a, b, (((0,), (0,)), ((), ())), preferred_element_type=F32)


def _sigmoid(x):
    return 0.5 * jnp.tanh(0.5 * x) + 0.5


def _ln_stats(r):
    mu = jnp.mean(r, axis=-1, keepdims=True)
    xc = r - mu
    var = jnp.mean(xc * xc, axis=-1, keepdims=True)
    rstd = lax.rsqrt(var + LN_EPS)
    return xc * rstd, rstd


def _ln_bwd(dh, xhat, rstd, g):
    dxh = dh * g
    m1 = jnp.mean(dxh, axis=-1, keepdims=True)
    m2 = jnp.mean(dxh * xhat, axis=-1, keepdims=True)
    return rstd * (dxh - m1 - xhat * m2)


def _colsum(x):
    return jnp.sum(x, axis=0, keepdims=True)


def _row_ids(tile, tm):
    return lax.broadcasted_iota(jnp.int32, (tm, 1), 0) + tile * tm


def _ffn_fwd(r_prev, g, b, w13g, w2g, alpha):
    lp, d = r_prev.shape
    nf = N_DEV // 2
    fc = w13g.shape[2]
    w2c = w2g.reshape(nf, fc, d)
    tm = _pick_tm(lp, DENSE_TM_FWD)

    def body(r_ref, g_ref, b_ref, w1_ref, w3_ref, w2_ref, out_ref, a_ref, u_ref, hb, acc):
        f = pl.program_id(1)

        @pl.when(f == 0)
        def _():
            xhat, _ = _ln_stats(r_ref[...])
            hb[...] = (xhat * g_ref[...] + b_ref[...]).astype(BF16)
            acc[...] = jnp.zeros_like(acc)

        h = hb[...]
        a = _dot(h, w1_ref[...])
        u = _dot(h, w3_ref[...])
        a_ref[...] = a.astype(BF16)
        u_ref[...] = u.astype(BF16)
        s = a * _sigmoid(a) * u
        acc[...] += _dot(s.astype(BF16), w2_ref[...])

        @pl.when(f == nf - 1)
        def _():
            xhat, _ = _ln_stats(r_ref[...])
            out_ref[...] = alpha * (xhat * g_ref[...] + b_ref[...]) + 0.5 * acc[...]

    return pl.pallas_call(
        body,
        grid=(lp // tm, nf),
        in_specs=[
            pl.BlockSpec((tm, d), lambda i, f: (i, 0)),
            pl.BlockSpec((1, d), lambda i, f: (0, 0)),
            pl.BlockSpec((1, d), lambda i, f: (0, 0)),
            pl.BlockSpec((None, d, fc), lambda i, f: (f, 0, 0)),
            pl.BlockSpec((None, d, fc), lambda i, f: (nf + f, 0, 0)),
            pl.BlockSpec((None, fc, d), lambda i, f: (f, 0, 0)),
        ],
        out_specs=[
            pl.BlockSpec((tm, d), lambda i, f: (i, 0)),
            pl.BlockSpec((tm, fc), lambda i, f: (i, f)),
            pl.BlockSpec((tm, fc), lambda i, f: (i, f)),
        ],
        out_shape=[
            jax.ShapeDtypeStruct((lp, d), F32),
            jax.ShapeDtypeStruct((lp, nf * fc), BF16),
            jax.ShapeDtypeStruct((lp, nf * fc), BF16),
        ],
        scratch_shapes=[pltpu.VMEM((tm, d), BF16), pltpu.VMEM((tm, d), F32)],
        compiler_params=_cparams(("parallel", "arbitrary")),
        name="ffn_fwd",
    )(r_prev, g, b, w13g, w13g, w2c)


def _ffn_bwd(dr_next, r_prev, a, u, g, b, w13g, w2g, alpha):
    lp, d = r_prev.shape
    nf = N_DEV // 2
    fc = w13g.shape[2]
    w2c = w2g.reshape(nf, fc, d)
    tm = _pick_tm(lp, DENSE_TM_BWD)
    parts = _row_parts(tm, FFN_ROW_PARTS)

    def body(dr_ref, r_ref, a_ref, u_ref, g_ref, b_ref, w1_ref, w3_ref, w2_ref,
             drp_ref, hb_ref, s_ref, dz_ref, dg_ref, db_ref, dyb, dhacc):
        i = pl.program_id(0)
        f = pl.program_id(1)

        @pl.when(jnp.logical_and(i == 0, f == 0))
        def _():
            dg_ref[...] = jnp.zeros_like(dg_ref)
            db_ref[...] = jnp.zeros_like(db_ref)

        @pl.when(f == 0)
        def _():
            dyb[...] = (0.5 * dr_ref[...]).astype(BF16)
            dhacc[...] = jnp.zeros_like(dhacc)
            xhat, _ = _ln_stats(r_ref[...])
            hb_ref[...] = (xhat * g_ref[...] + b_ref[...]).astype(BF16)

        ds = [_dot_nt(dyb[rows, :], w2_ref[...]) for rows in parts]
        da, du = [], []
        for rows, dsv in zip(parts, ds):
            av = a_ref[rows, :].astype(F32)
            uv = u_ref[rows, :].astype(F32)
            sig = _sigmoid(av)
            sl = av * sig
            da.append((dsv * uv * (sig * (1.0 + av * (1.0 - sig)))).astype(BF16))
            du.append((dsv * sl).astype(BF16))
            s_ref[rows, :] = (sl * uv).astype(BF16)
        dh = [_dot_nt(da[p], w1_ref[...]) + _dot_nt(du[p], w3_ref[...]) for p in range(len(parts))]
        for p, rows in enumerate(parts):
            dz_ref[rows, 0:fc] = da[p]
            dz_ref[rows, fc:2 * fc] = du[p]
            dhacc[rows, :] += dh[p]

        @pl.when(f == nf - 1)
        def _():
            dh = alpha * dr_ref[...] + dhacc[...]
            xhat, rstd = _ln_stats(r_ref[...])
            drp_ref[...] = _ln_bwd(dh, xhat, rstd, g_ref[...])
            dg_ref[...] += _colsum(dh * xhat)
            db_ref[...] += _colsum(dh)

    row = lambda i, f: (i, 0)
    const = lambda i, f: (0, 0)
    chunk = lambda i, f: (i, f)
    return pl.pallas_call(
        body,
        grid=(lp // tm, nf),
        in_specs=[
            pl.BlockSpec((tm, d), row),
            pl.BlockSpec((tm, d), row),
            pl.BlockSpec((tm, fc), chunk),
            pl.BlockSpec((tm, fc), chunk),
            pl.BlockSpec((1, d), const),
            pl.BlockSpec((1, d), const),
            pl.BlockSpec((None, d, fc), lambda i, f: (f, 0, 0)),
            pl.BlockSpec((None, d, fc), lambda i, f: (nf + f, 0, 0)),
            pl.BlockSpec((None, fc, d), lambda i, f: (f, 0, 0)),
        ],
        out_specs=[
            pl.BlockSpec((tm, d), row),
            pl.BlockSpec((tm, d), row),
            pl.BlockSpec((tm, fc), chunk),
            pl.BlockSpec((tm, 2 * fc), chunk),
            pl.BlockSpec((1, d), const),
            pl.BlockSpec((1, d), const),
        ],
        out_shape=[
            jax.ShapeDtypeStruct((lp, d), F32),
            jax.ShapeDtypeStruct((lp, d), BF16),
            jax.ShapeDtypeStruct((lp, nf * fc), BF16),
            jax.ShapeDtypeStruct((lp, 2 * nf * fc), BF16),
            jax.ShapeDtypeStruct((1, d), F32),
            jax.ShapeDtypeStruct((1, d), F32),
        ],
        scratch_shapes=[pltpu.VMEM((tm, d), BF16), pltpu.VMEM((tm, d), F32)],
        compiler_params=_cparams(("arbitrary", "arbitrary")),
        name="ffn_bwd",
    )(dr_next, r_prev, a, u, g, b, w13g, w13g, w2c)


def _mm_tn(a, b, scale=1.0, bn=None, slot_of=None):
    t, m = a.shape
    n = b.shape[1]
    if bn is None:
        bn = n if n <= 1408 else _pick_tm(n, 1408, LANE)
    bm = m if m <= 1536 else _pick_tm(m, 1536, LANE)
    tk = _pick_tm(t, MM_TN_TK)
    nt = t // tk

    def body(a_ref, b_ref, o_ref, acc):
        k = pl.program_id(2)

        @pl.when(k == 0)
        def _():
            acc[...] = jnp.zeros_like(acc)

        acc[...] += _dot_tn(a_ref[...].astype(BF16), b_ref[...].astype(BF16))

        @pl.when(k == nt - 1)
        def _():
            o_ref[...] = (acc[...] * scale).astype(o_ref.dtype)

    if slot_of is None:
        out_spec = pl.BlockSpec((bm, bn), lambda i, j, k: (i, j))
        out_shape = jax.ShapeDtypeStruct((m, n), WIRE)
    else:
        out_spec = pl.BlockSpec((None, bm, bn), lambda i, j, k: (slot_of(j), i, 0))
        out_shape = jax.ShapeDtypeStruct((n // bn, m, bn), WIRE)
    return pl.pallas_call(
        body,
        grid=(m // bm, n // bn, nt),
        in_specs=[
            pl.BlockSpec((tk, bm), lambda i, j, k: (k, i)),
            pl.BlockSpec((tk, bn), lambda i, j, k: (k, j)),
        ],
        out_specs=out_spec,
        out_shape=out_shape,
        scratch_shapes=[pltpu.VMEM((bm, bn), F32)],
        compiler_params=_cparams(("parallel", "parallel", "arbitrary")),
        name="mm_tn",
    )(a, b)


def _proj_fwd(r_prev, g, b, w_in):
    lp, d = r_prev.shape
    n = w_in.shape[1]
    tm = _pick_tm(lp, DENSE_TM_FWD)

    def body(r_ref, g_ref, b_ref, w_ref, z_ref):
        xhat, _ = _ln_stats(r_ref[...])
        h = (xhat * g_ref[...] + b_ref[...]).astype(BF16)
        z = _dot(h, w_ref[...])
        rows = _row_ids(pl.program_id(0), tm)
        z_ref[...] = jnp.where(rows >= PAD_ROWS, z, 0.0)

    return pl.pallas_call(
        body,
        grid=(lp // tm,),
        in_specs=[
            pl.BlockSpec((tm, d), lambda i: (i, 0)),
            pl.BlockSpec((1, d), lambda i: (0, 0)),
            pl.BlockSpec((1, d), lambda i: (0, 0)),
            pl.BlockSpec((d, n), lambda i: (0, 0)),
        ],
        out_specs=pl.BlockSpec((tm, n), lambda i: (i, 0)),
        out_shape=jax.ShapeDtypeStruct((lp, n), F32),
        compiler_params=_cparams(("parallel",)),
        name="proj_fwd",
    )(r_prev, g, b, w_in)


def _out_fwd(r_prev, g, b, cat, w_out, alpha):
    lp, d = r_prev.shape
    k = cat.shape[1]
    tm = _pick_tm(lp, DENSE_TM_FWD)

    def body(r_ref, g_ref, b_ref, c_ref, w_ref, o_ref):
        xhat, _ = _ln_stats(r_ref[...])
        o_ref[...] = alpha * (xhat * g_ref[...] + b_ref[...]) + _dot(c_ref[...], w_ref[...])

    return pl.pallas_call(
        body,
        grid=(lp // tm,),
        in_specs=[
            pl.BlockSpec((tm, d), lambda i: (i, 0)),
            pl.BlockSpec((1, d), lambda i: (0, 0)),
            pl.BlockSpec((1, d), lambda i: (0, 0)),
            pl.BlockSpec((tm, k), lambda i: (i, 0)),
            pl.BlockSpec((k, d), lambda i: (0, 0)),
        ],
        out_specs=pl.BlockSpec((tm, d), lambda i: (i, 0)),
        out_shape=jax.ShapeDtypeStruct((lp, d), F32),
        compiler_params=_cparams(("parallel",)),
        name="out_fwd",
    )(r_prev, g, b, cat, w_out)


def _mm_nt(x, w):
    lp, n = x.shape
    k = w.shape[0]
    tm = _pick_tm(lp, DENSE_TM_FWD)

    def body(x_ref, w_ref, o_ref):
        o_ref[...] = _dot_nt(x_ref[...].astype(BF16), w_ref[...])

    return pl.pallas_call(
        body,
        grid=(lp // tm,),
        in_specs=[pl.BlockSpec((tm, n), lambda i: (i, 0)), pl.BlockSpec((k, n), lambda i: (0, 0))],
        out_specs=pl.BlockSpec((tm, k), lambda i: (i, 0)),
        out_shape=jax.ShapeDtypeStruct((lp, k), F32),
        compiler_params=_cparams(("parallel",)),
        name="mm_nt",
    )(x, w)


def _in_bwd(dr_next, r_prev, dz, g, b, w_in, alpha):
    lp, d = r_prev.shape
    n = w_in.shape[1]
    tm = _pick_tm(lp, DENSE_TM_FWD)

    def body(dr_ref, r_ref, dz_ref, g_ref, b_ref, w_ref, drp_ref, hb_ref, dg_ref, db_ref):
        @pl.when(pl.program_id(0) == 0)
        def _():
            dg_ref[...] = jnp.zeros_like(dg_ref)
            db_ref[...] = jnp.zeros_like(db_ref)

        parts = _row_parts(tm, FFN_ROW_PARTS)
        back = [_dot_nt(dz_ref[rows, :], w_ref[...]) for rows in parts]
        dg_sum, db_sum = dg_ref[...], db_ref[...]
        for rows, through in zip(parts, back):
            dh = alpha * dr_ref[rows, :] + through
            xhat, rstd = _ln_stats(r_ref[rows, :])
            hb_ref[rows, :] = (xhat * g_ref[...] + b_ref[...]).astype(BF16)
            drp_ref[rows, :] = _ln_bwd(dh, xhat, rstd, g_ref[...])
            dg_sum = dg_sum + _colsum(dh * xhat)
            db_sum = db_sum + _colsum(dh)
        dg_ref[...] = dg_sum
        db_ref[...] = db_sum

    row = lambda i: (i, 0)
    const = lambda i: (0, 0)
    return pl.pallas_call(
        body,
        grid=(lp // tm,),
        in_specs=[
            pl.BlockSpec((tm, d), row),
            pl.BlockSpec((tm, d), row),
            pl.BlockSpec((tm, n), row),
            pl.BlockSpec((1, d), const),
            pl.BlockSpec((1, d), const),
            pl.BlockSpec((d, n), const),
        ],
        out_specs=[
            pl.BlockSpec((tm, d), row),
            pl.BlockSpec((tm, d), row),
            pl.BlockSpec((1, d), const),
            pl.BlockSpec((1, d), const),
        ],
        out_shape=[
            jax.ShapeDtypeStruct((lp, d), F32),
            jax.ShapeDtypeStruct((lp, d), BF16),
            jax.ShapeDtypeStruct((1, d), F32),
            jax.ShapeDtypeStruct((1, d), F32),
        ],
        compiler_params=_cparams(("arbitrary",)),
        name="in_bwd",
    )(dr_next, r_prev, dz, g, b, w_in)


def _loss_bwd(r_last, target, g, b):
    lp, d = r_last.shape
    tm = _pick_tm(lp, DENSE_TM_FWD)

    def body(r_ref, t_ref, g_ref, b_ref, dr_ref, dg_ref, db_ref, ls_ref):
        i = pl.program_id(0)

        @pl.when(i == 0)
        def _():
            dg_ref[...] = jnp.zeros_like(dg_ref)
            db_ref[...] = jnp.zeros_like(db_ref)
            ls_ref[...] = jnp.zeros_like(ls_ref)

        xhat, rstd = _ln_stats(r_ref[...])
        y = xhat * g_ref[...] + b_ref[...]
        rows = _row_ids(i, tm)
        err = jnp.where(rows >= FRONT, y - t_ref[...], 0.0)
        ls_ref[...] += _colsum(err * err)
        dy = err * (1.0 / d)
        dr_ref[...] = _ln_bwd(dy, xhat, rstd, g_ref[...])
        dg_ref[...] += _colsum(dy * xhat)
        db_ref[...] += _colsum(dy)

    row = lambda i: (i, 0)
    const = lambda i: (0, 0)
    return pl.pallas_call(
        body,
        grid=(lp // tm,),
        in_specs=[
            pl.BlockSpec((tm, d), row),
            pl.BlockSpec((tm, d), row),
            pl.BlockSpec((1, d), const),
            pl.BlockSpec((1, d), const),
        ],
        out_specs=[
            pl.BlockSpec((tm, d), row),
            pl.BlockSpec((1, d), const),
            pl.BlockSpec((1, d), const),
            pl.BlockSpec((1, d), const),
        ],
        out_shape=[
            jax.ShapeDtypeStruct((lp, d), F32),
            jax.ShapeDtypeStruct((1, d), F32),
            jax.ShapeDtypeStruct((1, d), F32),
            jax.ShapeDtypeStruct((1, d), F32),
        ],
        compiler_params=_cparams(("arbitrary",)),
        name="loss_bwd",
    )(r_last, target, g, b)


def _pool_counts(tile, tm, width):
    pos = _row_ids(tile, tm) - PAD_ROWS
    lane = lax.broadcasted_iota(jnp.int32, (1, width), 1)
    group = width // len(POOL_WINDOWS)
    win = jnp.full((1, width), POOL_WINDOWS[-1], jnp.int32)
    for gi in range(len(POOL_WINDOWS) - 2, -1, -1):
        win = jnp.where(lane < (gi + 1) * group, POOL_WINDOWS[gi], win)
    cnt = jnp.clip(pos + 1, 1, win)
    return cnt.astype(F32), lane, group


def _pool_select(sums, lane, group):
    out = sums[-1]
    for gi in range(len(POOL_WINDOWS) - 2, -1, -1):
        out = jnp.where(lane < (gi + 1) * group, sums[gi], out)
    return out


def _pool_window_sums(ext, tm, sign):
    base = POOL_HALO if sign < 0 else 0
    acc = ext[pl.ds(base, tm), :]
    sums, k = [], 1
    for w in POOL_WINDOWS:
        while k < w:
            acc = acc + ext[pl.ds(base + sign * k, tm), :]
            k += 1
        sums.append(acc)
    return sums


def _pool_fwd(z, wbd, scale):
    lp = z.shape[0]
    c = wbd.shape[0]
    tm = _pick_tm(lp, MIX_TM, CHUNK)

    def body(x_ref, w_ref, s_ref, y_ref, ext):
        i = pl.program_id(0)

        @pl.when(i == 0)
        def _():
            ext[0:POOL_HALO, :] = jnp.zeros((POOL_HALO, c), F32)

        x = x_ref[...]
        ext[POOL_HALO:POOL_HALO + tm, :] = x
        cnt, lane, group = _pool_counts(i, tm, c)
        sums = _pool_window_sums(ext, tm, -1)
        y = _pool_select(sums, lane, group) / cnt - x
        ext[0:POOL_HALO, :] = ext[tm:tm + POOL_HALO, :]
        y_ref[...] = (_dot(y.astype(BF16), w_ref[...]) * s_ref[...]).astype(BF16)

    return pl.pallas_call(
        body,
        grid=(lp // tm,),
        in_specs=[
            pl.BlockSpec((tm, c), lambda i: (i, 0)),
            pl.BlockSpec((c, c), lambda i: (0, 0)),
            pl.BlockSpec((1, c), lambda i: (0, 0)),
        ],
        out_specs=pl.BlockSpec((tm, c), lambda i: (i, 0)),
        out_shape=jax.ShapeDtypeStruct((lp, c), BF16),
        scratch_shapes=[pltpu.VMEM((tm + POOL_HALO, c), F32)],
        compiler_params=_cparams(("arbitrary",)),
        name="pool_fwd",
    )(z, wbd, scale)


def _pool_bwd(z, dcat, wbd, scale):
    lp = z.shape[0]
    c = wbd.shape[0]
    tm = _pick_tm(lp, MIX_TM, CHUNK)
    ni = lp // tm
    hpt = tm // POOL_HALO

    def body(x_ref, xh_ref, dy_ref, w_ref, s_ref, dx_ref, dw_ref, ds_ref, ext, ext2):
        i = pl.program_id(0)
        t = ni - 1 - i

        @pl.when(i == 0)
        def _():
            dw_ref[...] = jnp.zeros_like(dw_ref)
            ds_ref[...] = jnp.zeros_like(ds_ref)
            ext2[tm:tm + POOL_HALO, :] = jnp.zeros((POOL_HALO, c), F32)

        x = x_ref[...]
        ext[0:POOL_HALO, :] = jnp.where(t > 0, xh_ref[...], 0.0)
        ext[POOL_HALO:POOL_HALO + tm, :] = x
        cnt, lane, group = _pool_counts(t, tm, c)
        y = (_pool_select(_pool_window_sums(ext, tm, -1), lane, group) / cnt - x).astype(BF16)
        w = w_ref[...]
        dyv = dy_ref[...]
        ds_ref[...] += _colsum(_dot(y, w) * dyv)
        do = (dyv * s_ref[...]).astype(BF16)
        dw_ref[...] += _dot_tn(y, do)
        dyp = _dot_nt(do, w)
        ext2[0:tm, :] = dyp / cnt
        dx = _pool_select(_pool_window_sums(ext2, tm, 1), lane, group) - dyp
        ext2[tm:tm + POOL_HALO, :] = ext2[0:POOL_HALO, :]
        rows = _row_ids(t, tm)
        dx_ref[...] = jnp.where(rows >= PAD_ROWS, dx, 0.0).astype(BF16)

    rev = lambda i: (ni - 1 - i, 0)
    return pl.pallas_call(
        body,
        grid=(ni,),
        in_specs=[
            pl.BlockSpec((tm, c), rev),
            pl.BlockSpec((POOL_HALO, c), lambda i: (jnp.maximum((ni - 1 - i) * hpt - 1, 0), 0)),
            pl.BlockSpec((tm, c), rev),
            pl.BlockSpec((c, c), lambda i: (0, 0)),
            pl.BlockSpec((1, c), lambda i: (0, 0)),
        ],
        out_specs=[
            pl.BlockSpec((tm, c), rev),
            pl.BlockSpec((c, c), lambda i: (0, 0)),
            pl.BlockSpec((1, c), lambda i: (0, 0)),
        ],
        out_shape=[
            jax.ShapeDtypeStruct((lp, c), BF16),
            jax.ShapeDtypeStruct((c, c), F32),
            jax.ShapeDtypeStruct((1, c), F32),
        ],
        scratch_shapes=[pltpu.VMEM((tm + POOL_HALO, c), F32), pltpu.VMEM((tm + POOL_HALO, c), F32)],
        compiler_params=_cparams(("arbitrary",)),
        name="pool_bwd",
    )(z, z, dcat, wbd, scale)


def _conv_taps(ext, w_ref, tm, first):
    acc = w_ref[0:1, :] * ext[pl.ds(first, tm), :]
    for k in range(1, CONV_WIDTH):
        acc = acc + w_ref[k:k + 1, :] * ext[pl.ds(first + k, tm), :]
    return acc


def _conv_fwd(z, w_dw, b_dw, ln_g, ln_b, w_pw):
    lp = z.shape[0]
    c = w_pw.shape[0]
    tm = _pick_tm(lp, MIX_TM, CHUNK)

    def body(a_ref, gt_ref, w_ref, bd_ref, g_ref, b_ref, pw_ref, y_ref, acc_ref, ext):
        i = pl.program_id(0)

        @pl.when(i == 0)
        def _():
            ext[0:CONV_HALO, :] = jnp.zeros((CONV_HALO, c), F32)

        ext[CONV_HALO:CONV_HALO + tm, :] = a_ref[...] * _sigmoid(gt_ref[...])
        acc = _conv_taps(ext, w_ref, tm, CONV_HALO - CONV_WIDTH + 1) + bd_ref[...]
        acc_ref[...] = acc
        ext[0:CONV_HALO, :] = ext[tm:tm + CONV_HALO, :]
        xhat, _ = _ln_stats(acc)
        n = xhat * g_ref[...] + b_ref[...]
        act = n * _sigmoid(n)
        y_ref[...] = _dot(act.astype(BF16), pw_ref[...]).astype(BF16)

    const = lambda i: (0, 0)
    return pl.pallas_call(
        body,
        grid=(lp // tm,),
        in_specs=[
            pl.BlockSpec((tm, c), lambda i: (i, 1)),
            pl.BlockSpec((tm, c), lambda i: (i, 2)),
            pl.BlockSpec((CONV_HALO, c), const),
            pl.BlockSpec((1, c), const),
            pl.BlockSpec((1, c), const),
            pl.BlockSpec((1, c), const),
            pl.BlockSpec((c, c), const),
        ],
        out_specs=[pl.BlockSpec((tm, c), lambda i: (i, 0)), pl.BlockSpec((tm, c), lambda i: (i, 0))],
        out_shape=[jax.ShapeDtypeStruct((lp, c), BF16), jax.ShapeDtypeStruct((lp, c), F32)],
        scratch_shapes=[pltpu.VMEM((tm + CONV_HALO, c), F32)],
        compiler_params=_cparams(("arbitrary",)),
        name="conv_fwd",
    )(z, z, w_dw, b_dw, ln_g, ln_b, w_pw)


def _conv_bwd(z, dcat, acc_fwd, w_dw, ln_g, ln_b, w_pw):
    lp = z.shape[0]
    c = w_pw.shape[0]
    tm = _pick_tm(lp, MIX_TM, CHUNK)
    ni = lp // tm
    hpt = tm // CONV_HALO
    first = CONV_HALO - CONV_WIDTH + 1

    def body(a_ref, gt_ref, ah_ref, gh_ref, dy_ref, acc_ref, w_ref, g_ref, b_ref, pw_ref,
             dca_ref, dcg_ref, dw_ref, dbd_ref, dg_ref, db_ref, dpw_ref, extu, extd):
        i = pl.program_id(0)
        t = ni - 1 - i

        @pl.when(i == 0)
        def _():
            dw_ref[...] = jnp.zeros_like(dw_ref)
            dbd_ref[...] = jnp.zeros_like(dbd_ref)
            dg_ref[...] = jnp.zeros_like(dg_ref)
            db_ref[...] = jnp.zeros_like(db_ref)
            dpw_ref[...] = jnp.zeros_like(dpw_ref)
            extd[tm:tm + CONV_HALO, :] = jnp.zeros((CONV_HALO, c), F32)

        ca = a_ref[...]
        sg = _sigmoid(gt_ref[...])
        extu[0:CONV_HALO, :] = jnp.where(t > 0, ah_ref[...] * _sigmoid(gh_ref[...]), 0.0)
        extu[CONV_HALO:CONV_HALO + tm, :] = ca * sg
        xhat, rstd = _ln_stats(acc_ref[...])
        gam = g_ref[...]
        n = xhat * gam + b_ref[...]
        sn = _sigmoid(n)
        act = (n * sn).astype(BF16)
        do = dy_ref[...].astype(BF16)
        dpw_ref[...] += _dot_tn(act, do)
        dn = _dot_nt(do, pw_ref[...]) * (sn * (1.0 + n * (1.0 - sn)))
        dg_ref[...] += _colsum(dn * xhat)
        db_ref[...] += _colsum(dn)
        dyc = _ln_bwd(dn, xhat, rstd, gam)
        dbd_ref[...] += _colsum(dyc)
        extd[0:tm, :] = dyc
        du = None
        for k in range(CONV_WIDTH):
            dw_ref[k:k + 1, :] += _colsum(dyc * extu[pl.ds(first + k, tm), :])
            term = w_ref[k:k + 1, :] * extd[pl.ds(CONV_WIDTH - 1 - k, tm), :]
            du = term if du is None else du + term
        extd[tm:tm + CONV_HALO, :] = extd[0:CONV_HALO, :]
        du = jnp.where(_row_ids(t, tm) >= PAD_ROWS, du, 0.0)
        dca_ref[...] = (du * sg).astype(BF16)
        dcg_ref[...] = (du * ca * sg * (1.0 - sg)).astype(BF16)

    const = lambda i: (0, 0)
    rev = lambda col: (lambda i: (ni - 1 - i, col))
    halo = lambda col: (lambda i: (jnp.maximum((ni - 1 - i) * hpt - 1, 0), col))
    return pl.pallas_call(
        body,
        grid=(ni,),
        in_specs=[
            pl.BlockSpec((tm, c), rev(1)),
            pl.BlockSpec((tm, c), rev(2)),
            pl.BlockSpec((CONV_HALO, c), halo(1)),
            pl.BlockSpec((CONV_HALO, c), halo(2)),
            pl.BlockSpec((tm, c), rev(1)),
            pl.BlockSpec((tm, c), rev(0)),
            pl.BlockSpec((CONV_HALO, c), const),
            pl.BlockSpec((1, c), const),
            pl.BlockSpec((1, c), const),
            pl.BlockSpec((c, c), const),
        ],
        out_specs=[
            pl.BlockSpec((tm, c), rev(0)),
            pl.BlockSpec((tm, c), rev(0)),
            pl.BlockSpec((CONV_HALO, c), const),
            pl.BlockSpec((1, c), const),
            pl.BlockSpec((1, c), const),
            pl.BlockSpec((1, c), const),
            pl.BlockSpec((c, c), const),
        ],
        out_shape=[
            jax.ShapeDtypeStruct((lp, c), BF16),
            jax.ShapeDtypeStruct((lp, c), BF16),
            jax.ShapeDtypeStruct((CONV_HALO, c), F32),
            jax.ShapeDtypeStruct((1, c), F32),
            jax.ShapeDtypeStruct((1, c), F32),
            jax.ShapeDtypeStruct((1, c), F32),
            jax.ShapeDtypeStruct((c, c), F32),
        ],
        scratch_shapes=[pltpu.VMEM((tm + CONV_HALO, c), F32), pltpu.VMEM((tm + CONV_HALO, c), F32)],
        compiler_params=_cparams(("arbitrary",)),
        name="conv_bwd",
    )(z, z, z, z, dcat, acc_fwd, w_dw, ln_g, ln_b, w_pw)


def _rope(x, cos, sgn_sin):
    return x * cos + pltpu.roll(x, LANE // 2, 1) * sgn_sin


def _rope_t(dy, cos, sgn_sin):
    return dy * cos + pltpu.roll(dy * sgn_sin, LANE // 2, 1)


def _ret_chunk_fwd(q, k, v, s0, cos, sn, tabs):
    heads = range(len(q))
    bf = lambda xs: [x.astype(BF16) for x in xs]
    qr = [_rope(q[h], cos, sn) for h in heads]
    kr = [_rope(k[h], cos, sn) * (LANE ** -0.5) for h in heads]
    qrb, krb, vb, s0b = bf(qr), bf(kr), bf(v), bf(s0)
    amb = bf([_dot_nt(qrb[h], krb[h]) * tabs[h][0] for h in heads])
    qdb = bf([qr[h] * tabs[h][1] for h in heads])
    intra = [_dot(amb[h], vb[h]) for h in heads]
    cross = [_dot(qdb[h], s0b[h]) for h in heads]
    o = [intra[h] + cross[h] for h in heads]
    return qr, kr, qrb, krb, vb, s0b, amb, qdb, o


def _ret_specs(tm, q_blk, tile_of):
    def mk(col):
        return pl.BlockSpec((tm, LANE), lambda i: (tile_of(i), col))
    return [mk(q_blk + j) for j in range(4 * RET_HEADS)]


def _ret_tables(h, dm_ref, qd_ref, kd_ref, cd_ref, gn_ref):
    return dm_ref[h], qd_ref[h], kd_ref[h], cd_ref[h, 0:1, :], gn_ref[:, h * LANE:(h + 1) * LANE]


def _ret_fwd(z, rope_c, rope_s, dm, qd, kd, cd, gn, q_blk):
    lp = z.shape[0]
    tm = _pick_tm(lp, MIX_TM, CHUNK)
    nc = tm // CHUNK
    ni = lp // tm
    nh = RET_HEADS

    def body(*refs):
        q_refs, k_refs, v_refs, g_refs = refs[0:nh], refs[nh:2 * nh], refs[2 * nh:3 * nh], refs[3 * nh:4 * nh]
        c_ref, s_ref, dm_ref, qd_ref, kd_ref, cd_ref, gn_ref, y_ref, st_ref, state = refs[4 * nh:]

        @pl.when(pl.program_id(0) == 0)
        def _():
            state[...] = jnp.zeros_like(state)

        def chunk(c, carry):
            rows = pl.ds(pl.multiple_of(c * CHUNK, CHUNK), CHUNK)
            cos, sn = c_ref[rows, :], s_ref[rows, :]
            heads = range(nh)
            tabs = [_ret_tables(h, dm_ref, qd_ref, kd_ref, cd_ref, gn_ref) for h in heads]
            q = [q_refs[h][rows, :] for h in heads]
            k = [k_refs[h][rows, :] for h in heads]
            v = [v_refs[h][rows, :] for h in heads]
            gg = [g_refs[h][rows, :] for h in heads]
            s0 = [state[h] for h in heads]
            _, kr, _, _, vb, _, _, _, o = _ret_chunk_fwd(q, k, v, s0, cos, sn, tabs)
            kv = [_dot_tn((kr[h] * tabs[h][2]).astype(BF16), vb[h]) for h in heads]
            for h in heads:
                on, _ = _ln_stats(o[h])
                st_ref[c, h, :, :] = s0[h]
                state[h] = s0[h] * tabs[h][3] + kv[h]
                y_ref[rows, h * LANE:(h + 1) * LANE] = (gg[h] * _sigmoid(gg[h]) * (on * tabs[h][4])).astype(BF16)
            return carry

        lax.fori_loop(0, nc, chunk, 0, unroll=2)

    const3 = lambda s: pl.BlockSpec(s, lambda i: (0, 0, 0))
    return pl.pallas_call(
        body,
        grid=(ni,),
        in_specs=_ret_specs(tm, q_blk, lambda i: i) + [
            pl.BlockSpec((tm, LANE), lambda i: (i, 0)),
            pl.BlockSpec((tm, LANE), lambda i: (i, 0)),
            const3((nh, CHUNK, CHUNK)), const3((nh, CHUNK, LANE)), const3((nh, CHUNK, LANE)), const3((nh, 8, LANE)),
            pl.BlockSpec((1, nh * LANE), lambda i: (0, 0)),
        ],
        out_specs=[
            pl.BlockSpec((tm, nh * LANE), lambda i: (i, 0)),
            pl.BlockSpec((nc, nh, LANE, LANE), lambda i: (i, 0, 0, 0)),
        ],
        out_shape=[
            jax.ShapeDtypeStruct((lp, nh * LANE), BF16),
            jax.ShapeDtypeStruct((lp // CHUNK, nh, LANE, LANE), F32),
        ],
        scratch_shapes=[pltpu.VMEM((nh, LANE, LANE), F32)],
        compiler_params=_cparams(("arbitrary",)),
        name="ret_fwd",
    )(*([z] * (4 * nh)), rope_c, rope_s, dm, qd, kd, cd, gn)


def _ret_bwd(z, dcat, states, rope_c, rope_s, dm, qd, kd, cd, gn, q_blk, dy_blk):
    lp = z.shape[0]
    tm = _pick_tm(lp, MIX_TM, CHUNK)
    nc = tm // CHUNK
    ni = lp // tm

    nh = RET_HEADS

    def body(*refs):
        q_refs, k_refs, v_refs, g_refs = refs[0:nh], refs[nh:2 * nh], refs[2 * nh:3 * nh], refs[3 * nh:4 * nh]
        (dy_ref, st_ref, c_ref, s_ref, dm_ref, qd_ref, kd_ref, cd_ref, gn_ref,
         dq_ref, dk_ref, dv_ref, dgt_ref, dgn_ref, dstate) = refs[4 * nh:]
        i = pl.program_id(0)
        t = ni - 1 - i

        @pl.when(i == 0)
        def _():
            dstate[...] = jnp.zeros_like(dstate)
            dgn_ref[...] = jnp.zeros_like(dgn_ref)

        def chunk(cc, carry):
            c = nc - 1 - cc
            rows = pl.ds(pl.multiple_of(c * CHUNK, CHUNK), CHUNK)
            cos, sn = c_ref[rows, :], s_ref[rows, :]
            keep = (lax.broadcasted_iota(jnp.int32, (CHUNK, 1), 0) + (t * tm + c * CHUNK)) >= PAD_ROWS
            heads = range(nh)
            lanes = [slice(h * LANE, (h + 1) * LANE) for h in heads]
            tabs = [_ret_tables(h, dm_ref, qd_ref, kd_ref, cd_ref, gn_ref) for h in heads]
            q = [q_refs[h][rows, :] for h in heads]
            k = [k_refs[h][rows, :] for h in heads]
            v = [v_refs[h][rows, :] for h in heads]
            gg = [g_refs[h][rows, :] for h in heads]
            dyv = [dy_ref[rows, lanes[h]] for h in heads]
            s0 = [st_ref[c, h, :, :] for h in heads]
            ds1 = [dstate[h] for h in heads]
            dgn = [dgn_ref[:, lanes[h]] for h in heads]
            qr, kr, qrb, krb, vb, s0b, amb, qdb, o = _ret_chunk_fwd(q, k, v, s0, cos, sn, tabs)
            stats = [_ln_stats(o[h]) for h in heads]
            sg = [_sigmoid(gg[h]) for h in heads]
            sl = [gg[h] * sg[h] for h in heads]
            dgt = [dyv[h] * (stats[h][0] * tabs[h][4]) * (sg[h] * (1.0 + gg[h] * (1.0 - sg[h]))) for h in heads]
            dob = [_ln_bwd(dyv[h] * sl[h], stats[h][0], stats[h][1], tabs[h][4]).astype(BF16) for h in heads]
            ds1b = [ds1[h].astype(BF16) for h in heads]
            kdb = [(kr[h] * tabs[h][2]).astype(BF16) for h in heads]
            da = [(_dot_nt(dob[h], vb[h]) * tabs[h][0]).astype(BF16) for h in heads]
            dv_intra = [_dot_tn(amb[h], dob[h]) for h in heads]
            dv_cross = [_dot(kdb[h], ds1b[h]) for h in heads]
            dq_cross = [_dot_nt(dob[h], s0b[h]) for h in heads]
            dk_cross = [_dot_nt(vb[h], ds1b[h]) for h in heads]
            ds_inc = [_dot_tn(qdb[h], dob[h]) for h in heads]
            dq_intra = [_dot(da[h], krb[h]) for h in heads]
            dk_intra = [_dot_tn(da[h], qrb[h]) for h in heads]
            for h in heads:
                dqr = dq_intra[h] + dq_cross[h] * tabs[h][1]
                dkr = dk_intra[h] + dk_cross[h] * tabs[h][2]
                dq = _rope_t(dqr, cos, sn)
                dk = _rope_t(dkr * (LANE ** -0.5), cos, sn)
                dq_ref[rows, lanes[h]] = jnp.where(keep, dq, 0.0).astype(BF16)
                dk_ref[rows, lanes[h]] = jnp.where(keep, dk, 0.0).astype(BF16)
                dv_ref[rows, lanes[h]] = jnp.where(keep, dv_intra[h] + dv_cross[h], 0.0).astype(BF16)
                dgt_ref[rows, lanes[h]] = jnp.where(keep, dgt[h], 0.0).astype(BF16)
                dstate[h] = ds1[h] * tabs[h][3] + ds_inc[h]
                dgn_ref[:, lanes[h]] = dgn[h] + _colsum(dyv[h] * sl[h] * stats[h][0])
            return carry

        lax.fori_loop(0, nc, chunk, 0, unroll=2)

    rev = lambda i: ni - 1 - i
    const3 = lambda s: pl.BlockSpec(s, lambda i: (0, 0, 0))
    out_blk = pl.BlockSpec((tm, nh * LANE), lambda i: (rev(i), 0))
    out_sds = jax.ShapeDtypeStruct((lp, nh * LANE), BF16)
    return pl.pallas_call(
        body,
        grid=(ni,),
        in_specs=_ret_specs(tm, q_blk, rev) + [
            pl.BlockSpec((tm, nh * LANE), lambda i: (rev(i), dy_blk // nh)),
            pl.BlockSpec((nc, nh, LANE, LANE), lambda i: (rev(i), 0, 0, 0)),
            pl.BlockSpec((tm, LANE), lambda i: (rev(i), 0)),
            pl.BlockSpec((tm, LANE), lambda i: (rev(i), 0)),
            const3((nh, CHUNK, CHUNK)), const3((nh, CHUNK, LANE)), const3((nh, CHUNK, LANE)), const3((nh, 8, LANE)),
            pl.BlockSpec((1, nh * LANE), lambda i: (0, 0)),
        ],
        out_specs=[out_blk, out_blk, out_blk, out_blk, pl.BlockSpec((1, nh * LANE), lambda i: (0, 0))],
        out_shape=[out_sds, out_sds, out_sds, out_sds, jax.ShapeDtypeStruct((1, nh * LANE), F32)],
        scratch_shapes=[pltpu.VMEM((nh, LANE, LANE), F32)],
        compiler_params=_cparams(("arbitrary",)),
        name="ret_bwd",
    )(*([z] * (4 * nh)), dcat, states, rope_c, rope_s, dm, qd, kd, cd, gn)


def _adamw(parts_list, w, m, v):
    nl, r, c = w.shape
    assert len(parts_list) == nl
    tr = _pick_tm(r, max(8, (1 << 17) // c), 8)
    nr = r // tr

    def body(*refs):
        p_refs = refs[:nl]
        w_ref, m_ref, v_ref, g_ref, d_ref, nm_ref, nv_ref = refs[nl:]
        layer = pl.program_id(0)
        for k in range(nl):
            @pl.when(layer == k)
            def _(k=k):
                g = p_refs[k][0].astype(F32)
                for j in range(1, N_DEV):
                    g = g + p_refs[k][j].astype(F32)
                m1 = ADAM_B1 * m_ref[...] + (1.0 - ADAM_B1) * g
                v1 = ADAM_B2 * v_ref[...] + (1.0 - ADAM_B2) * (g * g)
                m_hat = m1 / (1.0 - ADAM_B1 ** ADAM_STEP)
                v_hat = v1 / (1.0 - ADAM_B2 ** ADAM_STEP)
                g_ref[...] = g
                d_ref[...] = -ADAM_LR * (m_hat / (jnp.sqrt(v_hat) + ADAM_EPS) + ADAM_WD * w_ref[...])
                nm_ref[...] = m1
                nv_ref[...] = v1

    def parts_spec(k):
        return pl.BlockSpec((N_DEV, tr, c), lambda l, i: (0, jnp.where(l == k, i, jnp.where(l < k, 0, nr - 1)), 0))

    blk = pl.BlockSpec((None, tr, c), lambda l, i: (l, i, 0))
    sds = jax.ShapeDtypeStruct((nl, r, c), F32)
    return pl.pallas_call(
        body,
        grid=(nl, nr),
        in_specs=[parts_spec(k) for k in range(nl)] + [blk, blk, blk],
        out_specs=[blk, blk, blk, blk],
        out_shape=[sds, sds, sds, sds],
        compiler_params=_cparams(("arbitrary", "arbitrary")),
        name="adamw",
    )(*parts_list, w, m, v)


def _flip(v, bit):
    return 1 - v if bit else v


def _exchange(arrs, scatter):
    n = len(arrs)
    shapes = [tuple(a.shape[1:] if scatter else a.shape) for a in arrs]

    def body(*refs):
        x_refs, o_refs = refs[:n], refs[n:2 * n]
        send_sems, recv_sems, local_sems = refs[2 * n:]
        mx, my, mc = lax.axis_index("x"), lax.axis_index("y"), lax.axis_index("c")
        me = 4 * mx + 2 * my + mc

        def peer_of(k):
            return (_flip(mx, (k >> 2) & 1), _flip(my, (k >> 1) & 1), _flip(mc, k & 1))

        def copy(a, k):
            peer = peer_of(k)
            src = x_refs[a].at[4 * peer[0] + 2 * peer[1] + peer[2]] if scatter else x_refs[a]
            return pltpu.make_async_remote_copy(
                src_ref=src, dst_ref=o_refs[a].at[me], send_sem=send_sems.at[a, k - 1],
                recv_sem=recv_sems.at[a, k - 1], device_id=peer, device_id_type=pl.DeviceIdType.MESH)

        def arrival(a, k):
            peer = peer_of(k)
            slot = o_refs[a].at[4 * peer[0] + 2 * peer[1] + peer[2]]
            return pltpu.make_async_remote_copy(
                src_ref=slot, dst_ref=slot, send_sem=send_sems.at[a, k - 1], recv_sem=recv_sems.at[a, k - 1],
                device_id=peer, device_id_type=pl.DeviceIdType.MESH)

        locals_ = [pltpu.make_async_copy(x_refs[a].at[me] if scatter else x_refs[a], o_refs[a].at[me],
                                         local_sems.at[a]) for a in range(n)]
        sends = [copy(a, k) for a in range(n) for k in range(1, N_DEV)]
        for cp in locals_ + sends:
            cp.start()
        for a in range(n):
            for k in range(1, N_DEV):
                arrival(a, k).wait_recv()
        for cp in sends:
            cp.wait_send()
        for cp in locals_:
            cp.wait()

    hbm = pl.BlockSpec(memory_space=pltpu.HBM)
    return pl.pallas_call(
        body,
        in_specs=[hbm] * n,
        out_specs=[hbm] * n,
        out_shape=[jax.ShapeDtypeStruct((N_DEV,) + s, a.dtype) for s, a in zip(shapes, arrs)],
        scratch_shapes=[
            pltpu.SemaphoreType.DMA((n, N_DEV - 1)),
            pltpu.SemaphoreType.DMA((n, N_DEV - 1)),
            pltpu.SemaphoreType.DMA((n,)),
        ],
        name="reduce_scatter_parts" if scatter else "all_gather",
    )(*arrs)


def _exchange_descriptors(x_refs, land_refs, send_sems, recv_sems, scatter):
    mx, my, mc = lax.axis_index("x"), lax.axis_index("y"), lax.axis_index("c")
    me = 4 * mx + 2 * my + mc
    sends, arrivals = [], []
    for a in range(len(x_refs)):
        for k in range(1, N_DEV):
            peer = (_flip(mx, (k >> 2) & 1), _flip(my, (k >> 1) & 1), _flip(mc, k & 1))
            slot = 4 * peer[0] + 2 * peer[1] + peer[2]
            si = a * (N_DEV - 1) + k - 1
            sems = dict(send_sem=send_sems.at[si], recv_sem=recv_sems.at[si],
                        device_id=peer, device_id_type=pl.DeviceIdType.MESH)
            sends.append(pltpu.make_async_remote_copy(
                src_ref=x_refs[a].at[slot] if scatter else x_refs[a], dst_ref=land_refs[a].at[me], **sems))
            arrivals.append(pltpu.make_async_remote_copy(
                src_ref=land_refs[a].at[slot], dst_ref=land_refs[a].at[slot], **sems))
    return sends, arrivals


def _own_slot_copies(x_refs, land_refs, local_sems, scatter):
    me = 4 * lax.axis_index("x") + 2 * lax.axis_index("y") + lax.axis_index("c")
    return [pltpu.make_async_copy(x_refs[a].at[me] if scatter else x_refs[a], land_refs[a].at[me], local_sems.at[a])
            for a in range(len(x_refs))]


def _exchange_start(arrs, scatter, name, after=None):
    n = len(arrs)
    shapes = [tuple(a.shape[1:] if scatter else a.shape) for a in arrs]
    lands = [lax.empty((N_DEV,) + s, a.dtype) for s, a in zip(shapes, arrs)]
    extra = [] if after is None else [after]

    def body(*refs):
        x_refs, land_refs = refs[:n], refs[n:2 * n]
        send_sems, recv_sems, local_sems = refs[2 * n + len(extra):2 * n + len(extra) + 3]
        token = refs[-1]
        sends, _ = _exchange_descriptors(x_refs, land_refs, send_sems, recv_sems, scatter)
        for cp in sends + _own_slot_copies(x_refs, land_refs, local_sems, scatter):
            cp.start()
        token[...] = jnp.zeros_like(token)

    hbm = pl.BlockSpec(memory_space=pltpu.HBM)
    sem = pl.BlockSpec(memory_space=pltpu.SEMAPHORE)
    sem_type = pltpu.SemaphoreType.DMA((n * (N_DEV - 1),))
    operands = [pltpu.with_memory_space_constraint(a, pltpu.HBM) for a in list(arrs) + lands]
    out = pl.pallas_call(
        body,
        in_specs=[hbm] * (2 * n) + [pl.BlockSpec(memory_space=pl.ANY)] * len(extra),
        out_specs=[sem, sem, sem] + [hbm] * (2 * n) + [pl.BlockSpec(memory_space=pltpu.VMEM)],
        out_shape=[sem_type, sem_type, pltpu.SemaphoreType.DMA((n,))] + [pltpu.HBM(a.shape, a.dtype) for a in operands]
        + [jax.ShapeDtypeStruct((8, LANE), F32)],
        input_output_aliases={i: 3 + i for i in range(2 * n)},
        compiler_params=pltpu.CompilerParams(has_side_effects=pltpu.SideEffectType.DATAFLOW_SIDE_EFFECTING),
        name=name,
    )(*operands, *extra)
    return (out[0:3], list(out[3:3 + n]), list(out[3 + n:3 + 2 * n]), scatter), out[-1]


def _exchange_wait(handle, after, name):
    sems, x_thru, land_thru, scatter = handle
    n = len(x_thru)

    def body(*refs):
        x_refs, land_refs = refs[:n], refs[n:2 * n]
        send_sems, recv_sems, local_sems = refs[2 * n:2 * n + 3]
        sends, arrivals = _exchange_descriptors(x_refs, land_refs, send_sems, recv_sems, scatter)
        for cp in sends:
            cp.wait_send()
        for cp in arrivals:
            cp.wait_recv()
        for cp in _own_slot_copies(x_refs, land_refs, local_sems, scatter):
            cp.wait()

    hbm = pl.BlockSpec(memory_space=pltpu.HBM)
    sem = pl.BlockSpec(memory_space=pltpu.SEMAPHORE)
    out = pl.pallas_call(
        body,
        in_specs=[hbm] * (2 * n) + [sem, sem, sem, pl.BlockSpec(memory_space=pl.ANY)],
        out_specs=[hbm] * (2 * n),
        out_shape=[pltpu.HBM(a.shape, a.dtype) for a in x_thru + land_thru],
        input_output_aliases={i: i for i in range(2 * n)},
        compiler_params=pltpu.CompilerParams(has_side_effects=pltpu.SideEffectType.DATAFLOW_SIDE_EFFECTING),
        name=name,
    )(*x_thru, *land_thru, *sems, after)
    return list(out[n:])


PACK_ALIGN = 2048


def _padded(n):
    return -(-n // PACK_ALIGN) * PACK_ALIGN


def _pad_to(a, axis, size):
    pad = [(0, 0)] * a.ndim
    pad[axis] = (0, size - a.shape[axis])
    return jnp.pad(a, pad)


def _pack(arrs, lead=0):
    flat = []
    for a in arrs:
        v = a.reshape(a.shape[:lead] + (-1,))
        flat.append(_pad_to(v, lead, _padded(v.shape[lead])))
    out = jnp.concatenate(flat, axis=lead)
    return out.reshape(out.shape[:lead] + (-1, LANE))


def _unpack(slab, shapes, lead=0):
    flat = slab.reshape(slab.shape[:lead] + (-1,))
    out, off = [], 0
    for s in shapes:
        n = math.prod(s)
        out.append(flat[..., off:off + n].reshape(slab.shape[:lead] + tuple(s)))
        off += _padded(n)
    return out


def _unshard(parts, ax):
    return jnp.concatenate([parts[j] for j in range(N_DEV)], axis=ax)


def _to_shards(full, ax):
    n = full.shape[ax] // N_DEV
    return jnp.stack([lax.slice_in_dim(full, j * n, (j + 1) * n, axis=ax) for j in range(N_DEV)])


def _pad_halves(w, hp):
    h = w.shape[-1] // 2
    zeros = jnp.zeros(w.shape[:-1] + (hp - h,), w.dtype)
    return jnp.concatenate([w[..., :h], zeros, w[..., h:], zeros], axis=-1)


def _unpad_halves(w, h):
    hp = w.shape[-1] // 2
    return jnp.concatenate([w[..., :h], w[..., hp:hp + h]], axis=-1)


SMALL_SHARDED = (("meta", 1), ("conv_dw", 2), ("ln_g", 2), ("ln_b", 2))
MATMUL_WEIGHTS = ("ffn1_w13", "ffn1_w2", "w_in", "conv_pw", "w_out", "ffn2_w13", "ffn2_w2")
REPLICATED = ("ln_in_g", "ln_in_b", "pool_w", "pool_scale", "conv_db", "conv_ln_g", "conv_ln_b", "ret_gn_g")
WEIGHT_ORDER = ("meta", "ln_in_g", "ln_in_b", "ffn1_w13", "ffn1_w2", "w_in", "pool_w", "pool_scale", "conv_dw",
                "conv_db", "conv_ln_g", "conv_ln_b", "conv_pw", "ret_gn_g", "w_out", "ffn2_w13", "ffn2_w2",
                "ln_g", "ln_b")


def _retention_tables(lp, heads):
    pos = jnp.arange(lp, dtype=F32) - PAD_ROWS
    inv_freq = ROPE_BASE ** (-jnp.arange(0, LANE, 2, dtype=F32) / LANE)
    ang = pos[:, None] * inv_freq[None, :]
    cos, sin = jnp.cos(ang), jnp.sin(ang)
    rope_c = jnp.concatenate([cos, cos], axis=1)
    rope_s = jnp.concatenate([-sin, sin], axis=1)
    log_gamma = jnp.log(1.0 - 2.0 ** (-5.0 - jnp.arange(heads, dtype=F32)))
    i = jnp.arange(CHUNK, dtype=F32)
    dm = jnp.exp(log_gamma[:, None, None] * jnp.abs(i[:, None] - i[None, :]))
    lanes = lambda t: jnp.broadcast_to(t[:, :, None], t.shape + (LANE,))
    qd = lanes(jnp.exp(log_gamma[:, None] * (i + 1.0)))
    kd = lanes(jnp.exp(log_gamma[:, None] * (CHUNK - 1.0 - i)))
    cd = lanes(jnp.broadcast_to(jnp.exp(log_gamma * CHUNK)[:, None], (heads, 8)))
    return rope_c, rope_s, dm, qd, kd, cd


def _block_diag(w):
    g, n, _ = w.shape
    rows = []
    for i in range(g):
        rows.append(jnp.concatenate([w[i] if j == i else jnp.zeros((n, n), w.dtype) for j in range(g)], axis=1))
    return jnp.concatenate(rows, axis=0)


def kernel(x, meta, ln_in_g, ln_in_b, ffn1_w13, ffn1_w2, w_in, pool_w, pool_scale, conv_dw, conv_db, conv_ln_g, conv_ln_b, conv_pw, ret_gn_g, w_out, ffn2_w13, ffn2_w2, ln_g, ln_b, loss_target, m_meta, m_ln_in_g, m_ln_in_b, m_ffn1_w13, m_ffn1_w2, m_w_in, m_pool_w, m_pool_scale, m_conv_dw, m_conv_db, m_conv_ln_g, m_conv_ln_b, m_conv_pw, m_ret_gn_g, m_w_out, m_ffn2_w13, m_ffn2_w2, m_ln_g, m_ln_b, v_meta, v_ln_in_g, v_ln_in_b, v_ffn1_w13, v_ffn1_w2, v_w_in, v_pool_w, v_pool_scale, v_conv_dw, v_conv_db, v_conv_ln_g, v_conv_ln_b, v_conv_pw, v_ret_gn_g, v_w_out, v_ffn2_w13, v_ffn2_w2, v_ln_g, v_ln_b):
    local = dict(meta=meta, ln_in_g=ln_in_g, ln_in_b=ln_in_b, ffn1_w13=ffn1_w13, ffn1_w2=ffn1_w2, w_in=w_in,
                 pool_w=pool_w, pool_scale=pool_scale, conv_dw=conv_dw, conv_db=conv_db, conv_ln_g=conv_ln_g,
                 conv_ln_b=conv_ln_b, conv_pw=conv_pw, ret_gn_g=ret_gn_g, w_out=w_out, ffn2_w13=ffn2_w13,
                 ffn2_w2=ffn2_w2, ln_g=ln_g, ln_b=ln_b)
    mom1 = dict(meta=m_meta, ln_in_g=m_ln_in_g, ln_in_b=m_ln_in_b, ffn1_w13=m_ffn1_w13, ffn1_w2=m_ffn1_w2,
                w_in=m_w_in, pool_w=m_pool_w, pool_scale=m_pool_scale, conv_dw=m_conv_dw, conv_db=m_conv_db,
                conv_ln_g=m_conv_ln_g, conv_ln_b=m_conv_ln_b, conv_pw=m_conv_pw, ret_gn_g=m_ret_gn_g,
                w_out=m_w_out, ffn2_w13=m_ffn2_w13, ffn2_w2=m_ffn2_w2, ln_g=m_ln_g, ln_b=m_ln_b)
    mom2 = dict(meta=v_meta, ln_in_g=v_ln_in_g, ln_in_b=v_ln_in_b, ffn1_w13=v_ffn1_w13, ffn1_w2=v_ffn1_w2,
                w_in=v_w_in, pool_w=v_pool_w, pool_scale=v_pool_scale, conv_dw=v_conv_dw, conv_db=v_conv_db,
                conv_ln_g=v_conv_ln_g, conv_ln_b=v_conv_ln_b, conv_pw=v_conv_pw, ret_gn_g=v_ret_gn_g,
                w_out=v_w_out, ffn2_w13=v_ffn2_w13, ffn2_w2=v_ffn2_w2, ln_g=v_ln_g, ln_b=v_ln_b)

    depth = ffn1_w13.shape[0]
    alpha = (2.0 * depth) ** 0.25
    seq, d = x.shape[1], x.shape[2]
    lp = FRONT + seq
    h_loc = ffn1_w2.shape[1]
    hp = -(-h_loc // LANE) * LANE
    c_pool = pool_scale.shape[1]
    c_conv = conv_db.shape[1]
    q_blk = (c_pool + 2 * c_conv) // LANE
    dy_blk = (c_pool + c_conv) // LANE
    heads = ret_gn_g.shape[1] // LANE
    assert meta.shape[0] == N_META and heads == RET_HEADS and conv_dw.shape[1] == CONV_WIDTH
    assert ffn1_w13.shape[2] == 2 * h_loc

    def to_wire(name, w):
        if name in ("ffn1_w13", "ffn2_w13"):
            return _pad_halves(w, hp)
        if name in ("ffn1_w2", "ffn2_w2"):
            return _pad_to(w, w.ndim - 2, hp)
        return w

    def from_wire(name, w):
        if name in ("ffn1_w13", "ffn2_w13"):
            return _unpad_halves(w, h_loc)
        if name in ("ffn1_w2", "ffn2_w2"):
            return w[..., :h_loc, :]
        return w

    small_names = [n for n, _ in SMALL_SHARDED]
    small_ax = dict(SMALL_SHARDED)
    first = ("ffn1_w13", "ffn1_w2")
    wire = lambda l, names: [to_wire(n, local[n][l]).astype(BF16) for n in names]
    layer0_groups = [first, ("w_in", "conv_pw"), ("w_out",), ("ffn2_w13", "ffn2_w2")]
    pending = {0: [(1 + i, names) for i, names in enumerate(layer0_groups)]}
    pending.update({l: [(len(layer0_groups) + l, MATMUL_WEIGHTS)] for l in range(1, depth)})
    groups = [[_pack([local[n] for n in small_names])]] + [wire(0, names) for names in layer0_groups]
    groups += [wire(l, MATMUL_WEIGHTS) for l in range(1, depth)]
    handles, token = [], None
    for gi, arrs in enumerate(groups):
        handle, token = _exchange_start(arrs, False, f"gather_start_{gi}", after=token)
        handles.append(handle)
    gathered = {l: {} for l in range(depth)}

    def weight(l, name, after):
        for gi, names in list(pending[l]):
            if name in names:
                gathered[l].update(zip(names, _exchange_wait(handles[gi], after, f"gather_wait_{gi}")))
                pending[l].remove((gi, names))
        return gathered[l][name]
    got = _exchange_wait(handles[0], token, "gather_wait_0")
    small_parts = _unpack(got[0], [local[n].shape for n in small_names], 1)
    small_full = {n: _unshard(p, small_ax[n]) for n, p in zip(small_names, small_parts)}

    row = lambda v: v.reshape(1, -1)
    ln_params = [(row(ln_in_g), row(ln_in_b))]
    for l in range(depth):
        ln_params += [(row(small_full["ln_g"][l, j]), row(small_full["ln_b"][l, j])) for j in range(3)]
    rope_c, rope_s, dm, qd, kd, cd = _retention_tables(lp, heads)

    r = [jnp.concatenate([jnp.zeros((PAD_ROWS, d), F32), small_full["meta"], x[0]], axis=0)]
    saved, layers = [], []
    for l in range(depth):
        k = 3 * l
        p = dict(pool=(_block_diag(pool_w[l]).astype(BF16), row(pool_scale[l])), gn=row(ret_gn_g[l]))
        p["ffn1"] = (weight(l, "ffn1_w13", r[k]), weight(l, "ffn1_w2", r[k]))
        r1, a1, u1 = _ffn_fwd(r[k], *ln_params[k], *p["ffn1"], alpha)
        p["w_in"] = _unshard(weight(l, "w_in", r1), 1)
        p["conv"] = (_pad_to(small_full["conv_dw"][l], 0, CONV_HALO), row(conv_db[l]), row(conv_ln_g[l]),
                     row(conv_ln_b[l]), weight(l, "conv_pw", r1).reshape(c_conv, c_conv))
        z = _proj_fwd(r1, *ln_params[k + 1], p["w_in"])
        y_pool = _pool_fwd(z, *p["pool"])
        y_conv, conv_acc = _conv_fwd(z, *p["conv"])
        y_ret, states = _ret_fwd(z, rope_c, rope_s, dm, qd, kd, cd, p["gn"], q_blk)
        cat = jnp.concatenate([y_pool, y_conv, y_ret], axis=1)
        p["w_out"] = weight(l, "w_out", cat).reshape(-1, d)
        r2 = _out_fwd(r1, *ln_params[k + 1], cat, p["w_out"], alpha)
        p["ffn2"] = (weight(l, "ffn2_w13", r2), weight(l, "ffn2_w2", r2))
        r3, a2, u2 = _ffn_fwd(r2, *ln_params[k + 2], *p["ffn2"], alpha)
        r += [r1, r2, r3]
        layers.append(p)
        saved.append((a1, u1, z, states, cat, a2, u2, conv_acc))

    target = jnp.concatenate([jnp.zeros((FRONT, d), F32), loss_target[0]], axis=0)
    dr, dg, db, loss_cols = _loss_bwd(r[-1], target, *ln_params[-1])
    loss = lax.psum(0.5 * jnp.sum(loss_cols) / d, MESH_AXES)
    ln_grads = {3 * depth: (dg, db)}
    w13_slot = lambda j: (j % 2) * (N_DEV // 2) + j // 2
    g_parts = [None] * depth
    g_rep = [None] * depth
    scatter_jobs = []

    def start_scatter(l, gp, names, extra=()):
        parts = [gp[n].astype(WIRE) for n in names] + list(extra)
        handle, token = _exchange_start(parts, True, f"scatter_start_{len(scatter_jobs)}")
        scatter_jobs.append((handle, l, names))
        return token

    def ffn_grads(dr_out, r_in, a, u, ln_p, wts, scatter_w13_of=None):
        dr_in, hb, s, dz, dg, db = _ffn_bwd(dr_out, r_in, a, u, *ln_p, *wts, alpha)
        g13 = _mm_tn(hb, dz, bn=2 * hp, slot_of=w13_slot)
        if scatter_w13_of is not None:
            start_scatter(scatter_w13_of, {"ffn1_w13": g13}, ("ffn1_w13",))
        g2 = _mm_tn(s, dr_out, 0.5).reshape(N_DEV, hp, d)
        return dr_in, (dg, db), g13, g2

    for l in reversed(range(depth)):
        p = layers[l]
        k = 3 * l
        a1, u1, z, states, cat, a2, u2, conv_acc = saved[l]
        gp, gr = {}, {}
        dr, ln_grads[k + 2], gp["ffn2_w13"], gp["ffn2_w2"] = ffn_grads(dr, r[k + 2], a2, u2, ln_params[k + 2], p["ffn2"])
        zero = start_scatter(l, gp, ("ffn2_w13", "ffn2_w2"))[0:1, 0:1]

        dr2 = dr
        dcat = _mm_nt(dr2, p["w_out"])
        gp["w_out"] = _mm_tn(cat, dr2).reshape(N_DEV, -1, d)
        dxp, dwbd, gr["pool_scale"] = _pool_bwd(z, dcat, p["pool"][0], p["pool"][1] + zero)
        dca, dcg, ddw, gr["conv_db"], gr["conv_ln_g"], gr["conv_ln_b"], dpw = _conv_bwd(
            z, dcat, conv_acc, p["conv"][0], p["conv"][2], p["conv"][3], p["conv"][4])
        dq, dk, dv, dgt, gr["ret_gn_g"] = _ret_bwd(z, dcat, states, rope_c, rope_s, dm, qd, kd, cd, p["gn"],
                                                  q_blk, dy_blk)
        dz = jnp.concatenate([dxp, dca, dcg, dq, dk, dv, dgt], axis=1)
        dr, hb, dg, db = _in_bwd(dr2, r[k + 1], dz, *ln_params[k + 1], p["w_in"], alpha)
        ln_grads[k + 1] = (dg, db)
        gp["w_in"] = _to_shards(_mm_tn(hb, dz), 1)
        gp["conv_pw"] = dpw.reshape(N_DEV, -1, c_conv)
        n_grp, grp = pool_w.shape[1], pool_w.shape[2]
        gr["pool_w"] = jnp.stack([dwbd[i * grp:(i + 1) * grp, i * grp:(i + 1) * grp] for i in range(n_grp)])
        gr["conv_dw"] = ddw[:CONV_WIDTH]

        zero = start_scatter(l, gp, ("w_in", "conv_pw", "w_out"))[0:1, 0:1]
        ln_g_in, ln_b_in = ln_params[k]
        dr, ln_grads[k], gp["ffn1_w13"], gp["ffn1_w2"] = ffn_grads(
            dr, r[k], a1, u1, (ln_g_in + zero, ln_b_in), p["ffn1"], scatter_w13_of=0 if l == 0 else None)
        g_parts[l], g_rep[l] = gp, gr
        if l > 0:
            zero = start_scatter(l, gp, ("ffn1_w13", "ffn1_w2"))[0:1, 0:1]
            g_prev, b_prev = ln_params[k - 1]
            ln_params[k - 1] = (g_prev + zero, b_prev)

    grad_x = dr[FRONT:][None]
    local_shape = lambda name: ((CONV_WIDTH, c_conv) if name == "conv_dw" else local[name].shape[1:])
    stack_layers = lambda name: jnp.stack([g_rep[l][name].reshape(local_shape(name)) for l in range(depth)])
    small_grad = dict(
        meta=dr[PAD_ROWS:FRONT],
        conv_dw=stack_layers("conv_dw"),
        ln_g=jnp.stack([jnp.stack([ln_grads[3 * l + j + 1][0][0] for j in range(3)]) for l in range(depth)]),
        ln_b=jnp.stack([jnp.stack([ln_grads[3 * l + j + 1][1][0] for j in range(3)]) for l in range(depth)]),
    )
    rep_grad = {n: stack_layers(n) for n in REPLICATED if n not in ("ln_in_g", "ln_in_b")}
    rep_grad["ln_in_g"], rep_grad["ln_in_b"] = ln_grads[0][0][0], ln_grads[0][1][0]

    small_pack8 = _pack([_to_shards(small_grad[n], small_ax[n]) for n in small_names], 1)
    after = start_scatter(0, g_parts[0], ("ffn1_w2",), extra=[small_pack8])
    rep_parts = _exchange([_pack([rep_grad[n] for n in REPLICATED])], False)[0]
    scattered, outs_by_name = {}, {}

    def update(names):
        nonlocal after
        for n in names:
            res = _adamw([scattered[l, n] for l in range(depth)], *[to_wire(n, src[n]) for src in (local, mom1, mom2)])
            outs_by_name[n] = [from_wire(n, t) for t in res]
            after = res[0]

    for ji, (handle, l, names) in enumerate(scatter_jobs):
        if ji == len(scatter_jobs) - 1:
            update([n for n in MATMUL_WEIGHTS if n not in names])
        got = _exchange_wait(handle, after, f"scatter_wait_{ji}")
        scattered.update({(l, n): t for n, t in zip(names, got)})
        after = got[0]
    small_scattered = got[-1]
    update(names)
    for names, parts in ((small_names, small_scattered), (REPLICATED, rep_parts)):
        shapes = [local[n].shape for n in names]
        res = _adamw([parts], *[_pack([src[n] for n in names])[None] for src in (local, mom1, mom2)])
        for n, vals in zip(names, zip(*[_unpack(t[0], shapes) for t in res])):
            outs_by_name[n] = list(vals)

    outs = [loss, grad_x]
    for kind in range(4):
        outs += [outs_by_name[n][kind] for n in WEIGHT_ORDER]
    return tuple(outs)
```
